```python
import math
import jax
import jax.numpy as jnp
from jax import lax
import numpy as np

D_MODEL = 2048
BATCH = 4
SEQ = 2048
DEPTH = 2
DEC_BATCH = 128
DEC_SEQ = 1
PAST_LEN = 16384
PAGE_SIZE = 128

MIX_WIDTH = D_MODEL
RET_WIDTH = 3 * D_MODEL // 8
MLSTM_WIDTH = 3 * D_MODEL // 8
POOL_WIDTH = MIX_WIDTH - RET_WIDTH - MLSTM_WIDTH
RET_HEADS = 6
RET_HEAD_DIM = RET_WIDTH // RET_HEADS
MLSTM_HEADS = 4
MLSTM_V_DIM = MLSTM_WIDTH // MLSTM_HEADS
MLSTM_QK_DIM = MLSTM_V_DIM // 2
POOL_WINDOWS = (2, 4, 8, 16)
POOL_GROUPS = len(POOL_WINDOWS)
POOL_GROUP_WIDTH = POOL_WIDTH // POOL_GROUPS
POOL_BUF = max(POOL_WINDOWS) - 1
D_FF = ((8 * D_MODEL + 3 * 256 - 1) // (3 * 256)) * 256
CHUNK = 128
ROPE_THETA = 10000.0
EPS = 1e-6
IN_SPLITS = (RET_WIDTH, RET_WIDTH, RET_WIDTH, RET_WIDTH,
             MLSTM_HEADS * MLSTM_QK_DIM, MLSTM_HEADS * MLSTM_QK_DIM,
             MLSTM_WIDTH, MLSTM_WIDTH, MLSTM_HEADS, MLSTM_HEADS, POOL_WIDTH)
IN_WIDTH = sum(IN_SPLITS)

kernel_name = 'hybrid_retention_mlstm_pool_step'


def _rms_norm(x, g):
    xf = x.astype(jnp.float32)
    y = xf * lax.rsqrt(jnp.mean(xf * xf, axis=-1, keepdims=True) + EPS)
    return (y * g.astype(jnp.float32)).astype(x.dtype)


def _head_rms_norm(x, g):
    b, l, h, d = x.shape
    y = x * lax.rsqrt(jnp.mean(x * x, axis=-1, keepdims=True) + EPS)
    return y.reshape(b, l, h * d) * g.astype(jnp.float32)


def _rotary(x, pos):
    half = x.shape[-1] // 2
    inv_freq = ROPE_THETA ** (-jnp.arange(half, dtype=jnp.float32) / half)
    ang = pos[:, None] * inv_freq[None, :]
    cos = jnp.cos(ang)[None, :, None, :]
    sin = jnp.sin(ang)[None, :, None, :]
    x1, x2 = x[..., :half], x[..., half:]
    return jnp.concatenate([x1 * cos - x2 * sin, x1 * sin + x2 * cos], axis=-1)


def _chunk_len(length):
    return length if length <= CHUNK else math.gcd(length, CHUNK)


def _to_chunks(t, c):
    b, l = t.shape[:2]
    t = t.reshape((b, l // c, c) + t.shape[2:])
    return jnp.swapaxes(jnp.moveaxis(t, 1, 0), 2, 3)


def _from_chunks(t):
    n, b, h, c, d = t.shape
    return jnp.moveaxis(jnp.swapaxes(t, 2, 3), 0, 1).reshape(b, n * c, h, d)


def _retention(q, k, v, s0, log_gamma):
    c = _chunk_len(q.shape[1])
    qc, kc, vc = (_to_chunks(t, c) for t in (q, k, v))
    idx = jnp.arange(c, dtype=jnp.float32)
    diff = idx[:, None] - idx[None, :]
    causal = diff >= 0
    lg = log_gamma[:, None, None]
    decay_intra = jnp.where(causal, jnp.exp(lg * jnp.where(causal, diff, 0.0)), 0.0)
    decay_q = jnp.exp(log_gamma[:, None] * (idx + 1.0))[None, :, :, None]
    decay_k = jnp.exp(log_gamma[:, None] * (c - 1.0 - idx))[None, :, :, None]
    decay_chunk = jnp.exp(log_gamma * c)[None, :, None, None]

    def step(s, inp):
        qi, ki, vi = inp
        scores = jnp.einsum('bhid,bhjd->bhij', qi, ki) * decay_intra
        o = (jnp.einsum('bhij,bhjv->bhiv', scores, vi)
             + jnp.einsum('bhid,bhdv->bhiv', qi, s) * decay_q)
        s_new = s * decay_chunk + jnp.einsum('bhjd,bhjv->bhdv', ki * decay_k, vi)
        return s_new, o

    s_fin, o = lax.scan(step, s0.astype(jnp.float32), (qc, kc, vc))
    return _from_chunks(o), s_fin


def _mlstm(q, k, v, i_pre, logf, c0, n0, m0):
    c = _chunk_len(q.shape[1])
    qc, kc, vc, ic, fc = (_to_chunks(t, c) for t in (q, k, v, i_pre, logf))
    causal = jnp.tril(jnp.ones((c, c), dtype=bool))

    def step(carry, inp):
        cm, nv, m = carry
        qi, ki, vi, ii, lf = inp
        b = jnp.cumsum(lf, axis=-1)
        logw = jnp.where(causal, b[..., :, None] - b[..., None, :] + ii[..., None, :], -jnp.inf)
        inter = b + m[..., None]
        m_t = jnp.maximum(inter, jnp.max(logw, axis=-1))
        w = jnp.exp(logw - m_t[..., None])
        a = jnp.exp(inter - m_t)
        s = jnp.einsum('bhtd,bhjd->bhtj', qi, ki) * w
        num = (jnp.einsum('bhtj,bhjv->bhtv', s, vi)
               + a[..., None] * jnp.einsum('bhtd,bhdv->bhtv', qi, cm))
        den = jnp.sum(s, axis=-1) + a * jnp.einsum('bhtd,bhd->bht', qi, nv)
        h = num / jnp.maximum(jnp.abs(den), jnp.exp(-m_t))[..., None]
        m_new = m_t[..., -1]
        wk = jnp.exp(b[..., -1:] - b + ii - m_new[..., None])[..., None] * ki
        a_end = jnp.exp(b[..., -1] + m - m_new)
        cm_new = a_end[..., None, None] * cm + jnp.einsum('bhjd,bhjv->bhdv', wk, vi)
        nv_new = a_end[..., None] * nv + jnp.sum(wk, axis=2)
        return (cm_new, nv_new, m_new), h

    f32 = jnp.float32
    (cf, nf, mf), h = lax.scan(step, (c0.astype(f32), n0.astype(f32), m0.astype(f32)),
                               (qc, kc, vc, ic, fc))
    return _from_chunks(h), cf, nf, mf


def _pool_mixer(u, buf, pos, w_pool, pool_scale):
    b, l, p = u.shape
    f32 = jnp.float32
    full = jnp.concatenate([buf.astype(f32), u], axis=1)
    cs = jnp.concatenate([jnp.zeros((b, 1, p), f32), jnp.cumsum(full, axis=1)], axis=1)
    upper = cs[:, POOL_BUF + 1:, :]
    z = []
    for g, win in enumerate(POOL_WINDOWS):
        sl = slice(g * POOL_GROUP_WIDTH, (g + 1) * POOL_GROUP_WIDTH)
        lower = cs[:, POOL_BUF + 1 - win:POOL_BUF + 1 - win + l, sl]
        count = jnp.minimum(pos + 1.0, float(win))[None, :, None]
        z.append((upper[..., sl] - lower) / count - u[..., sl])
    z = jnp.stack(z, axis=2)
    y = jnp.einsum('blgc,gce->blge', z, w_pool.astype(f32)).reshape(b, l, p)
    return y * pool_scale.astype(f32), full[:, -POOL_BUF:, :]


def _layer(x, pos, s_ret, s_c, s_n, s_m, s_pool,
           w_in, w_out, g_ret_norm, g_mlstm_norm, b_mlstm_i, b_mlstm_f, w_pool, pool_scale,
           g_pre_mix, g_post_mix, g_pre_ffn, g_post_ffn, w_ffn_gate, w_ffn_up, w_ffn_down):
    b, l, _ = x.shape
    f32 = jnp.float32
    xn = _rms_norm(x, g_pre_mix)
    proj = xn @ w_in
    parts = []
    start = 0
    for width in IN_SPLITS:
        parts.append(proj[..., start:start + width].astype(f32))
        start += width
    rq, rk, rv, rg, mq, mk, mv, mo, mi, mf, pu = parts

    rq = _rotary(rq.reshape(b, l, RET_HEADS, RET_HEAD_DIM), pos)
    rk = _rotary(rk.reshape(b, l, RET_HEADS, RET_HEAD_DIM), pos) * RET_HEAD_DIM ** -0.5
    rv = rv.reshape(b, l, RET_HEADS, RET_HEAD_DIM)
    log_gamma = jnp.log1p(-jnp.exp2(-5.0 - jnp.arange(RET_HEADS, dtype=f32)))
    ro, s_ret_new = _retention(rq, rk, rv, s_ret, log_gamma)
    ret_out = _head_rms_norm(ro, g_ret_norm) * jax.nn.silu(rg)

    mq = mq.reshape(b, l, MLSTM_HEADS, MLSTM_QK_DIM)
    mk = mk.reshape(b, l, MLSTM_HEADS, MLSTM_QK_DIM) * MLSTM_QK_DIM ** -0.5
    mv = mv.reshape(b, l, MLSTM_HEADS, MLSTM_V_DIM)
    i_pre = mi + b_mlstm_i.astype(f32)
    logf = jax.nn.log_sigmoid(mf + b_mlstm_f.astype(f32))
    mh, c_new, n_new, m_new = _mlstm(mq, mk, mv, i_pre, logf, s_c, s_n, s_m)
    mlstm_out = _head_rms_norm(mh, g_mlstm_norm) * jax.nn.sigmoid(mo)

    pool_out, pool_new = _pool_mixer(pu, s_pool, pos, w_pool, pool_scale)

    mixed = jnp.concatenate([ret_out, mlstm_out, pool_out], axis=-1).astype(x.dtype) @ w_out
    x = x + _rms_norm(mixed, g_post_mix)
    hn = _rms_norm(x, g_pre_ffn)
    ff = (jax.nn.silu(hn @ w_ffn_gate) * (hn @ w_ffn_up)) @ w_ffn_down
    x = x + _rms_norm(ff, g_post_ffn)
    return x, (s_ret_new, c_new, n_new, m_new, pool_new)


def setup_inputs(seed: int = 0) -> dict:
    key = jax.random.key(seed)
    ks = jax.random.split(key, 24)
    f32 = jnp.float32

    def nrm(k, shape, scale):
        return scale * jax.random.normal(k, shape, f32)

    def gain(k, shape):
        return 1.0 + 0.1 * jax.random.normal(k, shape, f32)

    return {
        'x_prompt': nrm(ks[0], (BATCH, SEQ, D_MODEL), 1.0),
        'x_sample': nrm(ks[1], (DEC_BATCH, DEC_SEQ, D_MODEL), 1.0),
        'state_ret': nrm(ks[2], (DEPTH, DEC_BATCH, RET_HEADS, RET_HEAD_DIM, RET_HEAD_DIM), RET_HEAD_DIM ** -0.5),
        'state_mlstm_c': nrm(ks[3], (DEPTH, DEC_BATCH, MLSTM_HEADS, MLSTM_QK_DIM, MLSTM_V_DIM), 0.5),
        'state_mlstm_n': nrm(ks[4], (DEPTH, DEC_BATCH, MLSTM_HEADS, MLSTM_QK_DIM), 0.5),
        'state_mlstm_m': nrm(ks[5], (DEPTH, DEC_BATCH, MLSTM_HEADS), 1.0),
        'state_pool': nrm(ks[6], (DEPTH, DEC_BATCH, POOL_BUF, POOL_WIDTH), 1.0),
        'w_in': nrm(ks[7], (DEPTH, D_MODEL, IN_WIDTH), D_MODEL ** -0.5),
        'w_out': nrm(ks[8], (DEPTH, MIX_WIDTH, D_MODEL), MIX_WIDTH ** -0.5),
        'g_ret_norm': gain(ks[9], (DEPTH, RET_WIDTH)),
        'g_mlstm_norm': gain(ks[10], (DEPTH, MLSTM_WIDTH)),
        'b_mlstm_i': nrm(ks[11], (DEPTH, MLSTM_HEADS), 0.1),
        'b_mlstm_f': jnp.linspace(3.0, 6.0, MLSTM_HEADS, dtype=f32)[None, :] + nrm(ks[12], (DEPTH, MLSTM_HEADS), 0.1),
        'w_pool': nrm(ks[13], (DEPTH, POOL_GROUPS, POOL_GROUP_WIDTH, POOL_GROUP_WIDTH), POOL_GROUP_WIDTH ** -0.5),
        'pool_scale': gain(ks[14], (DEPTH, POOL_WIDTH)),
        'g_pre_mix': gain(ks[15], (DEPTH, D_MODEL)),
        'g_post_mix': gain(ks[16], (DEPTH, D_MODEL)),
        'g_pre_ffn': gain(ks[17], (DEPTH, D_MODEL)),
        'g_post_ffn': gain(ks[18], (DEPTH, D_MODEL)),
        'w_ffn_gate': nrm(ks[19], (DEPTH, D_MODEL, D_FF), D_MODEL ** -0.5),
        'w_ffn_up': nrm(ks[20], (DEPTH, D_MODEL, D_FF), D_MODEL ** -0.5),
        'w_ffn_down': nrm(ks[21], (DEPTH, D_FF, D_MODEL), D_FF ** -0.5),
    }


def reference(x_prompt, x_sample, state_ret, state_mlstm_c, state_mlstm_n, state_mlstm_m, state_pool,
              w_in, w_out, g_ret_norm, g_mlstm_norm, b_mlstm_i, b_mlstm_f, w_pool, pool_scale,
              g_pre_mix, g_post_mix, g_pre_ffn, g_post_ffn, w_ffn_gate, w_ffn_up, w_ffn_down):
    f32 = jnp.float32
    bp, lp, _ = x_prompt.shape
    pos_prompt = jnp.arange(lp, dtype=f32)
    pos_sample = (PAST_LEN + jnp.arange(x_sample.shape[1])).astype(f32)
    zero_ret = jnp.zeros((bp, RET_HEADS, RET_HEAD_DIM, RET_HEAD_DIM), f32)
    zero_c = jnp.zeros((bp, MLSTM_HEADS, MLSTM_QK_DIM, MLSTM_V_DIM), f32)
    zero_n = jnp.zeros((bp, MLSTM_HEADS, MLSTM_QK_DIM), f32)
    zero_m = jnp.zeros((bp, MLSTM_HEADS), f32)
    zero_pool = jnp.zeros((bp, POOL_BUF, POOL_WIDTH), f32)

    y_prompt, y_sample = x_prompt, x_sample
    new_p, new_s = [], []
    for layer in range(DEPTH):
        params = (w_in[layer], w_out[layer], g_ret_norm[layer], g_mlstm_norm[layer],
                  b_mlstm_i[layer], b_mlstm_f[layer], w_pool[layer], pool_scale[layer],
                  g_pre_mix[layer], g_post_mix[layer], g_pre_ffn[layer], g_post_ffn[layer],
                  w_ffn_gate[layer], w_ffn_up[layer], w_ffn_down[layer])
        y_prompt, st_p = _layer(y_prompt, pos_prompt, zero_ret, zero_c, zero_n, zero_m, zero_pool, *params)
        y_sample, st_s = _layer(y_sample, pos_sample, state_ret[layer], state_mlstm_c[layer],
                                state_mlstm_n[layer], state_mlstm_m[layer], state_pool[layer], *params)
        new_p.append(st_p)
        new_s.append(st_s)

    def stack(states, i, dtype):
        return jnp.stack([s[i] for s in states], axis=0).astype(dtype)

    ret_p = stack(new_p, 0, state_ret.dtype)
    c_p = stack(new_p, 1, state_mlstm_c.dtype)
    n_p = stack(new_p, 2, state_mlstm_n.dtype)
    m_p = stack(new_p, 3, state_mlstm_m.dtype)
    pool_p = stack(new_p, 4, state_pool.dtype)
    ret_s = stack(new_s, 0, state_ret.dtype)
    c_s = stack(new_s, 1, state_mlstm_c.dtype)
    n_s = stack(new_s, 2, state_mlstm_n.dtype)
    m_s = stack(new_s, 3, state_mlstm_m.dtype)
    pool_s = stack(new_s, 4, state_pool.dtype)
    return (y_prompt, y_sample, ret_p, c_p, n_p, m_p, pool_p, ret_s, c_s, n_s, m_s, pool_s)
```

```python
import functools

import jax
import jax.numpy as jnp
from jax import lax
from jax.experimental import pallas as pl
from jax.experimental.pallas import tpu as pltpu

F32 = jnp.float32
BF16 = jnp.bfloat16

D_MODEL = 2048
PAST_LEN = 16384
RET_HEADS = 6
RET_DIM = 128
RET_WIDTH = RET_HEADS * RET_DIM
ML_HEADS = 4
ML_QK = 96
ML_V = 192
ML_WIDTH = ML_HEADS * ML_V
POOL_WIDTH = 512
POOL_WINDOWS = (2, 4, 8, 16)
POOL_GROUP = POOL_WIDTH // len(POOL_WINDOWS)
POOL_BUF = max(POOL_WINDOWS) - 1
D_FF = 5632
CHUNK = 128
ROPE_THETA = 10000.0
EPS = 1e-6

LANES = 128
SUBLANES = 8
VMEM_LIMIT = 56 * 1024 * 1024

ML_QK_PAD = LANES
ML_V_PAD = 2 * LANES
OFF_RQ = 0
OFF_RK = OFF_RQ + RET_WIDTH
OFF_RV = OFF_RK + RET_WIDTH
OFF_RG = OFF_RV + RET_WIDTH
OFF_MQ = OFF_RG + RET_WIDTH
OFF_MK = OFF_MQ + ML_HEADS * ML_QK_PAD
OFF_MV = OFF_MK + ML_HEADS * ML_QK_PAD
OFF_MO = OFF_MV + ML_HEADS * ML_V_PAD
OFF_PU = OFF_MO + ML_HEADS * ML_V_PAD
PROJ_PAD = OFF_PU + POOL_WIDTH
GATE_I_LANE = ML_QK
GATE_F_LANE = ML_QK + SUBLANES
MIX_OFF_ML = RET_WIDTH
MIX_OFF_POOL = MIX_OFF_ML + ML_HEADS * ML_V_PAD
MIX_PAD = MIX_OFF_POOL + POOL_WIDTH

_HI = lax.Precision.HIGHEST


def _dot(a, b):
    return jnp.dot(a, b, preferred_element_type=F32)


def _rms(x, width=None):
    n = x.shape[-1] if width is None else width
    return x * lax.rsqrt(jnp.sum(x * x, axis=-1, keepdims=True) / n + EPS)


def _log_sigmoid(x):
    return jnp.minimum(x, 0.0) - jnp.log1p(jnp.exp(-jnp.abs(x)))


def _rotate(x, cos2, sin2):
    return x * cos2 + pltpu.roll(x, RET_DIM // 2, 1) * sin2


def _inproj_kernel(x_ref, g_ref, w_ref, o_ref, xn_ref):
    @pl.when(pl.program_id(1) == 0)
    def _():
        xn_ref[...] = (_rms(x_ref[...]) * g_ref[...]).astype(BF16)

    o_ref[...] = _dot(xn_ref[...], w_ref[...])


def _inproj(x, g, w, tm, tn):
    m = x.shape[0]
    return pl.pallas_call(
        _inproj_kernel,
        grid=(m // tm, PROJ_PAD // tn),
        in_specs=[
            pl.BlockSpec((tm, D_MODEL), lambda i, j: (i, 0)),
            pl.BlockSpec((1, D_MODEL), lambda i, j: (0, 0)),
            pl.BlockSpec((D_MODEL, tn), lambda i, j: (0, j)),
        ],
        out_specs=pl.BlockSpec((tm, tn), lambda i, j: (i, j)),
        out_shape=jax.ShapeDtypeStruct((m, PROJ_PAD), F32),
        scratch_shapes=[pltpu.VMEM((tm, D_MODEL), BF16)],
        compiler_params=pltpu.CompilerParams(
            dimension_semantics=("parallel", "arbitrary"), vmem_limit_bytes=VMEM_LIMIT),
        name="inproj",
    )(x, g, w)


def _outproj_kernel(m_ref, w_ref, x_ref, gpost_ref, gpre_ref, x1_ref, hn_ref, acc_ref):
    k = pl.program_id(1)

    @pl.when(k == 0)
    def _():
        acc_ref[...] = jnp.zeros_like(acc_ref)

    acc_ref[...] += _dot(m_ref[...], w_ref[...])

    @pl.when(k == pl.num_programs(1) - 1)
    def _():
        x1 = x_ref[...] + _rms(acc_ref[...]) * gpost_ref[...]
        x1_ref[...] = x1
        hn_ref[...] = (_rms(x1) * gpre_ref[...]).astype(BF16)


def _outproj(mixed, w, x, g_post, g_pre, tm, tk):
    m = x.shape[0]
    return pl.pallas_call(
        _outproj_kernel,
        grid=(m // tm, MIX_PAD // tk),
        in_specs=[
            pl.BlockSpec((tm, tk), lambda i, k: (i, k)),
            pl.BlockSpec((tk, D_MODEL), lambda i, k: (k, 0)),
            pl.BlockSpec((tm, D_MODEL), lambda i, k: (i, 0)),
            pl.BlockSpec((1, D_MODEL), lambda i, k: (0, 0)),
            pl.BlockSpec((1, D_MODEL), lambda i, k: (0, 0)),
        ],
        out_specs=[
            pl.BlockSpec((tm, D_MODEL), lambda i, k: (i, 0)),
            pl.BlockSpec((tm, D_MODEL), lambda i, k: (i, 0)),
        ],
        out_shape=[
            jax.ShapeDtypeStruct((m, D_MODEL), F32),
            jax.ShapeDtypeStruct((m, D_MODEL), BF16),
        ],
        scratch_shapes=[pltpu.VMEM((tm, D_MODEL), F32)],
        compiler_params=pltpu.CompilerParams(
            dimension_semantics=("parallel", "arbitrary"), vmem_limit_bytes=VMEM_LIMIT),
        name="outproj",
    )(mixed, w, x, g_post, g_pre)


def _ffn_kernel(hn_ref, wg_ref, wu_ref, wd_ref, x1_ref, g_ref, o_ref):
    f = pl.program_id(1)
    hn = hn_ref[...]
    gate = _dot(hn, wg_ref[...])
    up = _dot(hn, wu_ref[...])
    act = (gate * jax.nn.sigmoid(gate) * up).astype(BF16)
    part = _dot(act, wd_ref[...])

    @pl.when(f == 0)
    def _():
        o_ref[...] = part

    @pl.when(f > 0)
    def _():
        o_ref[...] += part

    @pl.when(f == pl.num_programs(1) - 1)
    def _():
        o_ref[...] = x1_ref[...] + _rms(o_ref[...]) * g_ref[...]


def _ffn(hn, wg, wu, wd, x1, g, tm, tf):
    m = hn.shape[0]
    return pl.pallas_call(
        _ffn_kernel,
        grid=(m // tm, D_FF // tf),
        in_specs=[
            pl.BlockSpec((tm, D_MODEL), lambda i, f: (i, 0)),
            pl.BlockSpec((D_MODEL, tf), lambda i, f: (0, f)),
            pl.BlockSpec((D_MODEL, tf), lambda i, f: (0, f)),
            pl.BlockSpec((tf, D_MODEL), lambda i, f: (f, 0)),
            pl.BlockSpec((tm, D_MODEL), lambda i, f: (i, 0)),
            pl.BlockSpec((1, D_MODEL), lambda i, f: (0, 0)),
        ],
        out_specs=pl.BlockSpec((tm, D_MODEL), lambda i, f: (i, 0)),
        out_shape=jax.ShapeDtypeStruct((m, D_MODEL), F32),
        compiler_params=pltpu.CompilerParams(
            dimension_semantics=("parallel", "arbitrary"), vmem_limit_bytes=VMEM_LIMIT),
        name="ffn",
    )(hn, wg, wu, wd, x1, g)


def _prompt_mixer_kernel(p_ref, cos_ref, sin_ref, dintra_ref, dq_ref, dk_ref, dch_ref,
                         gret_ref, gml_ref, bi_r_ref, bf_r_ref, gb_c_ref, wpool_ref, pscale_ref,
                         mix_ref, sret_ref, sc_ref, sn_ref, sm_ref, spool_ref,
                         s_scr, c_scr, n_scr, m_scr, u_scr):
    b = pl.program_id(0)
    c = pl.program_id(1)

    @pl.when(c == 0)
    def _():
        s_scr[...] = jnp.zeros_like(s_scr)
        c_scr[...] = jnp.zeros_like(c_scr)
        n_scr[...] = jnp.zeros_like(n_scr)
        m_scr[...] = jnp.zeros_like(m_scr)
        u_scr[0:2 * SUBLANES, :] = jnp.zeros((2 * SUBLANES, POOL_WIDTH), F32)

    cos2 = cos_ref[...]
    sin2 = sin_ref[...]
    for h in range(RET_HEADS):
        lo, hi = h * RET_DIM, (h + 1) * RET_DIM
        q = p_ref[:, OFF_RQ + lo:OFF_RQ + hi]
        k = p_ref[:, OFF_RK + lo:OFF_RK + hi]
        vb = p_ref[:, OFF_RV + lo:OFF_RV + hi].astype(BF16)
        g = p_ref[:, OFF_RG + lo:OFF_RG + hi]
        qb = _rotate(q, cos2, sin2).astype(BF16)
        kr_t = (_rotate(k, cos2, sin2) * (RET_DIM ** -0.5)).T
        scores = _dot(qb, kr_t.astype(BF16)) * dintra_ref[h]
        s_old = s_scr[h]
        o = _dot(scores.astype(BF16), vb) + _dot(qb, s_old.astype(BF16)) * dq_ref[:, h:h + 1]
        s_scr[h] = s_old * dch_ref[h:h + 1, :] + _dot((kr_t * dk_ref[h:h + 1, :]).astype(BF16), vb)
        out = _rms(o) * gret_ref[:, lo:hi] * (g * jax.nn.sigmoid(g))
        mix_ref[:, lo:hi] = out.astype(BF16)

    row = lax.broadcasted_iota(jnp.int32, (CHUNK, CHUNK), 0)
    col = lax.broadcasted_iota(jnp.int32, (CHUNK, CHUNK), 1)
    causal = col <= row
    gq0 = p_ref[:, OFF_MQ:OFF_MQ + ML_QK_PAD]
    g_t = gq0.T
    i_r8 = g_t[GATE_I_LANE:GATE_I_LANE + SUBLANES, :] + bi_r_ref[...]
    f_r8 = _log_sigmoid(g_t[GATE_F_LANE:GATE_F_LANE + SUBLANES, :] + bf_r_ref[...])
    b_r8 = jnp.dot(f_r8, (row <= col).astype(F32), precision=_HI, preferred_element_type=F32)
    pre_c = gq0 + gb_c_ref[0:1, :]
    b_c_all = jnp.dot(causal.astype(F32), _log_sigmoid(pre_c), precision=_HI,
                      preferred_element_type=F32)
    for h in range(ML_HEADS):
        q = p_ref[:, OFF_MQ + h * ML_QK_PAD:OFF_MQ + (h + 1) * ML_QK_PAD]
        k = p_ref[:, OFF_MK + h * ML_QK_PAD:OFF_MK + (h + 1) * ML_QK_PAD] * (ML_QK ** -0.5)
        vb = p_ref[:, OFF_MV + h * ML_V_PAD:OFF_MV + (h + 1) * ML_V_PAD].astype(BF16)
        og = p_ref[:, OFF_MO + h * ML_V_PAD:OFF_MO + (h + 1) * ML_V_PAD]
        k_t = k.T
        qb = q.astype(BF16)
        b_c = b_c_all[:, GATE_F_LANE + h:GATE_F_LANE + h + 1]
        i_c = pre_c[:, GATE_I_LANE + h:GATE_I_LANE + h + 1]
        b_r = b_r8[h:h + 1, :]
        i_r = i_r8[h:h + 1, :]
        logw = jnp.where(causal, b_c - b_r + i_r, -jnp.inf)
        m_prev = m_scr[h:h + 1, 0:1]
        inter = b_c + m_prev
        m_t = jnp.maximum(inter, jnp.max(logw, axis=-1, keepdims=True))
        w = jnp.exp(logw - m_t)
        a = jnp.exp(inter - m_t)
        s = _dot(qb, k_t.astype(BF16)) * w
        c_old = c_scr[h]
        n_old = n_scr[h:h + 1, :]
        num = _dot(s.astype(BF16), vb) + a * _dot(qb, c_old.astype(BF16))
        den = (jnp.sum(s, axis=-1, keepdims=True)
               + a * jnp.sum(q * n_old, axis=-1, keepdims=True))
        hh = num / jnp.maximum(jnp.abs(den), jnp.exp(-m_t))
        m_new = m_t[CHUNK - 1:CHUNK, :]
        b_last = b_c[CHUNK - 1:CHUNK, :]
        w_r = jnp.exp(b_last - b_r + i_r - m_new)
        w_c = jnp.exp(b_last - b_c + i_c - m_new)
        a_end = jnp.exp(b_last + m_prev - m_new)
        c_scr[h] = a_end * c_old + _dot((k_t * w_r).astype(BF16), vb)
        n_scr[h:h + 1, :] = a_end * n_old + jnp.sum(k * w_c, axis=0, keepdims=True)
        m_scr[h:h + 1, :] = jnp.broadcast_to(m_new, (1, LANES))
        out = (_rms(hh, ML_V) * gml_ref[:, h * ML_V_PAD:(h + 1) * ML_V_PAD]
               * jax.nn.sigmoid(og))
        mix_ref[:, MIX_OFF_ML + h * ML_V_PAD:MIX_OFF_ML + (h + 1) * ML_V_PAD] = out.astype(BF16)

    base = 2 * SUBLANES
    u = p_ref[:, OFF_PU:OFF_PU + POOL_WIDTH]
    u_scr[base:base + CHUNK, :] = u
    pos = (c * CHUNK + lax.broadcasted_iota(jnp.int32, (CHUNK, 1), 0)).astype(F32)
    for g, win in enumerate(POOL_WINDOWS):
        lo, hi = g * POOL_GROUP, (g + 1) * POOL_GROUP
        wsum = u[:, lo:hi]
        for s in range(1, win):
            wsum = wsum + u_scr[base - s:base - s + CHUNK, lo:hi]
        z = wsum / jnp.minimum(pos + 1.0, float(win)) - u[:, lo:hi]
        y = _dot(z.astype(BF16), wpool_ref[g]) * pscale_ref[:, lo:hi]
        mix_ref[:, MIX_OFF_POOL + lo:MIX_OFF_POOL + hi] = y.astype(BF16)
    u_scr[0:base, :] = u_scr[CHUNK:CHUNK + base, :]

    @pl.when(c == pl.num_programs(1) - 1)
    def _():
        sret_ref[b] = s_scr[...]
        sc_ref[b] = c_scr[...]
        sn_ref[b] = n_scr[...]
        sm_ref[b] = m_scr[...]
        spool_ref[b] = u_scr[0:base, :]


def _prompt_mixer(proj, tables, gains, batch, seq):
    nchunk = seq // CHUNK
    const2 = lambda b, c: (0, 0)
    const3 = lambda b, c: (0, 0, 0)
    const4 = lambda b, c: (0, 0, 0, 0)
    in_specs = [
        pl.BlockSpec((CHUNK, PROJ_PAD), lambda b, c: (b * nchunk + c, 0)),
        pl.BlockSpec((CHUNK, RET_DIM), lambda b, c: (c, 0)),
        pl.BlockSpec((CHUNK, RET_DIM), lambda b, c: (c, 0)),
        pl.BlockSpec((RET_HEADS, CHUNK, CHUNK), const3),
        pl.BlockSpec((CHUNK, LANES), const2),
        pl.BlockSpec((SUBLANES, CHUNK), const2),
        pl.BlockSpec((SUBLANES, LANES), const2),
        pl.BlockSpec((1, RET_WIDTH), const2),
        pl.BlockSpec((1, ML_HEADS * ML_V_PAD), const2),
        pl.BlockSpec((SUBLANES, LANES), const2),
        pl.BlockSpec((SUBLANES, LANES), const2),
        pl.BlockSpec((SUBLANES, LANES), const2),
        pl.BlockSpec((len(POOL_WINDOWS), POOL_GROUP, POOL_GROUP), const3),
        pl.BlockSpec((1, POOL_WIDTH), const2),
    ]
    out_shape = [
        jax.ShapeDtypeStruct((batch * seq, MIX_PAD), BF16),
        jax.ShapeDtypeStruct((batch, RET_HEADS, RET_DIM, RET_DIM), F32),
        jax.ShapeDtypeStruct((batch, ML_HEADS, ML_QK_PAD, ML_V_PAD), F32),
        jax.ShapeDtypeStruct((batch, SUBLANES, LANES), F32),
        jax.ShapeDtypeStruct((batch, SUBLANES, LANES), F32),
        jax.ShapeDtypeStruct((batch, 2 * SUBLANES, POOL_WIDTH), F32),
    ]
    out_specs = [
        pl.BlockSpec((CHUNK, MIX_PAD), lambda b, c: (b * nchunk + c, 0)),
        pl.BlockSpec((batch, RET_HEADS, RET_DIM, RET_DIM), const4),
        pl.BlockSpec((batch, ML_HEADS, ML_QK_PAD, ML_V_PAD), const4),
        pl.BlockSpec((batch, SUBLANES, LANES), const3),
        pl.BlockSpec((batch, SUBLANES, LANES), const3),
        pl.BlockSpec((batch, 2 * SUBLANES, POOL_WIDTH), const3),
    ]
    scratch = [
        pltpu.VMEM((RET_HEADS, RET_DIM, RET_DIM), F32),
        pltpu.VMEM((ML_HEADS, ML_QK_PAD, ML_V_PAD), F32),
        pltpu.VMEM((SUBLANES, LANES), F32),
        pltpu.VMEM((SUBLANES, LANES), F32),
        pltpu.VMEM((2 * SUBLANES + CHUNK, POOL_WIDTH), F32),
    ]
    return pl.pallas_call(
        _prompt_mixer_kernel,
        grid=(batch, nchunk),
        in_specs=in_specs,
        out_specs=out_specs,
        out_shape=out_shape,
        scratch_shapes=scratch,
        compiler_params=pltpu.CompilerParams(
            dimension_semantics=("arbitrary", "arbitrary"), vmem_limit_bytes=VMEM_LIMIT),
        name="prompt_mixer",
    )(proj, tables["cos"], tables["sin"], tables["dintra"], tables["dq"], tables["dk"],
      tables["dch"], gains["g_ret"], gains["g_ml"], gains["bi_r"], gains["bf_r"], gains["gb_c"],
      gains["w_pool"], gains["pool_scale"])


DEC_BLOCK = 8


def _decode_mixer_kernel(p_ref, cos_ref, sin_ref, dch_ref, gret_ref, gml_ref, gb_c_ref,
                         wpool_ref, pscale_ref, s_ref, c_ref, n_ref, m_ref, pool_ref,
                         mix_ref, so_ref, co_ref, no_ref, mo_ref, poolo_ref,
                         qr_scr, kt_scr, v_scr, qs_scr, qm_scr, wkt_scr, vm_scr, qc_scr, a_scr,
                         cpad_scr):
    step = pl.program_id(0)
    nb = p_ref.shape[0]
    cos2 = cos_ref[0:1, :]
    sin2 = sin_ref[0:1, :]

    def ret_qkv(h):
        lo, hi = h * RET_DIM, (h + 1) * RET_DIM
        qr = _rotate(p_ref[:, OFF_RQ + lo:OFF_RQ + hi], cos2, sin2)
        kr = _rotate(p_ref[:, OFF_RK + lo:OFF_RK + hi], cos2, sin2) * (RET_DIM ** -0.5)
        return qr, kr, p_ref[:, OFF_RV + lo:OFF_RV + hi]

    def ml_gates(h):
        pre_c = p_ref[:, OFF_MQ:OFF_MQ + ML_QK_PAD] + gb_c_ref[0:1, :]
        i_pre = pre_c[:, GATE_I_LANE + h:GATE_I_LANE + h + 1]
        lf = _log_sigmoid(pre_c[:, GATE_F_LANE + h:GATE_F_LANE + h + 1])
        inter = lf + m_ref[:, h:h + 1]
        m_t = jnp.maximum(inter, i_pre)
        return jnp.exp(i_pre - m_t), jnp.exp(inter - m_t), m_t

    def ml_qkv(h):
        q = p_ref[:, OFF_MQ + h * ML_QK_PAD:OFF_MQ + (h + 1) * ML_QK_PAD]
        k = p_ref[:, OFF_MK + h * ML_QK_PAD:OFF_MK + (h + 1) * ML_QK_PAD] * (ML_QK ** -0.5)
        return q, k, p_ref[:, OFF_MV + h * ML_V_PAD:OFF_MV + (h + 1) * ML_V_PAD]

    @pl.when(step == 0)
    def _():
        for h in range(RET_HEADS):
            qr, kr, v = ret_qkv(h)
            qr_scr[h] = qr
            kt_scr[h] = kr.T
            v_scr[h] = v.astype(BF16)
        for h in range(ML_HEADS):
            w, a, _ = ml_gates(h)
            q, k, v = ml_qkv(h)
            qm_scr[h] = q
            wkt_scr[h] = (w * k).T
            vm_scr[h] = v.astype(BF16)
            a_scr[h] = jnp.broadcast_to(a, (nb, ML_V_PAD))
        cpad_scr[...] = jnp.zeros_like(cpad_scr)

    lane = lax.broadcasted_iota(jnp.int32, (LANES, nb), 1)

    def body(bb, carry):
        b = step * DEC_BLOCK + bb
        sel = lane == b
        for h in range(RET_HEADS):
            s_old = s_ref[bb, h]
            q16 = jnp.broadcast_to(qr_scr[h, pl.ds(b, 1), :], (2 * SUBLANES, RET_DIM)).astype(BF16)
            qs_scr[h, pl.ds(b, 1), :] = _dot(q16, s_old.astype(BF16))[0:1, :]
            k_m = jnp.where(sel, kt_scr[h], 0.0).astype(BF16)
            so_ref[bb, h] = s_old * dch_ref[h:h + 1, :] + _dot(k_m, v_scr[h])
        for h in range(ML_HEADS):
            cpad_scr[0:ML_QK, 0:ML_V] = c_ref[bb, h]
            c_old = cpad_scr[...]
            q16 = jnp.broadcast_to(qm_scr[h, pl.ds(b, 1), :], (2 * SUBLANES, ML_QK_PAD)).astype(BF16)
            qc_scr[h, pl.ds(b, 1), :] = _dot(q16, c_old.astype(BF16))[0:1, :]
            wk_m = jnp.where(sel, wkt_scr[h], 0.0).astype(BF16)
            c_new = a_scr[h, pl.ds(b, 1), :] * c_old + _dot(wk_m, vm_scr[h])
            co_ref[bb, h] = c_new[0:ML_QK, 0:ML_V]
        return carry

    lax.fori_loop(0, DEC_BLOCK, body, 0)

    @pl.when(step == pl.num_programs(0) - 1)
    def _():
        for h in range(RET_HEADS):
            lo, hi = h * RET_DIM, (h + 1) * RET_DIM
            qr, kr, v = ret_qkv(h)
            g = p_ref[:, OFF_RG + lo:OFF_RG + hi]
            o = jnp.sum(qr * kr, axis=-1, keepdims=True) * v + qs_scr[h] * dch_ref[h:h + 1, :]
            out = _rms(o) * gret_ref[:, lo:hi] * (g * jax.nn.sigmoid(g))
            mix_ref[:, lo:hi] = out.astype(BF16)
        for h in range(ML_HEADS):
            w, a, m_t = ml_gates(h)
            q, k, v = ml_qkv(h)
            og = p_ref[:, OFF_MO + h * ML_V_PAD:OFF_MO + (h + 1) * ML_V_PAD]
            n_old = n_ref[:, h * ML_QK_PAD:(h + 1) * ML_QK_PAD]
            s = jnp.sum(q * k, axis=-1, keepdims=True) * w
            num = s * v + a * qc_scr[h]
            den = s + a * jnp.sum(q * n_old, axis=-1, keepdims=True)
            hh = num / jnp.maximum(jnp.abs(den), jnp.exp(-m_t))
            out = (_rms(hh, ML_V) * gml_ref[:, h * ML_V_PAD:(h + 1) * ML_V_PAD]
                   * jax.nn.sigmoid(og))
            mix_ref[:, MIX_OFF_ML + h * ML_V_PAD:MIX_OFF_ML + (h + 1) * ML_V_PAD] = out.astype(BF16)
            no_ref[:, h * ML_QK_PAD:(h + 1) * ML_QK_PAD] = a * n_old + w * k
            mo_ref[:, h:h + 1] = m_t
        u = p_ref[:, OFF_PU:OFF_PU + POOL_WIDTH]
        for g, win in enumerate(POOL_WINDOWS):
            lo, hi = g * POOL_GROUP, (g + 1) * POOL_GROUP
            wsum = u[:, lo:hi]
            for s in range(1, win):
                r = (POOL_BUF - s) * POOL_WIDTH
                wsum = wsum + pool_ref[:, r + lo:r + hi]
            z = wsum / float(win) - u[:, lo:hi]
            y = _dot(z.astype(BF16), wpool_ref[g]) * pscale_ref[:, lo:hi]
            mix_ref[:, MIX_OFF_POOL + lo:MIX_OFF_POOL + hi] = y.astype(BF16)
        keep = (POOL_BUF - 1) * POOL_WIDTH
        poolo_ref[:, 0:keep] = pool_ref[:, POOL_WIDTH:POOL_WIDTH + keep]
        poolo_ref[:, keep:keep + POOL_WIDTH] = u


def _decode_mixer(proj, tables, gains, s_ret, s_c, s_n, s_m, s_pool):
    nb = proj.shape[0]
    const2 = lambda s: (0, 0)
    const3 = lambda s: (0, 0, 0)
    blk4 = lambda s: (s, 0, 0, 0)
    full2 = lambda shape: pl.BlockSpec(shape, const2)
    in_specs = [
        full2((nb, PROJ_PAD)),
        full2((SUBLANES, RET_DIM)),
        full2((SUBLANES, RET_DIM)),
        full2((SUBLANES, LANES)),
        full2((1, RET_WIDTH)),
        full2((1, ML_HEADS * ML_V_PAD)),
        full2((SUBLANES, LANES)),
        pl.BlockSpec((len(POOL_WINDOWS), POOL_GROUP, POOL_GROUP), const3),
        full2((1, POOL_WIDTH)),
        pl.BlockSpec((DEC_BLOCK, RET_HEADS, RET_DIM, RET_DIM), blk4),
        pl.BlockSpec((DEC_BLOCK, ML_HEADS, ML_QK, ML_V), blk4),
        full2((nb, ML_HEADS * ML_QK_PAD)),
        full2((nb, ML_HEADS)),
        full2((nb, POOL_BUF * POOL_WIDTH)),
    ]
    out_shape = [
        jax.ShapeDtypeStruct((nb, MIX_PAD), BF16),
        jax.ShapeDtypeStruct(s_ret.shape, F32),
        jax.ShapeDtypeStruct(s_c.shape, F32),
        jax.ShapeDtypeStruct((nb, ML_HEADS * ML_QK_PAD), F32),
        jax.ShapeDtypeStruct((nb, ML_HEADS), F32),
        jax.ShapeDtypeStruct((nb, POOL_BUF * POOL_WIDTH), F32),
    ]
    out_specs = [
        full2((nb, MIX_PAD)),
        pl.BlockSpec((DEC_BLOCK, RET_HEADS, RET_DIM, RET_DIM), blk4),
        pl.BlockSpec((DEC_BLOCK, ML_HEADS, ML_QK, ML_V), blk4),
        full2((nb, ML_HEADS * ML_QK_PAD)),
        full2((nb, ML_HEADS)),
        full2((nb, POOL_BUF * POOL_WIDTH)),
    ]
    scratch = [
        pltpu.VMEM((RET_HEADS, nb, RET_DIM), F32),
        pltpu.VMEM((RET_HEADS, RET_DIM, nb), F32),
        pltpu.VMEM((RET_HEADS, nb, RET_DIM), BF16),
        pltpu.VMEM((RET_HEADS, nb, RET_DIM), F32),
        pltpu.VMEM((ML_HEADS, nb, ML_QK_PAD), F32),
        pltpu.VMEM((ML_HEADS, ML_QK_PAD, nb), F32),
        pltpu.VMEM((ML_HEADS, nb, ML_V_PAD), BF16),
        pltpu.VMEM((ML_HEADS, nb, ML_V_PAD), F32),
        pltpu.VMEM((ML_HEADS, nb, ML_V_PAD), F32),
        pltpu.VMEM((ML_QK_PAD, ML_V_PAD), F32),
    ]
    return pl.pallas_call(
        _decode_mixer_kernel,
        grid=(nb // DEC_BLOCK,),
        in_specs=in_specs,
        out_specs=out_specs,
        out_shape=out_shape,
        scratch_shapes=scratch,
        compiler_params=pltpu.CompilerParams(
            dimension_semantics=("arbitrary",), vmem_limit_bytes=VMEM_LIMIT),
        name="decode_mixer",
    )(proj, tables["cos_s"], tables["sin_s"], tables["dch"], gains["g_ret"], gains["g_ml"],
      gains["gb_c"], gains["w_pool"], gains["pool_scale"], s_ret, s_c, s_n, s_m, s_pool)


def _tables(seq):
    half = RET_DIM // 2
    inv_freq = ROPE_THETA ** (-jnp.arange(half, dtype=F32) / half)

    def cos_sin(pos):
        ang = pos[:, None] * inv_freq[None, :]
        cos, sin = jnp.cos(ang), jnp.sin(ang)
        return jnp.concatenate([cos, cos], axis=-1), jnp.concatenate([-sin, sin], axis=-1)

    cos_p, sin_p = cos_sin(jnp.arange(seq, dtype=F32))
    cos_s, sin_s = cos_sin((PAST_LEN + jnp.arange(1)).astype(F32))
    log_gamma = jnp.log1p(-jnp.exp2(-5.0 - jnp.arange(RET_HEADS, dtype=F32)))
    idx = jnp.arange(CHUNK, dtype=F32)
    diff = idx[:, None] - idx[None, :]
    causal = diff >= 0
    lg = log_gamma[:, None, None]
    dintra = jnp.where(causal, jnp.exp(lg * jnp.where(causal, diff, 0.0)), 0.0)
    dq = jnp.exp(log_gamma[:, None] * (idx + 1.0))
    dk = jnp.exp(log_gamma[:, None] * (CHUNK - 1.0 - idx))

    def pad_rows(t, rows):
        return jnp.pad(t, ((0, rows - t.shape[0]), (0, 0)))

    def chunk_decay(c):
        return pad_rows(jnp.broadcast_to(jnp.exp(log_gamma * c)[:, None], (RET_HEADS, LANES)), SUBLANES)

    return {
        "cos": cos_p, "sin": sin_p,
        "cos_s": jnp.broadcast_to(cos_s, (SUBLANES, RET_DIM)),
        "sin_s": jnp.broadcast_to(sin_s, (SUBLANES, RET_DIM)),
        "dintra": dintra,
        "dq": jnp.pad(dq.T, ((0, 0), (0, LANES - RET_HEADS))),
        "dk": pad_rows(dk, SUBLANES),
        "dch": chunk_decay(float(CHUNK)),
        "dch_s": chunk_decay(1.0),
    }


def _pad_last(t, width):
    return jnp.pad(t, [(0, 0)] * (t.ndim - 1) + [(0, width - t.shape[-1])])


def _prep_w_in(w):
    d = w.shape[0]
    o = RET_WIDTH * 4
    mq = w[:, o:o + ML_HEADS * ML_QK].reshape(d, ML_HEADS, ML_QK)
    o += ML_HEADS * ML_QK
    mk = w[:, o:o + ML_HEADS * ML_QK].reshape(d, ML_HEADS, ML_QK)
    o += ML_HEADS * ML_QK
    mv = w[:, o:o + ML_WIDTH].reshape(d, ML_HEADS, ML_V)
    o += ML_WIDTH
    mo = w[:, o:o + ML_WIDTH].reshape(d, ML_HEADS, ML_V)
    o += ML_WIDTH
    mi = w[:, o:o + ML_HEADS]
    mf = w[:, o + ML_HEADS:o + 2 * ML_HEADS]
    pu = w[:, o + 2 * ML_HEADS:]
    gates = jnp.concatenate([_pad_last(mi, SUBLANES), _pad_last(mf, LANES - ML_QK - SUBLANES)], axis=-1)
    mq_pad = jnp.concatenate(
        [jnp.concatenate([mq[:, 0], gates], axis=-1)[:, None], _pad_last(mq[:, 1:], ML_QK_PAD)], axis=1)
    parts = [w[:, :RET_WIDTH * 4], mq_pad.reshape(d, -1), _pad_last(mk, ML_QK_PAD).reshape(d, -1),
             _pad_last(mv, ML_V_PAD).reshape(d, -1), _pad_last(mo, ML_V_PAD).reshape(d, -1), pu]
    return jnp.concatenate(parts, axis=-1).astype(BF16)


def _prep_w_out(w):
    d = w.shape[1]
    ml = w[RET_WIDTH:RET_WIDTH + ML_WIDTH].reshape(ML_HEADS, ML_V, d)
    ml = jnp.pad(ml, ((0, 0), (0, ML_V_PAD - ML_V), (0, 0))).reshape(ML_HEADS * ML_V_PAD, d)
    return jnp.concatenate([w[:RET_WIDTH], ml, w[RET_WIDTH + ML_WIDTH:]], axis=0).astype(BF16)


def _prep_gains(l, g_ret_norm, g_mlstm_norm, b_mlstm_i, b_mlstm_f, w_pool, pool_scale):
    bi, bf = b_mlstm_i[l], b_mlstm_f[l]
    rows = lambda t: jnp.broadcast_to(_pad_last(t, SUBLANES)[:, None], (SUBLANES, LANES))
    lanes = jnp.concatenate([jnp.zeros((GATE_I_LANE,), F32), _pad_last(bi, SUBLANES),
                             _pad_last(bf, LANES - GATE_F_LANE)])
    return {
        "g_ret": g_ret_norm[l][None, :],
        "g_ml": _pad_last(g_mlstm_norm[l].reshape(ML_HEADS, ML_V), ML_V_PAD).reshape(1, -1),
        "bi_r": rows(bi), "bf_r": rows(bf),
        "gb_c": jnp.broadcast_to(lanes[None, :], (SUBLANES, LANES)),
        "w_pool": w_pool[l].astype(BF16),
        "pool_scale": pool_scale[l][None, :],
    }


@jax.jit
def kernel(x_prompt, x_sample, state_ret, state_mlstm_c, state_mlstm_n, state_mlstm_m, state_pool,
           w_in, w_out, g_ret_norm, g_mlstm_norm, b_mlstm_i, b_mlstm_f, w_pool, pool_scale,
           g_pre_mix, g_post_mix, g_pre_ffn, g_post_ffn, w_ffn_gate, w_ffn_up, w_ffn_down):
    batch, seq, d = x_prompt.shape
    nb = x_sample.shape[0]
    depth = w_in.shape[0]
    tables = _tables(seq)
    tables_s = dict(tables, dch=tables["dch_s"])

    yp = x_prompt.reshape(batch * seq, d)
    ys = x_sample.reshape(nb, d)
    new_p, new_s = [], []
    for l in range(depth):
        w_in_l = _prep_w_in(w_in[l])
        w_out_l = _prep_w_out(w_out[l])
        wg, wu, wd = (w_ffn_gate[l].astype(BF16), w_ffn_up[l].astype(BF16),
                      w_ffn_down[l].astype(BF16))
        gains = _prep_gains(l, g_ret_norm, g_mlstm_norm, b_mlstm_i, b_mlstm_f, w_pool, pool_scale)
        row = lambda g: g[l][None, :]

        proj = _inproj(yp, row(g_pre_mix), w_in_l, tm=512, tn=512)
        mixed, s_ret, s_c, s_n, s_m, s_pool = _prompt_mixer(proj, tables, gains, batch, seq)
        x1, hn = _outproj(mixed, w_out_l, yp, row(g_post_mix), row(g_pre_ffn), tm=512, tk=768)
        yp = _ffn(hn, wg, wu, wd, x1, row(g_post_ffn), tm=512, tf=512)
        new_p.append((s_ret, s_c[:, :, :ML_QK, :ML_V], s_n[:, :ML_HEADS, :ML_QK],
                      s_m[:, :ML_HEADS, 0], s_pool[:, 1:, :]))

        proj = _inproj(ys, row(g_pre_mix), w_in_l, tm=nb, tn=512)
        n_in = _pad_last(state_mlstm_n[l], ML_QK_PAD).reshape(nb, ML_HEADS * ML_QK_PAD)
        pool_in = state_pool[l].reshape(nb, POOL_BUF * POOL_WIDTH)
        mixed, s_ret, s_c, s_n, s_m, s_pool = _decode_mixer(
            proj, tables_s, gains, state_ret[l], state_mlstm_c[l], n_in, state_mlstm_m[l], pool_in)
        x1, hn = _outproj(mixed, w_out_l, ys, row(g_post_mix), row(g_pre_ffn), tm=nb, tk=768)
        ys = _ffn(hn, wg, wu, wd, x1, row(g_post_ffn), tm=nb, tf=512)
        new_s.append((s_ret, s_c, s_n.reshape(nb, ML_HEADS, ML_QK_PAD)[:, :, :ML_QK], s_m,
                      s_pool.reshape(nb, POOL_BUF, POOL_WIDTH)))

    stack = lambda states, i: jnp.stack([s[i] for s in states], axis=0)
    return ((yp.reshape(batch, seq, d), ys.reshape(nb, 1, d))
            + tuple(stack(new_p, i) for i in range(5))
            + tuple(stack(new_s, i) for i in range(5)))
```

```python
import functools

import jax
import jax.numpy as jnp
from jax import lax
from jax.experimental import pallas as pl
from jax.experimental.pallas import tpu as pltpu

F32 = jnp.float32
BF16 = jnp.bfloat16

D_MODEL = 2048
PAST_LEN = 16384
RET_HEADS = 6
RET_DIM = 128
RET_WIDTH = RET_HEADS * RET_DIM
ML_HEADS = 4
ML_QK = 96
ML_V = 192
ML_WIDTH = ML_HEADS * ML_V
POOL_WIDTH = 512
POOL_WINDOWS = (2, 4, 8, 16)
POOL_GROUP = POOL_WIDTH // len(POOL_WINDOWS)
POOL_BUF = max(POOL_WINDOWS) - 1
D_FF = 5632
CHUNK = 128
ROPE_THETA = 10000.0
EPS = 1e-6

LANES = 128
SUBLANES = 8
VMEM_LIMIT = 56 * 1024 * 1024

ML_QK_PAD = LANES
ML_V_PAD = 2 * LANES
OFF_RQ = 0
OFF_RK = OFF_RQ + RET_WIDTH
OFF_RV = OFF_RK + RET_WIDTH
OFF_RG = OFF_RV + RET_WIDTH
OFF_MQ = OFF_RG + RET_WIDTH
OFF_MK = OFF_MQ + ML_HEADS * ML_QK_PAD
OFF_MV = OFF_MK + ML_HEADS * ML_QK_PAD
OFF_MO = OFF_MV + ML_HEADS * ML_V_PAD
OFF_PU = OFF_MO + ML_HEADS * ML_V_PAD
PROJ_PAD = OFF_PU + POOL_WIDTH
GATE_I_LANE = ML_QK
GATE_F_LANE = ML_QK + SUBLANES
MIX_OFF_ML = RET_WIDTH
MIX_OFF_POOL = MIX_OFF_ML + ML_HEADS * ML_V_PAD
MIX_PAD = MIX_OFF_POOL + POOL_WIDTH

_HI = lax.Precision.HIGHEST


def _token_tiles(m):
    return {
        "inproj_tm": min(m, 1024), "inproj_tn": 512,
        "outproj_tm": min(m, 512), "outproj_tk": 768,
        "ffn_tm": min(m, 1024), "ffn_tf": 256,
    }


def _dot(a, b):
    return jnp.dot(a, b, preferred_element_type=F32)


def _rms(x, width=None):
    n = x.shape[-1] if width is None else width
    return x * lax.rsqrt(jnp.sum(x * x, axis=-1, keepdims=True) / n + EPS)


def _log_sigmoid(x):
    return jnp.minimum(x, 0.0) - jnp.log1p(jnp.exp(-jnp.abs(x)))


def _rotate(x, cos2, sin2):
    return x * cos2 + pltpu.roll(x, RET_DIM // 2, 1) * sin2


def _inproj_kernel(x_ref, g_ref, w_ref, o_ref, xn_ref):
    @pl.when(pl.program_id(1) == 0)
    def _():
        xn_ref[...] = (_rms(x_ref[...]) * g_ref[...]).astype(BF16)

    o_ref[...] = _dot(xn_ref[...], w_ref[...])


def _inproj(x, g, w, tm, tn):
    m = x.shape[0]
    return pl.pallas_call(
        _inproj_kernel,
        grid=(m // tm, PROJ_PAD // tn),
        in_specs=[
            pl.BlockSpec((tm, D_MODEL), lambda i, j: (i, 0)),
            pl.BlockSpec((1, D_MODEL), lambda i, j: (0, 0)),
            pl.BlockSpec((D_MODEL, tn), lambda i, j: (0, j)),
        ],
        out_specs=pl.BlockSpec((tm, tn), lambda i, j: (i, j)),
        out_shape=jax.ShapeDtypeStruct((m, PROJ_PAD), F32),
        scratch_shapes=[pltpu.VMEM((tm, D_MODEL), BF16)],
        compiler_params=pltpu.CompilerParams(
            dimension_semantics=("parallel", "arbitrary"), vmem_limit_bytes=VMEM_LIMIT),
        name="inproj",
    )(x, g, w)


def _outproj_kernel(m_ref, w_ref, x_ref, gpost_ref, x1_ref):
    k = pl.program_id(1)

    @pl.when(k == 0)
    def _():
        x1_ref[...] = jnp.zeros_like(x1_ref)

    x1_ref[...] += _dot(m_ref[...], w_ref[...])

    @pl.when(k == pl.num_programs(1) - 1)
    def _():
        x1_ref[...] = x_ref[...] + _rms(x1_ref[...]) * gpost_ref[...]


def _outproj(mixed, w, x, g_post, tm, tk):
    m = x.shape[0]
    return pl.pallas_call(
        _outproj_kernel,
        grid=(m // tm, MIX_PAD // tk),
        in_specs=[
            pl.BlockSpec((tm, tk), lambda i, k: (i, k)),
            pl.BlockSpec((tk, D_MODEL), lambda i, k: (k, 0)),
            pl.BlockSpec((tm, D_MODEL), lambda i, k: (i, 0)),
            pl.BlockSpec((1, D_MODEL), lambda i, k: (0, 0)),
        ],
        out_specs=pl.BlockSpec((tm, D_MODEL), lambda i, k: (i, 0)),
        out_shape=jax.ShapeDtypeStruct((m, D_MODEL), F32),
        compiler_params=pltpu.CompilerParams(
            dimension_semantics=("parallel", "arbitrary"), vmem_limit_bytes=VMEM_LIMIT),
        name="outproj",
    )(mixed, w, x, g_post)


def _ffn_kernel(x1_ref, gpre_ref, wg_ref, wu_ref, wd_ref, gpost_ref, o_ref, hn_ref):
    f = pl.program_id(1)

    @pl.when(f == 0)
    def _():
        hn_ref[...] = (_rms(x1_ref[...]) * gpre_ref[...]).astype(BF16)
        o_ref[...] = jnp.zeros_like(o_ref)

    hn = hn_ref[...]
    gate = _dot(hn, wg_ref[...].astype(BF16))
    up = _dot(hn, wu_ref[...].astype(BF16))
    act = (gate * jax.nn.sigmoid(gate) * up).astype(BF16)
    o_ref[...] += _dot(act, wd_ref[...].astype(BF16))

    @pl.when(f == pl.num_programs(1) - 1)
    def _():
        o_ref[...] = x1_ref[...] + _rms(o_ref[...]) * gpost_ref[...]


def _ffn(x1, g_pre, wg, wu, wd, g_post, tm, tf):
    m = x1.shape[0]
    return pl.pallas_call(
        _ffn_kernel,
        grid=(m // tm, D_FF // tf),
        in_specs=[
            pl.BlockSpec((tm, D_MODEL), lambda i, f: (i, 0), pipeline_mode=pl.Buffered(1)),
            pl.BlockSpec((1, D_MODEL), lambda i, f: (0, 0)),
            pl.BlockSpec((D_MODEL, tf), lambda i, f: (0, f)),
            pl.BlockSpec((D_MODEL, tf), lambda i, f: (0, f)),
            pl.BlockSpec((tf, D_MODEL), lambda i, f: (f, 0)),
            pl.BlockSpec((1, D_MODEL), lambda i, f: (0, 0)),
        ],
        out_specs=pl.BlockSpec((tm, D_MODEL), lambda i, f: (i, 0)),
        out_shape=jax.ShapeDtypeStruct((m, D_MODEL), F32),
        scratch_shapes=[pltpu.VMEM((tm, D_MODEL), BF16)],
        compiler_params=pltpu.CompilerParams(
            dimension_semantics=("parallel", "arbitrary"), vmem_limit_bytes=VMEM_LIMIT),
        name="ffn",
    )(x1, g_pre, wg, wu, wd, g_post)


SRC_MQ = 4 * RET_WIDTH
SRC_MK = SRC_MQ + ML_HEADS * ML_QK
SRC_MV = SRC_MK + ML_HEADS * ML_QK
SRC_MO = SRC_MV + ML_WIDTH
SRC_MI = SRC_MO + ML_WIDTH
SRC_MF = SRC_MI + ML_HEADS
SRC_PU = SRC_MF + ML_HEADS
IN_WIDTH = SRC_PU + POOL_WIDTH


def _prep_w_in_kernel(w_ref, o_ref):
    def put(dst, src, width):
        o_ref[:, dst:dst + width] = w_ref[:, src:src + width].astype(BF16)

    def zero(dst, width):
        o_ref[:, dst:dst + width] = jnp.zeros((o_ref.shape[0], width), BF16)

    put(0, 0, 4 * RET_WIDTH)
    for h in range(ML_HEADS):
        put(OFF_MQ + h * ML_QK_PAD, SRC_MQ + h * ML_QK, ML_QK)
        put(OFF_MK + h * ML_QK_PAD, SRC_MK + h * ML_QK, ML_QK)
        put(OFF_MV + h * ML_V_PAD, SRC_MV + h * ML_V, ML_V)
        put(OFF_MO + h * ML_V_PAD, SRC_MO + h * ML_V, ML_V)
        zero(OFF_MQ + h * ML_QK_PAD + ML_QK, ML_QK_PAD - ML_QK)
        zero(OFF_MK + h * ML_QK_PAD + ML_QK, ML_QK_PAD - ML_QK)
        zero(OFF_MV + h * ML_V_PAD + ML_V, ML_V_PAD - ML_V)
        zero(OFF_MO + h * ML_V_PAD + ML_V, ML_V_PAD - ML_V)
    put(OFF_MQ + GATE_I_LANE, SRC_MI, ML_HEADS)
    put(OFF_MQ + GATE_F_LANE, SRC_MF, ML_HEADS)
    put(OFF_PU, SRC_PU, POOL_WIDTH)


def _prep_w_in(w, tk=256):
    d = w.shape[0]
    return pl.pallas_call(
        _prep_w_in_kernel,
        grid=(d // tk,),
        in_specs=[pl.BlockSpec((tk, IN_WIDTH), lambda i: (i, 0))],
        out_specs=pl.BlockSpec((tk, PROJ_PAD), lambda i: (i, 0)),
        out_shape=jax.ShapeDtypeStruct((d, PROJ_PAD), BF16),
        compiler_params=pltpu.CompilerParams(
            dimension_semantics=("parallel",), vmem_limit_bytes=VMEM_LIMIT),
        name="prep_w_in",
    )(w)


def _prep_w_out_kernel(w_ref, o_ref):
    def put(dst, src, rows):
        o_ref[dst:dst + rows, :] = w_ref[src:src + rows, :].astype(BF16)

    put(0, 0, RET_WIDTH)
    for h in range(ML_HEADS):
        dst = MIX_OFF_ML + h * ML_V_PAD
        put(dst, RET_WIDTH + h * ML_V, ML_V)
        o_ref[dst + ML_V:dst + ML_V_PAD, :] = jnp.zeros((ML_V_PAD - ML_V, o_ref.shape[1]), BF16)
    put(MIX_OFF_POOL, RET_WIDTH + ML_WIDTH, POOL_WIDTH)


def _prep_w_out(w, tn=512):
    k, n = w.shape
    return pl.pallas_call(
        _prep_w_out_kernel,
        grid=(n // tn,),
        in_specs=[pl.BlockSpec((k, tn), lambda j: (0, j))],
        out_specs=pl.BlockSpec((MIX_PAD, tn), lambda j: (0, j)),
        out_shape=jax.ShapeDtypeStruct((MIX_PAD, n), BF16),
        compiler_params=pltpu.CompilerParams(
            dimension_semantics=("parallel",), vmem_limit_bytes=VMEM_LIMIT),
        name="prep_w_out",
    )(w)


def _prompt_mixer_kernel(p_ref, cos_ref, sin_ref, dintra_ref, dq_ref, dk_ref, dch_ref,
                         gret_ref, gml_ref, bi_r_ref, bf_r_ref, gb_c_ref, wpool_ref, pscale_ref,
                         mix_ref, sret_ref, sc_ref, sn_ref, sm_ref, spool_ref,
                         s_scr, c_scr, n_scr, m_scr, u_scr):
    b = pl.program_id(0)
    c = pl.program_id(1)

    @pl.when(c == 0)
    def _():
        s_scr[...] = jnp.zeros_like(s_scr)
        c_scr[...] = jnp.zeros_like(c_scr)
        n_scr[...] = jnp.zeros_like(n_scr)
        m_scr[...] = jnp.zeros_like(m_scr)
        u_scr[0:2 * SUBLANES, :] = jnp.zeros((2 * SUBLANES, POOL_WIDTH), F32)

    cos2 = cos_ref[...]
    sin2 = sin_ref[...]
    for h in range(RET_HEADS):
        lo, hi = h * RET_DIM, (h + 1) * RET_DIM
        q = p_ref[:, OFF_RQ + lo:OFF_RQ + hi]
        k = p_ref[:, OFF_RK + lo:OFF_RK + hi]
        vb = p_ref[:, OFF_RV + lo:OFF_RV + hi].astype(BF16)
        g = p_ref[:, OFF_RG + lo:OFF_RG + hi]
        qb = _rotate(q, cos2, sin2).astype(BF16)
        kr_t = (_rotate(k, cos2, sin2) * (RET_DIM ** -0.5)).T
        scores = _dot(qb, kr_t.astype(BF16)) * dintra_ref[h]
        s_old = s_scr[h]
        o = _dot(scores.astype(BF16), vb) + _dot(qb, s_old.astype(BF16)) * dq_ref[:, h:h + 1]
        s_scr[h] = s_old * dch_ref[h:h + 1, :] + _dot((kr_t * dk_ref[h:h + 1, :]).astype(BF16), vb)
        out = _rms(o) * gret_ref[:, lo:hi] * (g * jax.nn.sigmoid(g))
        mix_ref[:, lo:hi] = out.astype(BF16)

    row = lax.broadcasted_iota(jnp.int32, (CHUNK, CHUNK), 0)
    col = lax.broadcasted_iota(jnp.int32, (CHUNK, CHUNK), 1)
    causal = col <= row
    gq0 = p_ref[:, OFF_MQ:OFF_MQ + ML_QK_PAD]
    g_t = gq0.T
    i_r8 = g_t[GATE_I_LANE:GATE_I_LANE + SUBLANES, :] + bi_r_ref[...]
    f_r8 = _log_sigmoid(g_t[GATE_F_LANE:GATE_F_LANE + SUBLANES, :] + bf_r_ref[...])
    b_r8 = jnp.dot(f_r8, (row <= col).astype(F32), precision=_HI, preferred_element_type=F32)
    pre_c = gq0 + gb_c_ref[0:1, :]
    b_c_all = jnp.dot(causal.astype(F32), _log_sigmoid(pre_c), precision=_HI,
                      preferred_element_type=F32)
    for h in range(ML_HEADS):
        q = p_ref[:, OFF_MQ + h * ML_QK_PAD:OFF_MQ + (h + 1) * ML_QK_PAD]
        k = p_ref[:, OFF_MK + h * ML_QK_PAD:OFF_MK + (h + 1) * ML_QK_PAD] * (ML_QK ** -0.5)
        vb = p_ref[:, OFF_MV + h * ML_V_PAD:OFF_MV + (h + 1) * ML_V_PAD].astype(BF16)
        og = p_ref[:, OFF_MO + h * ML_V_PAD:OFF_MO + (h + 1) * ML_V_PAD]
        k_t = k.T
        qb = q.astype(BF16)
        b_c = b_c_all[:, GATE_F_LANE + h:GATE_F_LANE + h + 1]
        i_c = pre_c[:, GATE_I_LANE + h:GATE_I_LANE + h + 1]
        b_r = b_r8[h:h + 1, :]
        i_r = i_r8[h:h + 1, :]
        logw = jnp.where(causal, b_c - b_r + i_r, -jnp.inf)
        m_prev = m_scr[h:h + 1, 0:1]
        inter = b_c + m_prev
        m_t = jnp.maximum(inter, jnp.max(logw, axis=-1, keepdims=True))
        w = jnp.exp(logw - m_t)
        a = jnp.exp(inter - m_t)
        s = _dot(qb, k_t.astype(BF16)) * w
        c_old = c_scr[h]
        n_old = n_scr[h:h + 1, :]
        num = _dot(s.astype(BF16), vb) + a * _dot(qb, c_old.astype(BF16))
        den = (jnp.sum(s, axis=-1, keepdims=True)
               + a * jnp.sum(q * n_old, axis=-1, keepdims=True))
        hh = num / jnp.maximum(jnp.abs(den), jnp.exp(-m_t))
        m_new = m_t[CHUNK - 1:CHUNK, :]
        b_last = b_c[CHUNK - 1:CHUNK, :]
        w_r = jnp.exp(b_last - b_r + i_r - m_new)
        w_c = jnp.exp(b_last - b_c + i_c - m_new)
        a_end = jnp.exp(b_last + m_prev - m_new)
        c_scr[h] = a_end * c_old + _dot((k_t * w_r).astype(BF16), vb)
        n_scr[h:h + 1, :] = a_end * n_old + jnp.sum(k * w_c, axis=0, keepdims=True)
        m_scr[h:h + 1, :] = jnp.broadcast_to(m_new, (1, LANES))
        out = (_rms(hh, ML_V) * gml_ref[:, h * ML_V_PAD:(h + 1) * ML_V_PAD]
               * jax.nn.sigmoid(og))
        mix_ref[:, MIX_OFF_ML + h * ML_V_PAD:MIX_OFF_ML + (h + 1) * ML_V_PAD] = out.astype(BF16)

    base = 2 * SUBLANES
    u = p_ref[:, OFF_PU:OFF_PU + POOL_WIDTH]
    u_scr[base:base + CHUNK, :] = u
    pos = (c * CHUNK + lax.broadcasted_iota(jnp.int32, (CHUNK, 1), 0)).astype(F32)
    for g, win in enumerate(POOL_WINDOWS):
        lo, hi = g * POOL_GROUP, (g + 1) * POOL_GROUP
        wsum = u[:, lo:hi]
        for s in range(1, win):
            wsum = wsum + u_scr[base - s:base - s + CHUNK, lo:hi]
        z = wsum / jnp.minimum(pos + 1.0, float(win)) - u[:, lo:hi]
        y = _dot(z.astype(BF16), wpool_ref[g]) * pscale_ref[:, lo:hi]
        mix_ref[:, MIX_OFF_POOL + lo:MIX_OFF_POOL + hi] = y.astype(BF16)
    u_scr[0:base, :] = u_scr[CHUNK:CHUNK + base, :]

    @pl.when(c == pl.num_programs(1) - 1)
    def _():
        sret_ref[b] = s_scr[...]
        sc_ref[b] = c_scr[...]
        sn_ref[b] = n_scr[...]
        sm_ref[b] = m_scr[...]
        spool_ref[b] = u_scr[0:base, :]


def _prompt_mixer(proj, tables, gains, batch, seq):
    nchunk = seq // CHUNK
    const2 = lambda b, c: (0, 0)
    const3 = lambda b, c: (0, 0, 0)
    const4 = lambda b, c: (0, 0, 0, 0)
    in_specs = [
        pl.BlockSpec((CHUNK, PROJ_PAD), lambda b, c: (b * nchunk + c, 0)),
        pl.BlockSpec((CHUNK, RET_DIM), lambda b, c: (c, 0)),
        pl.BlockSpec((CHUNK, RET_DIM), lambda b, c: (c, 0)),
        pl.BlockSpec((RET_HEADS, CHUNK, CHUNK), const3),
        pl.BlockSpec((CHUNK, LANES), const2),
        pl.BlockSpec((SUBLANES, CHUNK), const2),
        pl.BlockSpec((SUBLANES, LANES), const2),
        pl.BlockSpec((1, RET_WIDTH), const2),
        pl.BlockSpec((1, ML_HEADS * ML_V_PAD), const2),
        pl.BlockSpec((SUBLANES, LANES), const2),
        pl.BlockSpec((SUBLANES, LANES), const2),
        pl.BlockSpec((SUBLANES, LANES), const2),
        pl.BlockSpec((len(POOL_WINDOWS), POOL_GROUP, POOL_GROUP), const3),
        pl.BlockSpec((1, POOL_WIDTH), const2),
    ]
    out_shape = [
        jax.ShapeDtypeStruct((batch * seq, MIX_PAD), BF16),
        jax.ShapeDtypeStruct((batch, RET_HEADS, RET_DIM, RET_DIM), F32),
        jax.ShapeDtypeStruct((batch, ML_HEADS, ML_QK_PAD, ML_V_PAD), F32),
        jax.ShapeDtypeStruct((batch, SUBLANES, LANES), F32),
        jax.ShapeDtypeStruct((batch, SUBLANES, LANES), F32),
        jax.ShapeDtypeStruct((batch, 2 * SUBLANES, POOL_WIDTH), F32),
    ]
    out_specs = [
        pl.BlockSpec((CHUNK, MIX_PAD), lambda b, c: (b * nchunk + c, 0)),
        pl.BlockSpec((batch, RET_HEADS, RET_DIM, RET_DIM), const4),
        pl.BlockSpec((batch, ML_HEADS, ML_QK_PAD, ML_V_PAD), const4),
        pl.BlockSpec((batch, SUBLANES, LANES), const3),
        pl.BlockSpec((batch, SUBLANES, LANES), const3),
        pl.BlockSpec((batch, 2 * SUBLANES, POOL_WIDTH), const3),
    ]
    scratch = [
        pltpu.VMEM((RET_HEADS, RET_DIM, RET_DIM), F32),
        pltpu.VMEM((ML_HEADS, ML_QK_PAD, ML_V_PAD), F32),
        pltpu.VMEM((SUBLANES, LANES), F32),
        pltpu.VMEM((SUBLANES, LANES), F32),
        pltpu.VMEM((2 * SUBLANES + CHUNK, POOL_WIDTH), F32),
    ]
    return pl.pallas_call(
        _prompt_mixer_kernel,
        grid=(batch, nchunk),
        in_specs=in_specs,
        out_specs=out_specs,
        out_shape=out_shape,
        scratch_shapes=scratch,
        compiler_params=pltpu.CompilerParams(
            dimension_semantics=("arbitrary", "arbitrary"), vmem_limit_bytes=VMEM_LIMIT),
        name="prompt_mixer",
    )(proj, tables["cos"], tables["sin"], tables["dintra"], tables["dq"], tables["dk"],
      tables["dch"], gains["g_ret"], gains["g_ml"], gains["bi_r"], gains["bf_r"], gains["gb_c"],
      gains["w_pool"], gains["pool_scale"])


DEC_BLOCK = 8


def _decode_mixer_kernel(p_ref, cos_ref, sin_ref, dch_ref, gret_ref, gml_ref, gb_c_ref,
                         wpool_ref, pscale_ref, s_ref, c_ref, n_ref, m_ref, pool_ref,
                         mix_ref, so_ref, co_ref, no_ref, mo_ref, poolo_ref,
                         qr_scr, kt_scr, v_scr, qs_scr, qm_scr, wkt_scr, vm_scr, qc_scr, a_scr,
                         cpad_scr):
    step = pl.program_id(0)
    nb = p_ref.shape[0]
    cos2 = cos_ref[0:1, :]
    sin2 = sin_ref[0:1, :]

    def ret_qkv(h):
        lo, hi = h * RET_DIM, (h + 1) * RET_DIM
        qr = _rotate(p_ref[:, OFF_RQ + lo:OFF_RQ + hi], cos2, sin2)
        kr = _rotate(p_ref[:, OFF_RK + lo:OFF_RK + hi], cos2, sin2) * (RET_DIM ** -0.5)
        return qr, kr, p_ref[:, OFF_RV + lo:OFF_RV + hi]

    def ml_gates(h):
        pre_c = p_ref[:, OFF_MQ:OFF_MQ + ML_QK_PAD] + gb_c_ref[0:1, :]
        i_pre = pre_c[:, GATE_I_LANE + h:GATE_I_LANE + h + 1]
        lf = _log_sigmoid(pre_c[:, GATE_F_LANE + h:GATE_F_LANE + h + 1])
        inter = lf + m_ref[:, h:h + 1]
        m_t = jnp.maximum(inter, i_pre)
        return jnp.exp(i_pre - m_t), jnp.exp(inter - m_t), m_t

    def ml_qkv(h):
        q = p_ref[:, OFF_MQ + h * ML_QK_PAD:OFF_MQ + (h + 1) * ML_QK_PAD]
        k = p_ref[:, OFF_MK + h * ML_QK_PAD:OFF_MK + (h + 1) * ML_QK_PAD] * (ML_QK ** -0.5)
        return q, k, p_ref[:, OFF_MV + h * ML_V_PAD:OFF_MV + (h + 1) * ML_V_PAD]

    @pl.when(step == 0)
    def _():
        for h in range(RET_HEADS):
            qr, kr, v = ret_qkv(h)
            qr_scr[h] = qr
            kt_scr[h] = kr.T
            v_scr[h] = v.astype(BF16)
        for h in range(ML_HEADS):
            w, a, _ = ml_gates(h)
            q, k, v = ml_qkv(h)
            qm_scr[h] = q
            wkt_scr[h] = (w * k).T
            vm_scr[h] = v.astype(BF16)
            a_scr[h] = jnp.broadcast_to(a, (nb, ML_V_PAD))
        cpad_scr[...] = jnp.zeros_like(cpad_scr)

    lane = lax.broadcasted_iota(jnp.int32, (LANES, nb), 1)

    def body(bb, carry):
        b = step * DEC_BLOCK + bb
        sel = lane == b
        for h in range(RET_HEADS):
            s_old = s_ref[bb, h]
            q16 = jnp.broadcast_to(qr_scr[h, pl.ds(b, 1), :], (2 * SUBLANES, RET_DIM)).astype(BF16)
            qs_scr[h, pl.ds(b, 1), :] = _dot(q16, s_old.astype(BF16))[0:1, :]
            k_m = jnp.where(sel, kt_scr[h], 0.0).astype(BF16)
            so_ref[bb, h] = s_old * dch_ref[h:h + 1, :] + _dot(k_m, v_scr[h])
        for h in range(ML_HEADS):
            cpad_scr[0:ML_QK, 0:ML_V] = c_ref[bb, h]
            c_old = cpad_scr[...]
            q16 = jnp.broadcast_to(qm_scr[h, pl.ds(b, 1), :], (2 * SUBLANES, ML_QK_PAD)).astype(BF16)
            qc_scr[h, pl.ds(b, 1), :] = _dot(q16, c_old.astype(BF16))[0:1, :]
            wk_m = jnp.where(sel, wkt_scr[h], 0.0).astype(BF16)
            c_new = a_scr[h, pl.ds(b, 1), :] * c_old + _dot(wk_m, vm_scr[h])
            co_ref[bb, h] = c_new[0:ML_QK, 0:ML_V]
        return carry

    lax.fori_loop(0, DEC_BLOCK, body, 0)

    @pl.when(step == pl.num_programs(0) - 1)
    def _():
        for h in range(RET_HEADS):
            lo, hi = h * RET_DIM, (h + 1) * RET_DIM
            qr, kr, v = ret_qkv(h)
            g = p_ref[:, OFF_RG + lo:OFF_RG + hi]
            o = jnp.sum(qr * kr, axis=-1, keepdims=True) * v + qs_scr[h] * dch_ref[h:h + 1, :]
            out = _rms(o) * gret_ref[:, lo:hi] * (g * jax.nn.sigmoid(g))
            mix_ref[:, lo:hi] = out.astype(BF16)
        for h in range(ML_HEADS):
            w, a, m_t = ml_gates(h)
            q, k, v = ml_qkv(h)
            og = p_ref[:, OFF_MO + h * ML_V_PAD:OFF_MO + (h + 1) * ML_V_PAD]
            n_old = n_ref[:, h * ML_QK_PAD:(h + 1) * ML_QK_PAD]
            s = jnp.sum(q * k, axis=-1, keepdims=True) * w
            num = s * v + a * qc_scr[h]
            den = s + a * jnp.sum(q * n_old, axis=-1, keepdims=True)
            hh = num / jnp.maximum(jnp.abs(den), jnp.exp(-m_t))
            out = (_rms(hh, ML_V) * gml_ref[:, h * ML_V_PAD:(h + 1) * ML_V_PAD]
                   * jax.nn.sigmoid(og))
            mix_ref[:, MIX_OFF_ML + h * ML_V_PAD:MIX_OFF_ML + (h + 1) * ML_V_PAD] = out.astype(BF16)
            no_ref[:, h * ML_QK_PAD:(h + 1) * ML_QK_PAD] = a * n_old + w * k
            mo_ref[:, h:h + 1] = m_t
        u = p_ref[:, OFF_PU:OFF_PU + POOL_WIDTH]
        for g, win in enumerate(POOL_WINDOWS):
            lo, hi = g * POOL_GROUP, (g + 1) * POOL_GROUP
            wsum = u[:, lo:hi]
            for s in range(1, win):
                r = (POOL_BUF - s) * POOL_WIDTH
                wsum = wsum + pool_ref[:, r + lo:r + hi]
            z = wsum / float(win) - u[:, lo:hi]
            y = _dot(z.astype(BF16), wpool_ref[g]) * pscale_ref[:, lo:hi]
            mix_ref[:, MIX_OFF_POOL + lo:MIX_OFF_POOL + hi] = y.astype(BF16)
        keep = (POOL_BUF - 1) * POOL_WIDTH
        poolo_ref[:, 0:keep] = pool_ref[:, POOL_WIDTH:POOL_WIDTH + keep]
        poolo_ref[:, keep:keep + POOL_WIDTH] = u


def _decode_mixer(proj, tables, gains, s_ret, s_c, s_n, s_m, s_pool):
    nb = proj.shape[0]
    const2 = lambda s: (0, 0)
    const3 = lambda s: (0, 0, 0)
    blk4 = lambda s: (s, 0, 0, 0)
    full2 = lambda shape: pl.BlockSpec(shape, const2)
    in_specs = [
        full2((nb, PROJ_PAD)),
        full2((SUBLANES, RET_DIM)),
        full2((SUBLANES, RET_DIM)),
        full2((SUBLANES, LANES)),
        full2((1, RET_WIDTH)),
        full2((1, ML_HEADS * ML_V_PAD)),
        full2((SUBLANES, LANES)),
        pl.BlockSpec((len(POOL_WINDOWS), POOL_GROUP, POOL_GROUP), const3),
        full2((1, POOL_WIDTH)),
        pl.BlockSpec((DEC_BLOCK, RET_HEADS, RET_DIM, RET_DIM), blk4),
        pl.BlockSpec((DEC_BLOCK, ML_HEADS, ML_QK, ML_V), blk4),
        full2((nb, ML_HEADS * ML_QK_PAD)),
        full2((nb, ML_HEADS)),
        full2((nb, POOL_BUF * POOL_WIDTH)),
    ]
    out_shape = [
        jax.ShapeDtypeStruct((nb, MIX_PAD), BF16),
        jax.ShapeDtypeStruct(s_ret.shape, F32),
        jax.ShapeDtypeStruct(s_c.shape, F32),
        jax.ShapeDtypeStruct((nb, ML_HEADS * ML_QK_PAD), F32),
        jax.ShapeDtypeStruct((nb, ML_HEADS), F32),
        jax.ShapeDtypeStruct((nb, POOL_BUF * POOL_WIDTH), F32),
    ]
    out_specs = [
        full2((nb, MIX_PAD)),
        pl.BlockSpec((DEC_BLOCK, RET_HEADS, RET_DIM, RET_DIM), blk4),
        pl.BlockSpec((DEC_BLOCK, ML_HEADS, ML_QK, ML_V), blk4),
        full2((nb, ML_HEADS * ML_QK_PAD)),
        full2((nb, ML_HEADS)),
        full2((nb, POOL_BUF * POOL_WIDTH)),
    ]
    scratch = [
        pltpu.VMEM((RET_HEADS, nb, RET_DIM), F32),
        pltpu.VMEM((RET_HEADS, RET_DIM, nb), F32),
        pltpu.VMEM((RET_HEADS, nb, RET_DIM), BF16),
        pltpu.VMEM((RET_HEADS, nb, RET_DIM), F32),
        pltpu.VMEM((ML_HEADS, nb, ML_QK_PAD), F32),
        pltpu.VMEM((ML_HEADS, ML_QK_PAD, nb), F32),
        pltpu.VMEM((ML_HEADS, nb, ML_V_PAD), BF16),
        pltpu.VMEM((ML_HEADS, nb, ML_V_PAD), F32),
        pltpu.VMEM((ML_HEADS, nb, ML_V_PAD), F32),
        pltpu.VMEM((ML_QK_PAD, ML_V_PAD), F32),
    ]
    return pl.pallas_call(
        _decode_mixer_kernel,
        grid=(nb // DEC_BLOCK,),
        in_specs=in_specs,
        out_specs=out_specs,
        out_shape=out_shape,
        scratch_shapes=scratch,
        compiler_params=pltpu.CompilerParams(
            dimension_semantics=("arbitrary",), vmem_limit_bytes=VMEM_LIMIT),
        name="decode_mixer",
    )(proj, tables["cos_s"], tables["sin_s"], tables["dch"], gains["g_ret"], gains["g_ml"],
      gains["gb_c"], gains["w_pool"], gains["pool_scale"], s_ret, s_c, s_n, s_m, s_pool)


def _tables(seq):
    half = RET_DIM // 2
    inv_freq = ROPE_THETA ** (-jnp.arange(half, dtype=F32) / half)

    def cos_sin(pos):
        ang = pos[:, None] * inv_freq[None, :]
        cos, sin = jnp.cos(ang), jnp.sin(ang)
        return jnp.concatenate([cos, cos], axis=-1), jnp.concatenate([-sin, sin], axis=-1)

    cos_p, sin_p = cos_sin(jnp.arange(seq, dtype=F32))
    cos_s, sin_s = cos_sin((PAST_LEN + jnp.arange(1)).astype(F32))
    log_gamma = jnp.log1p(-jnp.exp2(-5.0 - jnp.arange(RET_HEADS, dtype=F32)))
    idx = jnp.arange(CHUNK, dtype=F32)
    diff = idx[:, None] - idx[None, :]
    causal = diff >= 0
    lg = log_gamma[:, None, None]
    dintra = jnp.where(causal, jnp.exp(lg * jnp.where(causal, diff, 0.0)), 0.0)
    dq = jnp.exp(log_gamma[:, None] * (idx + 1.0))
    dk = jnp.exp(log_gamma[:, None] * (CHUNK - 1.0 - idx))

    def pad_rows(t, rows):
        return jnp.pad(t, ((0, rows - t.shape[0]), (0, 0)))

    def chunk_decay(c):
        return pad_rows(jnp.broadcast_to(jnp.exp(log_gamma * c)[:, None], (RET_HEADS, LANES)), SUBLANES)

    return {
        "cos": cos_p, "sin": sin_p,
        "cos_s": jnp.broadcast_to(cos_s, (SUBLANES, RET_DIM)),
        "sin_s": jnp.broadcast_to(sin_s, (SUBLANES, RET_DIM)),
        "dintra": dintra,
        "dq": jnp.pad(dq.T, ((0, 0), (0, LANES - RET_HEADS))),
        "dk": pad_rows(dk, SUBLANES),
        "dch": chunk_decay(float(CHUNK)),
        "dch_s": chunk_decay(1.0),
    }


def _pad_last(t, width):
    return jnp.pad(t, [(0, 0)] * (t.ndim - 1) + [(0, width - t.shape[-1])])


def _prep_gains(l, g_ret_norm, g_mlstm_norm, b_mlstm_i, b_mlstm_f, w_pool, pool_scale):
    bi, bf = b_mlstm_i[l], b_mlstm_f[l]
    rows = lambda t: jnp.broadcast_to(_pad_last(t, SUBLANES)[:, None], (SUBLANES, LANES))
    lanes = jnp.concatenate([jnp.zeros((GATE_I_LANE,), F32), _pad_last(bi, SUBLANES),
                             _pad_last(bf, LANES - GATE_F_LANE)])
    return {
        "g_ret": g_ret_norm[l][None, :],
        "g_ml": _pad_last(g_mlstm_norm[l].reshape(ML_HEADS, ML_V), ML_V_PAD).reshape(1, -1),
        "bi_r": rows(bi), "bf_r": rows(bf),
        "gb_c": jnp.broadcast_to(lanes[None, :], (SUBLANES, LANES)),
        "w_pool": w_pool[l].astype(BF16),
        "pool_scale": pool_scale[l][None, :],
    }


@jax.jit
def kernel(x_prompt, x_sample, state_ret, state_mlstm_c, state_mlstm_n, state_mlstm_m, state_pool,
           w_in, w_out, g_ret_norm, g_mlstm_norm, b_mlstm_i, b_mlstm_f, w_pool, pool_scale,
           g_pre_mix, g_post_mix, g_pre_ffn, g_post_ffn, w_ffn_gate, w_ffn_up, w_ffn_down):
    batch, seq, d = x_prompt.shape
    nb = x_sample.shape[0]
    depth = w_in.shape[0]
    tables = _tables(seq)
    tables_s = dict(tables, dch=tables["dch_s"])

    yp = x_prompt.reshape(batch * seq, d)
    ys = x_sample.reshape(nb, d)
    new_p, new_s = [], []
    def token_ops(l, x, w_in_l, w_out_l, mixer):
        t = _token_tiles(x.shape[0])
        row = lambda g: g[l][None, :]
        proj = _inproj(x, row(g_pre_mix), w_in_l, tm=t["inproj_tm"], tn=t["inproj_tn"])
        mixed, *states = mixer(proj)
        x1 = _outproj(mixed, w_out_l, x, row(g_post_mix), tm=t["outproj_tm"], tk=t["outproj_tk"])
        y = _ffn(x1, row(g_pre_ffn), w_ffn_gate[l], w_ffn_up[l], w_ffn_down[l], row(g_post_ffn),
                 tm=t["ffn_tm"], tf=t["ffn_tf"])
        return y, states

    for l in range(depth):
        w_in_l = _prep_w_in(w_in[l])
        w_out_l = _prep_w_out(w_out[l])
        gains = _prep_gains(l, g_ret_norm, g_mlstm_norm, b_mlstm_i, b_mlstm_f, w_pool, pool_scale)

        yp, (s_ret, s_c, s_n, s_m, s_pool) = token_ops(
            l, yp, w_in_l, w_out_l, lambda proj: _prompt_mixer(proj, tables, gains, batch, seq))
        new_p.append((s_ret, s_c[:, :, :ML_QK, :ML_V], s_n[:, :ML_HEADS, :ML_QK],
                      s_m[:, :ML_HEADS, 0], s_pool[:, 1:, :]))

        n_in = _pad_last(state_mlstm_n[l], ML_QK_PAD).reshape(nb, ML_HEADS * ML_QK_PAD)
        pool_in = state_pool[l].reshape(nb, POOL_BUF * POOL_WIDTH)
        ys, (s_ret, s_c, s_n, s_m, s_pool) = token_ops(
            l, ys, w_in_l, w_out_l, lambda proj: _decode_mixer(proj, tables_s, gains, state_ret[l], state_mlstm_c[l],
                                              n_in, state_mlstm_m[l], pool_in))
        new_s.append((s_ret, s_c, s_n.reshape(nb, ML_HEADS, ML_QK_PAD)[:, :, :ML_QK], s_m,
                      s_pool.reshape(nb, POOL_BUF, POOL_WIDTH)))

    stack = lambda states, i: jnp.stack([s[i] for s in states], axis=0)
    return ((yp.reshape(batch, seq, d), ys.reshape(nb, 1, d))
            + tuple(stack(new_p, i) for i in range(5))
            + tuple(stack(new_s, i) for i in range(5)))
```

```python
import functools

import jax
import jax.numpy as jnp
from jax import lax
from jax.experimental import pallas as pl
from jax.experimental.pallas import tpu as pltpu

F32 = jnp.float32
BF16 = jnp.bfloat16

D_MODEL = 2048
PAST_LEN = 16384
RET_HEADS = 6
RET_DIM = 128
RET_WIDTH = RET_HEADS * RET_DIM
ML_HEADS = 4
ML_QK = 96
ML_V = 192
ML_WIDTH = ML_HEADS * ML_V
POOL_WIDTH = 512
POOL_WINDOWS = (2, 4, 8, 16)
POOL_GROUP = POOL_WIDTH // len(POOL_WINDOWS)
POOL_BUF = max(POOL_WINDOWS) - 1
D_FF = 5632
CHUNK = 128
ROPE_THETA = 10000.0
EPS = 1e-6

LANES = 128
SUBLANES = 8
VMEM_LIMIT = 56 * 1024 * 1024

ML_QK_PAD = LANES
ML_V_PAD = 2 * LANES
OFF_RQ = 0
OFF_RK = OFF_RQ + RET_WIDTH
OFF_RV = OFF_RK + RET_WIDTH
OFF_RG = OFF_RV + RET_WIDTH
OFF_MQ = OFF_RG + RET_WIDTH
OFF_MK = OFF_MQ + ML_HEADS * ML_QK_PAD
OFF_MV = OFF_MK + ML_HEADS * ML_QK_PAD
OFF_MO = OFF_MV + ML_HEADS * ML_V_PAD
OFF_PU = OFF_MO + ML_HEADS * ML_V_PAD
PROJ_PAD = OFF_PU + POOL_WIDTH
GATE_I_LANE = ML_QK
GATE_F_LANE = ML_QK + SUBLANES
MIX_OFF_ML = RET_WIDTH
MIX_OFF_POOL = MIX_OFF_ML + ML_HEADS * ML_V_PAD
MIX_PAD = MIX_OFF_POOL + POOL_WIDTH

_HI = lax.Precision.HIGHEST


def _token_tiles(m):
    return {
        "inproj_tm": min(m, 1024), "inproj_tn": 512,
        "outproj_tm": min(m, 512), "outproj_tk": 768,
        "ffn_tm": min(m, 1024), "ffn_tf": 256,
    }


def _dot(a, b):
    return jnp.dot(a, b, preferred_element_type=F32)


def _rms(x, width=None):
    n = x.shape[-1] if width is None else width
    return x * lax.rsqrt(jnp.sum(x * x, axis=-1, keepdims=True) / n + EPS)


def _log_sigmoid(x):
    return jnp.minimum(x, 0.0) - jnp.log1p(jnp.exp(-jnp.abs(x)))


def _rotate(x, cos2, sin2):
    return x * cos2 + pltpu.roll(x, RET_DIM // 2, 1) * sin2


def _inproj_kernel(x_ref, g_ref, w_ref, o_ref, xn_ref):
    @pl.when(pl.program_id(1) == 0)
    def _():
        xn_ref[...] = (_rms(x_ref[...]) * g_ref[...]).astype(BF16)

    o_ref[...] = _dot(xn_ref[...], w_ref[...])


def _inproj(x, g, w, tm, tn):
    m = x.shape[0]
    return pl.pallas_call(
        _inproj_kernel,
        grid=(m // tm, PROJ_PAD // tn),
        in_specs=[
            pl.BlockSpec((tm, D_MODEL), lambda i, j: (i, 0)),
            pl.BlockSpec((1, D_MODEL), lambda i, j: (0, 0)),
            pl.BlockSpec((D_MODEL, tn), lambda i, j: (0, j)),
        ],
        out_specs=pl.BlockSpec((tm, tn), lambda i, j: (i, j)),
        out_shape=jax.ShapeDtypeStruct((m, PROJ_PAD), F32),
        scratch_shapes=[pltpu.VMEM((tm, D_MODEL), BF16)],
        compiler_params=pltpu.CompilerParams(
            dimension_semantics=("parallel", "arbitrary"), vmem_limit_bytes=VMEM_LIMIT),
        name="inproj",
    )(x, g, w)


def _outproj_kernel(m_ref, w_ref, x_ref, gpost_ref, x1_ref):
    k = pl.program_id(1)

    @pl.when(k == 0)
    def _():
        x1_ref[...] = jnp.zeros_like(x1_ref)

    x1_ref[...] += _dot(m_ref[...], w_ref[...])

    @pl.when(k == pl.num_programs(1) - 1)
    def _():
        x1_ref[...] = x_ref[...] + _rms(x1_ref[...]) * gpost_ref[...]


def _outproj(mixed, w, x, g_post, tm, tk):
    m = x.shape[0]
    return pl.pallas_call(
        _outproj_kernel,
        grid=(m // tm, MIX_PAD // tk),
        in_specs=[
            pl.BlockSpec((tm, tk), lambda i, k: (i, k)),
            pl.BlockSpec((tk, D_MODEL), lambda i, k: (k, 0)),
            pl.BlockSpec((tm, D_MODEL), lambda i, k: (i, 0)),
            pl.BlockSpec((1, D_MODEL), lambda i, k: (0, 0)),
        ],
        out_specs=pl.BlockSpec((tm, D_MODEL), lambda i, k: (i, 0)),
        out_shape=jax.ShapeDtypeStruct((m, D_MODEL), F32),
        compiler_params=pltpu.CompilerParams(
            dimension_semantics=("parallel", "arbitrary"), vmem_limit_bytes=VMEM_LIMIT),
        name="outproj",
    )(mixed, w, x, g_post)


def _ffn_kernel(x1_ref, gpre_ref, wg_ref, wu_ref, wd_ref, gpost_ref, o_ref, hn_ref):
    f = pl.program_id(1)

    @pl.when(f == 0)
    def _():
        hn_ref[...] = (_rms(x1_ref[...]) * gpre_ref[...]).astype(BF16)
        o_ref[...] = jnp.zeros_like(o_ref)

    hn = hn_ref[...]
    gate = _dot(hn, wg_ref[...].astype(BF16))
    up = _dot(hn, wu_ref[...].astype(BF16))
    act = (gate * jax.nn.sigmoid(gate) * up).astype(BF16)
    o_ref[...] += _dot(act, wd_ref[...].astype(BF16))

    @pl.when(f == pl.num_programs(1) - 1)
    def _():
        o_ref[...] = x1_ref[...] + _rms(o_ref[...]) * gpost_ref[...]


def _ffn(x1, g_pre, wg, wu, wd, g_post, layer, tm, tf):
    m = x1.shape[0]
    return pl.pallas_call(
        _ffn_kernel,
        grid=(m // tm, D_FF // tf),
        in_specs=[
            pl.BlockSpec((tm, D_MODEL), lambda i, f: (i, 0), pipeline_mode=pl.Buffered(1)),
            pl.BlockSpec((1, D_MODEL), lambda i, f: (0, 0)),
            pl.BlockSpec((None, D_MODEL, tf), lambda i, f: (layer, 0, f)),
            pl.BlockSpec((None, D_MODEL, tf), lambda i, f: (layer, 0, f)),
            pl.BlockSpec((None, tf, D_MODEL), lambda i, f: (layer, f, 0)),
            pl.BlockSpec((1, D_MODEL), lambda i, f: (0, 0)),
        ],
        out_specs=pl.BlockSpec((tm, D_MODEL), lambda i, f: (i, 0)),
        out_shape=jax.ShapeDtypeStruct((m, D_MODEL), F32),
        scratch_shapes=[pltpu.VMEM((tm, D_MODEL), BF16)],
        compiler_params=pltpu.CompilerParams(
            dimension_semantics=("parallel", "arbitrary"), vmem_limit_bytes=VMEM_LIMIT),
        name="ffn",
    )(x1, g_pre, wg, wu, wd, g_post)


SRC_MQ = 4 * RET_WIDTH
SRC_MK = SRC_MQ + ML_HEADS * ML_QK
SRC_MV = SRC_MK + ML_HEADS * ML_QK
SRC_MO = SRC_MV + ML_WIDTH
SRC_MI = SRC_MO + ML_WIDTH
SRC_MF = SRC_MI + ML_HEADS
SRC_PU = SRC_MF + ML_HEADS
IN_WIDTH = SRC_PU + POOL_WIDTH


def _prep_w_in_kernel(w_ref, o_ref):
    def put(dst, src, width):
        o_ref[:, dst:dst + width] = w_ref[:, src:src + width].astype(BF16)

    def zero(dst, width):
        o_ref[:, dst:dst + width] = jnp.zeros((o_ref.shape[0], width), BF16)

    put(0, 0, 4 * RET_WIDTH)
    for h in range(ML_HEADS):
        put(OFF_MQ + h * ML_QK_PAD, SRC_MQ + h * ML_QK, ML_QK)
        put(OFF_MK + h * ML_QK_PAD, SRC_MK + h * ML_QK, ML_QK)
        put(OFF_MV + h * ML_V_PAD, SRC_MV + h * ML_V, ML_V)
        put(OFF_MO + h * ML_V_PAD, SRC_MO + h * ML_V, ML_V)
        zero(OFF_MQ + h * ML_QK_PAD + ML_QK, ML_QK_PAD - ML_QK)
        zero(OFF_MK + h * ML_QK_PAD + ML_QK, ML_QK_PAD - ML_QK)
        zero(OFF_MV + h * ML_V_PAD + ML_V, ML_V_PAD - ML_V)
        zero(OFF_MO + h * ML_V_PAD + ML_V, ML_V_PAD - ML_V)
    put(OFF_MQ + GATE_I_LANE, SRC_MI, ML_HEADS)
    put(OFF_MQ + GATE_F_LANE, SRC_MF, ML_HEADS)
    put(OFF_PU, SRC_PU, POOL_WIDTH)


def _prep_w_in(w, layer, tk=256):
    d = w.shape[1]
    return pl.pallas_call(
        _prep_w_in_kernel,
        grid=(d // tk,),
        in_specs=[pl.BlockSpec((None, tk, IN_WIDTH), lambda i: (layer, i, 0))],
        out_specs=pl.BlockSpec((tk, PROJ_PAD), lambda i: (i, 0)),
        out_shape=jax.ShapeDtypeStruct((d, PROJ_PAD), BF16),
        compiler_params=pltpu.CompilerParams(
            dimension_semantics=("parallel",), vmem_limit_bytes=VMEM_LIMIT),
        name="prep_w_in",
    )(w)


def _prep_w_out_kernel(w_ref, o_ref):
    def put(dst, src, rows):
        o_ref[dst:dst + rows, :] = w_ref[src:src + rows, :].astype(BF16)

    put(0, 0, RET_WIDTH)
    for h in range(ML_HEADS):
        dst = MIX_OFF_ML + h * ML_V_PAD
        put(dst, RET_WIDTH + h * ML_V, ML_V)
        o_ref[dst + ML_V:dst + ML_V_PAD, :] = jnp.zeros((ML_V_PAD - ML_V, o_ref.shape[1]), BF16)
    put(MIX_OFF_POOL, RET_WIDTH + ML_WIDTH, POOL_WIDTH)


def _prep_w_out(w, layer, tn=512):
    _, k, n = w.shape
    return pl.pallas_call(
        _prep_w_out_kernel,
        grid=(n // tn,),
        in_specs=[pl.BlockSpec((None, k, tn), lambda j: (layer, 0, j))],
        out_specs=pl.BlockSpec((MIX_PAD, tn), lambda j: (0, j)),
        out_shape=jax.ShapeDtypeStruct((MIX_PAD, n), BF16),
        compiler_params=pltpu.CompilerParams(
            dimension_semantics=("parallel",), vmem_limit_bytes=VMEM_LIMIT),
        name="prep_w_out",
    )(w)


def _prompt_mixer_kernel(p_ref, cos_ref, sin_ref, dintra_ref, dq_ref, dk_ref, dch_ref,
                         gret_ref, gml_ref, bi_r_ref, bf_r_ref, gb_c_ref, wpool_ref, pscale_ref,
                         mix_ref, sret_ref, sc_ref, sn_ref, sm_ref, spool_ref,
                         s_scr, c_scr, n_scr, m_scr, u_scr):
    b = pl.program_id(0)
    c = pl.program_id(1)

    @pl.when(c == 0)
    def _():
        s_scr[...] = jnp.zeros_like(s_scr)
        c_scr[...] = jnp.zeros_like(c_scr)
        n_scr[...] = jnp.zeros_like(n_scr)
        m_scr[...] = jnp.zeros_like(m_scr)
        u_scr[0:2 * SUBLANES, :] = jnp.zeros((2 * SUBLANES, POOL_WIDTH), F32)

    cos2 = cos_ref[...]
    sin2 = sin_ref[...]
    for h in range(RET_HEADS):
        lo, hi = h * RET_DIM, (h + 1) * RET_DIM
        q = p_ref[:, OFF_RQ + lo:OFF_RQ + hi]
        k = p_ref[:, OFF_RK + lo:OFF_RK + hi]
        vb = p_ref[:, OFF_RV + lo:OFF_RV + hi].astype(BF16)
        g = p_ref[:, OFF_RG + lo:OFF_RG + hi]
        qb = _rotate(q, cos2, sin2).astype(BF16)
        kr_t = (_rotate(k, cos2, sin2) * (RET_DIM ** -0.5)).T
        scores = _dot(qb, kr_t.astype(BF16)) * dintra_ref[h]
        s_old = s_scr[h]
        o = _dot(scores.astype(BF16), vb) + _dot(qb, s_old.astype(BF16)) * dq_ref[:, h:h + 1]
        s_scr[h] = s_old * dch_ref[h:h + 1, :] + _dot((kr_t * dk_ref[h:h + 1, :]).astype(BF16), vb)
        out = _rms(o) * gret_ref[:, lo:hi] * (g * jax.nn.sigmoid(g))
        mix_ref[:, lo:hi] = out.astype(BF16)

    row = lax.broadcasted_iota(jnp.int32, (CHUNK, CHUNK), 0)
    col = lax.broadcasted_iota(jnp.int32, (CHUNK, CHUNK), 1)
    causal = col <= row
    gq0 = p_ref[:, OFF_MQ:OFF_MQ + ML_QK_PAD]
    g_t = gq0.T
    i_r8 = g_t[GATE_I_LANE:GATE_I_LANE + SUBLANES, :] + bi_r_ref[...]
    f_r8 = _log_sigmoid(g_t[GATE_F_LANE:GATE_F_LANE + SUBLANES, :] + bf_r_ref[...])
    b_r8 = jnp.dot(f_r8, (row <= col).astype(F32), precision=_HI, preferred_element_type=F32)
    pre_c = gq0 + gb_c_ref[0:1, :]
    b_c_all = jnp.dot(causal.astype(F32), _log_sigmoid(pre_c), precision=_HI,
                      preferred_element_type=F32)
    for h in range(ML_HEADS):
        q = p_ref[:, OFF_MQ + h * ML_QK_PAD:OFF_MQ + (h + 1) * ML_QK_PAD]
        k = p_ref[:, OFF_MK + h * ML_QK_PAD:OFF_MK + (h + 1) * ML_QK_PAD] * (ML_QK ** -0.5)
        vb = p_ref[:, OFF_MV + h * ML_V_PAD:OFF_MV + (h + 1) * ML_V_PAD].astype(BF16)
        og = p_ref[:, OFF_MO + h * ML_V_PAD:OFF_MO + (h + 1) * ML_V_PAD]
        k_t = k.T
        qb = q.astype(BF16)
        b_c = b_c_all[:, GATE_F_LANE + h:GATE_F_LANE + h + 1]
        i_c = pre_c[:, GATE_I_LANE + h:GATE_I_LANE + h + 1]
        b_r = b_r8[h:h + 1, :]
        i_r = i_r8[h:h + 1, :]
        logw = jnp.where(causal, b_c - b_r + i_r, -jnp.inf)
        m_prev = m_scr[h:h + 1, 0:1]
        inter = b_c + m_prev
        m_t = jnp.maximum(inter, jnp.max(logw, axis=-1, keepdims=True))
        w = jnp.exp(logw - m_t)
        a = jnp.exp(inter - m_t)
        s = _dot(qb, k_t.astype(BF16)) * w
        c_old = c_scr[h]
        n_old = n_scr[h:h + 1, :]
        num = _dot(s.astype(BF16), vb) + a * _dot(qb, c_old.astype(BF16))
        den = (jnp.sum(s, axis=-1, keepdims=True)
               + a * jnp.sum(q * n_old, axis=-1, keepdims=True))
        hh = num / jnp.maximum(jnp.abs(den), jnp.exp(-m_t))
        m_new = m_t[CHUNK - 1:CHUNK, :]
        b_last = b_c[CHUNK - 1:CHUNK, :]
        w_r = jnp.exp(b_last - b_r + i_r - m_new)
        w_c = jnp.exp(b_last - b_c + i_c - m_new)
        a_end = jnp.exp(b_last + m_prev - m_new)
        c_scr[h] = a_end * c_old + _dot((k_t * w_r).astype(BF16), vb)
        n_scr[h:h + 1, :] = a_end * n_old + jnp.sum(k * w_c, axis=0, keepdims=True)
        m_scr[h:h + 1, :] = jnp.broadcast_to(m_new, (1, LANES))
        out = (_rms(hh, ML_V) * gml_ref[:, h * ML_V_PAD:(h + 1) * ML_V_PAD]
               * jax.nn.sigmoid(og))
        mix_ref[:, MIX_OFF_ML + h * ML_V_PAD:MIX_OFF_ML + (h + 1) * ML_V_PAD] = out.astype(BF16)

    base = 2 * SUBLANES
    u = p_ref[:, OFF_PU:OFF_PU + POOL_WIDTH]
    u_scr[base:base + CHUNK, :] = u
    pos = (c * CHUNK + lax.broadcasted_iota(jnp.int32, (CHUNK, 1), 0)).astype(F32)
    for g, win in enumerate(POOL_WINDOWS):
        lo, hi = g * POOL_GROUP, (g + 1) * POOL_GROUP
        wsum = u[:, lo:hi]
        for s in range(1, win):
            wsum = wsum + u_scr[base - s:base - s + CHUNK, lo:hi]
        z = wsum / jnp.minimum(pos + 1.0, float(win)) - u[:, lo:hi]
        y = _dot(z.astype(BF16), wpool_ref[g]) * pscale_ref[:, lo:hi]
        mix_ref[:, MIX_OFF_POOL + lo:MIX_OFF_POOL + hi] = y.astype(BF16)
    u_scr[0:base, :] = u_scr[CHUNK:CHUNK + base, :]

    @pl.when(c == pl.num_programs(1) - 1)
    def _():
        sret_ref[b] = s_scr[...]
        sc_ref[b] = c_scr[...]
        sn_ref[b] = n_scr[...]
        sm_ref[b] = m_scr[...]
        spool_ref[b] = u_scr[0:base, :]


def _prompt_mixer(proj, tables, gains, batch, seq):
    nchunk = seq // CHUNK
    const2 = lambda b, c: (0, 0)
    const3 = lambda b, c: (0, 0, 0)
    const4 = lambda b, c: (0, 0, 0, 0)
    in_specs = [
        pl.BlockSpec((CHUNK, PROJ_PAD), lambda b, c: (b * nchunk + c, 0)),
        pl.BlockSpec((CHUNK, RET_DIM), lambda b, c: (c, 0)),
        pl.BlockSpec((CHUNK, RET_DIM), lambda b, c: (c, 0)),
        pl.BlockSpec((RET_HEADS, CHUNK, CHUNK), const3),
        pl.BlockSpec((CHUNK, LANES), const2),
        pl.BlockSpec((SUBLANES, CHUNK), const2),
        pl.BlockSpec((SUBLANES, LANES), const2),
        pl.BlockSpec((1, RET_WIDTH), const2),
        pl.BlockSpec((1, ML_HEADS * ML_V_PAD), const2),
        pl.BlockSpec((SUBLANES, LANES), const2),
        pl.BlockSpec((SUBLANES, LANES), const2),
        pl.BlockSpec((SUBLANES, LANES), const2),
        pl.BlockSpec((len(POOL_WINDOWS), POOL_GROUP, POOL_GROUP), const3),
        pl.BlockSpec((1, POOL_WIDTH), const2),
    ]
    out_shape = [
        jax.ShapeDtypeStruct((batch * seq, MIX_PAD), BF16),
        jax.ShapeDtypeStruct((batch, RET_HEADS, RET_DIM, RET_DIM), F32),
        jax.ShapeDtypeStruct((batch, ML_HEADS, ML_QK_PAD, ML_V_PAD), F32),
        jax.ShapeDtypeStruct((batch, SUBLANES, LANES), F32),
        jax.ShapeDtypeStruct((batch, SUBLANES, LANES), F32),
        jax.ShapeDtypeStruct((batch, 2 * SUBLANES, POOL_WIDTH), F32),
    ]
    out_specs = [
        pl.BlockSpec((CHUNK, MIX_PAD), lambda b, c: (b * nchunk + c, 0)),
        pl.BlockSpec((batch, RET_HEADS, RET_DIM, RET_DIM), const4),
        pl.BlockSpec((batch, ML_HEADS, ML_QK_PAD, ML_V_PAD), const4),
        pl.BlockSpec((batch, SUBLANES, LANES), const3),
        pl.BlockSpec((batch, SUBLANES, LANES), const3),
        pl.BlockSpec((batch, 2 * SUBLANES, POOL_WIDTH), const3),
    ]
    scratch = [
        pltpu.VMEM((RET_HEADS, RET_DIM, RET_DIM), F32),
        pltpu.VMEM((ML_HEADS, ML_QK_PAD, ML_V_PAD), F32),
        pltpu.VMEM((SUBLANES, LANES), F32),
        pltpu.VMEM((SUBLANES, LANES), F32),
        pltpu.VMEM((2 * SUBLANES + CHUNK, POOL_WIDTH), F32),
    ]
    return pl.pallas_call(
        _prompt_mixer_kernel,
        grid=(batch, nchunk),
        in_specs=in_specs,
        out_specs=out_specs,
        out_shape=out_shape,
        scratch_shapes=scratch,
        compiler_params=pltpu.CompilerParams(
            dimension_semantics=("arbitrary", "arbitrary"), vmem_limit_bytes=VMEM_LIMIT),
        name="prompt_mixer",
    )(proj, tables["cos"], tables["sin"], tables["dintra"], tables["dq"], tables["dk"],
      tables["dch"], gains["g_ret"], gains["g_ml"], gains["bi_r"], gains["bf_r"], gains["gb_c"],
      gains["w_pool"], gains["pool_scale"])


DEC_BLOCK = 8


DEC_INPUTS = 14


def _decode_mixer_kernel(*refs, n_aliased):
    (p_ref, cos_ref, sin_ref, dch_ref, gret_ref, gml_ref, gb_c_ref, wpool_ref, pscale_ref,
     s_ref, c_ref, n_ref, m_ref, pool_ref) = refs[:DEC_INPUTS]
    (mix_ref, so_ref, co_ref, no_ref, mo_ref, poolo_ref,
     qr_scr, kt_scr, v_scr, qs_scr, qm_scr, wkt_scr, vm_scr, qc_scr, a_scr,
     cpad_scr) = refs[DEC_INPUTS + n_aliased:]
    if n_aliased == 0:
        for later in range(1, so_ref.shape[0]):
            so_ref[later] = jnp.zeros(so_ref.shape[1:], F32)
            co_ref[later] = jnp.zeros(co_ref.shape[1:], F32)
        so_ref, co_ref = so_ref.at[0], co_ref.at[0]
    step = pl.program_id(0)
    nb = p_ref.shape[0]
    cos2 = cos_ref[0:1, :]
    sin2 = sin_ref[0:1, :]

    def ret_qkv(h):
        lo, hi = h * RET_DIM, (h + 1) * RET_DIM
        qr = _rotate(p_ref[:, OFF_RQ + lo:OFF_RQ + hi], cos2, sin2)
        kr = _rotate(p_ref[:, OFF_RK + lo:OFF_RK + hi], cos2, sin2) * (RET_DIM ** -0.5)
        return qr, kr, p_ref[:, OFF_RV + lo:OFF_RV + hi]

    def ml_gates(h):
        pre_c = p_ref[:, OFF_MQ:OFF_MQ + ML_QK_PAD] + gb_c_ref[0:1, :]
        i_pre = pre_c[:, GATE_I_LANE + h:GATE_I_LANE + h + 1]
        lf = _log_sigmoid(pre_c[:, GATE_F_LANE + h:GATE_F_LANE + h + 1])
        inter = lf + m_ref[:, h:h + 1]
        m_t = jnp.maximum(inter, i_pre)
        return jnp.exp(i_pre - m_t), jnp.exp(inter - m_t), m_t

    def ml_qkv(h):
        q = p_ref[:, OFF_MQ + h * ML_QK_PAD:OFF_MQ + (h + 1) * ML_QK_PAD]
        k = p_ref[:, OFF_MK + h * ML_QK_PAD:OFF_MK + (h + 1) * ML_QK_PAD] * (ML_QK ** -0.5)
        return q, k, p_ref[:, OFF_MV + h * ML_V_PAD:OFF_MV + (h + 1) * ML_V_PAD]

    @pl.when(step == 0)
    def _():
        for h in range(RET_HEADS):
            qr, kr, v = ret_qkv(h)
            qr_scr[h] = qr
            kt_scr[h] = kr.T
            v_scr[h] = v.astype(BF16)
        for h in range(ML_HEADS):
            w, a, _ = ml_gates(h)
            q, k, v = ml_qkv(h)
            qm_scr[h] = q
            wkt_scr[h] = (w * k).T
            vm_scr[h] = v.astype(BF16)
            a_scr[h] = jnp.broadcast_to(a, (nb, ML_V_PAD))
        cpad_scr[...] = jnp.zeros_like(cpad_scr)

    lane = lax.broadcasted_iota(jnp.int32, (LANES, nb), 1)

    def body(bb, carry):
        b = step * DEC_BLOCK + bb
        sel = lane == b
        for h in range(RET_HEADS):
            s_old = s_ref[bb, h]
            q16 = jnp.broadcast_to(qr_scr[h, pl.ds(b, 1), :], (2 * SUBLANES, RET_DIM)).astype(BF16)
            qs_scr[h, pl.ds(b, 1), :] = _dot(q16, s_old.astype(BF16))[0:1, :]
            k_m = jnp.where(sel, kt_scr[h], 0.0).astype(BF16)
            so_ref[bb, h] = s_old * dch_ref[h:h + 1, :] + _dot(k_m, v_scr[h])
        for h in range(ML_HEADS):
            cpad_scr[0:ML_QK, 0:ML_V] = c_ref[bb, h]
            c_old = cpad_scr[...]
            q16 = jnp.broadcast_to(qm_scr[h, pl.ds(b, 1), :], (2 * SUBLANES, ML_QK_PAD)).astype(BF16)
            qc_scr[h, pl.ds(b, 1), :] = _dot(q16, c_old.astype(BF16))[0:1, :]
            wk_m = jnp.where(sel, wkt_scr[h], 0.0).astype(BF16)
            c_new = a_scr[h, pl.ds(b, 1), :] * c_old + _dot(wk_m, vm_scr[h])
            co_ref[bb, h] = c_new[0:ML_QK, 0:ML_V]
        return carry

    lax.fori_loop(0, DEC_BLOCK, body, 0)

    @pl.when(step == pl.num_programs(0) - 1)
    def _():
        for h in range(RET_HEADS):
            lo, hi = h * RET_DIM, (h + 1) * RET_DIM
            qr, kr, v = ret_qkv(h)
            g = p_ref[:, OFF_RG + lo:OFF_RG + hi]
            o = jnp.sum(qr * kr, axis=-1, keepdims=True) * v + qs_scr[h] * dch_ref[h:h + 1, :]
            out = _rms(o) * gret_ref[:, lo:hi] * (g * jax.nn.sigmoid(g))
            mix_ref[:, lo:hi] = out.astype(BF16)
        for h in range(ML_HEADS):
            w, a, m_t = ml_gates(h)
            q, k, v = ml_qkv(h)
            og = p_ref[:, OFF_MO + h * ML_V_PAD:OFF_MO + (h + 1) * ML_V_PAD]
            n_old = n_ref[:, h * ML_QK_PAD:(h + 1) * ML_QK_PAD]
            s = jnp.sum(q * k, axis=-1, keepdims=True) * w
            num = s * v + a * qc_scr[h]
            den = s + a * jnp.sum(q * n_old, axis=-1, keepdims=True)
            hh = num / jnp.maximum(jnp.abs(den), jnp.exp(-m_t))
            out = (_rms(hh, ML_V) * gml_ref[:, h * ML_V_PAD:(h + 1) * ML_V_PAD]
                   * jax.nn.sigmoid(og))
            mix_ref[:, MIX_OFF_ML + h * ML_V_PAD:MIX_OFF_ML + (h + 1) * ML_V_PAD] = out.astype(BF16)
            no_ref[:, h * ML_QK_PAD:(h + 1) * ML_QK_PAD] = a * n_old + w * k
            mo_ref[:, h:h + 1] = m_t
        u = p_ref[:, OFF_PU:OFF_PU + POOL_WIDTH]
        for g, win in enumerate(POOL_WINDOWS):
            lo, hi = g * POOL_GROUP, (g + 1) * POOL_GROUP
            wsum = u[:, lo:hi]
            for s in range(1, win):
                r = (POOL_BUF - s) * POOL_WIDTH
                wsum = wsum + pool_ref[:, r + lo:r + hi]
            z = wsum / float(win) - u[:, lo:hi]
            y = _dot(z.astype(BF16), wpool_ref[g]) * pscale_ref[:, lo:hi]
            mix_ref[:, MIX_OFF_POOL + lo:MIX_OFF_POOL + hi] = y.astype(BF16)
        keep = (POOL_BUF - 1) * POOL_WIDTH
        poolo_ref[:, 0:keep] = pool_ref[:, POOL_WIDTH:POOL_WIDTH + keep]
        poolo_ref[:, keep:keep + POOL_WIDTH] = u


def _decode_mixer(proj, tables, gains, layer, s_ret, s_c, s_n, s_m, s_pool, stacked_prev):
    nb = proj.shape[0]
    const2 = lambda s: (0, 0)
    const3 = lambda s: (0, 0, 0)
    blk5 = lambda s: (layer, s, 0, 0, 0)
    full2 = lambda shape: pl.BlockSpec(shape, const2)
    s_blk = pl.BlockSpec((None, DEC_BLOCK, RET_HEADS, RET_DIM, RET_DIM), blk5)
    c_blk = pl.BlockSpec((None, DEC_BLOCK, ML_HEADS, ML_QK, ML_V), blk5)
    in_specs = [
        full2((nb, PROJ_PAD)),
        full2((SUBLANES, RET_DIM)),
        full2((SUBLANES, RET_DIM)),
        full2((SUBLANES, LANES)),
        full2((1, RET_WIDTH)),
        full2((1, ML_HEADS * ML_V_PAD)),
        full2((SUBLANES, LANES)),
        pl.BlockSpec((len(POOL_WINDOWS), POOL_GROUP, POOL_GROUP), const3),
        full2((1, POOL_WIDTH)),
        s_blk,
        c_blk,
        full2((nb, ML_HEADS * ML_QK_PAD)),
        full2((nb, ML_HEADS)),
        full2((nb, POOL_BUF * POOL_WIDTH)),
    ]
    assert len(in_specs) == DEC_INPUTS
    in_specs += [pl.BlockSpec(memory_space=pl.ANY)] * len(stacked_prev)
    out_shape = [
        jax.ShapeDtypeStruct((nb, MIX_PAD), BF16),
        jax.ShapeDtypeStruct(s_ret.shape, F32),
        jax.ShapeDtypeStruct(s_c.shape, F32),
        jax.ShapeDtypeStruct((nb, ML_HEADS * ML_QK_PAD), F32),
        jax.ShapeDtypeStruct((nb, ML_HEADS), F32),
        jax.ShapeDtypeStruct((nb, POOL_BUF * POOL_WIDTH), F32),
    ]
    if stacked_prev:
        so_blk, co_blk = s_blk, c_blk
    else:
        assert layer == 0
        all_layers = lambda s: (0, s, 0, 0, 0)
        so_blk = pl.BlockSpec((s_ret.shape[0],) + s_blk.block_shape[1:], all_layers)
        co_blk = pl.BlockSpec((s_c.shape[0],) + c_blk.block_shape[1:], all_layers)
    out_specs = [
        full2((nb, MIX_PAD)),
        so_blk,
        co_blk,
        full2((nb, ML_HEADS * ML_QK_PAD)),
        full2((nb, ML_HEADS)),
        full2((nb, POOL_BUF * POOL_WIDTH)),
    ]
    scratch = [
        pltpu.VMEM((RET_HEADS, nb, RET_DIM), F32),
        pltpu.VMEM((RET_HEADS, RET_DIM, nb), F32),
        pltpu.VMEM((RET_HEADS, nb, RET_DIM), BF16),
        pltpu.VMEM((RET_HEADS, nb, RET_DIM), F32),
        pltpu.VMEM((ML_HEADS, nb, ML_QK_PAD), F32),
        pltpu.VMEM((ML_HEADS, ML_QK_PAD, nb), F32),
        pltpu.VMEM((ML_HEADS, nb, ML_V_PAD), BF16),
        pltpu.VMEM((ML_HEADS, nb, ML_V_PAD), F32),
        pltpu.VMEM((ML_HEADS, nb, ML_V_PAD), F32),
        pltpu.VMEM((ML_QK_PAD, ML_V_PAD), F32),
    ]
    return pl.pallas_call(
        functools.partial(_decode_mixer_kernel, n_aliased=len(stacked_prev)),
        grid=(nb // DEC_BLOCK,),
        in_specs=in_specs,
        out_specs=out_specs,
        out_shape=out_shape,
        scratch_shapes=scratch,
        input_output_aliases={DEC_INPUTS + i: 1 + i for i in range(len(stacked_prev))},
        compiler_params=pltpu.CompilerParams(
            dimension_semantics=("arbitrary",), vmem_limit_bytes=VMEM_LIMIT),
        name="decode_mixer",
    )(proj, tables["cos_s"], tables["sin_s"], tables["dch"], gains["g_ret"], gains["g_ml"],
      gains["gb_c"], gains["w_pool"], gains["pool_scale"], s_ret, s_c, s_n, s_m, s_pool,
      *stacked_prev)


def _tables(seq):
    half = RET_DIM // 2
    inv_freq = ROPE_THETA ** (-jnp.arange(half, dtype=F32) / half)

    def cos_sin(pos):
        ang = pos[:, None] * inv_freq[None, :]
        cos, sin = jnp.cos(ang), jnp.sin(ang)
        return jnp.concatenate([cos, cos], axis=-1), jnp.concatenate([-sin, sin], axis=-1)

    cos_p, sin_p = cos_sin(jnp.arange(seq, dtype=F32))
    cos_s, sin_s = cos_sin((PAST_LEN + jnp.arange(1)).astype(F32))
    log_gamma = jnp.log1p(-jnp.exp2(-5.0 - jnp.arange(RET_HEADS, dtype=F32)))
    idx = jnp.arange(CHUNK, dtype=F32)
    diff = idx[:, None] - idx[None, :]
    causal = diff >= 0
    lg = log_gamma[:, None, None]
    dintra = jnp.where(causal, jnp.exp(lg * jnp.where(causal, diff, 0.0)), 0.0)
    dq = jnp.exp(log_gamma[:, None] * (idx + 1.0))
    dk = jnp.exp(log_gamma[:, None] * (CHUNK - 1.0 - idx))

    def pad_rows(t, rows):
        return jnp.pad(t, ((0, rows - t.shape[0]), (0, 0)))

    def chunk_decay(c):
        return pad_rows(jnp.broadcast_to(jnp.exp(log_gamma * c)[:, None], (RET_HEADS, LANES)), SUBLANES)

    return {
        "cos": cos_p, "sin": sin_p,
        "cos_s": jnp.broadcast_to(cos_s, (SUBLANES, RET_DIM)),
        "sin_s": jnp.broadcast_to(sin_s, (SUBLANES, RET_DIM)),
        "dintra": dintra,
        "dq": jnp.pad(dq.T, ((0, 0), (0, LANES - RET_HEADS))),
        "dk": pad_rows(dk, SUBLANES),
        "dch": chunk_decay(float(CHUNK)),
        "dch_s": chunk_decay(1.0),
    }


def _pad_last(t, width):
    return jnp.pad(t, [(0, 0)] * (t.ndim - 1) + [(0, width - t.shape[-1])])


def _prep_gains(l, g_ret_norm, g_mlstm_norm, b_mlstm_i, b_mlstm_f, w_pool, pool_scale):
    bi, bf = b_mlstm_i[l], b_mlstm_f[l]
    rows = lambda t: jnp.broadcast_to(_pad_last(t, SUBLANES)[:, None], (SUBLANES, LANES))
    lanes = jnp.concatenate([jnp.zeros((GATE_I_LANE,), F32), _pad_last(bi, SUBLANES),
                             _pad_last(bf, LANES - GATE_F_LANE)])
    return {
        "g_ret": g_ret_norm[l][None, :],
        "g_ml": _pad_last(g_mlstm_norm[l].reshape(ML_HEADS, ML_V), ML_V_PAD).reshape(1, -1),
        "bi_r": rows(bi), "bf_r": rows(bf),
        "gb_c": jnp.broadcast_to(lanes[None, :], (SUBLANES, LANES)),
        "w_pool": w_pool[l].astype(BF16),
        "pool_scale": pool_scale[l][None, :],
    }


@jax.jit
def kernel(x_prompt, x_sample, state_ret, state_mlstm_c, state_mlstm_n, state_mlstm_m, state_pool,
           w_in, w_out, g_ret_norm, g_mlstm_norm, b_mlstm_i, b_mlstm_f, w_pool, pool_scale,
           g_pre_mix, g_post_mix, g_pre_ffn, g_post_ffn, w_ffn_gate, w_ffn_up, w_ffn_down):
    batch, seq, d = x_prompt.shape
    nb = x_sample.shape[0]
    depth = w_in.shape[0]
    tables = _tables(seq)
    tables_s = dict(tables, dch=tables["dch_s"])

    yp = x_prompt.reshape(batch * seq, d)
    ys = x_sample.reshape(nb, d)
    new_p, new_s = [], []
    stacked_s = ()

    def token_ops(l, x, w_in_l, w_out_l, mixer):
        t = _token_tiles(x.shape[0])
        row = lambda g: g[l][None, :]
        proj = _inproj(x, row(g_pre_mix), w_in_l, tm=t["inproj_tm"], tn=t["inproj_tn"])
        mixed, *states = mixer(proj)
        x1 = _outproj(mixed, w_out_l, x, row(g_post_mix), tm=t["outproj_tm"], tk=t["outproj_tk"])
        y = _ffn(x1, row(g_pre_ffn), w_ffn_gate, w_ffn_up, w_ffn_down, row(g_post_ffn), l,
                 tm=t["ffn_tm"], tf=t["ffn_tf"])
        return y, states

    for l in range(depth):
        w_in_l = _prep_w_in(w_in, l)
        w_out_l = _prep_w_out(w_out, l)
        gains = _prep_gains(l, g_ret_norm, g_mlstm_norm, b_mlstm_i, b_mlstm_f, w_pool, pool_scale)

        yp, (s_ret, s_c, s_n, s_m, s_pool) = token_ops(
            l, yp, w_in_l, w_out_l, lambda proj: _prompt_mixer(proj, tables, gains, batch, seq))
        new_p.append((s_ret, s_c[:, :, :ML_QK, :ML_V], s_n[:, :ML_HEADS, :ML_QK],
                      s_m[:, :ML_HEADS, 0], s_pool[:, 1:, :]))

        n_in = _pad_last(state_mlstm_n[l], ML_QK_PAD).reshape(nb, ML_HEADS * ML_QK_PAD)
        pool_in = state_pool[l].reshape(nb, POOL_BUF * POOL_WIDTH)
        ys, (s_ret, s_c, s_n, s_m, s_pool) = token_ops(
            l, ys, w_in_l, w_out_l,
            lambda proj: _decode_mixer(proj, tables_s, gains, l, state_ret, state_mlstm_c, n_in,
                                       state_mlstm_m[l], pool_in, stacked_s))
        stacked_s = (s_ret, s_c)
        new_s.append((s_n.reshape(nb, ML_HEADS, ML_QK_PAD)[:, :, :ML_QK], s_m,
                      s_pool.reshape(nb, POOL_BUF, POOL_WIDTH)))

    stack = lambda states, i: jnp.stack([s[i] for s in states], axis=0)
    return ((yp.reshape(batch, seq, d), ys.reshape(nb, 1, d))
            + tuple(stack(new_p, i) for i in range(5))
            + stacked_s + tuple(stack(new_s, i) for i in range(3)))
```

```python
import functools

import jax
import jax.numpy as jnp
from jax import lax
from jax.experimental import pallas as pl
from jax.experimental.pallas import tpu as pltpu

F32 = jnp.float32
BF16 = jnp.bfloat16

D_MODEL = 2048
PAST_LEN = 16384
RET_HEADS = 6
RET_DIM = 128
RET_WIDTH = RET_HEADS * RET_DIM
ML_HEADS = 4
ML_QK = 96
ML_V = 192
ML_WIDTH = ML_HEADS * ML_V
POOL_WIDTH = 512
POOL_WINDOWS = (2, 4, 8, 16)
POOL_GROUP = POOL_WIDTH // len(POOL_WINDOWS)
POOL_BUF = max(POOL_WINDOWS) - 1
D_FF = 5632
CHUNK = 128
ROPE_THETA = 10000.0
EPS = 1e-6

LANES = 128
SUBLANES = 8
VMEM_LIMIT = 56 * 1024 * 1024

ML_QK_PAD = LANES
ML_V_PAD = 2 * LANES
OFF_RQ = 0
OFF_RK = OFF_RQ + RET_WIDTH
OFF_RV = OFF_RK + RET_WIDTH
OFF_RG = OFF_RV + RET_WIDTH
OFF_MQ = OFF_RG + RET_WIDTH
OFF_MK = OFF_MQ + ML_HEADS * ML_QK_PAD
OFF_MV = OFF_MK + ML_HEADS * ML_QK_PAD
OFF_MO = OFF_MV + ML_HEADS * ML_V_PAD
OFF_PU = OFF_MO + ML_HEADS * ML_V_PAD
PROJ_PAD = OFF_PU + POOL_WIDTH
GATE_I_LANE = ML_QK
GATE_F_LANE = ML_QK + ML_HEADS
MIX_OFF_ML = RET_WIDTH
MIX_OFF_POOL = MIX_OFF_ML + ML_HEADS * ML_V_PAD
MIX_PAD = MIX_OFF_POOL + POOL_WIDTH

_HI = lax.Precision.HIGHEST


def _token_tiles(m):
    return {
        "inproj_tm": min(m, 1024), "inproj_tn": 512,
        "outproj_tm": min(m, 512), "outproj_tk": 768,
        "ffn_tm": min(m, 1024), "ffn_tf": 256,
    }


def _dot(a, b):
    return jnp.dot(a, b, preferred_element_type=F32)


def _rms(x, width=None):
    n = x.shape[-1] if width is None else width
    return x * lax.rsqrt(jnp.sum(x * x, axis=-1, keepdims=True) / n + EPS)


def _log_sigmoid(x):
    return jnp.minimum(x, 0.0) - jnp.log1p(jnp.exp(-jnp.abs(x)))


def _rotate(x, cos2, sin2):
    return x * cos2 + pltpu.roll(x, RET_DIM // 2, 1) * sin2


def _inproj_kernel(x_ref, g_ref, w_ref, o_ref, xn_ref):
    @pl.when(pl.program_id(1) == 0)
    def _():
        xn_ref[...] = (_rms(x_ref[...]) * g_ref[...]).astype(BF16)

    o_ref[...] = lax.dot_general(xn_ref[...], w_ref[...], (((1,), (1,)), ((), ())),
                                 preferred_element_type=F32)


def _inproj(x, g, w_t, tm, tn):
    m = x.shape[0]
    return pl.pallas_call(
        _inproj_kernel,
        grid=(m // tm, PROJ_PAD // tn),
        in_specs=[
            pl.BlockSpec((tm, D_MODEL), lambda i, j: (i, 0)),
            pl.BlockSpec((1, D_MODEL), lambda i, j: (0, 0)),
            pl.BlockSpec((tn, D_MODEL), lambda i, j: (j, 0)),
        ],
        out_specs=pl.BlockSpec((tm, tn), lambda i, j: (i, j)),
        out_shape=jax.ShapeDtypeStruct((m, PROJ_PAD), F32),
        scratch_shapes=[pltpu.VMEM((tm, D_MODEL), BF16)],
        compiler_params=pltpu.CompilerParams(
            dimension_semantics=("parallel", "arbitrary"), vmem_limit_bytes=VMEM_LIMIT),
        name="inproj",
    )(x, g, w_t)


def _outproj_kernel(m_ref, w_ref, x_ref, gpost_ref, x1_ref):
    k = pl.program_id(1)

    @pl.when(k == 0)
    def _():
        x1_ref[...] = jnp.zeros_like(x1_ref)

    x1_ref[...] += _dot(m_ref[...], w_ref[...])

    @pl.when(k == pl.num_programs(1) - 1)
    def _():
        x1_ref[...] = x_ref[...] + _rms(x1_ref[...]) * gpost_ref[...]


def _outproj(mixed, w, x, g_post, tm, tk):
    m = x.shape[0]
    return pl.pallas_call(
        _outproj_kernel,
        grid=(m // tm, MIX_PAD // tk),
        in_specs=[
            pl.BlockSpec((tm, tk), lambda i, k: (i, k)),
            pl.BlockSpec((tk, D_MODEL), lambda i, k: (k, 0)),
            pl.BlockSpec((tm, D_MODEL), lambda i, k: (i, 0)),
            pl.BlockSpec((1, D_MODEL), lambda i, k: (0, 0)),
        ],
        out_specs=pl.BlockSpec((tm, D_MODEL), lambda i, k: (i, 0)),
        out_shape=jax.ShapeDtypeStruct((m, D_MODEL), F32),
        compiler_params=pltpu.CompilerParams(
            dimension_semantics=("parallel", "arbitrary"), vmem_limit_bytes=VMEM_LIMIT),
        name="outproj",
    )(mixed, w, x, g_post)


def _ffn_kernel(x1_ref, gpre_ref, wg_ref, wu_ref, wd_ref, gpost_ref, o_ref, hn_ref):
    f = pl.program_id(1)

    @pl.when(f == 0)
    def _():
        hn_ref[...] = (_rms(x1_ref[...]) * gpre_ref[...]).astype(BF16)
        o_ref[...] = jnp.zeros_like(o_ref)

    hn = hn_ref[...]
    gate = _dot(hn, wg_ref[...].astype(BF16))
    up = _dot(hn, wu_ref[...].astype(BF16))
    act = (gate * jax.nn.sigmoid(gate) * up).astype(BF16)
    o_ref[...] += _dot(act, wd_ref[...].astype(BF16))

    @pl.when(f == pl.num_programs(1) - 1)
    def _():
        o_ref[...] = x1_ref[...] + _rms(o_ref[...]) * gpost_ref[...]


def _ffn(x1, g_pre, wg, wu, wd, g_post, layer, tm, tf):
    m = x1.shape[0]
    return pl.pallas_call(
        _ffn_kernel,
        grid=(m // tm, D_FF // tf),
        in_specs=[
            pl.BlockSpec((tm, D_MODEL), lambda i, f: (i, 0), pipeline_mode=pl.Buffered(1)),
            pl.BlockSpec((1, D_MODEL), lambda i, f: (0, 0)),
            pl.BlockSpec((None, D_MODEL, tf), lambda i, f: (layer, 0, f)),
            pl.BlockSpec((None, D_MODEL, tf), lambda i, f: (layer, 0, f)),
            pl.BlockSpec((None, tf, D_MODEL), lambda i, f: (layer, f, 0)),
            pl.BlockSpec((1, D_MODEL), lambda i, f: (0, 0)),
        ],
        out_specs=pl.BlockSpec((tm, D_MODEL), lambda i, f: (i, 0)),
        out_shape=jax.ShapeDtypeStruct((m, D_MODEL), F32),
        scratch_shapes=[pltpu.VMEM((tm, D_MODEL), BF16)],
        compiler_params=pltpu.CompilerParams(
            dimension_semantics=("parallel", "arbitrary"), vmem_limit_bytes=VMEM_LIMIT),
        name="ffn",
    )(x1, g_pre, wg, wu, wd, g_post)


SRC_MQ = 4 * RET_WIDTH
SRC_MK = SRC_MQ + ML_HEADS * ML_QK
SRC_MV = SRC_MK + ML_HEADS * ML_QK
SRC_MO = SRC_MV + ML_WIDTH
SRC_MI = SRC_MO + ML_WIDTH
SRC_MF = SRC_MI + ML_HEADS
SRC_PU = SRC_MF + ML_HEADS
IN_WIDTH = SRC_PU + POOL_WIDTH


def _prep_w_in_kernel(w_ref, o_ref):
    def put(dst, src, rows):
        o_ref[dst:dst + rows, :] = w_ref[src:src + rows, :].astype(BF16)

    def zero(dst, rows):
        o_ref[dst:dst + rows, :] = jnp.zeros((rows, o_ref.shape[1]), BF16)

    put(0, 0, 4 * RET_WIDTH)
    for h in range(ML_HEADS):
        put(OFF_MQ + h * ML_QK_PAD, SRC_MQ + h * ML_QK, ML_QK)
        put(OFF_MK + h * ML_QK_PAD, SRC_MK + h * ML_QK, ML_QK)
        put(OFF_MV + h * ML_V_PAD, SRC_MV + h * ML_V, ML_V)
        put(OFF_MO + h * ML_V_PAD, SRC_MO + h * ML_V, ML_V)
        zero(OFF_MQ + h * ML_QK_PAD + ML_QK, ML_QK_PAD - ML_QK)
        zero(OFF_MK + h * ML_QK_PAD + ML_QK, ML_QK_PAD - ML_QK)
        zero(OFF_MV + h * ML_V_PAD + ML_V, ML_V_PAD - ML_V)
        zero(OFF_MO + h * ML_V_PAD + ML_V, ML_V_PAD - ML_V)
    gates = jnp.concatenate(
        [w_ref[SRC_MI:SRC_MI + 2 * ML_HEADS, :],
         jnp.zeros((ML_QK_PAD - ML_QK - 2 * ML_HEADS, w_ref.shape[1]), F32)], axis=0)
    o_ref[OFF_MQ + GATE_I_LANE:OFF_MQ + ML_QK_PAD, :] = gates.astype(BF16)
    put(OFF_PU, SRC_PU, POOL_WIDTH)


def _prep_w_in(w_t, layer, tk=256):
    d = w_t.shape[2]
    return pl.pallas_call(
        _prep_w_in_kernel,
        grid=(d // tk,),
        in_specs=[pl.BlockSpec((None, IN_WIDTH, tk), lambda i: (layer, 0, i))],
        out_specs=pl.BlockSpec((PROJ_PAD, tk), lambda i: (0, i)),
        out_shape=jax.ShapeDtypeStruct((PROJ_PAD, d), BF16),
        compiler_params=pltpu.CompilerParams(
            dimension_semantics=("parallel",), vmem_limit_bytes=VMEM_LIMIT),
        name="prep_w_in",
    )(w_t)


def _prep_w_out_kernel(w_ref, o_ref):
    def put(dst, src, rows):
        o_ref[dst:dst + rows, :] = w_ref[src:src + rows, :].astype(BF16)

    put(0, 0, RET_WIDTH)
    for h in range(ML_HEADS):
        dst = MIX_OFF_ML + h * ML_V_PAD
        put(dst, RET_WIDTH + h * ML_V, ML_V)
        o_ref[dst + ML_V:dst + ML_V_PAD, :] = jnp.zeros((ML_V_PAD - ML_V, o_ref.shape[1]), BF16)
    put(MIX_OFF_POOL, RET_WIDTH + ML_WIDTH, POOL_WIDTH)


def _prep_w_out(w, layer, tn=512):
    _, k, n = w.shape
    return pl.pallas_call(
        _prep_w_out_kernel,
        grid=(n // tn,),
        in_specs=[pl.BlockSpec((None, k, tn), lambda j: (layer, 0, j))],
        out_specs=pl.BlockSpec((MIX_PAD, tn), lambda j: (0, j)),
        out_shape=jax.ShapeDtypeStruct((MIX_PAD, n), BF16),
        compiler_params=pltpu.CompilerParams(
            dimension_semantics=("parallel",), vmem_limit_bytes=VMEM_LIMIT),
        name="prep_w_out",
    )(w)


def _prompt_mixer_kernel(p_ref, cos_ref, sin_ref, dintra_ref, dq_ref, dk_ref, dch_ref,
                         gret_ref, gml_ref, gb_r_ref, gb_c_ref, wpool_ref, pscale_ref,
                         mix_ref, sret_ref, sc_ref, sn_ref, sm_ref, spool_ref,
                         s_scr, c_scr, n_scr, m_scr, u_scr):
    b = pl.program_id(0)
    c = pl.program_id(1)

    @pl.when(c == 0)
    def _():
        s_scr[...] = jnp.zeros_like(s_scr)
        c_scr[...] = jnp.zeros_like(c_scr)
        n_scr[...] = jnp.zeros_like(n_scr)
        m_scr[...] = jnp.zeros_like(m_scr)
        u_scr[0:2 * SUBLANES, :] = jnp.zeros((2 * SUBLANES, POOL_WIDTH), F32)

    cos2 = cos_ref[...]
    sin2 = sin_ref[...]
    for h in range(RET_HEADS):
        lo, hi = h * RET_DIM, (h + 1) * RET_DIM
        q = p_ref[:, OFF_RQ + lo:OFF_RQ + hi]
        k = p_ref[:, OFF_RK + lo:OFF_RK + hi]
        vb = p_ref[:, OFF_RV + lo:OFF_RV + hi].astype(BF16)
        g = p_ref[:, OFF_RG + lo:OFF_RG + hi]
        qb = _rotate(q, cos2, sin2).astype(BF16)
        kr_t = (_rotate(k, cos2, sin2) * (RET_DIM ** -0.5)).T
        scores = _dot(qb, kr_t.astype(BF16)) * dintra_ref[h]
        s_old = s_scr[h]
        o = _dot(scores.astype(BF16), vb) + _dot(qb, s_old.astype(BF16)) * dq_ref[:, h:h + 1]
        s_scr[h] = s_old * dch_ref[h:h + 1, :] + _dot((kr_t * dk_ref[h:h + 1, :]).astype(BF16), vb)
        out = _rms(o) * gret_ref[:, lo:hi] * (g * jax.nn.sigmoid(g))
        mix_ref[:, lo:hi] = out.astype(BF16)

    row = lax.broadcasted_iota(jnp.int32, (CHUNK, CHUNK), 0)
    col = lax.broadcasted_iota(jnp.int32, (CHUNK, CHUNK), 1)
    causal = col <= row
    gq0 = p_ref[:, OFF_MQ:OFF_MQ + ML_QK_PAD]
    g_t = gq0.T
    i_r8 = g_t[GATE_I_LANE:GATE_I_LANE + SUBLANES, :] + gb_r_ref[...]
    b_r8 = jnp.dot(_log_sigmoid(i_r8), (row <= col).astype(F32), precision=_HI,
                   preferred_element_type=F32)
    pre_c = gq0 + gb_c_ref[0:1, :]
    b_c_all = jnp.dot(causal.astype(F32), _log_sigmoid(pre_c), precision=_HI,
                      preferred_element_type=F32)
    for h in range(ML_HEADS):
        q = p_ref[:, OFF_MQ + h * ML_QK_PAD:OFF_MQ + (h + 1) * ML_QK_PAD]
        k = p_ref[:, OFF_MK + h * ML_QK_PAD:OFF_MK + (h + 1) * ML_QK_PAD] * (ML_QK ** -0.5)
        vb = p_ref[:, OFF_MV + h * ML_V_PAD:OFF_MV + (h + 1) * ML_V_PAD].astype(BF16)
        og = p_ref[:, OFF_MO + h * ML_V_PAD:OFF_MO + (h + 1) * ML_V_PAD]
        k_t = k.T
        qb = q.astype(BF16)
        b_c = b_c_all[:, GATE_F_LANE + h:GATE_F_LANE + h + 1]
        i_c = pre_c[:, GATE_I_LANE + h:GATE_I_LANE + h + 1]
        b_r = b_r8[ML_HEADS + h:ML_HEADS + h + 1, :]
        i_r = i_r8[h:h + 1, :]
        logw = jnp.where(causal, b_c - b_r + i_r, -jnp.inf)
        m_prev = m_scr[h:h + 1, 0:1]
        inter = b_c + m_prev
        m_t = jnp.maximum(inter, jnp.max(logw, axis=-1, keepdims=True))
        w = jnp.exp(logw - m_t)
        a = jnp.exp(inter - m_t)
        s = _dot(qb, k_t.astype(BF16)) * w
        c_old = c_scr[h]
        n_old = n_scr[h:h + 1, :]
        num = _dot(s.astype(BF16), vb) + a * _dot(qb, c_old.astype(BF16))
        den = (jnp.sum(s, axis=-1, keepdims=True)
               + a * jnp.sum(q * n_old, axis=-1, keepdims=True))
        hh = num / jnp.maximum(jnp.abs(den), jnp.exp(-m_t))
        m_new = m_t[CHUNK - 1:CHUNK, :]
        b_last = b_c[CHUNK - 1:CHUNK, :]
        w_r = jnp.exp(b_last - b_r + i_r - m_new)
        w_c = jnp.exp(b_last - b_c + i_c - m_new)
        a_end = jnp.exp(b_last + m_prev - m_new)
        c_scr[h] = a_end * c_old + _dot((k_t * w_r).astype(BF16), vb)
        n_scr[h:h + 1, :] = a_end * n_old + jnp.sum(k * w_c, axis=0, keepdims=True)
        m_scr[h:h + 1, :] = jnp.broadcast_to(m_new, (1, LANES))
        out = (_rms(hh, ML_V) * gml_ref[:, h * ML_V_PAD:(h + 1) * ML_V_PAD]
               * jax.nn.sigmoid(og))
        mix_ref[:, MIX_OFF_ML + h * ML_V_PAD:MIX_OFF_ML + (h + 1) * ML_V_PAD] = out.astype(BF16)

    base = 2 * SUBLANES
    u = p_ref[:, OFF_PU:OFF_PU + POOL_WIDTH]
    u_scr[base:base + CHUNK, :] = u
    pos = (c * CHUNK + lax.broadcasted_iota(jnp.int32, (CHUNK, 1), 0)).astype(F32)
    for g, win in enumerate(POOL_WINDOWS):
        lo, hi = g * POOL_GROUP, (g + 1) * POOL_GROUP
        wsum = u[:, lo:hi]
        for s in range(1, win):
            wsum = wsum + u_scr[base - s:base - s + CHUNK, lo:hi]
        z = wsum / jnp.minimum(pos + 1.0, float(win)) - u[:, lo:hi]
        y = _dot(z.astype(BF16), wpool_ref[g]) * pscale_ref[:, lo:hi]
        mix_ref[:, MIX_OFF_POOL + lo:MIX_OFF_POOL + hi] = y.astype(BF16)
    u_scr[0:base, :] = u_scr[CHUNK:CHUNK + base, :]

    @pl.when(c == pl.num_programs(1) - 1)
    def _():
        sret_ref[b] = s_scr[...]
        sc_ref[b] = c_scr[...]
        sn_ref[b] = n_scr[...]
        sm_ref[b] = m_scr[...]
        spool_ref[b] = u_scr[0:base, :]


def _prompt_mixer(proj, tables, gains, batch, seq):
    nchunk = seq // CHUNK
    const2 = lambda b, c: (0, 0)
    const3 = lambda b, c: (0, 0, 0)
    const4 = lambda b, c: (0, 0, 0, 0)
    in_specs = [
        pl.BlockSpec((CHUNK, PROJ_PAD), lambda b, c: (b * nchunk + c, 0)),
        pl.BlockSpec((CHUNK, RET_DIM), lambda b, c: (c, 0)),
        pl.BlockSpec((CHUNK, RET_DIM), lambda b, c: (c, 0)),
        pl.BlockSpec((RET_HEADS, CHUNK, CHUNK), const3),
        pl.BlockSpec((CHUNK, LANES), const2),
        pl.BlockSpec((SUBLANES, CHUNK), const2),
        pl.BlockSpec((SUBLANES, LANES), const2),
        pl.BlockSpec((1, RET_WIDTH), const2),
        pl.BlockSpec((1, ML_HEADS * ML_V_PAD), const2),
        pl.BlockSpec((SUBLANES, LANES), const2),
        pl.BlockSpec((SUBLANES, LANES), const2),
        pl.BlockSpec((len(POOL_WINDOWS), POOL_GROUP, POOL_GROUP), const3),
        pl.BlockSpec((1, POOL_WIDTH), const2),
    ]
    out_shape = [
        jax.ShapeDtypeStruct((batch * seq, MIX_PAD), BF16),
        jax.ShapeDtypeStruct((batch, RET_HEADS, RET_DIM, RET_DIM), F32),
        jax.ShapeDtypeStruct((batch, ML_HEADS, ML_QK_PAD, ML_V_PAD), F32),
        jax.ShapeDtypeStruct((batch, SUBLANES, LANES), F32),
        jax.ShapeDtypeStruct((batch, SUBLANES, LANES), F32),
        jax.ShapeDtypeStruct((batch, 2 * SUBLANES, POOL_WIDTH), F32),
    ]
    out_specs = [
        pl.BlockSpec((CHUNK, MIX_PAD), lambda b, c: (b * nchunk + c, 0)),
        pl.BlockSpec((batch, RET_HEADS, RET_DIM, RET_DIM), const4),
        pl.BlockSpec((batch, ML_HEADS, ML_QK_PAD, ML_V_PAD), const4),
        pl.BlockSpec((batch, SUBLANES, LANES), const3),
        pl.BlockSpec((batch, SUBLANES, LANES), const3),
        pl.BlockSpec((batch, 2 * SUBLANES, POOL_WIDTH), const3),
    ]
    scratch = [
        pltpu.VMEM((RET_HEADS, RET_DIM, RET_DIM), F32),
        pltpu.VMEM((ML_HEADS, ML_QK_PAD, ML_V_PAD), F32),
        pltpu.VMEM((SUBLANES, LANES), F32),
        pltpu.VMEM((SUBLANES, LANES), F32),
        pltpu.VMEM((2 * SUBLANES + CHUNK, POOL_WIDTH), F32),
    ]
    return pl.pallas_call(
        _prompt_mixer_kernel,
        grid=(batch, nchunk),
        in_specs=in_specs,
        out_specs=out_specs,
        out_shape=out_shape,
        scratch_shapes=scratch,
        compiler_params=pltpu.CompilerParams(
            dimension_semantics=("arbitrary", "arbitrary"), vmem_limit_bytes=VMEM_LIMIT),
        name="prompt_mixer",
    )(proj, tables["cos"], tables["sin"], tables["dintra"], tables["dq"], tables["dk"],
      tables["dch"], gains["g_ret"], gains["g_ml"], gains["gb_r"], gains["gb_c"],
      gains["w_pool"], gains["pool_scale"])


DEC_BLOCK = 8


DEC_INPUTS = 14


def _decode_mixer_kernel(*refs, n_aliased):
    (p_ref, cos_ref, sin_ref, dch_ref, gret_ref, gml_ref, gb_c_ref, wpool_ref, pscale_ref,
     s_ref, c_ref, n_ref, m_ref, pool_ref) = refs[:DEC_INPUTS]
    (mix_ref, so_ref, co_ref, no_ref, mo_ref, poolo_ref,
     qr_scr, kt_scr, v_scr, qs_scr, qm_scr, wkt_scr, vm_scr, qc_scr, a_scr,
     cpad_scr) = refs[DEC_INPUTS + n_aliased:]
    if n_aliased == 0:
        for later in range(1, so_ref.shape[0]):
            so_ref[later] = jnp.zeros(so_ref.shape[1:], F32)
            co_ref[later] = jnp.zeros(co_ref.shape[1:], F32)
        so_ref, co_ref = so_ref.at[0], co_ref.at[0]
    step = pl.program_id(0)
    nb = p_ref.shape[0]
    cos2 = cos_ref[0:1, :]
    sin2 = sin_ref[0:1, :]

    def ret_qkv(h):
        lo, hi = h * RET_DIM, (h + 1) * RET_DIM
        qr = _rotate(p_ref[:, OFF_RQ + lo:OFF_RQ + hi], cos2, sin2)
        kr = _rotate(p_ref[:, OFF_RK + lo:OFF_RK + hi], cos2, sin2) * (RET_DIM ** -0.5)
        return qr, kr, p_ref[:, OFF_RV + lo:OFF_RV + hi]

    def ml_gates(h):
        pre_c = p_ref[:, OFF_MQ:OFF_MQ + ML_QK_PAD] + gb_c_ref[0:1, :]
        i_pre = pre_c[:, GATE_I_LANE + h:GATE_I_LANE + h + 1]
        lf = _log_sigmoid(pre_c[:, GATE_F_LANE + h:GATE_F_LANE + h + 1])
        inter = lf + m_ref[:, h:h + 1]
        m_t = jnp.maximum(inter, i_pre)
        return jnp.exp(i_pre - m_t), jnp.exp(inter - m_t), m_t

    def ml_qkv(h):
        q = p_ref[:, OFF_MQ + h * ML_QK_PAD:OFF_MQ + (h + 1) * ML_QK_PAD]
        k = p_ref[:, OFF_MK + h * ML_QK_PAD:OFF_MK + (h + 1) * ML_QK_PAD] * (ML_QK ** -0.5)
        return q, k, p_ref[:, OFF_MV + h * ML_V_PAD:OFF_MV + (h + 1) * ML_V_PAD]

    @pl.when(step == 0)
    def _():
        for h in range(RET_HEADS):
            qr, kr, v = ret_qkv(h)
            qr_scr[h] = qr
            kt_scr[h] = kr.T
            v_scr[h] = v.astype(BF16)
        for h in range(ML_HEADS):
            w, a, _ = ml_gates(h)
            q, k, v = ml_qkv(h)
            qm_scr[h] = q
            wkt_scr[h] = (w * k).T
            vm_scr[h] = v.astype(BF16)
            a_scr[h] = jnp.broadcast_to(a, (nb, ML_V_PAD))
        cpad_scr[...] = jnp.zeros_like(cpad_scr)

    lane = lax.broadcasted_iota(jnp.int32, (LANES, nb), 1)

    def body(bb, carry):
        b = step * DEC_BLOCK + bb
        sel = lane == b
        for h in range(RET_HEADS):
            s_old = s_ref[bb, h]
            q16 = jnp.broadcast_to(qr_scr[h, pl.ds(b, 1), :], (2 * SUBLANES, RET_DIM)).astype(BF16)
            qs_scr[h, pl.ds(b, 1), :] = _dot(q16, s_old.astype(BF16))[0:1, :]
            k_m = jnp.where(sel, kt_scr[h], 0.0).astype(BF16)
            so_ref[bb, h] = s_old * dch_ref[h:h + 1, :] + _dot(k_m, v_scr[h])
        for h in range(ML_HEADS):
            cpad_scr[0:ML_QK, 0:ML_V] = c_ref[bb, h]
            c_old = cpad_scr[...]
            q16 = jnp.broadcast_to(qm_scr[h, pl.ds(b, 1), :], (2 * SUBLANES, ML_QK_PAD)).astype(BF16)
            qc_scr[h, pl.ds(b, 1), :] = _dot(q16, c_old.astype(BF16))[0:1, :]
            wk_m = jnp.where(sel, wkt_scr[h], 0.0).astype(BF16)
            c_new = a_scr[h, pl.ds(b, 1), :] * c_old + _dot(wk_m, vm_scr[h])
            co_ref[bb, h] = c_new[0:ML_QK, 0:ML_V]
        return carry

    lax.fori_loop(0, DEC_BLOCK, body, 0)

    @pl.when(step == pl.num_programs(0) - 1)
    def _():
        for h in range(RET_HEADS):
            lo, hi = h * RET_DIM, (h + 1) * RET_DIM
            qr, kr, v = ret_qkv(h)
            g = p_ref[:, OFF_RG + lo:OFF_RG + hi]
            o = jnp.sum(qr * kr, axis=-1, keepdims=True) * v + qs_scr[h] * dch_ref[h:h + 1, :]
            out = _rms(o) * gret_ref[:, lo:hi] * (g * jax.nn.sigmoid(g))
            mix_ref[:, lo:hi] = out.astype(BF16)
        for h in range(ML_HEADS):
            w, a, m_t = ml_gates(h)
            q, k, v = ml_qkv(h)
            og = p_ref[:, OFF_MO + h * ML_V_PAD:OFF_MO + (h + 1) * ML_V_PAD]
            n_old = n_ref[:, h * ML_QK_PAD:(h + 1) * ML_QK_PAD]
            s = jnp.sum(q * k, axis=-1, keepdims=True) * w
            num = s * v + a * qc_scr[h]
            den = s + a * jnp.sum(q * n_old, axis=-1, keepdims=True)
            hh = num / jnp.maximum(jnp.abs(den), jnp.exp(-m_t))
            out = (_rms(hh, ML_V) * gml_ref[:, h * ML_V_PAD:(h + 1) * ML_V_PAD]
                   * jax.nn.sigmoid(og))
            mix_ref[:, MIX_OFF_ML + h * ML_V_PAD:MIX_OFF_ML + (h + 1) * ML_V_PAD] = out.astype(BF16)
            no_ref[:, h * ML_QK_PAD:(h + 1) * ML_QK_PAD] = a * n_old + w * k
            mo_ref[:, h:h + 1] = m_t
        u = p_ref[:, OFF_PU:OFF_PU + POOL_WIDTH]
        for g, win in enumerate(POOL_WINDOWS):
            lo, hi = g * POOL_GROUP, (g + 1) * POOL_GROUP
            wsum = u[:, lo:hi]
            for s in range(1, win):
                r = (POOL_BUF - s) * POOL_WIDTH
                wsum = wsum + pool_ref[:, r + lo:r + hi]
            z = wsum / float(win) - u[:, lo:hi]
            y = _dot(z.astype(BF16), wpool_ref[g]) * pscale_ref[:, lo:hi]
            mix_ref[:, MIX_OFF_POOL + lo:MIX_OFF_POOL + hi] = y.astype(BF16)
        keep = (POOL_BUF - 1) * POOL_WIDTH
        poolo_ref[:, 0:keep] = pool_ref[:, POOL_WIDTH:POOL_WIDTH + keep]
        poolo_ref[:, keep:keep + POOL_WIDTH] = u


def _decode_mixer(proj, tables, gains, layer, s_ret, s_c, s_n, s_m, s_pool, stacked_prev):
    nb = proj.shape[0]
    const2 = lambda s: (0, 0)
    const3 = lambda s: (0, 0, 0)
    blk5 = lambda s: (layer, s, 0, 0, 0)
    full2 = lambda shape: pl.BlockSpec(shape, const2)
    s_blk = pl.BlockSpec((None, DEC_BLOCK, RET_HEADS, RET_DIM, RET_DIM), blk5)
    c_blk = pl.BlockSpec((None, DEC_BLOCK, ML_HEADS, ML_QK, ML_V), blk5)
    in_specs = [
        full2((nb, PROJ_PAD)),
        full2((SUBLANES, RET_DIM)),
        full2((SUBLANES, RET_DIM)),
        full2((SUBLANES, LANES)),
        full2((1, RET_WIDTH)),
        full2((1, ML_HEADS * ML_V_PAD)),
        full2((SUBLANES, LANES)),
        pl.BlockSpec((len(POOL_WINDOWS), POOL_GROUP, POOL_GROUP), const3),
        full2((1, POOL_WIDTH)),
        s_blk,
        c_blk,
        full2((nb, ML_HEADS * ML_QK_PAD)),
        full2((nb, ML_HEADS)),
        full2((nb, POOL_BUF * POOL_WIDTH)),
    ]
    assert len(in_specs) == DEC_INPUTS
    in_specs += [pl.BlockSpec(memory_space=pl.ANY)] * len(stacked_prev)
    out_shape = [
        jax.ShapeDtypeStruct((nb, MIX_PAD), BF16),
        jax.ShapeDtypeStruct(s_ret.shape, F32),
        jax.ShapeDtypeStruct(s_c.shape, F32),
        jax.ShapeDtypeStruct((nb, ML_HEADS * ML_QK_PAD), F32),
        jax.ShapeDtypeStruct((nb, ML_HEADS), F32),
        jax.ShapeDtypeStruct((nb, POOL_BUF * POOL_WIDTH), F32),
    ]
    if stacked_prev:
        so_blk, co_blk = s_blk, c_blk
    else:
        assert layer == 0
        all_layers = lambda s: (0, s, 0, 0, 0)
        so_blk = pl.BlockSpec((s_ret.shape[0],) + s_blk.block_shape[1:], all_layers)
        co_blk = pl.BlockSpec((s_c.shape[0],) + c_blk.block_shape[1:], all_layers)
    out_specs = [
        full2((nb, MIX_PAD)),
        so_blk,
        co_blk,
        full2((nb, ML_HEADS * ML_QK_PAD)),
        full2((nb, ML_HEADS)),
        full2((nb, POOL_BUF * POOL_WIDTH)),
    ]
    scratch = [
        pltpu.VMEM((RET_HEADS, nb, RET_DIM), F32),
        pltpu.VMEM((RET_HEADS, RET_DIM, nb), F32),
        pltpu.VMEM((RET_HEADS, nb, RET_DIM), BF16),
        pltpu.VMEM((RET_HEADS, nb, RET_DIM), F32),
        pltpu.VMEM((ML_HEADS, nb, ML_QK_PAD), F32),
        pltpu.VMEM((ML_HEADS, ML_QK_PAD, nb), F32),
        pltpu.VMEM((ML_HEADS, nb, ML_V_PAD), BF16),
        pltpu.VMEM((ML_HEADS, nb, ML_V_PAD), F32),
        pltpu.VMEM((ML_HEADS, nb, ML_V_PAD), F32),
        pltpu.VMEM((ML_QK_PAD, ML_V_PAD), F32),
    ]
    return pl.pallas_call(
        functools.partial(_decode_mixer_kernel, n_aliased=len(stacked_prev)),
        grid=(nb // DEC_BLOCK,),
        in_specs=in_specs,
        out_specs=out_specs,
        out_shape=out_shape,
        scratch_shapes=scratch,
        input_output_aliases={DEC_INPUTS + i: 1 + i for i in range(len(stacked_prev))},
        compiler_params=pltpu.CompilerParams(
            dimension_semantics=("arbitrary",), vmem_limit_bytes=VMEM_LIMIT),
        name="decode_mixer",
    )(proj, tables["cos_s"], tables["sin_s"], tables["dch"], gains["g_ret"], gains["g_ml"],
      gains["gb_c"], gains["w_pool"], gains["pool_scale"], s_ret, s_c, s_n, s_m, s_pool,
      *stacked_prev)


def _tables(seq):
    half = RET_DIM // 2
    inv_freq = ROPE_THETA ** (-jnp.arange(half, dtype=F32) / half)

    def cos_sin(pos):
        ang = pos[:, None] * inv_freq[None, :]
        cos, sin = jnp.cos(ang), jnp.sin(ang)
        return jnp.concatenate([cos, cos], axis=-1), jnp.concatenate([-sin, sin], axis=-1)

    cos_p, sin_p = cos_sin(jnp.arange(seq, dtype=F32))
    cos_s, sin_s = cos_sin((PAST_LEN + jnp.arange(1)).astype(F32))
    log_gamma = jnp.log1p(-jnp.exp2(-5.0 - jnp.arange(RET_HEADS, dtype=F32)))
    idx = jnp.arange(CHUNK, dtype=F32)
    diff = idx[:, None] - idx[None, :]
    causal = diff >= 0
    lg = log_gamma[:, None, None]
    dintra = jnp.where(causal, jnp.exp(lg * jnp.where(causal, diff, 0.0)), 0.0)
    dq = jnp.exp(log_gamma[:, None] * (idx + 1.0))
    dk = jnp.exp(log_gamma[:, None] * (CHUNK - 1.0 - idx))

    def pad_rows(t, rows):
        return jnp.pad(t, ((0, rows - t.shape[0]), (0, 0)))

    def chunk_decay(c):
        return pad_rows(jnp.broadcast_to(jnp.exp(log_gamma * c)[:, None], (RET_HEADS, LANES)), SUBLANES)

    return {
        "cos": cos_p, "sin": sin_p,
        "cos_s": jnp.broadcast_to(cos_s, (SUBLANES, RET_DIM)),
        "sin_s": jnp.broadcast_to(sin_s, (SUBLANES, RET_DIM)),
        "dintra": dintra,
        "dq": jnp.pad(dq.T, ((0, 0), (0, LANES - RET_HEADS))),
        "dk": pad_rows(dk, SUBLANES),
        "dch": chunk_decay(float(CHUNK)),
        "dch_s": chunk_decay(1.0),
    }


def _pad_last(t, width):
    return jnp.pad(t, [(0, 0)] * (t.ndim - 1) + [(0, width - t.shape[-1])])


def _prep_gains(l, g_ret_norm, g_mlstm_norm, b_mlstm_i, b_mlstm_f, w_pool, pool_scale):
    gate_bias = jnp.concatenate([b_mlstm_i[l], b_mlstm_f[l]])
    lanes = jnp.concatenate([jnp.zeros((GATE_I_LANE,), F32), _pad_last(gate_bias, LANES - GATE_I_LANE)])
    return {
        "g_ret": g_ret_norm[l][None, :],
        "g_ml": _pad_last(g_mlstm_norm[l].reshape(ML_HEADS, ML_V), ML_V_PAD).reshape(1, -1),
        "gb_r": jnp.broadcast_to(gate_bias[:, None], (SUBLANES, LANES)),
        "gb_c": jnp.broadcast_to(lanes[None, :], (SUBLANES, LANES)),
        "w_pool": w_pool[l].astype(BF16),
        "pool_scale": pool_scale[l][None, :],
    }


@jax.jit
def kernel(x_prompt, x_sample, state_ret, state_mlstm_c, state_mlstm_n, state_mlstm_m, state_pool,
           w_in, w_out, g_ret_norm, g_mlstm_norm, b_mlstm_i, b_mlstm_f, w_pool, pool_scale,
           g_pre_mix, g_post_mix, g_pre_ffn, g_post_ffn, w_ffn_gate, w_ffn_up, w_ffn_down):
    batch, seq, d = x_prompt.shape
    nb = x_sample.shape[0]
    depth = w_in.shape[0]
    tables = _tables(seq)
    tables_s = dict(tables, dch=tables["dch_s"])
    w_in_t = jnp.swapaxes(w_in, 1, 2)

    yp = x_prompt.reshape(batch * seq, d)
    ys = x_sample.reshape(nb, d)
    new_p, new_s = [], []
    stacked_s = ()

    def token_ops(l, x, w_in_l, w_out_l, mixer):
        t = _token_tiles(x.shape[0])
        row = lambda g: g[l][None, :]
        proj = _inproj(x, row(g_pre_mix), w_in_l, tm=t["inproj_tm"], tn=t["inproj_tn"])
        mixed, *states = mixer(proj)
        x1 = _outproj(mixed, w_out_l, x, row(g_post_mix), tm=t["outproj_tm"], tk=t["outproj_tk"])
        y = _ffn(x1, row(g_pre_ffn), w_ffn_gate, w_ffn_up, w_ffn_down, row(g_post_ffn), l,
                 tm=t["ffn_tm"], tf=t["ffn_tf"])
        return y, states

    for l in range(depth):
        w_in_l = _prep_w_in(w_in_t, l)
        w_out_l = _prep_w_out(w_out, l)
        gains = _prep_gains(l, g_ret_norm, g_mlstm_norm, b_mlstm_i, b_mlstm_f, w_pool, pool_scale)

        yp, (s_ret, s_c, s_n, s_m, s_pool) = token_ops(
            l, yp, w_in_l, w_out_l, lambda proj: _prompt_mixer(proj, tables, gains, batch, seq))
        new_p.append((s_ret, s_c[:, :, :ML_QK, :ML_V], s_n[:, :ML_HEADS, :ML_QK],
                      s_m[:, :ML_HEADS, 0], s_pool[:, 1:, :]))

        n_in = _pad_last(state_mlstm_n[l], ML_QK_PAD).reshape(nb, ML_HEADS * ML_QK_PAD)
        pool_in = state_pool[l].reshape(nb, POOL_BUF * POOL_WIDTH)
        ys, (s_ret, s_c, s_n, s_m, s_pool) = token_ops(
            l, ys, w_in_l, w_out_l,
            lambda proj: _decode_mixer(proj, tables_s, gains, l, state_ret, state_mlstm_c, n_in,
                                       state_mlstm_m[l], pool_in, stacked_s))
        stacked_s = (s_ret, s_c)
        new_s.append((s_n.reshape(nb, ML_HEADS, ML_QK_PAD)[:, :, :ML_QK], s_m,
                      s_pool.reshape(nb, POOL_BUF, POOL_WIDTH)))

    stack = lambda states, i: jnp.stack([s[i] for s in states], axis=0)
    return ((yp.reshape(batch, seq, d), ys.reshape(nb, 1, d))
            + tuple(stack(new_p, i) for i in range(5))
            + stacked_s + tuple(stack(new_s, i) for i in range(3)))
```

```python
import functools

import jax
import jax.numpy as jnp
from jax import lax
from jax.experimental import pallas as pl
from jax.experimental.pallas import tpu as pltpu

F32 = jnp.float32
BF16 = jnp.bfloat16

D_MODEL = 2048
PAST_LEN = 16384
RET_HEADS = 6
RET_DIM = 128
RET_WIDTH = RET_HEADS * RET_DIM
ML_HEADS = 4
ML_QK = 96
ML_V = 192
ML_WIDTH = ML_HEADS * ML_V
POOL_WIDTH = 512
POOL_WINDOWS = (2, 4, 8, 16)
POOL_GROUP = POOL_WIDTH // len(POOL_WINDOWS)
POOL_BUF = max(POOL_WINDOWS) - 1
D_FF = 5632
CHUNK = 128
ROPE_THETA = 10000.0
EPS = 1e-6

LANES = 128
SUBLANES = 8
VMEM_LIMIT = 56 * 1024 * 1024

ML_QK_PAD = LANES
ML_V_PAD = 2 * LANES
OFF_RQ = 0
OFF_RK = OFF_RQ + RET_WIDTH
OFF_RV = OFF_RK + RET_WIDTH
OFF_RG = OFF_RV + RET_WIDTH
OFF_MQ = OFF_RG + RET_WIDTH
OFF_MK = OFF_MQ + ML_HEADS * ML_QK_PAD
OFF_MV = OFF_MK + ML_HEADS * ML_QK_PAD
OFF_MO = OFF_MV + ML_HEADS * ML_V_PAD
OFF_PU = OFF_MO + ML_HEADS * ML_V_PAD
PROJ_PAD = OFF_PU + POOL_WIDTH
GATE_I_LANE = ML_QK
GATE_F_LANE = ML_QK + ML_HEADS
MIX_OFF_ML = RET_WIDTH
MIX_OFF_POOL = MIX_OFF_ML + ML_HEADS * ML_V_PAD
MIX_PAD = MIX_OFF_POOL + POOL_WIDTH

_HI = lax.Precision.HIGHEST


def _token_tiles(m):
    return {
        "inproj_tm": min(m, 1024), "inproj_tn": 512,
        "outproj_tm": min(m, 512), "outproj_tk": 768,
        "ffn_tm": min(m, 1024), "ffn_tf": 256,
    }


def _dot(a, b):
    return jnp.dot(a, b, preferred_element_type=F32)


def _rms(x, width=None):
    n = x.shape[-1] if width is None else width
    return x * lax.rsqrt(jnp.sum(x * x, axis=-1, keepdims=True) / n + EPS)


def _log_sigmoid(x):
    return jnp.minimum(x, 0.0) - jnp.log1p(jnp.exp(-jnp.abs(x)))


def _rotate(x, cos2, sin2):
    return x * cos2 + pltpu.roll(x, RET_DIM // 2, 1) * sin2


def _inproj_kernel(x_ref, g_ref, w_ref, o_ref, xn_ref):
    @pl.when(pl.program_id(1) == 0)
    def _():
        xn_ref[...] = (_rms(x_ref[...]) * g_ref[...]).astype(BF16)

    o_ref[...] = _dot(xn_ref[...], w_ref[...])


def _inproj(x, g, w, tm, tn):
    m = x.shape[0]
    return pl.pallas_call(
        _inproj_kernel,
        grid=(m // tm, PROJ_PAD // tn),
        in_specs=[
            pl.BlockSpec((tm, D_MODEL), lambda i, j: (i, 0)),
            pl.BlockSpec((1, D_MODEL), lambda i, j: (0, 0)),
            pl.BlockSpec((D_MODEL, tn), lambda i, j: (0, j)),
        ],
        out_specs=pl.BlockSpec((tm, tn), lambda i, j: (i, j)),
        out_shape=jax.ShapeDtypeStruct((m, PROJ_PAD), F32),
        scratch_shapes=[pltpu.VMEM((tm, D_MODEL), BF16)],
        compiler_params=pltpu.CompilerParams(
            dimension_semantics=("parallel", "arbitrary"), vmem_limit_bytes=VMEM_LIMIT),
        name="inproj",
    )(x, g, w)


def _outproj_kernel(m_ref, w_ref, x_ref, gpost_ref, x1_ref):
    k = pl.program_id(1)

    @pl.when(k == 0)
    def _():
        x1_ref[...] = jnp.zeros_like(x1_ref)

    x1_ref[...] += _dot(m_ref[...], w_ref[...])

    @pl.when(k == pl.num_programs(1) - 1)
    def _():
        x1_ref[...] = x_ref[...] + _rms(x1_ref[...]) * gpost_ref[...]


def _outproj(mixed, w, x, g_post, tm, tk):
    m = x.shape[0]
    return pl.pallas_call(
        _outproj_kernel,
        grid=(m // tm, MIX_PAD // tk),
        in_specs=[
            pl.BlockSpec((tm, tk), lambda i, k: (i, k)),
            pl.BlockSpec((tk, D_MODEL), lambda i, k: (k, 0)),
            pl.BlockSpec((tm, D_MODEL), lambda i, k: (i, 0)),
            pl.BlockSpec((1, D_MODEL), lambda i, k: (0, 0)),
        ],
        out_specs=pl.BlockSpec((tm, D_MODEL), lambda i, k: (i, 0)),
        out_shape=jax.ShapeDtypeStruct((m, D_MODEL), F32),
        compiler_params=pltpu.CompilerParams(
            dimension_semantics=("parallel", "arbitrary"), vmem_limit_bytes=VMEM_LIMIT),
        name="outproj",
    )(mixed, w, x, g_post)


def _ffn_kernel(x1_ref, gpre_ref, wg_ref, wu_ref, wd_ref, gpost_ref, o_ref, hn_ref):
    f = pl.program_id(1)

    @pl.when(f == 0)
    def _():
        hn_ref[...] = (_rms(x1_ref[...]) * gpre_ref[...]).astype(BF16)
        o_ref[...] = jnp.zeros_like(o_ref)

    hn = hn_ref[...]
    gate = _dot(hn, wg_ref[...].astype(BF16))
    up = _dot(hn, wu_ref[...].astype(BF16))
    act = (gate * jax.nn.sigmoid(gate) * up).astype(BF16)
    o_ref[...] += _dot(act, wd_ref[...].astype(BF16))

    @pl.when(f == pl.num_programs(1) - 1)
    def _():
        o_ref[...] = x1_ref[...] + _rms(o_ref[...]) * gpost_ref[...]


def _ffn(x1, g_pre, wg, wu, wd, g_post, layer, tm, tf):
    m = x1.shape[0]
    return pl.pallas_call(
        _ffn_kernel,
        grid=(m // tm, D_FF // tf),
        in_specs=[
            pl.BlockSpec((tm, D_MODEL), lambda i, f: (i, 0), pipeline_mode=pl.Buffered(1)),
            pl.BlockSpec((1, D_MODEL), lambda i, f: (0, 0)),
            pl.BlockSpec((None, D_MODEL, tf), lambda i, f: (layer, 0, f)),
            pl.BlockSpec((None, D_MODEL, tf), lambda i, f: (layer, 0, f)),
            pl.BlockSpec((None, tf, D_MODEL), lambda i, f: (layer, f, 0)),
            pl.BlockSpec((1, D_MODEL), lambda i, f: (0, 0)),
        ],
        out_specs=pl.BlockSpec((tm, D_MODEL), lambda i, f: (i, 0)),
        out_shape=jax.ShapeDtypeStruct((m, D_MODEL), F32),
        scratch_shapes=[pltpu.VMEM((tm, D_MODEL), BF16)],
        compiler_params=pltpu.CompilerParams(
            dimension_semantics=("parallel", "arbitrary"), vmem_limit_bytes=VMEM_LIMIT),
        name="ffn",
    )(x1, g_pre, wg, wu, wd, g_post)


SRC_MQ = 4 * RET_WIDTH
SRC_MK = SRC_MQ + ML_HEADS * ML_QK
SRC_MV = SRC_MK + ML_HEADS * ML_QK
SRC_MO = SRC_MV + ML_WIDTH
SRC_MI = SRC_MO + ML_WIDTH
SRC_MF = SRC_MI + ML_HEADS
SRC_PU = SRC_MF + ML_HEADS
IN_WIDTH = SRC_PU + POOL_WIDTH


def _prep_w_in_kernel(w_ref, o_ref):
    tk = w_ref.shape[1]

    def put(dst, pieces, width):
        rows = [w_ref[src:src + n, :] for src, n in pieces]
        have = sum(n for _, n in pieces)
        if have < width:
            rows.append(jnp.zeros((width - have, tk), F32))
        block = rows[0] if len(rows) == 1 else jnp.concatenate(rows, axis=0)
        o_ref[:, dst:dst + width] = block.T.astype(BF16)

    for r in range(0, 4 * RET_WIDTH, 2 * LANES):
        put(r, [(r, 2 * LANES)], 2 * LANES)
    for h in range(ML_HEADS):
        q_rows = [(SRC_MQ + h * ML_QK, ML_QK)]
        if h == 0:
            q_rows.append((SRC_MI, 2 * ML_HEADS))
        put(OFF_MQ + h * ML_QK_PAD, q_rows, ML_QK_PAD)
        put(OFF_MK + h * ML_QK_PAD, [(SRC_MK + h * ML_QK, ML_QK)], ML_QK_PAD)
        put(OFF_MV + h * ML_V_PAD, [(SRC_MV + h * ML_V, ML_V)], ML_V_PAD)
        put(OFF_MO + h * ML_V_PAD, [(SRC_MO + h * ML_V, ML_V)], ML_V_PAD)
    for r in range(0, POOL_WIDTH, 2 * LANES):
        put(OFF_PU + r, [(SRC_PU + r, 2 * LANES)], 2 * LANES)


def _prep_w_in(w_t, layer, tk=256):
    d = w_t.shape[2]
    return pl.pallas_call(
        _prep_w_in_kernel,
        grid=(d // tk,),
        in_specs=[pl.BlockSpec((None, IN_WIDTH, tk), lambda i: (layer, 0, i))],
        out_specs=pl.BlockSpec((tk, PROJ_PAD), lambda i: (i, 0)),
        out_shape=jax.ShapeDtypeStruct((d, PROJ_PAD), BF16),
        compiler_params=pltpu.CompilerParams(
            dimension_semantics=("parallel",), vmem_limit_bytes=VMEM_LIMIT),
        name="prep_w_in",
    )(w_t)


def _prep_w_out_kernel(w_ref, o_ref):
    def put(dst, src, rows):
        o_ref[dst:dst + rows, :] = w_ref[src:src + rows, :].astype(BF16)

    put(0, 0, RET_WIDTH)
    for h in range(ML_HEADS):
        dst = MIX_OFF_ML + h * ML_V_PAD
        put(dst, RET_WIDTH + h * ML_V, ML_V)
        o_ref[dst + ML_V:dst + ML_V_PAD, :] = jnp.zeros((ML_V_PAD - ML_V, o_ref.shape[1]), BF16)
    put(MIX_OFF_POOL, RET_WIDTH + ML_WIDTH, POOL_WIDTH)


def _prep_w_out(w, layer, tn=512):
    _, k, n = w.shape
    return pl.pallas_call(
        _prep_w_out_kernel,
        grid=(n // tn,),
        in_specs=[pl.BlockSpec((None, k, tn), lambda j: (layer, 0, j))],
        out_specs=pl.BlockSpec((MIX_PAD, tn), lambda j: (0, j)),
        out_shape=jax.ShapeDtypeStruct((MIX_PAD, n), BF16),
        compiler_params=pltpu.CompilerParams(
            dimension_semantics=("parallel",), vmem_limit_bytes=VMEM_LIMIT),
        name="prep_w_out",
    )(w)


PROMPT_BLOCK = 2


def _ret_head(q, k, vb, g, cos2, sin2, dintra, dq_col, dk_row, dch_row, g_norm, s_old):
    qb = _rotate(q, cos2, sin2).astype(BF16)
    kr_t = (_rotate(k, cos2, sin2) * (RET_DIM ** -0.5)).T
    scores = _dot(qb, kr_t.astype(BF16)) * dintra
    o = _dot(scores.astype(BF16), vb) + _dot(qb, s_old.astype(BF16)) * dq_col
    s_new = s_old * dch_row + _dot((kr_t * dk_row).astype(BF16), vb)
    return _rms(o) * g_norm * (g * jax.nn.sigmoid(g)), s_new


def _ml_head(q, k, vb, og, causal, b_c, i_c, b_r, i_r, g_norm, c_old, n_old, m_prev):
    k_t = k.T
    qb = q.astype(BF16)
    logw = jnp.where(causal, b_c - b_r + i_r, -jnp.inf)
    inter = b_c + m_prev
    m_t = jnp.maximum(inter, jnp.max(logw, axis=-1, keepdims=True))
    w = jnp.exp(logw - m_t)
    a = jnp.exp(inter - m_t)
    s = _dot(qb, k_t.astype(BF16)) * w
    num = _dot(s.astype(BF16), vb) + a * _dot(qb, c_old.astype(BF16))
    den = jnp.sum(s, axis=-1, keepdims=True) + a * jnp.sum(q * n_old, axis=-1, keepdims=True)
    hh = num / jnp.maximum(jnp.abs(den), jnp.exp(-m_t))
    m_new = m_t[CHUNK - 1:CHUNK, :]
    b_last = b_c[CHUNK - 1:CHUNK, :]
    w_r = jnp.exp(b_last - b_r + i_r - m_new)
    w_c = jnp.exp(b_last - b_c + i_c - m_new)
    a_end = jnp.exp(b_last + m_prev - m_new)
    c_new = a_end * c_old + _dot((k_t * w_r).astype(BF16), vb)
    n_new = a_end * n_old + jnp.sum(k * w_c, axis=0, keepdims=True)
    out = _rms(hh, ML_V) * g_norm * jax.nn.sigmoid(og)
    return out, c_new, n_new, m_new


def _prompt_front_kernel(x_ref, gpre_ref, w_ref, cos_ref, sin_ref, dintra_ref, dq_ref, dk_ref,
                         dch_ref, gret_ref, gml_ref, gb_r_ref, gb_c_ref, wpool_ref, pscale_ref,
                         mix_ref, sret_ref, sc_ref, sn_ref, sm_ref, spool_ref,
                         s_scr, c_scr, n_scr, m_scr, u_scr):
    bp = pl.program_id(0)
    c = pl.program_id(1)
    nseq = x_ref.shape[0]

    @pl.when(c == 0)
    def _():
        s_scr[...] = jnp.zeros_like(s_scr)
        c_scr[...] = jnp.zeros_like(c_scr)
        n_scr[...] = jnp.zeros_like(n_scr)
        m_scr[...] = jnp.zeros_like(m_scr)
        u_scr[:, 0:2 * SUBLANES, :] = jnp.zeros((nseq, 2 * SUBLANES, POOL_WIDTH), F32)

    xn = (_rms(x_ref[...].reshape(nseq * CHUNK, D_MODEL)) * gpre_ref[...]).astype(BF16)

    def proj(lo, width):
        return _dot(xn, w_ref[:, lo:lo + width])

    def seq_rows(t, i):
        return t[i * CHUNK:(i + 1) * CHUNK]

    cos2 = cos_ref[...]
    sin2 = sin_ref[...]
    pair = 2 * RET_DIM
    for h0 in range(0, RET_HEADS, 2):
        pq, pk, pv, pg = (proj(off + h0 * RET_DIM, pair) for off in (OFF_RQ, OFF_RK, OFF_RV, OFF_RG))
        for h in (h0, h0 + 1):
            lo, hi = (h - h0) * RET_DIM, (h - h0 + 1) * RET_DIM
            for i in range(nseq):
                out, s_new = _ret_head(
                    seq_rows(pq, i)[:, lo:hi], seq_rows(pk, i)[:, lo:hi],
                    seq_rows(pv, i)[:, lo:hi].astype(BF16), seq_rows(pg, i)[:, lo:hi],
                    cos2, sin2, dintra_ref[h], dq_ref[:, h:h + 1], dk_ref[h:h + 1, :],
                    dch_ref[h:h + 1, :], gret_ref[:, h * RET_DIM:(h + 1) * RET_DIM], s_scr[i, h])
                s_scr[i, h] = s_new
                mix_ref[i, :, h * RET_DIM:(h + 1) * RET_DIM] = out.astype(BF16)

    row = lax.broadcasted_iota(jnp.int32, (CHUNK, CHUNK), 0)
    col = lax.broadcasted_iota(jnp.int32, (CHUNK, CHUNK), 1)
    causal = col <= row
    upper = (row <= col).astype(F32)
    lower = causal.astype(F32)
    gates = [None] * nseq
    for h0 in range(0, ML_HEADS, 2):
        pq = proj(OFF_MQ + h0 * ML_QK_PAD, 2 * ML_QK_PAD)
        pk = proj(OFF_MK + h0 * ML_QK_PAD, 2 * ML_QK_PAD)
        pv = proj(OFF_MV + h0 * ML_V_PAD, 2 * ML_V_PAD)
        po = proj(OFF_MO + h0 * ML_V_PAD, 2 * ML_V_PAD)
        if h0 == 0:
            for i in range(nseq):
                gq0 = seq_rows(pq, i)[:, 0:ML_QK_PAD]
                i_r8 = gq0.T[GATE_I_LANE:GATE_I_LANE + SUBLANES, :] + gb_r_ref[...]
                b_r8 = jnp.dot(_log_sigmoid(i_r8), upper, precision=_HI, preferred_element_type=F32)
                pre_c = gq0 + gb_c_ref[0:1, :]
                b_c_all = jnp.dot(lower, _log_sigmoid(pre_c), precision=_HI,
                                  preferred_element_type=F32)
                gates[i] = (i_r8, b_r8, pre_c, b_c_all)
        for h in (h0, h0 + 1):
            qlo = (h - h0) * ML_QK_PAD
            vlo = (h - h0) * ML_V_PAD
            for i in range(nseq):
                i_r8, b_r8, pre_c, b_c_all = gates[i]
                out, c_new, n_new, m_new = _ml_head(
                    seq_rows(pq, i)[:, qlo:qlo + ML_QK_PAD],
                    seq_rows(pk, i)[:, qlo:qlo + ML_QK_PAD] * (ML_QK ** -0.5),
                    seq_rows(pv, i)[:, vlo:vlo + ML_V_PAD].astype(BF16),
                    seq_rows(po, i)[:, vlo:vlo + ML_V_PAD],
                    causal,
                    b_c_all[:, GATE_F_LANE + h:GATE_F_LANE + h + 1],
                    pre_c[:, GATE_I_LANE + h:GATE_I_LANE + h + 1],
                    b_r8[ML_HEADS + h:ML_HEADS + h + 1, :], i_r8[h:h + 1, :],
                    gml_ref[:, h * ML_V_PAD:(h + 1) * ML_V_PAD],
                    c_scr[i, h], n_scr[i, h:h + 1, :], m_scr[i, h:h + 1, 0:1])
                c_scr[i, h] = c_new
                n_scr[i, h:h + 1, :] = n_new
                m_scr[i, h:h + 1, :] = jnp.broadcast_to(m_new, (1, LANES))
                mix_ref[i, :, MIX_OFF_ML + h * ML_V_PAD:MIX_OFF_ML + (h + 1) * ML_V_PAD] = (
                    out.astype(BF16))

    base = 2 * SUBLANES
    pu = proj(OFF_PU, POOL_WIDTH)
    pos = (c * CHUNK + lax.broadcasted_iota(jnp.int32, (CHUNK, 1), 0)).astype(F32)
    for i in range(nseq):
        u = seq_rows(pu, i)
        u_scr[i, base:base + CHUNK, :] = u
        for g, win in enumerate(POOL_WINDOWS):
            lo, hi = g * POOL_GROUP, (g + 1) * POOL_GROUP
            wsum = u[:, lo:hi]
            for s in range(1, win):
                wsum = wsum + u_scr[i, base - s:base - s + CHUNK, lo:hi]
            z = wsum / jnp.minimum(pos + 1.0, float(win)) - u[:, lo:hi]
            y = _dot(z.astype(BF16), wpool_ref[g]) * pscale_ref[:, lo:hi]
            mix_ref[i, :, MIX_OFF_POOL + lo:MIX_OFF_POOL + hi] = y.astype(BF16)
        u_scr[i, 0:base, :] = u_scr[i, CHUNK:CHUNK + base, :]

    @pl.when(c == pl.num_programs(1) - 1)
    def _():
        for i in range(nseq):
            b = bp * nseq + i
            sret_ref[b] = s_scr[i]
            sc_ref[b] = c_scr[i]
            sn_ref[b] = n_scr[i]
            sm_ref[b] = m_scr[i]
            spool_ref[b] = u_scr[i, 0:base, :]


def _prompt_front(x, g_pre, w, tables, gains):
    batch, seq, _ = x.shape
    nchunk = seq // CHUNK
    nseq = PROMPT_BLOCK
    const2 = lambda b, c: (0, 0)
    const3 = lambda b, c: (0, 0, 0)
    const4 = lambda b, c: (0, 0, 0, 0)
    in_specs = [
        pl.BlockSpec((nseq, CHUNK, D_MODEL), lambda b, c: (b, c, 0)),
        pl.BlockSpec((1, D_MODEL), const2),
        pl.BlockSpec((D_MODEL, PROJ_PAD), const2, pipeline_mode=pl.Buffered(1)),
        pl.BlockSpec((CHUNK, RET_DIM), lambda b, c: (c, 0)),
        pl.BlockSpec((CHUNK, RET_DIM), lambda b, c: (c, 0)),
        pl.BlockSpec((RET_HEADS, CHUNK, CHUNK), const3),
        pl.BlockSpec((CHUNK, LANES), const2),
        pl.BlockSpec((SUBLANES, CHUNK), const2),
        pl.BlockSpec((SUBLANES, LANES), const2),
        pl.BlockSpec((1, RET_WIDTH), const2),
        pl.BlockSpec((1, ML_HEADS * ML_V_PAD), const2),
        pl.BlockSpec((SUBLANES, LANES), const2),
        pl.BlockSpec((SUBLANES, LANES), const2),
        pl.BlockSpec((len(POOL_WINDOWS), POOL_GROUP, POOL_GROUP), const3),
        pl.BlockSpec((1, POOL_WIDTH), const2),
    ]
    out_shape = [
        jax.ShapeDtypeStruct((batch, seq, MIX_PAD), BF16),
        jax.ShapeDtypeStruct((batch, RET_HEADS, RET_DIM, RET_DIM), F32),
        jax.ShapeDtypeStruct((batch, ML_HEADS, ML_QK_PAD, ML_V_PAD), F32),
        jax.ShapeDtypeStruct((batch, SUBLANES, LANES), F32),
        jax.ShapeDtypeStruct((batch, SUBLANES, LANES), F32),
        jax.ShapeDtypeStruct((batch, 2 * SUBLANES, POOL_WIDTH), F32),
    ]
    out_specs = [
        pl.BlockSpec((nseq, CHUNK, MIX_PAD), lambda b, c: (b, c, 0)),
        pl.BlockSpec((batch, RET_HEADS, RET_DIM, RET_DIM), const4),
        pl.BlockSpec((batch, ML_HEADS, ML_QK_PAD, ML_V_PAD), const4),
        pl.BlockSpec((batch, SUBLANES, LANES), const3),
        pl.BlockSpec((batch, SUBLANES, LANES), const3),
        pl.BlockSpec((batch, 2 * SUBLANES, POOL_WIDTH), const3),
    ]
    scratch = [
        pltpu.VMEM((nseq, RET_HEADS, RET_DIM, RET_DIM), F32),
        pltpu.VMEM((nseq, ML_HEADS, ML_QK_PAD, ML_V_PAD), F32),
        pltpu.VMEM((nseq, SUBLANES, LANES), F32),
        pltpu.VMEM((nseq, SUBLANES, LANES), F32),
        pltpu.VMEM((nseq, 2 * SUBLANES + CHUNK, POOL_WIDTH), F32),
    ]
    mixed, *states = pl.pallas_call(
        _prompt_front_kernel,
        grid=(batch // nseq, nchunk),
        in_specs=in_specs,
        out_specs=out_specs,
        out_shape=out_shape,
        scratch_shapes=scratch,
        compiler_params=pltpu.CompilerParams(
            dimension_semantics=("arbitrary", "arbitrary"), vmem_limit_bytes=VMEM_LIMIT),
        name="prompt_front",
    )(x, g_pre, w, tables["cos"], tables["sin"], tables["dintra"], tables["dq"], tables["dk"],
      tables["dch"], gains["g_ret"], gains["g_ml"], gains["gb_r"], gains["gb_c"],
      gains["w_pool"], gains["pool_scale"])
    return (mixed.reshape(batch * seq, MIX_PAD), *states)


DEC_BLOCK = 8
DEC_INPUTS = 14


def _decode_mixer_kernel(*refs, n_aliased):
    (p_ref, cos_ref, sin_ref, dch_ref, gret_ref, gml_ref, gb_c_ref, wpool_ref, pscale_ref,
     s_ref, c_ref, n_ref, m_ref, pool_ref) = refs[:DEC_INPUTS]
    (mix_ref, so_ref, co_ref, no_ref, mo_ref, poolo_ref,
     qr_scr, kt_scr, v_scr, qs_scr, qm_scr, wkt_scr, vm_scr, qc_scr, a_scr,
     cpad_scr) = refs[DEC_INPUTS + n_aliased:]
    if n_aliased == 0:
        for later in range(1, so_ref.shape[0]):
            so_ref[later] = jnp.zeros(so_ref.shape[1:], F32)
            co_ref[later] = jnp.zeros(co_ref.shape[1:], F32)
        so_ref, co_ref = so_ref.at[0], co_ref.at[0]
    step = pl.program_id(0)
    nb = p_ref.shape[0]
    cos2 = cos_ref[0:1, :]
    sin2 = sin_ref[0:1, :]

    def ret_qkv(h):
        lo, hi = h * RET_DIM, (h + 1) * RET_DIM
        qr = _rotate(p_ref[:, OFF_RQ + lo:OFF_RQ + hi], cos2, sin2)
        kr = _rotate(p_ref[:, OFF_RK + lo:OFF_RK + hi], cos2, sin2) * (RET_DIM ** -0.5)
        return qr, kr, p_ref[:, OFF_RV + lo:OFF_RV + hi]

    def ml_gates(h):
        pre_c = p_ref[:, OFF_MQ:OFF_MQ + ML_QK_PAD] + gb_c_ref[0:1, :]
        i_pre = pre_c[:, GATE_I_LANE + h:GATE_I_LANE + h + 1]
        lf = _log_sigmoid(pre_c[:, GATE_F_LANE + h:GATE_F_LANE + h + 1])
        inter = lf + m_ref[:, h:h + 1]
        m_t = jnp.maximum(inter, i_pre)
        return jnp.exp(i_pre - m_t), jnp.exp(inter - m_t), m_t

    def ml_qkv(h):
        q = p_ref[:, OFF_MQ + h * ML_QK_PAD:OFF_MQ + (h + 1) * ML_QK_PAD]
        k = p_ref[:, OFF_MK + h * ML_QK_PAD:OFF_MK + (h + 1) * ML_QK_PAD] * (ML_QK ** -0.5)
        return q, k, p_ref[:, OFF_MV + h * ML_V_PAD:OFF_MV + (h + 1) * ML_V_PAD]

    @pl.when(step == 0)
    def _():
        for h in range(RET_HEADS):
            qr, kr, v = ret_qkv(h)
            qr_scr[h] = qr
            kt_scr[h] = kr.T
            v_scr[h] = v.astype(BF16)
        for h in range(ML_HEADS):
            w, a, _ = ml_gates(h)
            q, k, v = ml_qkv(h)
            qm_scr[h] = q
            wkt_scr[h] = (w * k).T
            vm_scr[h] = v.astype(BF16)
            a_scr[h] = jnp.broadcast_to(a, (nb, ML_V_PAD))
        cpad_scr[...] = jnp.zeros_like(cpad_scr)

    lane = lax.broadcasted_iota(jnp.int32, (LANES, nb), 1)

    def body(bb, carry):
        b = step * DEC_BLOCK + bb
        sel = lane == b
        for h in range(RET_HEADS):
            s_old = s_ref[bb, h]
            q16 = jnp.broadcast_to(qr_scr[h, pl.ds(b, 1), :], (2 * SUBLANES, RET_DIM)).astype(BF16)
            qs_scr[h, pl.ds(b, 1), :] = _dot(q16, s_old.astype(BF16))[0:1, :]
            k_m = jnp.where(sel, kt_scr[h], 0.0).astype(BF16)
            so_ref[bb, h] = s_old * dch_ref[h:h + 1, :] + _dot(k_m, v_scr[h])
        for h in range(ML_HEADS):
            cpad_scr[0:ML_QK, 0:ML_V] = c_ref[bb, h]
            c_old = cpad_scr[...]
            q16 = jnp.broadcast_to(qm_scr[h, pl.ds(b, 1), :], (2 * SUBLANES, ML_QK_PAD)).astype(BF16)
            qc_scr[h, pl.ds(b, 1), :] = _dot(q16, c_old.astype(BF16))[0:1, :]
            wk_m = jnp.where(sel, wkt_scr[h], 0.0).astype(BF16)
            c_new = a_scr[h, pl.ds(b, 1), :] * c_old + _dot(wk_m, vm_scr[h])
            co_ref[bb, h] = c_new[0:ML_QK, 0:ML_V]
        return carry

    lax.fori_loop(0, DEC_BLOCK, body, 0)

    @pl.when(step == pl.num_programs(0) - 1)
    def _():
        for h in range(RET_HEADS):
            lo, hi = h * RET_DIM, (h + 1) * RET_DIM
            qr, kr, v = ret_qkv(h)
            g = p_ref[:, OFF_RG + lo:OFF_RG + hi]
            o = jnp.sum(qr * kr, axis=-1, keepdims=True) * v + qs_scr[h] * dch_ref[h:h + 1, :]
            out = _rms(o) * gret_ref[:, lo:hi] * (g * jax.nn.sigmoid(g))
            mix_ref[:, lo:hi] = out.astype(BF16)
        for h in range(ML_HEADS):
            w, a, m_t = ml_gates(h)
            q, k, v = ml_qkv(h)
            og = p_ref[:, OFF_MO + h * ML_V_PAD:OFF_MO + (h + 1) * ML_V_PAD]
            n_old = n_ref[:, h * ML_QK_PAD:(h + 1) * ML_QK_PAD]
            s = jnp.sum(q * k, axis=-1, keepdims=True) * w
            num = s * v + a * qc_scr[h]
            den = s + a * jnp.sum(q * n_old, axis=-1, keepdims=True)
            hh = num / jnp.maximum(jnp.abs(den), jnp.exp(-m_t))
            out = (_rms(hh, ML_V) * gml_ref[:, h * ML_V_PAD:(h + 1) * ML_V_PAD]
                   * jax.nn.sigmoid(og))
            mix_ref[:, MIX_OFF_ML + h * ML_V_PAD:MIX_OFF_ML + (h + 1) * ML_V_PAD] = out.astype(BF16)
            no_ref[:, h * ML_QK_PAD:(h + 1) * ML_QK_PAD] = a * n_old + w * k
            mo_ref[:, h:h + 1] = m_t
        u = p_ref[:, OFF_PU:OFF_PU + POOL_WIDTH]
        for g, win in enumerate(POOL_WINDOWS):
            lo, hi = g * POOL_GROUP, (g + 1) * POOL_GROUP
            wsum = u[:, lo:hi]
            for s in range(1, win):
                r = (POOL_BUF - s) * POOL_WIDTH
                wsum = wsum + pool_ref[:, r + lo:r + hi]
            z = wsum / float(win) - u[:, lo:hi]
            y = _dot(z.astype(BF16), wpool_ref[g]) * pscale_ref[:, lo:hi]
            mix_ref[:, MIX_OFF_POOL + lo:MIX_OFF_POOL + hi] = y.astype(BF16)
        keep = (POOL_BUF - 1) * POOL_WIDTH
        poolo_ref[:, 0:keep] = pool_ref[:, POOL_WIDTH:POOL_WIDTH + keep]
        poolo_ref[:, keep:keep + POOL_WIDTH] = u


def _decode_mixer(proj, tables, gains, layer, s_ret, s_c, s_n, s_m, s_pool, stacked_prev):
    nb = proj.shape[0]
    const2 = lambda s: (0, 0)
    const3 = lambda s: (0, 0, 0)
    blk5 = lambda s: (layer, s, 0, 0, 0)
    full2 = lambda shape: pl.BlockSpec(shape, const2)
    s_blk = pl.BlockSpec((None, DEC_BLOCK, RET_HEADS, RET_DIM, RET_DIM), blk5)
    c_blk = pl.BlockSpec((None, DEC_BLOCK, ML_HEADS, ML_QK, ML_V), blk5)
    in_specs = [
        full2((nb, PROJ_PAD)),
        full2((SUBLANES, RET_DIM)),
        full2((SUBLANES, RET_DIM)),
        full2((SUBLANES, LANES)),
        full2((1, RET_WIDTH)),
        full2((1, ML_HEADS * ML_V_PAD)),
        full2((SUBLANES, LANES)),
        pl.BlockSpec((len(POOL_WINDOWS), POOL_GROUP, POOL_GROUP), const3),
        full2((1, POOL_WIDTH)),
        s_blk,
        c_blk,
        full2((nb, ML_HEADS * ML_QK_PAD)),
        full2((nb, ML_HEADS)),
        full2((nb, POOL_BUF * POOL_WIDTH)),
    ]
    assert len(in_specs) == DEC_INPUTS
    in_specs += [pl.BlockSpec(memory_space=pl.ANY)] * len(stacked_prev)
    out_shape = [
        jax.ShapeDtypeStruct((nb, MIX_PAD), BF16),
        jax.ShapeDtypeStruct(s_ret.shape, F32),
        jax.ShapeDtypeStruct(s_c.shape, F32),
        jax.ShapeDtypeStruct((nb, ML_HEADS * ML_QK_PAD), F32),
        jax.ShapeDtypeStruct((nb, ML_HEADS), F32),
        jax.ShapeDtypeStruct((nb, POOL_BUF * POOL_WIDTH), F32),
    ]
    if stacked_prev:
        so_blk, co_blk = s_blk, c_blk
    else:
        assert layer == 0
        all_layers = lambda s: (0, s, 0, 0, 0)
        so_blk = pl.BlockSpec((s_ret.shape[0],) + s_blk.block_shape[1:], all_layers)
        co_blk = pl.BlockSpec((s_c.shape[0],) + c_blk.block_shape[1:], all_layers)
    out_specs = [
        full2((nb, MIX_PAD)),
        so_blk,
        co_blk,
        full2((nb, ML_HEADS * ML_QK_PAD)),
        full2((nb, ML_HEADS)),
        full2((nb, POOL_BUF * POOL_WIDTH)),
    ]
    scratch = [
        pltpu.VMEM((RET_HEADS, nb, RET_DIM), F32),
        pltpu.VMEM((RET_HEADS, RET_DIM, nb), F32),
        pltpu.VMEM((RET_HEADS, nb, RET_DIM), BF16),
        pltpu.VMEM((RET_HEADS, nb, RET_DIM), F32),
        pltpu.VMEM((ML_HEADS, nb, ML_QK_PAD), F32),
        pltpu.VMEM((ML_HEADS, ML_QK_PAD, nb), F32),
        pltpu.VMEM((ML_HEADS, nb, ML_V_PAD), BF16),
        pltpu.VMEM((ML_HEADS, nb, ML_V_PAD), F32),
        pltpu.VMEM((ML_HEADS, nb, ML_V_PAD), F32),
        pltpu.VMEM((ML_QK_PAD, ML_V_PAD), F32),
    ]
    return pl.pallas_call(
        functools.partial(_decode_mixer_kernel, n_aliased=len(stacked_prev)),
        grid=(nb // DEC_BLOCK,),
        in_specs=in_specs,
        out_specs=out_specs,
        out_shape=out_shape,
        scratch_shapes=scratch,
        input_output_aliases={DEC_INPUTS + i: 1 + i for i in range(len(stacked_prev))},
        compiler_params=pltpu.CompilerParams(
            dimension_semantics=("arbitrary",), vmem_limit_bytes=VMEM_LIMIT),
        name="decode_mixer",
    )(proj, tables["cos_s"], tables["sin_s"], tables["dch"], gains["g_ret"], gains["g_ml"],
      gains["gb_c"], gains["w_pool"], gains["pool_scale"], s_ret, s_c, s_n, s_m, s_pool,
      *stacked_prev)


def _tables(seq):
    half = RET_DIM // 2
    inv_freq = ROPE_THETA ** (-jnp.arange(half, dtype=F32) / half)

    def cos_sin(pos):
        ang = pos[:, None] * inv_freq[None, :]
        cos, sin = jnp.cos(ang), jnp.sin(ang)
        return jnp.concatenate([cos, cos], axis=-1), jnp.concatenate([-sin, sin], axis=-1)

    cos_p, sin_p = cos_sin(jnp.arange(seq, dtype=F32))
    cos_s, sin_s = cos_sin((PAST_LEN + jnp.arange(1)).astype(F32))
    log_gamma = jnp.log1p(-jnp.exp2(-5.0 - jnp.arange(RET_HEADS, dtype=F32)))
    idx = jnp.arange(CHUNK, dtype=F32)
    diff = idx[:, None] - idx[None, :]
    causal = diff >= 0
    lg = log_gamma[:, None, None]
    dintra = jnp.where(causal, jnp.exp(lg * jnp.where(causal, diff, 0.0)), 0.0)
    dq = jnp.exp(log_gamma[:, None] * (idx + 1.0))
    dk = jnp.exp(log_gamma[:, None] * (CHUNK - 1.0 - idx))

    def pad_rows(t, rows):
        return jnp.pad(t, ((0, rows - t.shape[0]), (0, 0)))

    def chunk_decay(c):
        return pad_rows(jnp.broadcast_to(jnp.exp(log_gamma * c)[:, None], (RET_HEADS, LANES)), SUBLANES)

    return {
        "cos": cos_p, "sin": sin_p,
        "cos_s": jnp.broadcast_to(cos_s, (SUBLANES, RET_DIM)),
        "sin_s": jnp.broadcast_to(sin_s, (SUBLANES, RET_DIM)),
        "dintra": dintra,
        "dq": jnp.pad(dq.T, ((0, 0), (0, LANES - RET_HEADS))),
        "dk": pad_rows(dk, SUBLANES),
        "dch": chunk_decay(float(CHUNK)),
        "dch_s": chunk_decay(1.0),
    }


def _pad_last(t, width):
    return jnp.pad(t, [(0, 0)] * (t.ndim - 1) + [(0, width - t.shape[-1])])


def _prep_gains(l, g_ret_norm, g_mlstm_norm, b_mlstm_i, b_mlstm_f, w_pool, pool_scale):
    gate_bias = jnp.concatenate([b_mlstm_i[l], b_mlstm_f[l]])
    lanes = jnp.concatenate([jnp.zeros((GATE_I_LANE,), F32), _pad_last(gate_bias, LANES - GATE_I_LANE)])
    return {
        "g_ret": g_ret_norm[l][None, :],
        "g_ml": _pad_last(g_mlstm_norm[l].reshape(ML_HEADS, ML_V), ML_V_PAD).reshape(1, -1),
        "gb_r": jnp.broadcast_to(gate_bias[:, None], (SUBLANES, LANES)),
        "gb_c": jnp.broadcast_to(lanes[None, :], (SUBLANES, LANES)),
        "w_pool": w_pool[l].astype(BF16),
        "pool_scale": pool_scale[l][None, :],
    }


@jax.jit
def kernel(x_prompt, x_sample, state_ret, state_mlstm_c, state_mlstm_n, state_mlstm_m, state_pool,
           w_in, w_out, g_ret_norm, g_mlstm_norm, b_mlstm_i, b_mlstm_f, w_pool, pool_scale,
           g_pre_mix, g_post_mix, g_pre_ffn, g_post_ffn, w_ffn_gate, w_ffn_up, w_ffn_down):
    batch, seq, d = x_prompt.shape
    nb = x_sample.shape[0]
    depth = w_in.shape[0]
    tables = _tables(seq)
    tables_s = dict(tables, dch=tables["dch_s"])
    w_in_t = jnp.swapaxes(w_in, 1, 2)

    yp = x_prompt.reshape(batch * seq, d)
    ys = x_sample.reshape(nb, d)
    new_p, new_s = [], []
    stacked_s = ()

    def back_half(l, x, mixed, w_out_l):
        t = _token_tiles(x.shape[0])
        row = lambda g: g[l][None, :]
        x1 = _outproj(mixed, w_out_l, x, row(g_post_mix), tm=t["outproj_tm"], tk=t["outproj_tk"])
        return _ffn(x1, row(g_pre_ffn), w_ffn_gate, w_ffn_up, w_ffn_down, row(g_post_ffn), l,
                    tm=t["ffn_tm"], tf=t["ffn_tf"])

    for l in range(depth):
        w_in_l = _prep_w_in(w_in_t, l)
        w_out_l = _prep_w_out(w_out, l)
        gains = _prep_gains(l, g_ret_norm, g_mlstm_norm, b_mlstm_i, b_mlstm_f, w_pool, pool_scale)
        g_pre = g_pre_mix[l][None, :]

        mixed, s_ret, s_c, s_n, s_m, s_pool = _prompt_front(
            yp.reshape(batch, seq, d), g_pre, w_in_l, tables, gains)
        yp = back_half(l, yp, mixed, w_out_l)
        new_p.append((s_ret, s_c[:, :, :ML_QK, :ML_V], s_n[:, :ML_HEADS, :ML_QK],
                      s_m[:, :ML_HEADS, 0], s_pool[:, 1:, :]))

        n_in = _pad_last(state_mlstm_n[l], ML_QK_PAD).reshape(nb, ML_HEADS * ML_QK_PAD)
        pool_in = state_pool[l].reshape(nb, POOL_BUF * POOL_WIDTH)
        t = _token_tiles(nb)
        proj = _inproj(ys, g_pre, w_in_l, tm=t["inproj_tm"], tn=t["inproj_tn"])
        mixed, s_ret, s_c, s_n, s_m, s_pool = _decode_mixer(
            proj, tables_s, gains, l, state_ret, state_mlstm_c, n_in, state_mlstm_m[l], pool_in,
            stacked_s)
        ys = back_half(l, ys, mixed, w_out_l)
        stacked_s = (s_ret, s_c)
        new_s.append((s_n.reshape(nb, ML_HEADS, ML_QK_PAD)[:, :, :ML_QK], s_m,
                      s_pool.reshape(nb, POOL_BUF, POOL_WIDTH)))

    stack = lambda states, i: jnp.stack([s[i] for s in states], axis=0)
    return ((yp.reshape(batch, seq, d), ys.reshape(nb, 1, d))
            + tuple(stack(new_p, i) for i in range(5))
            + stacked_s + tuple(stack(new_s, i) for i in range(3)))
```

```python
import functools

import jax
import jax.numpy as jnp
from jax import lax
from jax.experimental import pallas as pl
from jax.experimental.pallas import tpu as pltpu

F32 = jnp.float32
BF16 = jnp.bfloat16

D_MODEL = 2048
PAST_LEN = 16384
RET_HEADS = 6
RET_DIM = 128
RET_WIDTH = RET_HEADS * RET_DIM
ML_HEADS = 4
ML_QK = 96
ML_V = 192
ML_WIDTH = ML_HEADS * ML_V
POOL_WIDTH = 512
POOL_WINDOWS = (2, 4, 8, 16)
POOL_GROUP = POOL_WIDTH // len(POOL_WINDOWS)
POOL_BUF = max(POOL_WINDOWS) - 1
D_FF = 5632
CHUNK = 128
ROPE_THETA = 10000.0
EPS = 1e-6

LANES = 128
SUBLANES = 8
VMEM_LIMIT = 56 * 1024 * 1024

ML_QK_PAD = LANES
ML_V_PAD = 2 * LANES
OFF_RQ = 0
OFF_RK = OFF_RQ + RET_WIDTH
OFF_RV = OFF_RK + RET_WIDTH
OFF_RG = OFF_RV + RET_WIDTH
OFF_MQ = OFF_RG + RET_WIDTH
OFF_MK = OFF_MQ + ML_HEADS * ML_QK_PAD
OFF_MV = OFF_MK + ML_HEADS * ML_QK_PAD
OFF_MO = OFF_MV + ML_HEADS * ML_V_PAD
OFF_PU = OFF_MO + ML_HEADS * ML_V_PAD
PROJ_PAD = OFF_PU + POOL_WIDTH
GATE_I_LANE = ML_QK
GATE_F_LANE = ML_QK + ML_HEADS
MIX_OFF_ML = RET_WIDTH
MIX_OFF_POOL = MIX_OFF_ML + ML_HEADS * ML_V_PAD
MIX_PAD = MIX_OFF_POOL + POOL_WIDTH

_HI = lax.Precision.HIGHEST


def _token_tiles(m):
    return {
        "inproj_tm": min(m, 1024), "inproj_tn": 512,
        "outproj_tm": min(m, 512), "outproj_tk": 768,
        "ffn_tm": min(m, 1024), "ffn_tf": 256,
    }


def _dot(a, b):
    return jnp.dot(a, b, preferred_element_type=F32)


def _rms(x, width=None):
    n = x.shape[-1] if width is None else width
    return x * lax.rsqrt(jnp.sum(x * x, axis=-1, keepdims=True) / n + EPS)


def _log_sigmoid(x):
    return jnp.minimum(x, 0.0) - jnp.log1p(jnp.exp(-jnp.abs(x)))


def _rotate(x, cos2, sin2):
    return x * cos2 + pltpu.roll(x, RET_DIM // 2, 1) * sin2


def _inproj_kernel(x_ref, g_ref, w_ref, o_ref, xn_ref):
    @pl.when(pl.program_id(1) == 0)
    def _():
        xn_ref[...] = (_rms(x_ref[...]) * g_ref[...]).astype(BF16)

    o_ref[...] = _dot(xn_ref[...], w_ref[...])


def _inproj(x, g, w, tm, tn):
    m = x.shape[0]
    return pl.pallas_call(
        _inproj_kernel,
        grid=(m // tm, PROJ_PAD // tn),
        in_specs=[
            pl.BlockSpec((tm, D_MODEL), lambda i, j: (i, 0)),
            pl.BlockSpec((1, D_MODEL), lambda i, j: (0, 0)),
            pl.BlockSpec((D_MODEL, tn), lambda i, j: (0, j)),
        ],
        out_specs=pl.BlockSpec((tm, tn), lambda i, j: (i, j)),
        out_shape=jax.ShapeDtypeStruct((m, PROJ_PAD), F32),
        scratch_shapes=[pltpu.VMEM((tm, D_MODEL), BF16)],
        compiler_params=pltpu.CompilerParams(
            dimension_semantics=("parallel", "arbitrary"), vmem_limit_bytes=VMEM_LIMIT),
        name="inproj",
    )(x, g, w)


def _outproj_kernel(m_ref, w_ref, x_ref, gpost_ref, x1_ref):
    k = pl.program_id(1)

    @pl.when(k == 0)
    def _():
        x1_ref[...] = jnp.zeros_like(x1_ref)

    x1_ref[...] += _dot(m_ref[...], w_ref[...])

    @pl.when(k == pl.num_programs(1) - 1)
    def _():
        x1_ref[...] = x_ref[...] + _rms(x1_ref[...]) * gpost_ref[...]


def _outproj(mixed, w, x, g_post, tm, tk):
    m = x.shape[0]
    return pl.pallas_call(
        _outproj_kernel,
        grid=(m // tm, MIX_PAD // tk),
        in_specs=[
            pl.BlockSpec((tm, tk), lambda i, k: (i, k)),
            pl.BlockSpec((tk, D_MODEL), lambda i, k: (k, 0)),
            pl.BlockSpec((tm, D_MODEL), lambda i, k: (i, 0)),
            pl.BlockSpec((1, D_MODEL), lambda i, k: (0, 0)),
        ],
        out_specs=pl.BlockSpec((tm, D_MODEL), lambda i, k: (i, 0)),
        out_shape=jax.ShapeDtypeStruct((m, D_MODEL), F32),
        compiler_params=pltpu.CompilerParams(
            dimension_semantics=("parallel", "arbitrary"), vmem_limit_bytes=VMEM_LIMIT),
        name="outproj",
    )(mixed, w, x, g_post)


def _ffn_kernel(x1_ref, gpre_ref, wg_ref, wu_ref, wd_ref, gpost_ref, o_ref, hn_ref):
    f = pl.program_id(1)

    @pl.when(f == 0)
    def _():
        hn_ref[...] = (_rms(x1_ref[...]) * gpre_ref[...]).astype(BF16)
        o_ref[...] = jnp.zeros_like(o_ref)

    hn = hn_ref[...]
    gate = _dot(hn, wg_ref[...].astype(BF16))
    up = _dot(hn, wu_ref[...].astype(BF16))
    act = (gate * jax.nn.sigmoid(gate) * up).astype(BF16)
    o_ref[...] += _dot(act, wd_ref[...].astype(BF16))

    @pl.when(f == pl.num_programs(1) - 1)
    def _():
        o_ref[...] = x1_ref[...] + _rms(o_ref[...]) * gpost_ref[...]


def _ffn(x1, g_pre, wg, wu, wd, g_post, layer, tm, tf):
    m = x1.shape[0]
    return pl.pallas_call(
        _ffn_kernel,
        grid=(m // tm, D_FF // tf),
        in_specs=[
            pl.BlockSpec((tm, D_MODEL), lambda i, f: (i, 0), pipeline_mode=pl.Buffered(1)),
            pl.BlockSpec((1, D_MODEL), lambda i, f: (0, 0)),
            pl.BlockSpec((None, D_MODEL, tf), lambda i, f: (layer, 0, f)),
            pl.BlockSpec((None, D_MODEL, tf), lambda i, f: (layer, 0, f)),
            pl.BlockSpec((None, tf, D_MODEL), lambda i, f: (layer, f, 0)),
            pl.BlockSpec((1, D_MODEL), lambda i, f: (0, 0)),
        ],
        out_specs=pl.BlockSpec((tm, D_MODEL), lambda i, f: (i, 0)),
        out_shape=jax.ShapeDtypeStruct((m, D_MODEL), F32),
        scratch_shapes=[pltpu.VMEM((tm, D_MODEL), BF16)],
        compiler_params=pltpu.CompilerParams(
            dimension_semantics=("parallel", "arbitrary"), vmem_limit_bytes=VMEM_LIMIT),
        name="ffn",
    )(x1, g_pre, wg, wu, wd, g_post)


SRC_MQ = 4 * RET_WIDTH
SRC_MK = SRC_MQ + ML_HEADS * ML_QK
SRC_MV = SRC_MK + ML_HEADS * ML_QK
SRC_MO = SRC_MV + ML_WIDTH
SRC_MI = SRC_MO + ML_WIDTH
SRC_MF = SRC_MI + ML_HEADS
SRC_PU = SRC_MF + ML_HEADS
IN_WIDTH = SRC_PU + POOL_WIDTH


def _prep_w_in_kernel(w_ref, o_ref):
    tk = w_ref.shape[1]

    def put(dst, pieces, width):
        rows = [w_ref[src:src + n, :] for src, n in pieces]
        have = sum(n for _, n in pieces)
        if have < width:
            rows.append(jnp.zeros((width - have, tk), F32))
        block = rows[0] if len(rows) == 1 else jnp.concatenate(rows, axis=0)
        o_ref[:, dst:dst + width] = block.T.astype(BF16)

    for r in range(0, 4 * RET_WIDTH, 2 * LANES):
        put(r, [(r, 2 * LANES)], 2 * LANES)
    for h in range(ML_HEADS):
        q_rows = [(SRC_MQ + h * ML_QK, ML_QK)]
        if h == 0:
            q_rows.append((SRC_MI, 2 * ML_HEADS))
        put(OFF_MQ + h * ML_QK_PAD, q_rows, ML_QK_PAD)
        put(OFF_MK + h * ML_QK_PAD, [(SRC_MK + h * ML_QK, ML_QK)], ML_QK_PAD)
        put(OFF_MV + h * ML_V_PAD, [(SRC_MV + h * ML_V, ML_V)], ML_V_PAD)
        put(OFF_MO + h * ML_V_PAD, [(SRC_MO + h * ML_V, ML_V)], ML_V_PAD)
    for r in range(0, POOL_WIDTH, 2 * LANES):
        put(OFF_PU + r, [(SRC_PU + r, 2 * LANES)], 2 * LANES)


def _prep_w_in(w_t, layer, tk=256):
    d = w_t.shape[2]
    return pl.pallas_call(
        _prep_w_in_kernel,
        grid=(d // tk,),
        in_specs=[pl.BlockSpec((None, IN_WIDTH, tk), lambda i: (layer, 0, i))],
        out_specs=pl.BlockSpec((tk, PROJ_PAD), lambda i: (i, 0)),
        out_shape=jax.ShapeDtypeStruct((d, PROJ_PAD), BF16),
        compiler_params=pltpu.CompilerParams(
            dimension_semantics=("parallel",), vmem_limit_bytes=VMEM_LIMIT),
        name="prep_w_in",
    )(w_t)


def _prep_w_out_kernel(w_ref, o_ref):
    def put(dst, src, rows):
        o_ref[dst:dst + rows, :] = w_ref[src:src + rows, :].astype(BF16)

    put(0, 0, RET_WIDTH)
    for h in range(ML_HEADS):
        dst = MIX_OFF_ML + h * ML_V_PAD
        put(dst, RET_WIDTH + h * ML_V, ML_V)
        o_ref[dst + ML_V:dst + ML_V_PAD, :] = jnp.zeros((ML_V_PAD - ML_V, o_ref.shape[1]), BF16)
    put(MIX_OFF_POOL, RET_WIDTH + ML_WIDTH, POOL_WIDTH)


def _prep_w_out(w, layer, tn=512):
    _, k, n = w.shape
    return pl.pallas_call(
        _prep_w_out_kernel,
        grid=(n // tn,),
        in_specs=[pl.BlockSpec((None, k, tn), lambda j: (layer, 0, j))],
        out_specs=pl.BlockSpec((MIX_PAD, tn), lambda j: (0, j)),
        out_shape=jax.ShapeDtypeStruct((MIX_PAD, n), BF16),
        compiler_params=pltpu.CompilerParams(
            dimension_semantics=("parallel",), vmem_limit_bytes=VMEM_LIMIT),
        name="prep_w_out",
    )(w)


PROMPT_BLOCK = 2


def _ret_head(q, k, vb, g, cos2, sin2, dintra, dq_col, dk_row, dch_row, g_norm, s_old, store):
    qb = _rotate(q, cos2, sin2).astype(BF16)
    kr_t = (_rotate(k, cos2, sin2) * (RET_DIM ** -0.5)).T
    kb = kr_t.astype(BF16)
    kdb = (kr_t * dk_row).astype(BF16)
    gate = g * jax.nn.sigmoid(g)
    yield
    scores = (_dot(qb, kb) * dintra).astype(BF16)
    yield
    o = _dot(scores, vb) + _dot(qb, s_old.astype(BF16)) * dq_col
    s_new = s_old * dch_row + _dot(kdb, vb)
    store(_rms(o) * g_norm * gate, s_new)


def _ml_head(q, k, vb, og, causal, b_c, i_c, b_r, i_r, g_norm, c_old, n_old, m_prev, store):
    k_t = k.T
    qb = q.astype(BF16)
    logw = jnp.where(causal, b_c - b_r + i_r, -jnp.inf)
    inter = b_c + m_prev
    m_t = jnp.maximum(inter, jnp.max(logw, axis=-1, keepdims=True))
    w = jnp.exp(logw - m_t)
    a = jnp.exp(inter - m_t)
    m_new = m_t[CHUNK - 1:CHUNK, :]
    b_last = b_c[CHUNK - 1:CHUNK, :]
    w_r = jnp.exp(b_last - b_r + i_r - m_new)
    w_c = jnp.exp(b_last - b_c + i_c - m_new)
    a_end = jnp.exp(b_last + m_prev - m_new)
    kwb = (k_t * w_r).astype(BF16)
    n_new = a_end * n_old + jnp.sum(k * w_c, axis=0, keepdims=True)
    qn = jnp.sum(q * n_old, axis=-1, keepdims=True)
    floor = jnp.exp(-m_t)
    gate = jax.nn.sigmoid(og)
    yield
    s = _dot(qb, k_t.astype(BF16)) * w
    den = jnp.sum(s, axis=-1, keepdims=True) + a * qn
    sb = s.astype(BF16)
    yield
    num = _dot(sb, vb) + a * _dot(qb, c_old.astype(BF16))
    c_new = a_end * c_old + _dot(kwb, vb)
    hh = num / jnp.maximum(jnp.abs(den), floor)
    store(_rms(hh, ML_V) * g_norm * gate, c_new, n_new, m_new)


def _prompt_front_kernel(x_ref, gpre_ref, w_ref, cos_ref, sin_ref, dintra_ref, dq_ref, dk_ref,
                         dch_ref, gret_ref, gml_ref, gb_r_ref, gb_c_ref, wpool_ref, pscale_ref,
                         mix_ref, sret_ref, sc_ref, sn_ref, sm_ref, spool_ref,
                         s_scr, c_scr, n_scr, m_scr, u_scr):
    bp = pl.program_id(0)
    c = pl.program_id(1)
    nseq = x_ref.shape[0]

    @pl.when(c == 0)
    def _():
        s_scr[...] = jnp.zeros_like(s_scr)
        c_scr[...] = jnp.zeros_like(c_scr)
        n_scr[...] = jnp.zeros_like(n_scr)
        m_scr[...] = jnp.zeros_like(m_scr)
        u_scr[:, 0:2 * SUBLANES, :] = jnp.zeros((nseq, 2 * SUBLANES, POOL_WIDTH), F32)

    xn = (_rms(x_ref[...].reshape(nseq * CHUNK, D_MODEL)) * gpre_ref[...]).astype(BF16)

    def proj(lo, width):
        return _dot(xn, w_ref[:, lo:lo + width])

    def seq_rows(t, i):
        return t[i * CHUNK:(i + 1) * CHUNK]

    cos2 = cos_ref[...]
    sin2 = sin_ref[...]
    pair = 2 * RET_DIM

    def ret_project(h0):
        return [functools.partial(proj, off + h0 * RET_DIM, pair)
                for off in (OFF_RQ, OFF_RK, OFF_RV, OFF_RG)]

    def ret_mix(h0, projected):
        pq, pk, pv, pg = projected
        instances = []
        for h in (h0, h0 + 1):
            lo, hi = (h - h0) * RET_DIM, (h - h0 + 1) * RET_DIM
            for i in range(nseq):
                def store(out, s_new, h=h, i=i):
                    s_scr[i, h] = s_new
                    mix_ref[i, :, h * RET_DIM:(h + 1) * RET_DIM] = out.astype(BF16)

                instances.append(_ret_head(
                    seq_rows(pq, i)[:, lo:hi], seq_rows(pk, i)[:, lo:hi],
                    seq_rows(pv, i)[:, lo:hi].astype(BF16), seq_rows(pg, i)[:, lo:hi],
                    cos2, sin2, dintra_ref[h], dq_ref[:, h:h + 1], dk_ref[h:h + 1, :],
                    dch_ref[h:h + 1, :], gret_ref[:, h * RET_DIM:(h + 1) * RET_DIM], s_scr[i, h],
                    store))
        return instances

    row = lax.broadcasted_iota(jnp.int32, (CHUNK, CHUNK), 0)
    col = lax.broadcasted_iota(jnp.int32, (CHUNK, CHUNK), 1)
    causal = col <= row
    upper = (row <= col).astype(F32)
    lower = causal.astype(F32)
    gates = [None] * nseq

    def ml_project(h0):
        return [functools.partial(proj, OFF_MQ + h0 * ML_QK_PAD, 2 * ML_QK_PAD),
                functools.partial(proj, OFF_MK + h0 * ML_QK_PAD, 2 * ML_QK_PAD),
                functools.partial(proj, OFF_MV + h0 * ML_V_PAD, 2 * ML_V_PAD),
                functools.partial(proj, OFF_MO + h0 * ML_V_PAD, 2 * ML_V_PAD)]

    def ml_mix(h0, projected):
        pq, pk, pv, po = projected
        if h0 == 0:
            for i in range(nseq):
                gq0 = seq_rows(pq, i)[:, 0:ML_QK_PAD]
                i_r8 = gq0.T[GATE_I_LANE:GATE_I_LANE + SUBLANES, :] + gb_r_ref[...]
                b_r8 = jnp.dot(_log_sigmoid(i_r8), upper, precision=_HI, preferred_element_type=F32)
                pre_c = gq0 + gb_c_ref[0:1, :]
                b_c_all = jnp.dot(lower, _log_sigmoid(pre_c), precision=_HI,
                                  preferred_element_type=F32)
                gates[i] = (i_r8, b_r8, pre_c, b_c_all)
        instances = []
        for h in (h0, h0 + 1):
            qlo = (h - h0) * ML_QK_PAD
            vlo = (h - h0) * ML_V_PAD
            for i in range(nseq):
                def store(out, c_new, n_new, m_new, h=h, i=i):
                    c_scr[i, h] = c_new
                    n_scr[i, h:h + 1, :] = n_new
                    m_scr[i, h:h + 1, :] = jnp.broadcast_to(m_new, (1, LANES))
                    mix_ref[i, :, MIX_OFF_ML + h * ML_V_PAD:MIX_OFF_ML + (h + 1) * ML_V_PAD] = (
                        out.astype(BF16))

                i_r8, b_r8, pre_c, b_c_all = gates[i]
                instances.append(_ml_head(
                    seq_rows(pq, i)[:, qlo:qlo + ML_QK_PAD],
                    seq_rows(pk, i)[:, qlo:qlo + ML_QK_PAD] * (ML_QK ** -0.5),
                    seq_rows(pv, i)[:, vlo:vlo + ML_V_PAD].astype(BF16),
                    seq_rows(po, i)[:, vlo:vlo + ML_V_PAD],
                    causal,
                    b_c_all[:, GATE_F_LANE + h:GATE_F_LANE + h + 1],
                    pre_c[:, GATE_I_LANE + h:GATE_I_LANE + h + 1],
                    b_r8[ML_HEADS + h:ML_HEADS + h + 1, :], i_r8[h:h + 1, :],
                    gml_ref[:, h * ML_V_PAD:(h + 1) * ML_V_PAD],
                    c_scr[i, h], n_scr[i, h:h + 1, :], m_scr[i, h:h + 1, 0:1], store))
        return instances

    base = 2 * SUBLANES
    pos = (c * CHUNK + lax.broadcasted_iota(jnp.int32, (CHUNK, 1), 0)).astype(F32)

    def pool_seq(u, i):
        u_scr[i, base:base + CHUNK, :] = u
        zs = []
        for g, win in enumerate(POOL_WINDOWS):
            lo, hi = g * POOL_GROUP, (g + 1) * POOL_GROUP
            wsum = u[:, lo:hi]
            for s in range(1, win):
                wsum = wsum + u_scr[i, base - s:base - s + CHUNK, lo:hi]
            zs.append((wsum / jnp.minimum(pos + 1.0, float(win)) - u[:, lo:hi]).astype(BF16))
        u_scr[i, 0:base, :] = u_scr[i, CHUNK:CHUNK + base, :]
        yield
        for g, z in enumerate(zs):
            lo, hi = g * POOL_GROUP, (g + 1) * POOL_GROUP
            y = _dot(z, wpool_ref[g]) * pscale_ref[:, lo:hi]
            mix_ref[i, :, MIX_OFF_POOL + lo:MIX_OFF_POOL + hi] = y.astype(BF16)

    def pool_mix(projected):
        pu, = projected
        return [pool_seq(seq_rows(pu, i), i) for i in range(nseq)]

    stages = [(ret_project(h0), functools.partial(ret_mix, h0)) for h0 in range(0, RET_HEADS, 2)]
    stages += [(ml_project(h0), functools.partial(ml_mix, h0)) for h0 in range(0, ML_HEADS, 2)]
    stages.append(([functools.partial(proj, OFF_PU, POOL_WIDTH)], pool_mix))
    projected = [dot() for dot in stages[0][0]]
    for n, (_, mix) in enumerate(stages):
        pending = list(stages[n + 1][0]) if n + 1 < len(stages) else []
        upcoming = []
        live = mix(projected)
        while live:
            for _ in range(max(1, len(pending) // 2) if pending else 0):
                upcoming.append(pending.pop(0)())
            live = [inst for inst in live if next(inst, StopIteration) is not StopIteration]
        upcoming += [dot() for dot in pending]
        projected = upcoming

    @pl.when(c == pl.num_programs(1) - 1)
    def _():
        for i in range(nseq):
            b = bp * nseq + i
            sret_ref[b] = s_scr[i]
            sc_ref[b] = c_scr[i]
            sn_ref[b] = n_scr[i]
            sm_ref[b] = m_scr[i]
            spool_ref[b] = u_scr[i, 0:base, :]


def _prompt_front(x, g_pre, w, tables, gains):
    batch, seq, _ = x.shape
    nchunk = seq // CHUNK
    nseq = PROMPT_BLOCK
    const2 = lambda b, c: (0, 0)
    const3 = lambda b, c: (0, 0, 0)
    const4 = lambda b, c: (0, 0, 0, 0)
    in_specs = [
        pl.BlockSpec((nseq, CHUNK, D_MODEL), lambda b, c: (b, c, 0)),
        pl.BlockSpec((1, D_MODEL), const2),
        pl.BlockSpec((D_MODEL, PROJ_PAD), const2, pipeline_mode=pl.Buffered(1)),
        pl.BlockSpec((CHUNK, RET_DIM), lambda b, c: (c, 0)),
        pl.BlockSpec((CHUNK, RET_DIM), lambda b, c: (c, 0)),
        pl.BlockSpec((RET_HEADS, CHUNK, CHUNK), const3),
        pl.BlockSpec((CHUNK, LANES), const2),
        pl.BlockSpec((SUBLANES, CHUNK), const2),
        pl.BlockSpec((SUBLANES, LANES), const2),
        pl.BlockSpec((1, RET_WIDTH), const2),
        pl.BlockSpec((1, ML_HEADS * ML_V_PAD), const2),
        pl.BlockSpec((SUBLANES, LANES), const2),
        pl.BlockSpec((SUBLANES, LANES), const2),
        pl.BlockSpec((len(POOL_WINDOWS), POOL_GROUP, POOL_GROUP), const3),
        pl.BlockSpec((1, POOL_WIDTH), const2),
    ]
    out_shape = [
        jax.ShapeDtypeStruct((batch, seq, MIX_PAD), BF16),
        jax.ShapeDtypeStruct((batch, RET_HEADS, RET_DIM, RET_DIM), F32),
        jax.ShapeDtypeStruct((batch, ML_HEADS, ML_QK_PAD, ML_V_PAD), F32),
        jax.ShapeDtypeStruct((batch, SUBLANES, LANES), F32),
        jax.ShapeDtypeStruct((batch, SUBLANES, LANES), F32),
        jax.ShapeDtypeStruct((batch, 2 * SUBLANES, POOL_WIDTH), F32),
    ]
    out_specs = [
        pl.BlockSpec((nseq, CHUNK, MIX_PAD), lambda b, c: (b, c, 0)),
        pl.BlockSpec((batch, RET_HEADS, RET_DIM, RET_DIM), const4),
        pl.BlockSpec((batch, ML_HEADS, ML_QK_PAD, ML_V_PAD), const4),
        pl.BlockSpec((batch, SUBLANES, LANES), const3),
        pl.BlockSpec((batch, SUBLANES, LANES), const3),
        pl.BlockSpec((batch, 2 * SUBLANES, POOL_WIDTH), const3),
    ]
    scratch = [
        pltpu.VMEM((nseq, RET_HEADS, RET_DIM, RET_DIM), F32),
        pltpu.VMEM((nseq, ML_HEADS, ML_QK_PAD, ML_V_PAD), F32),
        pltpu.VMEM((nseq, SUBLANES, LANES), F32),
        pltpu.VMEM((nseq, SUBLANES, LANES), F32),
        pltpu.VMEM((nseq, 2 * SUBLANES + CHUNK, POOL_WIDTH), F32),
    ]
    mixed, *states = pl.pallas_call(
        _prompt_front_kernel,
        grid=(batch // nseq, nchunk),
        in_specs=in_specs,
        out_specs=out_specs,
        out_shape=out_shape,
        scratch_shapes=scratch,
        compiler_params=pltpu.CompilerParams(
            dimension_semantics=("arbitrary", "arbitrary"), vmem_limit_bytes=VMEM_LIMIT),
        name="prompt_front",
    )(x, g_pre, w, tables["cos"], tables["sin"], tables["dintra"], tables["dq"], tables["dk"],
      tables["dch"], gains["g_ret"], gains["g_ml"], gains["gb_r"], gains["gb_c"],
      gains["w_pool"], gains["pool_scale"])
    return (mixed.reshape(batch * seq, MIX_PAD), *states)


DEC_BLOCK = 8
DEC_INPUTS = 14


def _decode_mixer_kernel(*refs, n_aliased):
    (p_ref, cos_ref, sin_ref, dch_ref, gret_ref, gml_ref, gb_c_ref, wpool_ref, pscale_ref,
     s_ref, c_ref, n_ref, m_ref, pool_ref) = refs[:DEC_INPUTS]
    (mix_ref, so_ref, co_ref, no_ref, mo_ref, poolo_ref,
     qr_scr, kt_scr, v_scr, qs_scr, qm_scr, wkt_scr, vm_scr, qc_scr, a_scr,
     cpad_scr) = refs[DEC_INPUTS + n_aliased:]
    if n_aliased == 0:
        for later in range(1, so_ref.shape[0]):
            so_ref[later] = jnp.zeros(so_ref.shape[1:], F32)
            co_ref[later] = jnp.zeros(co_ref.shape[1:], F32)
        so_ref, co_ref = so_ref.at[0], co_ref.at[0]
    step = pl.program_id(0)
    nb = p_ref.shape[0]
    cos2 = cos_ref[0:1, :]
    sin2 = sin_ref[0:1, :]

    def ret_qkv(h):
        lo, hi = h * RET_DIM, (h + 1) * RET_DIM
        qr = _rotate(p_ref[:, OFF_RQ + lo:OFF_RQ + hi], cos2, sin2)
        kr = _rotate(p_ref[:, OFF_RK + lo:OFF_RK + hi], cos2, sin2) * (RET_DIM ** -0.5)
        return qr, kr, p_ref[:, OFF_RV + lo:OFF_RV + hi]

    def ml_gates(h):
        pre_c = p_ref[:, OFF_MQ:OFF_MQ + ML_QK_PAD] + gb_c_ref[0:1, :]
        i_pre = pre_c[:, GATE_I_LANE + h:GATE_I_LANE + h + 1]
        lf = _log_sigmoid(pre_c[:, GATE_F_LANE + h:GATE_F_LANE + h + 1])
        inter = lf + m_ref[:, h:h + 1]
        m_t = jnp.maximum(inter, i_pre)
        return jnp.exp(i_pre - m_t), jnp.exp(inter - m_t), m_t

    def ml_qkv(h):
        q = p_ref[:, OFF_MQ + h * ML_QK_PAD:OFF_MQ + (h + 1) * ML_QK_PAD]
        k = p_ref[:, OFF_MK + h * ML_QK_PAD:OFF_MK + (h + 1) * ML_QK_PAD] * (ML_QK ** -0.5)
        return q, k, p_ref[:, OFF_MV + h * ML_V_PAD:OFF_MV + (h + 1) * ML_V_PAD]

    @pl.when(step == 0)
    def _():
        for h in range(RET_HEADS):
            qr, kr, v = ret_qkv(h)
            qr_scr[h] = qr
            kt_scr[h] = kr.T
            v_scr[h] = v.astype(BF16)
        for h in range(ML_HEADS):
            w, a, _ = ml_gates(h)
            q, k, v = ml_qkv(h)
            qm_scr[h] = q
            wkt_scr[h] = (w * k).T
            vm_scr[h] = v.astype(BF16)
            a_scr[h] = jnp.broadcast_to(a, (nb, ML_V_PAD))
        cpad_scr[...] = jnp.zeros_like(cpad_scr)

    lane = lax.broadcasted_iota(jnp.int32, (LANES, nb), 1)

    def body(bb, carry):
        b = step * DEC_BLOCK + bb
        sel = lane == b
        for h in range(RET_HEADS):
            s_old = s_ref[bb, h]
            q16 = jnp.broadcast_to(qr_scr[h, pl.ds(b, 1), :], (2 * SUBLANES, RET_DIM)).astype(BF16)
            qs_scr[h, pl.ds(b, 1), :] = _dot(q16, s_old.astype(BF16))[0:1, :]
            k_m = jnp.where(sel, kt_scr[h], 0.0).astype(BF16)
            so_ref[bb, h] = s_old * dch_ref[h:h + 1, :] + _dot(k_m, v_scr[h])
        for h in range(ML_HEADS):
            cpad_scr[0:ML_QK, 0:ML_V] = c_ref[bb, h]
            c_old = cpad_scr[...]
            q16 = jnp.broadcast_to(qm_scr[h, pl.ds(b, 1), :], (2 * SUBLANES, ML_QK_PAD)).astype(BF16)
            qc_scr[h, pl.ds(b, 1), :] = _dot(q16, c_old.astype(BF16))[0:1, :]
            wk_m = jnp.where(sel, wkt_scr[h], 0.0).astype(BF16)
            c_new = a_scr[h, pl.ds(b, 1), :] * c_old + _dot(wk_m, vm_scr[h])
            co_ref[bb, h] = c_new[0:ML_QK, 0:ML_V]
        return carry

    lax.fori_loop(0, DEC_BLOCK, body, 0)

    @pl.when(step == pl.num_programs(0) - 1)
    def _():
        for h in range(RET_HEADS):
            lo, hi = h * RET_DIM, (h + 1) * RET_DIM
            qr, kr, v = ret_qkv(h)
            g = p_ref[:, OFF_RG + lo:OFF_RG + hi]
            o = jnp.sum(qr * kr, axis=-1, keepdims=True) * v + qs_scr[h] * dch_ref[h:h + 1, :]
            out = _rms(o) * gret_ref[:, lo:hi] * (g * jax.nn.sigmoid(g))
            mix_ref[:, lo:hi] = out.astype(BF16)
        for h in range(ML_HEADS):
            w, a, m_t = ml_gates(h)
            q, k, v = ml_qkv(h)
            og = p_ref[:, OFF_MO + h * ML_V_PAD:OFF_MO + (h + 1) * ML_V_PAD]
            n_old = n_ref[:, h * ML_QK_PAD:(h + 1) * ML_QK_PAD]
            s = jnp.sum(q * k, axis=-1, keepdims=True) * w
            num = s * v + a * qc_scr[h]
            den = s + a * jnp.sum(q * n_old, axis=-1, keepdims=True)
            hh = num / jnp.maximum(jnp.abs(den), jnp.exp(-m_t))
            out = (_rms(hh, ML_V) * gml_ref[:, h * ML_V_PAD:(h + 1) * ML_V_PAD]
                   * jax.nn.sigmoid(og))
            mix_ref[:, MIX_OFF_ML + h * ML_V_PAD:MIX_OFF_ML + (h + 1) * ML_V_PAD] = out.astype(BF16)
            no_ref[:, h * ML_QK_PAD:(h + 1) * ML_QK_PAD] = a * n_old + w * k
            mo_ref[:, h:h + 1] = m_t
        u = p_ref[:, OFF_PU:OFF_PU + POOL_WIDTH]
        for g, win in enumerate(POOL_WINDOWS):
            lo, hi = g * POOL_GROUP, (g + 1) * POOL_GROUP
            wsum = u[:, lo:hi]
            for s in range(1, win):
                r = (POOL_BUF - s) * POOL_WIDTH
                wsum = wsum + pool_ref[:, r + lo:r + hi]
            z = wsum / float(win) - u[:, lo:hi]
            y = _dot(z.astype(BF16), wpool_ref[g]) * pscale_ref[:, lo:hi]
            mix_ref[:, MIX_OFF_POOL + lo:MIX_OFF_POOL + hi] = y.astype(BF16)
        keep = (POOL_BUF - 1) * POOL_WIDTH
        poolo_ref[:, 0:keep] = pool_ref[:, POOL_WIDTH:POOL_WIDTH + keep]
        poolo_ref[:, keep:keep + POOL_WIDTH] = u


def _decode_mixer(proj, tables, gains, layer, s_ret, s_c, s_n, s_m, s_pool, stacked_prev):
    nb = proj.shape[0]
    const2 = lambda s: (0, 0)
    const3 = lambda s: (0, 0, 0)
    blk5 = lambda s: (layer, s, 0, 0, 0)
    full2 = lambda shape: pl.BlockSpec(shape, const2)
    s_blk = pl.BlockSpec((None, DEC_BLOCK, RET_HEADS, RET_DIM, RET_DIM), blk5)
    c_blk = pl.BlockSpec((None, DEC_BLOCK, ML_HEADS, ML_QK, ML_V), blk5)
    in_specs = [
        full2((nb, PROJ_PAD)),
        full2((SUBLANES, RET_DIM)),
        full2((SUBLANES, RET_DIM)),
        full2((SUBLANES, LANES)),
        full2((1, RET_WIDTH)),
        full2((1, ML_HEADS * ML_V_PAD)),
        full2((SUBLANES, LANES)),
        pl.BlockSpec((len(POOL_WINDOWS), POOL_GROUP, POOL_GROUP), const3),
        full2((1, POOL_WIDTH)),
        s_blk,
        c_blk,
        full2((nb, ML_HEADS * ML_QK_PAD)),
        full2((nb, ML_HEADS)),
        full2((nb, POOL_BUF * POOL_WIDTH)),
    ]
    assert len(in_specs) == DEC_INPUTS
    in_specs += [pl.BlockSpec(memory_space=pl.ANY)] * len(stacked_prev)
    out_shape = [
        jax.ShapeDtypeStruct((nb, MIX_PAD), BF16),
        jax.ShapeDtypeStruct(s_ret.shape, F32),
        jax.ShapeDtypeStruct(s_c.shape, F32),
        jax.ShapeDtypeStruct((nb, ML_HEADS * ML_QK_PAD), F32),
        jax.ShapeDtypeStruct((nb, ML_HEADS), F32),
        jax.ShapeDtypeStruct((nb, POOL_BUF * POOL_WIDTH), F32),
    ]
    if stacked_prev:
        so_blk, co_blk = s_blk, c_blk
    else:
        assert layer == 0
        all_layers = lambda s: (0, s, 0, 0, 0)
        so_blk = pl.BlockSpec((s_ret.shape[0],) + s_blk.block_shape[1:], all_layers)
        co_blk = pl.BlockSpec((s_c.shape[0],) + c_blk.block_shape[1:], all_layers)
    out_specs = [
        full2((nb, MIX_PAD)),
        so_blk,
        co_blk,
        full2((nb, ML_HEADS * ML_QK_PAD)),
        full2((nb, ML_HEADS)),
        full2((nb, POOL_BUF * POOL_WIDTH)),
    ]
    scratch = [
        pltpu.VMEM((RET_HEADS, nb, RET_DIM), F32),
        pltpu.VMEM((RET_HEADS, RET_DIM, nb), F32),
        pltpu.VMEM((RET_HEADS, nb, RET_DIM), BF16),
        pltpu.VMEM((RET_HEADS, nb, RET_DIM), F32),
        pltpu.VMEM((ML_HEADS, nb, ML_QK_PAD), F32),
        pltpu.VMEM((ML_HEADS, ML_QK_PAD, nb), F32),
        pltpu.VMEM((ML_HEADS, nb, ML_V_PAD), BF16),
        pltpu.VMEM((ML_HEADS, nb, ML_V_PAD), F32),
        pltpu.VMEM((ML_HEADS, nb, ML_V_PAD), F32),
        pltpu.VMEM((ML_QK_PAD, ML_V_PAD), F32),
    ]
    return pl.pallas_call(
        functools.partial(_decode_mixer_kernel, n_aliased=len(stacked_prev)),
        grid=(nb // DEC_BLOCK,),
        in_specs=in_specs,
        out_specs=out_specs,
        out_shape=out_shape,
        scratch_shapes=scratch,
        input_output_aliases={DEC_INPUTS + i: 1 + i for i in range(len(stacked_prev))},
        compiler_params=pltpu.CompilerParams(
            dimension_semantics=("arbitrary",), vmem_limit_bytes=VMEM_LIMIT),
        name="decode_mixer",
    )(proj, tables["cos_s"], tables["sin_s"], tables["dch"], gains["g_ret"], gains["g_ml"],
      gains["gb_c"], gains["w_pool"], gains["pool_scale"], s_ret, s_c, s_n, s_m, s_pool,
      *stacked_prev)


def _tables(seq):
    half = RET_DIM // 2
    inv_freq = ROPE_THETA ** (-jnp.arange(half, dtype=F32) / half)

    def cos_sin(pos):
        ang = pos[:, None] * inv_freq[None, :]
        cos, sin = jnp.cos(ang), jnp.sin(ang)
        return jnp.concatenate([cos, cos], axis=-1), jnp.concatenate([-sin, sin], axis=-1)

    cos_p, sin_p = cos_sin(jnp.arange(seq, dtype=F32))
    cos_s, sin_s = cos_sin((PAST_LEN + jnp.arange(1)).astype(F32))
    log_gamma = jnp.log1p(-jnp.exp2(-5.0 - jnp.arange(RET_HEADS, dtype=F32)))
    idx = jnp.arange(CHUNK, dtype=F32)
    diff = idx[:, None] - idx[None, :]
    causal = diff >= 0
    lg = log_gamma[:, None, None]
    dintra = jnp.where(causal, jnp.exp(lg * jnp.where(causal, diff, 0.0)), 0.0)
    dq = jnp.exp(log_gamma[:, None] * (idx + 1.0))
    dk = jnp.exp(log_gamma[:, None] * (CHUNK - 1.0 - idx))

    def pad_rows(t, rows):
        return jnp.pad(t, ((0, rows - t.shape[0]), (0, 0)))

    def chunk_decay(c):
        return pad_rows(jnp.broadcast_to(jnp.exp(log_gamma * c)[:, None], (RET_HEADS, LANES)), SUBLANES)

    return {
        "cos": cos_p, "sin": sin_p,
        "cos_s": jnp.broadcast_to(cos_s, (SUBLANES, RET_DIM)),
        "sin_s": jnp.broadcast_to(sin_s, (SUBLANES, RET_DIM)),
        "dintra": dintra,
        "dq": jnp.pad(dq.T, ((0, 0), (0, LANES - RET_HEADS))),
        "dk": pad_rows(dk, SUBLANES),
        "dch": chunk_decay(float(CHUNK)),
        "dch_s": chunk_decay(1.0),
    }


def _pad_last(t, width):
    return jnp.pad(t, [(0, 0)] * (t.ndim - 1) + [(0, width - t.shape[-1])])


def _prep_gains(l, g_ret_norm, g_mlstm_norm, b_mlstm_i, b_mlstm_f, w_pool, pool_scale):
    gate_bias = jnp.concatenate([b_mlstm_i[l], b_mlstm_f[l]])
    lanes = jnp.concatenate([jnp.zeros((GATE_I_LANE,), F32), _pad_last(gate_bias, LANES - GATE_I_LANE)])
    return {
        "g_ret": g_ret_norm[l][None, :],
        "g_ml": _pad_last(g_mlstm_norm[l].reshape(ML_HEADS, ML_V), ML_V_PAD).reshape(1, -1),
        "gb_r": jnp.broadcast_to(gate_bias[:, None], (SUBLANES, LANES)),
        "gb_c": jnp.broadcast_to(lanes[None, :], (SUBLANES, LANES)),
        "w_pool": w_pool[l].astype(BF16),
        "pool_scale": pool_scale[l][None, :],
    }


@jax.jit
def kernel(x_prompt, x_sample, state_ret, state_mlstm_c, state_mlstm_n, state_mlstm_m, state_pool,
           w_in, w_out, g_ret_norm, g_mlstm_norm, b_mlstm_i, b_mlstm_f, w_pool, pool_scale,
           g_pre_mix, g_post_mix, g_pre_ffn, g_post_ffn, w_ffn_gate, w_ffn_up, w_ffn_down):
    batch, seq, d = x_prompt.shape
    nb = x_sample.shape[0]
    depth = w_in.shape[0]
    tables = _tables(seq)
    tables_s = dict(tables, dch=tables["dch_s"])
    w_in_t = jnp.swapaxes(w_in, 1, 2)

    yp = x_prompt.reshape(batch * seq, d)
    ys = x_sample.reshape(nb, d)
    new_p, new_s = [], []
    stacked_s = ()

    def back_half(l, x, mixed, w_out_l):
        t = _token_tiles(x.shape[0])
        row = lambda g: g[l][None, :]
        x1 = _outproj(mixed, w_out_l, x, row(g_post_mix), tm=t["outproj_tm"], tk=t["outproj_tk"])
        return _ffn(x1, row(g_pre_ffn), w_ffn_gate, w_ffn_up, w_ffn_down, row(g_post_ffn), l,
                    tm=t["ffn_tm"], tf=t["ffn_tf"])

    for l in range(depth):
        w_in_l = _prep_w_in(w_in_t, l)
        w_out_l = _prep_w_out(w_out, l)
        gains = _prep_gains(l, g_ret_norm, g_mlstm_norm, b_mlstm_i, b_mlstm_f, w_pool, pool_scale)
        g_pre = g_pre_mix[l][None, :]

        mixed, s_ret, s_c, s_n, s_m, s_pool = _prompt_front(
            yp.reshape(batch, seq, d), g_pre, w_in_l, tables, gains)
        yp = back_half(l, yp, mixed, w_out_l)
        new_p.append((s_ret, s_c[:, :, :ML_QK, :ML_V], s_n[:, :ML_HEADS, :ML_QK],
                      s_m[:, :ML_HEADS, 0], s_pool[:, 1:, :]))

        n_in = _pad_last(state_mlstm_n[l], ML_QK_PAD).reshape(nb, ML_HEADS * ML_QK_PAD)
        pool_in = state_pool[l].reshape(nb, POOL_BUF * POOL_WIDTH)
        t = _token_tiles(nb)
        proj = _inproj(ys, g_pre, w_in_l, tm=t["inproj_tm"], tn=t["inproj_tn"])
        mixed, s_ret, s_c, s_n, s_m, s_pool = _decode_mixer(
            proj, tables_s, gains, l, state_ret, state_mlstm_c, n_in, state_mlstm_m[l], pool_in,
            stacked_s)
        ys = back_half(l, ys, mixed, w_out_l)
        stacked_s = (s_ret, s_c)
        new_s.append((s_n.reshape(nb, ML_HEADS, ML_QK_PAD)[:, :, :ML_QK], s_m,
                      s_pool.reshape(nb, POOL_BUF, POOL_WIDTH)))

    stack = lambda states, i: jnp.stack([s[i] for s in states], axis=0)
    return ((yp.reshape(batch, seq, d), ys.reshape(nb, 1, d))
            + tuple(stack(new_p, i) for i in range(5))
            + stacked_s + tuple(stack(new_s, i) for i in range(3)))
```

```python
import functools

import jax
import jax.numpy as jnp
from jax import lax
from jax.experimental import pallas as pl
from jax.experimental.pallas import tpu as pltpu

F32 = jnp.float32
BF16 = jnp.bfloat16

D_MODEL = 2048
PAST_LEN = 16384
RET_HEADS = 6
RET_DIM = 128
RET_WIDTH = RET_HEADS * RET_DIM
ML_HEADS = 4
ML_QK = 96
ML_V = 192
ML_WIDTH = ML_HEADS * ML_V
POOL_WIDTH = 512
POOL_WINDOWS = (2, 4, 8, 16)
POOL_GROUP = POOL_WIDTH // len(POOL_WINDOWS)
POOL_BUF = max(POOL_WINDOWS) - 1
D_FF = 5632
CHUNK = 128
ROPE_THETA = 10000.0
EPS = 1e-6

LANES = 128
SUBLANES = 8
VMEM_LIMIT = 56 * 1024 * 1024

ML_QK_PAD = LANES
ML_V_PAD = 2 * LANES
OFF_RQ = 0
OFF_RK = OFF_RQ + RET_WIDTH
OFF_RV = OFF_RK + RET_WIDTH
OFF_RG = OFF_RV + RET_WIDTH
OFF_MQ = OFF_RG + RET_WIDTH
OFF_MK = OFF_MQ + ML_HEADS * ML_QK_PAD
OFF_MV = OFF_MK + ML_HEADS * ML_QK_PAD
OFF_MO = OFF_MV + ML_HEADS * ML_V_PAD
OFF_PU = OFF_MO + ML_HEADS * ML_V_PAD
PROJ_PAD = OFF_PU + POOL_WIDTH
GATE_I_LANE = ML_QK
GATE_F_LANE = ML_QK + ML_HEADS
MIX_OFF_ML = RET_WIDTH
MIX_OFF_POOL = MIX_OFF_ML + ML_HEADS * ML_V_PAD
MIX_PAD = MIX_OFF_POOL + POOL_WIDTH

_HI = lax.Precision.HIGHEST


def _token_tiles(m):
    return {
        "inproj_tm": min(m, 1024), "inproj_tn": 512,
        "outproj_tm": min(m, 1024), "outproj_tk": 768,
        "ffn_tm": min(m, 1024), "ffn_tf": 256,
    }


def _dot(a, b):
    return jnp.dot(a, b, preferred_element_type=F32)


def _rms(x, width=None):
    n = x.shape[-1] if width is None else width
    return x * lax.rsqrt(jnp.sum(x * x, axis=-1, keepdims=True) / n + EPS)


def _log_sigmoid(x):
    return jnp.minimum(x, 0.0) - jnp.log1p(jnp.exp(-jnp.abs(x)))


def _rotate(x, cos2, sin2):
    return x * cos2 + pltpu.roll(x, RET_DIM // 2, 1) * sin2


def _inproj_kernel(x_ref, g_ref, w_ref, o_ref, xn_ref):
    @pl.when(pl.program_id(1) == 0)
    def _():
        xn_ref[...] = (_rms(x_ref[...]) * g_ref[...]).astype(BF16)

    o_ref[...] = _dot(xn_ref[...], w_ref[...])


def _inproj(x, g, w, tm, tn):
    m = x.shape[0]
    return pl.pallas_call(
        _inproj_kernel,
        grid=(m // tm, PROJ_PAD // tn),
        in_specs=[
            pl.BlockSpec((tm, D_MODEL), lambda i, j: (i, 0)),
            pl.BlockSpec((1, D_MODEL), lambda i, j: (0, 0)),
            pl.BlockSpec((D_MODEL, tn), lambda i, j: (0, j)),
        ],
        out_specs=pl.BlockSpec((tm, tn), lambda i, j: (i, j)),
        out_shape=jax.ShapeDtypeStruct((m, PROJ_PAD), F32),
        scratch_shapes=[pltpu.VMEM((tm, D_MODEL), BF16)],
        compiler_params=pltpu.CompilerParams(
            dimension_semantics=("parallel", "arbitrary"), vmem_limit_bytes=VMEM_LIMIT),
        name="inproj",
    )(x, g, w)


def _outproj_kernel(m_ref, ms_ref, w_ref, x_ref, xs_ref, gpost_ref, x1_ref, x1s_ref):
    first_tile = pl.program_id(0) == 0
    k = pl.program_id(1)
    last = pl.num_programs(1) - 1

    @pl.when(k == 0)
    def _():
        x1_ref[...] = jnp.zeros_like(x1_ref)

    @pl.when(first_tile & (k == 0))
    def _():
        x1s_ref[...] = jnp.zeros_like(x1s_ref)

    @pl.when(first_tile)
    def _():
        x1s_ref[...] += _dot(ms_ref[...], w_ref[...])

    x1_ref[...] += _dot(m_ref[...], w_ref[...])

    @pl.when(k == last)
    def _():
        x1_ref[...] = x_ref[...] + _rms(x1_ref[...]) * gpost_ref[...]

    @pl.when(first_tile & (k == last))
    def _():
        x1s_ref[...] = xs_ref[...] + _rms(x1s_ref[...]) * gpost_ref[...]


def _outproj(mixed, mixed_s, w, x, x_s, g_post, tm, tk):
    m, ns = x.shape[0], x_s.shape[0]
    return pl.pallas_call(
        _outproj_kernel,
        grid=(m // tm, MIX_PAD // tk),
        in_specs=[
            pl.BlockSpec((tm, tk), lambda i, k: (i, k)),
            pl.BlockSpec((ns, tk), lambda i, k: (0, k)),
            pl.BlockSpec((tk, D_MODEL), lambda i, k: (k, 0)),
            pl.BlockSpec((tm, D_MODEL), lambda i, k: (i, 0)),
            pl.BlockSpec((ns, D_MODEL), lambda i, k: (0, 0)),
            pl.BlockSpec((1, D_MODEL), lambda i, k: (0, 0)),
        ],
        out_specs=[
            pl.BlockSpec((tm, D_MODEL), lambda i, k: (i, 0)),
            pl.BlockSpec((ns, D_MODEL), lambda i, k: (0, 0)),
        ],
        out_shape=[
            jax.ShapeDtypeStruct((m, D_MODEL), F32),
            jax.ShapeDtypeStruct((ns, D_MODEL), F32),
        ],
        compiler_params=pltpu.CompilerParams(
            dimension_semantics=("arbitrary", "arbitrary"), vmem_limit_bytes=VMEM_LIMIT),
        name="outproj",
    )(mixed, mixed_s, w, x, x_s, g_post)


def _ffn_kernel(x1_ref, x1s_ref, gpre_ref, wg_ref, wu_ref, wd_ref, gpost_ref, o_ref, os_ref,
                hn_ref, hns_ref):
    first_tile = pl.program_id(0) == 0
    f = pl.program_id(1)
    last = pl.num_programs(1) - 1

    def begin(x_ref, h_ref, acc_ref):
        h_ref[...] = (_rms(x_ref[...]) * gpre_ref[...]).astype(BF16)
        acc_ref[...] = jnp.zeros_like(acc_ref)

    def accumulate(h_ref, acc_ref):
        hn = h_ref[...]
        gate = _dot(hn, wg_ref[...].astype(BF16))
        up = _dot(hn, wu_ref[...].astype(BF16))
        act = (gate * jax.nn.sigmoid(gate) * up).astype(BF16)
        acc_ref[...] += _dot(act, wd_ref[...].astype(BF16))

    def finish(x_ref, acc_ref):
        acc_ref[...] = x_ref[...] + _rms(acc_ref[...]) * gpost_ref[...]

    pl.when(f == 0)(functools.partial(begin, x1_ref, hn_ref, o_ref))
    pl.when(first_tile & (f == 0))(functools.partial(begin, x1s_ref, hns_ref, os_ref))
    pl.when(first_tile)(functools.partial(accumulate, hns_ref, os_ref))
    accumulate(hn_ref, o_ref)
    pl.when(f == last)(functools.partial(finish, x1_ref, o_ref))
    pl.when(first_tile & (f == last))(functools.partial(finish, x1s_ref, os_ref))


def _ffn(x1, x1_s, g_pre, wg, wu, wd, g_post, layer, tm, tf):
    m, ns = x1.shape[0], x1_s.shape[0]
    return pl.pallas_call(
        _ffn_kernel,
        grid=(m // tm, D_FF // tf),
        in_specs=[
            pl.BlockSpec((tm, D_MODEL), lambda i, f: (i, 0), pipeline_mode=pl.Buffered(1)),
            pl.BlockSpec((ns, D_MODEL), lambda i, f: (0, 0)),
            pl.BlockSpec((1, D_MODEL), lambda i, f: (0, 0)),
            pl.BlockSpec((None, D_MODEL, tf), lambda i, f: (layer, 0, f)),
            pl.BlockSpec((None, D_MODEL, tf), lambda i, f: (layer, 0, f)),
            pl.BlockSpec((None, tf, D_MODEL), lambda i, f: (layer, f, 0)),
            pl.BlockSpec((1, D_MODEL), lambda i, f: (0, 0)),
        ],
        out_specs=[
            pl.BlockSpec((tm, D_MODEL), lambda i, f: (i, 0)),
            pl.BlockSpec((ns, D_MODEL), lambda i, f: (0, 0)),
        ],
        out_shape=[
            jax.ShapeDtypeStruct((m, D_MODEL), F32),
            jax.ShapeDtypeStruct((ns, D_MODEL), F32),
        ],
        scratch_shapes=[pltpu.VMEM((tm, D_MODEL), BF16), pltpu.VMEM((ns, D_MODEL), BF16)],
        compiler_params=pltpu.CompilerParams(
            dimension_semantics=("arbitrary", "arbitrary"), vmem_limit_bytes=VMEM_LIMIT),
        name="ffn",
    )(x1, x1_s, g_pre, wg, wu, wd, g_post)


SRC_MQ = 4 * RET_WIDTH
SRC_MK = SRC_MQ + ML_HEADS * ML_QK
SRC_MV = SRC_MK + ML_HEADS * ML_QK
SRC_MO = SRC_MV + ML_WIDTH
SRC_MI = SRC_MO + ML_WIDTH
SRC_MF = SRC_MI + ML_HEADS
SRC_PU = SRC_MF + ML_HEADS
IN_WIDTH = SRC_PU + POOL_WIDTH


def _prep_w_in_kernel(w_ref, o_ref):
    tk = w_ref.shape[1]

    def put(dst, pieces, width):
        rows = [w_ref[src:src + n, :] for src, n in pieces]
        have = sum(n for _, n in pieces)
        if have < width:
            rows.append(jnp.zeros((width - have, tk), F32))
        block = rows[0] if len(rows) == 1 else jnp.concatenate(rows, axis=0)
        o_ref[:, dst:dst + width] = block.T.astype(BF16)

    for r in range(0, 4 * RET_WIDTH, 2 * LANES):
        put(r, [(r, 2 * LANES)], 2 * LANES)
    for h in range(ML_HEADS):
        q_rows = [(SRC_MQ + h * ML_QK, ML_QK)]
        if h == 0:
            q_rows.append((SRC_MI, 2 * ML_HEADS))
        put(OFF_MQ + h * ML_QK_PAD, q_rows, ML_QK_PAD)
        put(OFF_MK + h * ML_QK_PAD, [(SRC_MK + h * ML_QK, ML_QK)], ML_QK_PAD)
        put(OFF_MV + h * ML_V_PAD, [(SRC_MV + h * ML_V, ML_V)], ML_V_PAD)
        put(OFF_MO + h * ML_V_PAD, [(SRC_MO + h * ML_V, ML_V)], ML_V_PAD)
    for r in range(0, POOL_WIDTH, 2 * LANES):
        put(OFF_PU + r, [(SRC_PU + r, 2 * LANES)], 2 * LANES)


def _prep_w_in(w_t, layer, tk=256):
    d = w_t.shape[2]
    return pl.pallas_call(
        _prep_w_in_kernel,
        grid=(d // tk,),
        in_specs=[pl.BlockSpec((None, IN_WIDTH, tk), lambda i: (layer, 0, i))],
        out_specs=pl.BlockSpec((tk, PROJ_PAD), lambda i: (i, 0)),
        out_shape=jax.ShapeDtypeStruct((d, PROJ_PAD), BF16),
        compiler_params=pltpu.CompilerParams(
            dimension_semantics=("parallel",), vmem_limit_bytes=VMEM_LIMIT),
        name="prep_w_in",
    )(w_t)


def _prep_w_out_kernel(w_ref, o_ref):
    def put(dst, src, rows):
        o_ref[dst:dst + rows, :] = w_ref[src:src + rows, :].astype(BF16)

    put(0, 0, RET_WIDTH)
    for h in range(ML_HEADS):
        dst = MIX_OFF_ML + h * ML_V_PAD
        put(dst, RET_WIDTH + h * ML_V, ML_V)
        o_ref[dst + ML_V:dst + ML_V_PAD, :] = jnp.zeros((ML_V_PAD - ML_V, o_ref.shape[1]), BF16)
    put(MIX_OFF_POOL, RET_WIDTH + ML_WIDTH, POOL_WIDTH)


def _prep_w_out(w, layer, tn=512):
    _, k, n = w.shape
    return pl.pallas_call(
        _prep_w_out_kernel,
        grid=(n // tn,),
        in_specs=[pl.BlockSpec((None, k, tn), lambda j: (layer, 0, j))],
        out_specs=pl.BlockSpec((MIX_PAD, tn), lambda j: (0, j)),
        out_shape=jax.ShapeDtypeStruct((MIX_PAD, n), BF16),
        compiler_params=pltpu.CompilerParams(
            dimension_semantics=("parallel",), vmem_limit_bytes=VMEM_LIMIT),
        name="prep_w_out",
    )(w)


PROMPT_BLOCK = 2


def _ret_head(q, k, vb, g, cos2, sin2, dintra, dq_col, dk_row, dch_row, g_norm, s_old, store):
    qb = _rotate(q, cos2, sin2).astype(BF16)
    kr_t = (_rotate(k, cos2, sin2) * (RET_DIM ** -0.5)).T
    kb = kr_t.astype(BF16)
    kdb = (kr_t * dk_row).astype(BF16)
    gate = g * jax.nn.sigmoid(g)
    yield
    scores = (_dot(qb, kb) * dintra).astype(BF16)
    yield
    o = _dot(scores, vb) + _dot(qb, s_old.astype(BF16)) * dq_col
    s_new = s_old * dch_row + _dot(kdb, vb)
    store(_rms(o) * g_norm * gate, s_new)


def _ml_head(q, k, vb, og, causal, b_c, i_c, b_r, i_r, g_norm, c_old, n_old, m_prev, store):
    k_t = k.T
    qb = q.astype(BF16)
    logw = jnp.where(causal, b_c - b_r + i_r, -jnp.inf)
    inter = b_c + m_prev
    m_t = jnp.maximum(inter, jnp.max(logw, axis=-1, keepdims=True))
    w = jnp.exp(logw - m_t)
    a = jnp.exp(inter - m_t)
    m_new = m_t[CHUNK - 1:CHUNK, :]
    b_last = b_c[CHUNK - 1:CHUNK, :]
    w_r = jnp.exp(b_last - b_r + i_r - m_new)
    w_c = jnp.exp(b_last - b_c + i_c - m_new)
    a_end = jnp.exp(b_last + m_prev - m_new)
    kwb = (k_t * w_r).astype(BF16)
    n_new = a_end * n_old + jnp.sum(k * w_c, axis=0, keepdims=True)
    qn = jnp.sum(q * n_old, axis=-1, keepdims=True)
    floor = jnp.exp(-m_t)
    gate = jax.nn.sigmoid(og)
    yield
    s = _dot(qb, k_t.astype(BF16)) * w
    den = jnp.sum(s, axis=-1, keepdims=True) + a * qn
    sb = s.astype(BF16)
    yield
    num = _dot(sb, vb) + a * _dot(qb, c_old.astype(BF16))
    c_new = a_end * c_old + _dot(kwb, vb)
    hh = num / jnp.maximum(jnp.abs(den), floor)
    store(_rms(hh, ML_V) * g_norm * gate, c_new, n_new, m_new)


def _prompt_front_kernel(x_ref, gpre_ref, w_ref, cos_ref, sin_ref, dintra_ref, dq_ref, dk_ref,
                         dch_ref, gret_ref, gml_ref, gb_r_ref, gb_c_ref, wpool_ref, pscale_ref,
                         mix_ref, sret_ref, sc_ref, sn_ref, sm_ref, spool_ref,
                         s_scr, c_scr, n_scr, m_scr, u_scr):
    bp = pl.program_id(0)
    c = pl.program_id(1)
    nseq = x_ref.shape[0]

    @pl.when(c == 0)
    def _():
        s_scr[...] = jnp.zeros_like(s_scr)
        c_scr[...] = jnp.zeros_like(c_scr)
        n_scr[...] = jnp.zeros_like(n_scr)
        m_scr[...] = jnp.zeros_like(m_scr)
        u_scr[:, 0:2 * SUBLANES, :] = jnp.zeros((nseq, 2 * SUBLANES, POOL_WIDTH), F32)

    xn = (_rms(x_ref[...].reshape(nseq * CHUNK, D_MODEL)) * gpre_ref[...]).astype(BF16)

    def proj(lo, width):
        return _dot(xn, w_ref[:, lo:lo + width])

    def seq_rows(t, i):
        return t[i * CHUNK:(i + 1) * CHUNK]

    cos2 = cos_ref[...]
    sin2 = sin_ref[...]
    pair = 2 * RET_DIM

    def ret_project(h0):
        return [functools.partial(proj, off + h0 * RET_DIM, pair)
                for off in (OFF_RQ, OFF_RK, OFF_RV, OFF_RG)]

    def ret_mix(h0, projected):
        pq, pk, pv, pg = projected
        instances = []
        for h in (h0, h0 + 1):
            lo, hi = (h - h0) * RET_DIM, (h - h0 + 1) * RET_DIM
            for i in range(nseq):
                def store(out, s_new, h=h, i=i):
                    s_scr[i, h] = s_new
                    mix_ref[i, :, h * RET_DIM:(h + 1) * RET_DIM] = out.astype(BF16)

                instances.append(_ret_head(
                    seq_rows(pq, i)[:, lo:hi], seq_rows(pk, i)[:, lo:hi],
                    seq_rows(pv, i)[:, lo:hi].astype(BF16), seq_rows(pg, i)[:, lo:hi],
                    cos2, sin2, dintra_ref[h], dq_ref[:, h:h + 1], dk_ref[h:h + 1, :],
                    dch_ref[h:h + 1, :], gret_ref[:, h * RET_DIM:(h + 1) * RET_DIM], s_scr[i, h],
                    store))
        return instances

    row = lax.broadcasted_iota(jnp.int32, (CHUNK, CHUNK), 0)
    col = lax.broadcasted_iota(jnp.int32, (CHUNK, CHUNK), 1)
    causal = col <= row
    upper = (row <= col).astype(F32)
    lower = causal.astype(F32)
    gates = [None] * nseq

    def ml_project(h0):
        return [functools.partial(proj, OFF_MQ + h0 * ML_QK_PAD, 2 * ML_QK_PAD),
                functools.partial(proj, OFF_MK + h0 * ML_QK_PAD, 2 * ML_QK_PAD),
                functools.partial(proj, OFF_MV + h0 * ML_V_PAD, 2 * ML_V_PAD),
                functools.partial(proj, OFF_MO + h0 * ML_V_PAD, 2 * ML_V_PAD)]

    def ml_mix(h0, projected):
        pq, pk, pv, po = projected
        if h0 == 0:
            for i in range(nseq):
                gq0 = seq_rows(pq, i)[:, 0:ML_QK_PAD]
                i_r8 = gq0.T[GATE_I_LANE:GATE_I_LANE + SUBLANES, :] + gb_r_ref[...]
                b_r8 = jnp.dot(_log_sigmoid(i_r8), upper, precision=_HI, preferred_element_type=F32)
                pre_c = gq0 + gb_c_ref[0:1, :]
                b_c_all = jnp.dot(lower, _log_sigmoid(pre_c), precision=_HI,
                                  preferred_element_type=F32)
                gates[i] = (i_r8, b_r8, pre_c, b_c_all)
        instances = []
        for h in (h0, h0 + 1):
            qlo = (h - h0) * ML_QK_PAD
            vlo = (h - h0) * ML_V_PAD
            for i in range(nseq):
                def store(out, c_new, n_new, m_new, h=h, i=i):
                    c_scr[i, h] = c_new
                    n_scr[i, h:h + 1, :] = n_new
                    m_scr[i, h:h + 1, :] = jnp.broadcast_to(m_new, (1, LANES))
                    mix_ref[i, :, MIX_OFF_ML + h * ML_V_PAD:MIX_OFF_ML + (h + 1) * ML_V_PAD] = (
                        out.astype(BF16))

                i_r8, b_r8, pre_c, b_c_all = gates[i]
                instances.append(_ml_head(
                    seq_rows(pq, i)[:, qlo:qlo + ML_QK_PAD],
                    seq_rows(pk, i)[:, qlo:qlo + ML_QK_PAD] * (ML_QK ** -0.5),
                    seq_rows(pv, i)[:, vlo:vlo + ML_V_PAD].astype(BF16),
                    seq_rows(po, i)[:, vlo:vlo + ML_V_PAD],
                    causal,
                    b_c_all[:, GATE_F_LANE + h:GATE_F_LANE + h + 1],
                    pre_c[:, GATE_I_LANE + h:GATE_I_LANE + h + 1],
                    b_r8[ML_HEADS + h:ML_HEADS + h + 1, :], i_r8[h:h + 1, :],
                    gml_ref[:, h * ML_V_PAD:(h + 1) * ML_V_PAD],
                    c_scr[i, h], n_scr[i, h:h + 1, :], m_scr[i, h:h + 1, 0:1], store))
        return instances

    base = 2 * SUBLANES
    pos = (c * CHUNK + lax.broadcasted_iota(jnp.int32, (CHUNK, 1), 0)).astype(F32)

    def pool_seq(u, i):
        u_scr[i, base:base + CHUNK, :] = u
        zs = []
        for g, win in enumerate(POOL_WINDOWS):
            lo, hi = g * POOL_GROUP, (g + 1) * POOL_GROUP
            wsum = u[:, lo:hi]
            for s in range(1, win):
                wsum = wsum + u_scr[i, base - s:base - s + CHUNK, lo:hi]
            zs.append((wsum / jnp.minimum(pos + 1.0, float(win)) - u[:, lo:hi]).astype(BF16))
        u_scr[i, 0:base, :] = u_scr[i, CHUNK:CHUNK + base, :]
        yield
        for g, z in enumerate(zs):
            lo, hi = g * POOL_GROUP, (g + 1) * POOL_GROUP
            y = _dot(z, wpool_ref[g]) * pscale_ref[:, lo:hi]
            mix_ref[i, :, MIX_OFF_POOL + lo:MIX_OFF_POOL + hi] = y.astype(BF16)

    def pool_mix(projected):
        pu, = projected
        return [pool_seq(seq_rows(pu, i), i) for i in range(nseq)]

    stages = [(ret_project(h0), functools.partial(ret_mix, h0)) for h0 in range(0, RET_HEADS, 2)]
    stages += [(ml_project(h0), functools.partial(ml_mix, h0)) for h0 in range(0, ML_HEADS, 2)]
    stages.append(([functools.partial(proj, OFF_PU, POOL_WIDTH)], pool_mix))
    projected = [dot() for dot in stages[0][0]]
    for n, (_, mix) in enumerate(stages):
        pending = list(stages[n + 1][0]) if n + 1 < len(stages) else []
        upcoming = []
        live = mix(projected)
        while live:
            for _ in range(max(1, len(pending) // 2) if pending else 0):
                upcoming.append(pending.pop(0)())
            live = [inst for inst in live if next(inst, StopIteration) is not StopIteration]
        upcoming += [dot() for dot in pending]
        projected = upcoming

    @pl.when(c == pl.num_programs(1) - 1)
    def _():
        for i in range(nseq):
            b = bp * nseq + i
            sret_ref[b] = s_scr[i]
            sc_ref[b] = c_scr[i]
            sn_ref[b] = n_scr[i]
            sm_ref[b] = m_scr[i]
            spool_ref[b] = u_scr[i, 0:base, :]


def _prompt_front(x, g_pre, w, tables, gains):
    batch, seq, _ = x.shape
    nchunk = seq // CHUNK
    nseq = PROMPT_BLOCK
    const2 = lambda b, c: (0, 0)
    const3 = lambda b, c: (0, 0, 0)
    const4 = lambda b, c: (0, 0, 0, 0)
    in_specs = [
        pl.BlockSpec((nseq, CHUNK, D_MODEL), lambda b, c: (b, c, 0)),
        pl.BlockSpec((1, D_MODEL), const2),
        pl.BlockSpec((D_MODEL, PROJ_PAD), const2, pipeline_mode=pl.Buffered(1)),
        pl.BlockSpec((CHUNK, RET_DIM), lambda b, c: (c, 0)),
        pl.BlockSpec((CHUNK, RET_DIM), lambda b, c: (c, 0)),
        pl.BlockSpec((RET_HEADS, CHUNK, CHUNK), const3),
        pl.BlockSpec((CHUNK, LANES), const2),
        pl.BlockSpec((SUBLANES, CHUNK), const2),
        pl.BlockSpec((SUBLANES, LANES), const2),
        pl.BlockSpec((1, RET_WIDTH), const2),
        pl.BlockSpec((1, ML_HEADS * ML_V_PAD), const2),
        pl.BlockSpec((SUBLANES, LANES), const2),
        pl.BlockSpec((SUBLANES, LANES), const2),
        pl.BlockSpec((len(POOL_WINDOWS), POOL_GROUP, POOL_GROUP), const3),
        pl.BlockSpec((1, POOL_WIDTH), const2),
    ]
    out_shape = [
        jax.ShapeDtypeStruct((batch, seq, MIX_PAD), BF16),
        jax.ShapeDtypeStruct((batch, RET_HEADS, RET_DIM, RET_DIM), F32),
        jax.ShapeDtypeStruct((batch, ML_HEADS, ML_QK_PAD, ML_V_PAD), F32),
        jax.ShapeDtypeStruct((batch, SUBLANES, LANES), F32),
        jax.ShapeDtypeStruct((batch, SUBLANES, LANES), F32),
        jax.ShapeDtypeStruct((batch, 2 * SUBLANES, POOL_WIDTH), F32),
    ]
    out_specs = [
        pl.BlockSpec((nseq, CHUNK, MIX_PAD), lambda b, c: (b, c, 0)),
        pl.BlockSpec((batch, RET_HEADS, RET_DIM, RET_DIM), const4),
        pl.BlockSpec((batch, ML_HEADS, ML_QK_PAD, ML_V_PAD), const4),
        pl.BlockSpec((batch, SUBLANES, LANES), const3),
        pl.BlockSpec((batch, SUBLANES, LANES), const3),
        pl.BlockSpec((batch, 2 * SUBLANES, POOL_WIDTH), const3),
    ]
    scratch = [
        pltpu.VMEM((nseq, RET_HEADS, RET_DIM, RET_DIM), F32),
        pltpu.VMEM((nseq, ML_HEADS, ML_QK_PAD, ML_V_PAD), F32),
        pltpu.VMEM((nseq, SUBLANES, LANES), F32),
        pltpu.VMEM((nseq, SUBLANES, LANES), F32),
        pltpu.VMEM((nseq, 2 * SUBLANES + CHUNK, POOL_WIDTH), F32),
    ]
    mixed, *states = pl.pallas_call(
        _prompt_front_kernel,
        grid=(batch // nseq, nchunk),
        in_specs=in_specs,
        out_specs=out_specs,
        out_shape=out_shape,
        scratch_shapes=scratch,
        compiler_params=pltpu.CompilerParams(
            dimension_semantics=("arbitrary", "arbitrary"), vmem_limit_bytes=VMEM_LIMIT),
        name="prompt_front",
    )(x, g_pre, w, tables["cos"], tables["sin"], tables["dintra"], tables["dq"], tables["dk"],
      tables["dch"], gains["g_ret"], gains["g_ml"], gains["gb_r"], gains["gb_c"],
      gains["w_pool"], gains["pool_scale"])
    return (mixed.reshape(batch * seq, MIX_PAD), *states)


DEC_BLOCK = 8
DEC_INPUTS = 14


def _decode_mixer_kernel(*refs, n_aliased):
    (p_ref, cos_ref, sin_ref, dch_ref, gret_ref, gml_ref, gb_c_ref, wpool_ref, pscale_ref,
     s_ref, c_ref, n_ref, m_ref, pool_ref) = refs[:DEC_INPUTS]
    (mix_ref, so_ref, co_ref, no_ref, mo_ref, poolo_ref,
     qr_scr, kt_scr, v_scr, qs_scr, qm_scr, wkt_scr, vm_scr, qc_scr, a_scr,
     cpad_scr) = refs[DEC_INPUTS + n_aliased:]
    if n_aliased == 0:
        for later in range(1, so_ref.shape[0]):
            so_ref[later] = jnp.zeros(so_ref.shape[1:], F32)
            co_ref[later] = jnp.zeros(co_ref.shape[1:], F32)
        so_ref, co_ref = so_ref.at[0], co_ref.at[0]
    step = pl.program_id(0)
    nb = p_ref.shape[0]
    cos2 = cos_ref[0:1, :]
    sin2 = sin_ref[0:1, :]

    def ret_qkv(h):
        lo, hi = h * RET_DIM, (h + 1) * RET_DIM
        qr = _rotate(p_ref[:, OFF_RQ + lo:OFF_RQ + hi], cos2, sin2)
        kr = _rotate(p_ref[:, OFF_RK + lo:OFF_RK + hi], cos2, sin2) * (RET_DIM ** -0.5)
        return qr, kr, p_ref[:, OFF_RV + lo:OFF_RV + hi]

    def ml_gates(h):
        pre_c = p_ref[:, OFF_MQ:OFF_MQ + ML_QK_PAD] + gb_c_ref[0:1, :]
        i_pre = pre_c[:, GATE_I_LANE + h:GATE_I_LANE + h + 1]
        lf = _log_sigmoid(pre_c[:, GATE_F_LANE + h:GATE_F_LANE + h + 1])
        inter = lf + m_ref[:, h:h + 1]
        m_t = jnp.maximum(inter, i_pre)
        return jnp.exp(i_pre - m_t), jnp.exp(inter - m_t), m_t

    def ml_qkv(h):
        q = p_ref[:, OFF_MQ + h * ML_QK_PAD:OFF_MQ + (h + 1) * ML_QK_PAD]
        k = p_ref[:, OFF_MK + h * ML_QK_PAD:OFF_MK + (h + 1) * ML_QK_PAD] * (ML_QK ** -0.5)
        return q, k, p_ref[:, OFF_MV + h * ML_V_PAD:OFF_MV + (h + 1) * ML_V_PAD]

    @pl.when(step == 0)
    def _():
        for h in range(RET_HEADS):
            qr, kr, v = ret_qkv(h)
            qr_scr[h] = qr
            kt_scr[h] = kr.T
            v_scr[h] = v.astype(BF16)
        for h in range(ML_HEADS):
            w, a, _ = ml_gates(h)
            q, k, v = ml_qkv(h)
            qm_scr[h] = q
            wkt_scr[h] = (w * k).T
            vm_scr[h] = v.astype(BF16)
            a_scr[h] = jnp.broadcast_to(a, (nb, ML_V_PAD))
        cpad_scr[...] = jnp.zeros_like(cpad_scr)

    lane = lax.broadcasted_iota(jnp.int32, (LANES, nb), 1)

    def body(bb, carry):
        b = step * DEC_BLOCK + bb
        sel = lane == b
        for h in range(RET_HEADS):
            s_old = s_ref[bb, h]
            q16 = jnp.broadcast_to(qr_scr[h, pl.ds(b, 1), :], (2 * SUBLANES, RET_DIM)).astype(BF16)
            qs_scr[h, pl.ds(b, 1), :] = _dot(q16, s_old.astype(BF16))[0:1, :]
            k_m = jnp.where(sel, kt_scr[h], 0.0).astype(BF16)
            so_ref[bb, h] = s_old * dch_ref[h:h + 1, :] + _dot(k_m, v_scr[h])
        for h in range(ML_HEADS):
            cpad_scr[0:ML_QK, 0:ML_V] = c_ref[bb, h]
            c_old = cpad_scr[...]
            q16 = jnp.broadcast_to(qm_scr[h, pl.ds(b, 1), :], (2 * SUBLANES, ML_QK_PAD)).astype(BF16)
            qc_scr[h, pl.ds(b, 1), :] = _dot(q16, c_old.astype(BF16))[0:1, :]
            wk_m = jnp.where(sel, wkt_scr[h], 0.0).astype(BF16)
            c_new = a_scr[h, pl.ds(b, 1), :] * c_old + _dot(wk_m, vm_scr[h])
            co_ref[bb, h] = c_new[0:ML_QK, 0:ML_V]
        return carry

    lax.fori_loop(0, DEC_BLOCK, body, 0)

    @pl.when(step == pl.num_programs(0) - 1)
    def _():
        for h in range(RET_HEADS):
            lo, hi = h * RET_DIM, (h + 1) * RET_DIM
            qr, kr, v = ret_qkv(h)
            g = p_ref[:, OFF_RG + lo:OFF_RG + hi]
            o = jnp.sum(qr * kr, axis=-1, keepdims=True) * v + qs_scr[h] * dch_ref[h:h + 1, :]
            out = _rms(o) * gret_ref[:, lo:hi] * (g * jax.nn.sigmoid(g))
            mix_ref[:, lo:hi] = out.astype(BF16)
        for h in range(ML_HEADS):
            w, a, m_t = ml_gates(h)
            q, k, v = ml_qkv(h)
            og = p_ref[:, OFF_MO + h * ML_V_PAD:OFF_MO + (h + 1) * ML_V_PAD]
            n_old = n_ref[:, h * ML_QK_PAD:(h + 1) * ML_QK_PAD]
            s = jnp.sum(q * k, axis=-1, keepdims=True) * w
            num = s * v + a * qc_scr[h]
            den = s + a * jnp.sum(q * n_old, axis=-1, keepdims=True)
            hh = num / jnp.maximum(jnp.abs(den), jnp.exp(-m_t))
            out = (_rms(hh, ML_V) * gml_ref[:, h * ML_V_PAD:(h + 1) * ML_V_PAD]
                   * jax.nn.sigmoid(og))
            mix_ref[:, MIX_OFF_ML + h * ML_V_PAD:MIX_OFF_ML + (h + 1) * ML_V_PAD] = out.astype(BF16)
            no_ref[:, h * ML_QK_PAD:(h + 1) * ML_QK_PAD] = a * n_old + w * k
            mo_ref[:, h:h + 1] = m_t
        u = p_ref[:, OFF_PU:OFF_PU + POOL_WIDTH]
        for g, win in enumerate(POOL_WINDOWS):
            lo, hi = g * POOL_GROUP, (g + 1) * POOL_GROUP
            wsum = u[:, lo:hi]
            for s in range(1, win):
                r = (POOL_BUF - s) * POOL_WIDTH
                wsum = wsum + pool_ref[:, r + lo:r + hi]
            z = wsum / float(win) - u[:, lo:hi]
            y = _dot(z.astype(BF16), wpool_ref[g]) * pscale_ref[:, lo:hi]
            mix_ref[:, MIX_OFF_POOL + lo:MIX_OFF_POOL + hi] = y.astype(BF16)
        keep = (POOL_BUF - 1) * POOL_WIDTH
        poolo_ref[:, 0:keep] = pool_ref[:, POOL_WIDTH:POOL_WIDTH + keep]
        poolo_ref[:, keep:keep + POOL_WIDTH] = u


def _decode_mixer(proj, tables, gains, layer, s_ret, s_c, s_n, s_m, s_pool, stacked_prev):
    nb = proj.shape[0]
    const2 = lambda s: (0, 0)
    const3 = lambda s: (0, 0, 0)
    blk5 = lambda s: (layer, s, 0, 0, 0)
    full2 = lambda shape: pl.BlockSpec(shape, const2)
    s_blk = pl.BlockSpec((None, DEC_BLOCK, RET_HEADS, RET_DIM, RET_DIM), blk5)
    c_blk = pl.BlockSpec((None, DEC_BLOCK, ML_HEADS, ML_QK, ML_V), blk5)
    in_specs = [
        full2((nb, PROJ_PAD)),
        full2((SUBLANES, RET_DIM)),
        full2((SUBLANES, RET_DIM)),
        full2((SUBLANES, LANES)),
        full2((1, RET_WIDTH)),
        full2((1, ML_HEADS * ML_V_PAD)),
        full2((SUBLANES, LANES)),
        pl.BlockSpec((len(POOL_WINDOWS), POOL_GROUP, POOL_GROUP), const3),
        full2((1, POOL_WIDTH)),
        s_blk,
        c_blk,
        full2((nb, ML_HEADS * ML_QK_PAD)),
        full2((nb, ML_HEADS)),
        full2((nb, POOL_BUF * POOL_WIDTH)),
    ]
    assert len(in_specs) == DEC_INPUTS
    in_specs += [pl.BlockSpec(memory_space=pl.ANY)] * len(stacked_prev)
    out_shape = [
        jax.ShapeDtypeStruct((nb, MIX_PAD), BF16),
        jax.ShapeDtypeStruct(s_ret.shape, F32),
        jax.ShapeDtypeStruct(s_c.shape, F32),
        jax.ShapeDtypeStruct((nb, ML_HEADS * ML_QK_PAD), F32),
        jax.ShapeDtypeStruct((nb, ML_HEADS), F32),
        jax.ShapeDtypeStruct((nb, POOL_BUF * POOL_WIDTH), F32),
    ]
    if stacked_prev:
        so_blk, co_blk = s_blk, c_blk
    else:
        assert layer == 0
        all_layers = lambda s: (0, s, 0, 0, 0)
        so_blk = pl.BlockSpec((s_ret.shape[0],) + s_blk.block_shape[1:], all_layers)
        co_blk = pl.BlockSpec((s_c.shape[0],) + c_blk.block_shape[1:], all_layers)
    out_specs = [
        full2((nb, MIX_PAD)),
        so_blk,
        co_blk,
        full2((nb, ML_HEADS * ML_QK_PAD)),
        full2((nb, ML_HEADS)),
        full2((nb, POOL_BUF * POOL_WIDTH)),
    ]
    scratch = [
        pltpu.VMEM((RET_HEADS, nb, RET_DIM), F32),
        pltpu.VMEM((RET_HEADS, RET_DIM, nb), F32),
        pltpu.VMEM((RET_HEADS, nb, RET_DIM), BF16),
        pltpu.VMEM((RET_HEADS, nb, RET_DIM), F32),
        pltpu.VMEM((ML_HEADS, nb, ML_QK_PAD), F32),
        pltpu.VMEM((ML_HEADS, ML_QK_PAD, nb), F32),
        pltpu.VMEM((ML_HEADS, nb, ML_V_PAD), BF16),
        pltpu.VMEM((ML_HEADS, nb, ML_V_PAD), F32),
        pltpu.VMEM((ML_HEADS, nb, ML_V_PAD), F32),
        pltpu.VMEM((ML_QK_PAD, ML_V_PAD), F32),
    ]
    return pl.pallas_call(
        functools.partial(_decode_mixer_kernel, n_aliased=len(stacked_prev)),
        grid=(nb // DEC_BLOCK,),
        in_specs=in_specs,
        out_specs=out_specs,
        out_shape=out_shape,
        scratch_shapes=scratch,
        input_output_aliases={DEC_INPUTS + i: 1 + i for i in range(len(stacked_prev))},
        compiler_params=pltpu.CompilerParams(
            dimension_semantics=("arbitrary",), vmem_limit_bytes=VMEM_LIMIT),
        name="decode_mixer",
    )(proj, tables["cos_s"], tables["sin_s"], tables["dch"], gains["g_ret"], gains["g_ml"],
      gains["gb_c"], gains["w_pool"], gains["pool_scale"], s_ret, s_c, s_n, s_m, s_pool,
      *stacked_prev)


def _tables(seq):
    half = RET_DIM // 2
    inv_freq = ROPE_THETA ** (-jnp.arange(half, dtype=F32) / half)

    def cos_sin(pos):
        ang = pos[:, None] * inv_freq[None, :]
        cos, sin = jnp.cos(ang), jnp.sin(ang)
        return jnp.concatenate([cos, cos], axis=-1), jnp.concatenate([-sin, sin], axis=-1)

    cos_p, sin_p = cos_sin(jnp.arange(seq, dtype=F32))
    cos_s, sin_s = cos_sin((PAST_LEN + jnp.arange(1)).astype(F32))
    log_gamma = jnp.log1p(-jnp.exp2(-5.0 - jnp.arange(RET_HEADS, dtype=F32)))
    idx = jnp.arange(CHUNK, dtype=F32)
    diff = idx[:, None] - idx[None, :]
    causal = diff >= 0
    lg = log_gamma[:, None, None]
    dintra = jnp.where(causal, jnp.exp(lg * jnp.where(causal, diff, 0.0)), 0.0)
    dq = jnp.exp(log_gamma[:, None] * (idx + 1.0))
    dk = jnp.exp(log_gamma[:, None] * (CHUNK - 1.0 - idx))

    def pad_rows(t, rows):
        return jnp.pad(t, ((0, rows - t.shape[0]), (0, 0)))

    def chunk_decay(c):
        return pad_rows(jnp.broadcast_to(jnp.exp(log_gamma * c)[:, None], (RET_HEADS, LANES)), SUBLANES)

    return {
        "cos": cos_p, "sin": sin_p,
        "cos_s": jnp.broadcast_to(cos_s, (SUBLANES, RET_DIM)),
        "sin_s": jnp.broadcast_to(sin_s, (SUBLANES, RET_DIM)),
        "dintra": dintra,
        "dq": jnp.pad(dq.T, ((0, 0), (0, LANES - RET_HEADS))),
        "dk": pad_rows(dk, SUBLANES),
        "dch": chunk_decay(float(CHUNK)),
        "dch_s": chunk_decay(1.0),
    }


def _pad_last(t, width):
    return jnp.pad(t, [(0, 0)] * (t.ndim - 1) + [(0, width - t.shape[-1])])


def _prep_gains(l, g_ret_norm, g_mlstm_norm, b_mlstm_i, b_mlstm_f, w_pool, pool_scale):
    gate_bias = jnp.concatenate([b_mlstm_i[l], b_mlstm_f[l]])
    lanes = jnp.concatenate([jnp.zeros((GATE_I_LANE,), F32), _pad_last(gate_bias, LANES - GATE_I_LANE)])
    return {
        "g_ret": g_ret_norm[l][None, :],
        "g_ml": _pad_last(g_mlstm_norm[l].reshape(ML_HEADS, ML_V), ML_V_PAD).reshape(1, -1),
        "gb_r": jnp.broadcast_to(gate_bias[:, None], (SUBLANES, LANES)),
        "gb_c": jnp.broadcast_to(lanes[None, :], (SUBLANES, LANES)),
        "w_pool": w_pool[l].astype(BF16),
        "pool_scale": pool_scale[l][None, :],
    }


@jax.jit
def kernel(x_prompt, x_sample, state_ret, state_mlstm_c, state_mlstm_n, state_mlstm_m, state_pool,
           w_in, w_out, g_ret_norm, g_mlstm_norm, b_mlstm_i, b_mlstm_f, w_pool, pool_scale,
           g_pre_mix, g_post_mix, g_pre_ffn, g_post_ffn, w_ffn_gate, w_ffn_up, w_ffn_down):
    batch, seq, d = x_prompt.shape
    nb = x_sample.shape[0]
    depth = w_in.shape[0]
    tables = _tables(seq)
    tables_s = dict(tables, dch=tables["dch_s"])
    w_in_t = jnp.swapaxes(w_in, 1, 2)

    yp = x_prompt.reshape(batch * seq, d)
    ys = x_sample.reshape(nb, d)
    new_p, new_s = [], []
    stacked_s = ()

    t = _token_tiles(batch * seq)
    t_s = _token_tiles(nb)
    for l in range(depth):
        w_in_l = _prep_w_in(w_in_t, l)
        w_out_l = _prep_w_out(w_out, l)
        gains = _prep_gains(l, g_ret_norm, g_mlstm_norm, b_mlstm_i, b_mlstm_f, w_pool, pool_scale)
        row = lambda g: g[l][None, :]

        mixed, s_ret, s_c, s_n, s_m, s_pool = _prompt_front(
            yp.reshape(batch, seq, d), row(g_pre_mix), w_in_l, tables, gains)
        new_p.append((s_ret, s_c[:, :, :ML_QK, :ML_V], s_n[:, :ML_HEADS, :ML_QK],
                      s_m[:, :ML_HEADS, 0], s_pool[:, 1:, :]))

        n_in = _pad_last(state_mlstm_n[l], ML_QK_PAD).reshape(nb, ML_HEADS * ML_QK_PAD)
        pool_in = state_pool[l].reshape(nb, POOL_BUF * POOL_WIDTH)
        proj = _inproj(ys, row(g_pre_mix), w_in_l, tm=t_s["inproj_tm"], tn=t_s["inproj_tn"])
        mixed_s, s_ret, s_c, s_n, s_m, s_pool = _decode_mixer(
            proj, tables_s, gains, l, state_ret, state_mlstm_c, n_in, state_mlstm_m[l], pool_in,
            stacked_s)

        x1, x1_s = _outproj(mixed, mixed_s, w_out_l, yp, ys, row(g_post_mix),
                            tm=t["outproj_tm"], tk=t["outproj_tk"])
        yp, ys = _ffn(x1, x1_s, row(g_pre_ffn), w_ffn_gate, w_ffn_up, w_ffn_down, row(g_post_ffn),
                      l, tm=t["ffn_tm"], tf=t["ffn_tf"])
        stacked_s = (s_ret, s_c)
        new_s.append((s_n.reshape(nb, ML_HEADS, ML_QK_PAD)[:, :, :ML_QK], s_m,
                      s_pool.reshape(nb, POOL_BUF, POOL_WIDTH)))

    stack = lambda states, i: jnp.stack([s[i] for s in states], axis=0)
    return ((yp.reshape(batch, seq, d), ys.reshape(nb, 1, d))
            + tuple(stack(new_p, i) for i in range(5))
            + stacked_s + tuple(stack(new_s, i) for i in range(3)))
```

```python
import functools

import jax
import jax.numpy as jnp
from jax import lax
from jax.experimental import pallas as pl
from jax.experimental.pallas import tpu as pltpu

F32 = jnp.float32
BF16 = jnp.bfloat16

D_MODEL = 2048
PAST_LEN = 16384
RET_HEADS = 6
RET_DIM = 128
RET_WIDTH = RET_HEADS * RET_DIM
ML_HEADS = 4
ML_QK = 96
ML_V = 192
ML_WIDTH = ML_HEADS * ML_V
POOL_WIDTH = 512
POOL_WINDOWS = (2, 4, 8, 16)
POOL_GROUP = POOL_WIDTH // len(POOL_WINDOWS)
POOL_BUF = max(POOL_WINDOWS) - 1
D_FF = 5632
CHUNK = 128
ROPE_THETA = 10000.0
EPS = 1e-6

LANES = 128
SUBLANES = 8
VMEM_LIMIT = 56 * 1024 * 1024

ML_QK_PAD = LANES
ML_V_PAD = 2 * LANES
OFF_RQ = 0
OFF_RK = OFF_RQ + RET_WIDTH
OFF_RV = OFF_RK + RET_WIDTH
OFF_RG = OFF_RV + RET_WIDTH
OFF_MQ = OFF_RG + RET_WIDTH
OFF_MK = OFF_MQ + ML_HEADS * ML_QK_PAD
OFF_MV = OFF_MK + ML_HEADS * ML_QK_PAD
OFF_MO = OFF_MV + ML_HEADS * ML_V_PAD
OFF_PU = OFF_MO + ML_HEADS * ML_V_PAD
PROJ_PAD = OFF_PU + POOL_WIDTH
GATE_I_LANE = ML_QK
GATE_F_LANE = ML_QK + ML_HEADS
MIX_OFF_ML = RET_WIDTH
MIX_OFF_POOL = MIX_OFF_ML + ML_HEADS * ML_V_PAD
MIX_PAD = MIX_OFF_POOL + POOL_WIDTH

_HI = lax.Precision.HIGHEST


def _token_tiles(m):
    return {
        "inproj_tm": min(m, 1024), "inproj_tn": 512,
        "outproj_tm": min(m, 1024), "outproj_tk": 768,
        "ffn_tm": min(m, 1024), "ffn_tf": 256,
    }


def _dot(a, b):
    return jnp.dot(a, b, preferred_element_type=F32)


def _rms(x, width=None):
    n = x.shape[-1] if width is None else width
    return x * lax.rsqrt(jnp.sum(x * x, axis=-1, keepdims=True) / n + EPS)


def _log_sigmoid(x):
    return jnp.minimum(x, 0.0) - jnp.log1p(jnp.exp(-jnp.abs(x)))


def _rotate(x, cos2, sin2):
    return x * cos2 + pltpu.roll(x, RET_DIM // 2, 1) * sin2


def _inproj_kernel(x_ref, g_ref, w_ref, o_ref, ot_ref, xn_ref):
    @pl.when(pl.program_id(1) == 0)
    def _():
        xn_ref[...] = (_rms(x_ref[...]) * g_ref[...]).astype(BF16)

    res = _dot(xn_ref[...], w_ref[...])
    o_ref[...] = res
    ot_ref[...] = res.T


def _inproj(x, g, w, tm, tn):
    m = x.shape[0]
    return pl.pallas_call(
        _inproj_kernel,
        grid=(m // tm, PROJ_PAD // tn),
        in_specs=[
            pl.BlockSpec((tm, D_MODEL), lambda i, j: (i, 0)),
            pl.BlockSpec((1, D_MODEL), lambda i, j: (0, 0)),
            pl.BlockSpec((D_MODEL, tn), lambda i, j: (0, j)),
        ],
        out_specs=[
            pl.BlockSpec((tm, tn), lambda i, j: (i, j)),
            pl.BlockSpec((tn, tm), lambda i, j: (j, i)),
        ],
        out_shape=[
            jax.ShapeDtypeStruct((m, PROJ_PAD), F32),
            jax.ShapeDtypeStruct((PROJ_PAD, m), F32),
        ],
        scratch_shapes=[pltpu.VMEM((tm, D_MODEL), BF16)],
        compiler_params=pltpu.CompilerParams(
            dimension_semantics=("parallel", "arbitrary"), vmem_limit_bytes=VMEM_LIMIT),
        name="inproj",
    )(x, g, w)


def _outproj_kernel(m_ref, ms_ref, w_ref, x_ref, xs_ref, gpost_ref, x1_ref, x1s_ref):
    first_tile = pl.program_id(0) == 0
    k = pl.program_id(1)
    last = pl.num_programs(1) - 1

    @pl.when(k == 0)
    def _():
        x1_ref[...] = jnp.zeros_like(x1_ref)

    @pl.when(first_tile & (k == 0))
    def _():
        x1s_ref[...] = jnp.zeros_like(x1s_ref)

    @pl.when(first_tile)
    def _():
        x1s_ref[...] += _dot(ms_ref[...], w_ref[...])

    x1_ref[...] += _dot(m_ref[...], w_ref[...])

    @pl.when(k == last)
    def _():
        x1_ref[...] = x_ref[...] + _rms(x1_ref[...]) * gpost_ref[...]

    @pl.when(first_tile & (k == last))
    def _():
        x1s_ref[...] = xs_ref[...] + _rms(x1s_ref[...]) * gpost_ref[...]


def _outproj(mixed, mixed_s, w, x, x_s, g_post, tm, tk):
    m, ns = x.shape[0], x_s.shape[0]
    return pl.pallas_call(
        _outproj_kernel,
        grid=(m // tm, MIX_PAD // tk),
        in_specs=[
            pl.BlockSpec((tm, tk), lambda i, k: (i, k)),
            pl.BlockSpec((ns, tk), lambda i, k: (0, k)),
            pl.BlockSpec((tk, D_MODEL), lambda i, k: (k, 0)),
            pl.BlockSpec((tm, D_MODEL), lambda i, k: (i, 0)),
            pl.BlockSpec((ns, D_MODEL), lambda i, k: (0, 0)),
            pl.BlockSpec((1, D_MODEL), lambda i, k: (0, 0)),
        ],
        out_specs=[
            pl.BlockSpec((tm, D_MODEL), lambda i, k: (i, 0)),
            pl.BlockSpec((ns, D_MODEL), lambda i, k: (0, 0)),
        ],
        out_shape=[
            jax.ShapeDtypeStruct((m, D_MODEL), F32),
            jax.ShapeDtypeStruct((ns, D_MODEL), F32),
        ],
        compiler_params=pltpu.CompilerParams(
            dimension_semantics=("arbitrary", "arbitrary"), vmem_limit_bytes=VMEM_LIMIT),
        name="outproj",
    )(mixed, mixed_s, w, x, x_s, g_post)


def _ffn_kernel(x1_ref, x1s_ref, gpre_ref, wg_ref, wu_ref, wd_ref, gpost_ref, o_ref, os_ref,
                hn_ref, hns_ref):
    first_tile = pl.program_id(0) == 0
    f = pl.program_id(1)
    last = pl.num_programs(1) - 1

    def begin(x_ref, h_ref, acc_ref):
        h_ref[...] = (_rms(x_ref[...]) * gpre_ref[...]).astype(BF16)
        acc_ref[...] = jnp.zeros_like(acc_ref)

    def accumulate(h_ref, acc_ref):
        hn = h_ref[...]
        gate = _dot(hn, wg_ref[...].astype(BF16))
        up = _dot(hn, wu_ref[...].astype(BF16))
        act = (gate * jax.nn.sigmoid(gate) * up).astype(BF16)
        acc_ref[...] += _dot(act, wd_ref[...].astype(BF16))

    def finish(x_ref, acc_ref):
        acc_ref[...] = x_ref[...] + _rms(acc_ref[...]) * gpost_ref[...]

    pl.when(f == 0)(functools.partial(begin, x1_ref, hn_ref, o_ref))
    pl.when(first_tile & (f == 0))(functools.partial(begin, x1s_ref, hns_ref, os_ref))
    pl.when(first_tile)(functools.partial(accumulate, hns_ref, os_ref))
    accumulate(hn_ref, o_ref)
    pl.when(f == last)(functools.partial(finish, x1_ref, o_ref))
    pl.when(first_tile & (f == last))(functools.partial(finish, x1s_ref, os_ref))


def _ffn(x1, x1_s, g_pre, wg, wu, wd, g_post, layer, tm, tf):
    m, ns = x1.shape[0], x1_s.shape[0]
    return pl.pallas_call(
        _ffn_kernel,
        grid=(m // tm, D_FF // tf),
        in_specs=[
            pl.BlockSpec((tm, D_MODEL), lambda i, f: (i, 0), pipeline_mode=pl.Buffered(1)),
            pl.BlockSpec((ns, D_MODEL), lambda i, f: (0, 0)),
            pl.BlockSpec((1, D_MODEL), lambda i, f: (0, 0)),
            pl.BlockSpec((None, D_MODEL, tf), lambda i, f: (layer, 0, f)),
            pl.BlockSpec((None, D_MODEL, tf), lambda i, f: (layer, 0, f)),
            pl.BlockSpec((None, tf, D_MODEL), lambda i, f: (layer, f, 0)),
            pl.BlockSpec((1, D_MODEL), lambda i, f: (0, 0)),
        ],
        out_specs=[
            pl.BlockSpec((tm, D_MODEL), lambda i, f: (i, 0)),
            pl.BlockSpec((ns, D_MODEL), lambda i, f: (0, 0)),
        ],
        out_shape=[
            jax.ShapeDtypeStruct((m, D_MODEL), F32),
            jax.ShapeDtypeStruct((ns, D_MODEL), F32),
        ],
        scratch_shapes=[pltpu.VMEM((tm, D_MODEL), BF16), pltpu.VMEM((ns, D_MODEL), BF16)],
        compiler_params=pltpu.CompilerParams(
            dimension_semantics=("arbitrary", "arbitrary"), vmem_limit_bytes=VMEM_LIMIT),
        name="ffn",
    )(x1, x1_s, g_pre, wg, wu, wd, g_post)


SRC_MQ = 4 * RET_WIDTH
SRC_MK = SRC_MQ + ML_HEADS * ML_QK
SRC_MV = SRC_MK + ML_HEADS * ML_QK
SRC_MO = SRC_MV + ML_WIDTH
SRC_MI = SRC_MO + ML_WIDTH
SRC_MF = SRC_MI + ML_HEADS
SRC_PU = SRC_MF + ML_HEADS
IN_WIDTH = SRC_PU + POOL_WIDTH


def _prep_w_in_kernel(w_ref, o_ref):
    tk = w_ref.shape[1]

    def put(dst, pieces, width):
        rows = [w_ref[src:src + n, :] for src, n in pieces]
        have = sum(n for _, n in pieces)
        if have < width:
            rows.append(jnp.zeros((width - have, tk), F32))
        block = rows[0] if len(rows) == 1 else jnp.concatenate(rows, axis=0)
        o_ref[:, dst:dst + width] = block.T.astype(BF16)

    for r in range(0, 4 * RET_WIDTH, 2 * LANES):
        put(r, [(r, 2 * LANES)], 2 * LANES)
    for h in range(ML_HEADS):
        q_rows = [(SRC_MQ + h * ML_QK, ML_QK)]
        if h == 0:
            q_rows.append((SRC_MI, 2 * ML_HEADS))
        put(OFF_MQ + h * ML_QK_PAD, q_rows, ML_QK_PAD)
        put(OFF_MK + h * ML_QK_PAD, [(SRC_MK + h * ML_QK, ML_QK)], ML_QK_PAD)
        put(OFF_MV + h * ML_V_PAD, [(SRC_MV + h * ML_V, ML_V)], ML_V_PAD)
        put(OFF_MO + h * ML_V_PAD, [(SRC_MO + h * ML_V, ML_V)], ML_V_PAD)
    for r in range(0, POOL_WIDTH, 2 * LANES):
        put(OFF_PU + r, [(SRC_PU + r, 2 * LANES)], 2 * LANES)


def _prep_w_in(w_t, layer, tk=256):
    d = w_t.shape[2]
    return pl.pallas_call(
        _prep_w_in_kernel,
        grid=(d // tk,),
        in_specs=[pl.BlockSpec((None, IN_WIDTH, tk), lambda i: (layer, 0, i))],
        out_specs=pl.BlockSpec((tk, PROJ_PAD), lambda i: (i, 0)),
        out_shape=jax.ShapeDtypeStruct((d, PROJ_PAD), BF16),
        compiler_params=pltpu.CompilerParams(
            dimension_semantics=("parallel",), vmem_limit_bytes=VMEM_LIMIT),
        name="prep_w_in",
    )(w_t)


def _prep_w_out_kernel(w_ref, o_ref):
    def put(dst, src, rows):
        o_ref[dst:dst + rows, :] = w_ref[src:src + rows, :].astype(BF16)

    put(0, 0, RET_WIDTH)
    for h in range(ML_HEADS):
        dst = MIX_OFF_ML + h * ML_V_PAD
        put(dst, RET_WIDTH + h * ML_V, ML_V)
        o_ref[dst + ML_V:dst + ML_V_PAD, :] = jnp.zeros((ML_V_PAD - ML_V, o_ref.shape[1]), BF16)
    put(MIX_OFF_POOL, RET_WIDTH + ML_WIDTH, POOL_WIDTH)


def _prep_w_out(w, layer, tn=512):
    _, k, n = w.shape
    return pl.pallas_call(
        _prep_w_out_kernel,
        grid=(n // tn,),
        in_specs=[pl.BlockSpec((None, k, tn), lambda j: (layer, 0, j))],
        out_specs=pl.BlockSpec((MIX_PAD, tn), lambda j: (0, j)),
        out_shape=jax.ShapeDtypeStruct((MIX_PAD, n), BF16),
        compiler_params=pltpu.CompilerParams(
            dimension_semantics=("parallel",), vmem_limit_bytes=VMEM_LIMIT),
        name="prep_w_out",
    )(w)


PROMPT_BLOCK = 2


def _ret_head(q, k, vb, g, cos2, sin2, dintra, dq_col, dk_row, dch_row, g_norm, s_old, store):
    qb = _rotate(q, cos2, sin2).astype(BF16)
    kr_t = (_rotate(k, cos2, sin2) * (RET_DIM ** -0.5)).T
    kb = kr_t.astype(BF16)
    kdb = (kr_t * dk_row).astype(BF16)
    gate = g * jax.nn.sigmoid(g)
    yield
    scores = (_dot(qb, kb) * dintra).astype(BF16)
    yield
    o = _dot(scores, vb) + _dot(qb, s_old.astype(BF16)) * dq_col
    s_new = s_old * dch_row + _dot(kdb, vb)
    store(_rms(o) * g_norm * gate, s_new)


def _ml_head(q, k, vb, og, causal, b_c, i_c, b_r, i_r, g_norm, c_old, n_old, m_prev, store):
    k_t = k.T
    qb = q.astype(BF16)
    logw = jnp.where(causal, b_c - b_r + i_r, -jnp.inf)
    inter = b_c + m_prev
    m_t = jnp.maximum(inter, jnp.max(logw, axis=-1, keepdims=True))
    w = jnp.exp(logw - m_t)
    a = jnp.exp(inter - m_t)
    m_new = m_t[CHUNK - 1:CHUNK, :]
    b_last = b_c[CHUNK - 1:CHUNK, :]
    w_r = jnp.exp(b_last - b_r + i_r - m_new)
    w_c = jnp.exp(b_last - b_c + i_c - m_new)
    a_end = jnp.exp(b_last + m_prev - m_new)
    kwb = (k_t * w_r).astype(BF16)
    n_new = a_end * n_old + jnp.sum(k * w_c, axis=0, keepdims=True)
    qn = jnp.sum(q * n_old, axis=-1, keepdims=True)
    floor = jnp.exp(-m_t)
    gate = jax.nn.sigmoid(og)
    yield
    s = _dot(qb, k_t.astype(BF16)) * w
    den = jnp.sum(s, axis=-1, keepdims=True) + a * qn
    sb = s.astype(BF16)
    yield
    num = _dot(sb, vb) + a * _dot(qb, c_old.astype(BF16))
    c_new = a_end * c_old + _dot(kwb, vb)
    hh = num / jnp.maximum(jnp.abs(den), floor)
    store(_rms(hh, ML_V) * g_norm * gate, c_new, n_new, m_new)


def _prompt_front_kernel(x_ref, gpre_ref, w_ref, cos_ref, sin_ref, dintra_ref, dq_ref, dk_ref,
                         dch_ref, gret_ref, gml_ref, gb_r_ref, gb_c_ref, wpool_ref, pscale_ref,
                         mix_ref, sret_ref, sc_ref, sn_ref, sm_ref, spool_ref,
                         s_scr, c_scr, n_scr, m_scr, u_scr):
    bp = pl.program_id(0)
    c = pl.program_id(1)
    nseq = x_ref.shape[0]

    @pl.when(c == 0)
    def _():
        s_scr[...] = jnp.zeros_like(s_scr)
        c_scr[...] = jnp.zeros_like(c_scr)
        n_scr[...] = jnp.zeros_like(n_scr)
        m_scr[...] = jnp.zeros_like(m_scr)
        u_scr[:, 0:2 * SUBLANES, :] = jnp.zeros((nseq, 2 * SUBLANES, POOL_WIDTH), F32)

    xn = (_rms(x_ref[...].reshape(nseq * CHUNK, D_MODEL)) * gpre_ref[...]).astype(BF16)

    def proj(lo, width):
        return _dot(xn, w_ref[:, lo:lo + width])

    def seq_rows(t, i):
        return t[i * CHUNK:(i + 1) * CHUNK]

    cos2 = cos_ref[...]
    sin2 = sin_ref[...]
    pair = 2 * RET_DIM

    def ret_project(h0):
        return [functools.partial(proj, off + h0 * RET_DIM, pair)
                for off in (OFF_RQ, OFF_RK, OFF_RV, OFF_RG)]

    def ret_mix(h0, projected):
        pq, pk, pv, pg = projected
        instances = []
        for h in (h0, h0 + 1):
            lo, hi = (h - h0) * RET_DIM, (h - h0 + 1) * RET_DIM
            for i in range(nseq):
                def store(out, s_new, h=h, i=i):
                    s_scr[i, h] = s_new
                    mix_ref[i, :, h * RET_DIM:(h + 1) * RET_DIM] = out.astype(BF16)

                instances.append(_ret_head(
                    seq_rows(pq, i)[:, lo:hi], seq_rows(pk, i)[:, lo:hi],
                    seq_rows(pv, i)[:, lo:hi].astype(BF16), seq_rows(pg, i)[:, lo:hi],
                    cos2, sin2, dintra_ref[h], dq_ref[:, h:h + 1], dk_ref[h:h + 1, :],
                    dch_ref[h:h + 1, :], gret_ref[:, h * RET_DIM:(h + 1) * RET_DIM], s_scr[i, h],
                    store))
        return instances

    row = lax.broadcasted_iota(jnp.int32, (CHUNK, CHUNK), 0)
    col = lax.broadcasted_iota(jnp.int32, (CHUNK, CHUNK), 1)
    causal = col <= row
    upper = (row <= col).astype(F32)
    lower = causal.astype(F32)
    gates = [None] * nseq

    def ml_project(h0):
        return [functools.partial(proj, OFF_MQ + h0 * ML_QK_PAD, 2 * ML_QK_PAD),
                functools.partial(proj, OFF_MK + h0 * ML_QK_PAD, 2 * ML_QK_PAD),
                functools.partial(proj, OFF_MV + h0 * ML_V_PAD, 2 * ML_V_PAD),
                functools.partial(proj, OFF_MO + h0 * ML_V_PAD, 2 * ML_V_PAD)]

    def ml_mix(h0, projected):
        pq, pk, pv, po = projected
        if h0 == 0:
            for i in range(nseq):
                gq0 = seq_rows(pq, i)[:, 0:ML_QK_PAD]
                i_r8 = gq0.T[GATE_I_LANE:GATE_I_LANE + SUBLANES, :] + gb_r_ref[...]
                b_r8 = jnp.dot(_log_sigmoid(i_r8), upper, precision=_HI, preferred_element_type=F32)
                pre_c = gq0 + gb_c_ref[0:1, :]
                b_c_all = jnp.dot(lower, _log_sigmoid(pre_c), precision=_HI,
                                  preferred_element_type=F32)
                gates[i] = (i_r8, b_r8, pre_c, b_c_all)
        instances = []
        for h in (h0, h0 + 1):
            qlo = (h - h0) * ML_QK_PAD
            vlo = (h - h0) * ML_V_PAD
            for i in range(nseq):
                def store(out, c_new, n_new, m_new, h=h, i=i):
                    c_scr[i, h] = c_new
                    n_scr[i, h:h + 1, :] = n_new
                    m_scr[i, h:h + 1, :] = jnp.broadcast_to(m_new, (1, LANES))
                    mix_ref[i, :, MIX_OFF_ML + h * ML_V_PAD:MIX_OFF_ML + (h + 1) * ML_V_PAD] = (
                        out.astype(BF16))

                i_r8, b_r8, pre_c, b_c_all = gates[i]
                instances.append(_ml_head(
                    seq_rows(pq, i)[:, qlo:qlo + ML_QK_PAD],
                    seq_rows(pk, i)[:, qlo:qlo + ML_QK_PAD] * (ML_QK ** -0.5),
                    seq_rows(pv, i)[:, vlo:vlo + ML_V_PAD].astype(BF16),
                    seq_rows(po, i)[:, vlo:vlo + ML_V_PAD],
                    causal,
                    b_c_all[:, GATE_F_LANE + h:GATE_F_LANE + h + 1],
                    pre_c[:, GATE_I_LANE + h:GATE_I_LANE + h + 1],
                    b_r8[ML_HEADS + h:ML_HEADS + h + 1, :], i_r8[h:h + 1, :],
                    gml_ref[:, h * ML_V_PAD:(h + 1) * ML_V_PAD],
                    c_scr[i, h], n_scr[i, h:h + 1, :], m_scr[i, h:h + 1, 0:1], store))
        return instances

    base = 2 * SUBLANES
    pos = (c * CHUNK + lax.broadcasted_iota(jnp.int32, (CHUNK, 1), 0)).astype(F32)

    def pool_seq(u, i):
        u_scr[i, base:base + CHUNK, :] = u
        zs = []
        for g, win in enumerate(POOL_WINDOWS):
            lo, hi = g * POOL_GROUP, (g + 1) * POOL_GROUP
            wsum = u[:, lo:hi]
            for s in range(1, win):
                wsum = wsum + u_scr[i, base - s:base - s + CHUNK, lo:hi]
            zs.append((wsum / jnp.minimum(pos + 1.0, float(win)) - u[:, lo:hi]).astype(BF16))
        u_scr[i, 0:base, :] = u_scr[i, CHUNK:CHUNK + base, :]
        yield
        for g, z in enumerate(zs):
            lo, hi = g * POOL_GROUP, (g + 1) * POOL_GROUP
            y = _dot(z, wpool_ref[g]) * pscale_ref[:, lo:hi]
            mix_ref[i, :, MIX_OFF_POOL + lo:MIX_OFF_POOL + hi] = y.astype(BF16)

    def pool_mix(projected):
        pu, = projected
        return [pool_seq(seq_rows(pu, i), i) for i in range(nseq)]

    stages = [(ret_project(h0), functools.partial(ret_mix, h0)) for h0 in range(0, RET_HEADS, 2)]
    stages += [(ml_project(h0), functools.partial(ml_mix, h0)) for h0 in range(0, ML_HEADS, 2)]
    stages.append(([functools.partial(proj, OFF_PU, POOL_WIDTH)], pool_mix))
    projected = [dot() for dot in stages[0][0]]
    for n, (_, mix) in enumerate(stages):
        pending = list(stages[n + 1][0]) if n + 1 < len(stages) else []
        upcoming = []
        live = mix(projected)
        while live:
            for _ in range(max(1, len(pending) // 2) if pending else 0):
                upcoming.append(pending.pop(0)())
            live = [inst for inst in live if next(inst, StopIteration) is not StopIteration]
        upcoming += [dot() for dot in pending]
        projected = upcoming

    @pl.when(c == pl.num_programs(1) - 1)
    def _():
        for i in range(nseq):
            b = bp * nseq + i
            sret_ref[b] = s_scr[i]
            sc_ref[b] = c_scr[i]
            sn_ref[b] = n_scr[i]
            sm_ref[b] = m_scr[i]
            spool_ref[b] = u_scr[i, 0:base, :]


def _prompt_front(x, g_pre, w, tables, gains):
    batch, seq, _ = x.shape
    nchunk = seq // CHUNK
    nseq = PROMPT_BLOCK
    const2 = lambda b, c: (0, 0)
    const3 = lambda b, c: (0, 0, 0)
    const4 = lambda b, c: (0, 0, 0, 0)
    in_specs = [
        pl.BlockSpec((nseq, CHUNK, D_MODEL), lambda b, c: (b, c, 0)),
        pl.BlockSpec((1, D_MODEL), const2),
        pl.BlockSpec((D_MODEL, PROJ_PAD), const2, pipeline_mode=pl.Buffered(1)),
        pl.BlockSpec((CHUNK, RET_DIM), lambda b, c: (c, 0)),
        pl.BlockSpec((CHUNK, RET_DIM), lambda b, c: (c, 0)),
        pl.BlockSpec((RET_HEADS, CHUNK, CHUNK), const3),
        pl.BlockSpec((CHUNK, LANES), const2),
        pl.BlockSpec((SUBLANES, CHUNK), const2),
        pl.BlockSpec((SUBLANES, LANES), const2),
        pl.BlockSpec((1, RET_WIDTH), const2),
        pl.BlockSpec((1, ML_HEADS * ML_V_PAD), const2),
        pl.BlockSpec((SUBLANES, LANES), const2),
        pl.BlockSpec((SUBLANES, LANES), const2),
        pl.BlockSpec((len(POOL_WINDOWS), POOL_GROUP, POOL_GROUP), const3),
        pl.BlockSpec((1, POOL_WIDTH), const2),
    ]
    out_shape = [
        jax.ShapeDtypeStruct((batch, seq, MIX_PAD), BF16),
        jax.ShapeDtypeStruct((batch, RET_HEADS, RET_DIM, RET_DIM), F32),
        jax.ShapeDtypeStruct((batch, ML_HEADS, ML_QK_PAD, ML_V_PAD), F32),
        jax.ShapeDtypeStruct((batch, SUBLANES, LANES), F32),
        jax.ShapeDtypeStruct((batch, SUBLANES, LANES), F32),
        jax.ShapeDtypeStruct((batch, 2 * SUBLANES, POOL_WIDTH), F32),
    ]
    out_specs = [
        pl.BlockSpec((nseq, CHUNK, MIX_PAD), lambda b, c: (b, c, 0)),
        pl.BlockSpec((batch, RET_HEADS, RET_DIM, RET_DIM), const4),
        pl.BlockSpec((batch, ML_HEADS, ML_QK_PAD, ML_V_PAD), const4),
        pl.BlockSpec((batch, SUBLANES, LANES), const3),
        pl.BlockSpec((batch, SUBLANES, LANES), const3),
        pl.BlockSpec((batch, 2 * SUBLANES, POOL_WIDTH), const3),
    ]
    scratch = [
        pltpu.VMEM((nseq, RET_HEADS, RET_DIM, RET_DIM), F32),
        pltpu.VMEM((nseq, ML_HEADS, ML_QK_PAD, ML_V_PAD), F32),
        pltpu.VMEM((nseq, SUBLANES, LANES), F32),
        pltpu.VMEM((nseq, SUBLANES, LANES), F32),
        pltpu.VMEM((nseq, 2 * SUBLANES + CHUNK, POOL_WIDTH), F32),
    ]
    mixed, *states = pl.pallas_call(
        _prompt_front_kernel,
        grid=(batch // nseq, nchunk),
        in_specs=in_specs,
        out_specs=out_specs,
        out_shape=out_shape,
        scratch_shapes=scratch,
        compiler_params=pltpu.CompilerParams(
            dimension_semantics=("arbitrary", "arbitrary"), vmem_limit_bytes=VMEM_LIMIT),
        name="prompt_front",
    )(x, g_pre, w, tables["cos"], tables["sin"], tables["dintra"], tables["dq"], tables["dk"],
      tables["dch"], gains["g_ret"], gains["g_ml"], gains["gb_r"], gains["gb_c"],
      gains["w_pool"], gains["pool_scale"])
    return (mixed.reshape(batch * seq, MIX_PAD), *states)


DEC_BLOCK = 8
DEC_RET_INPUTS = 9


def _decode_ret_pool_kernel(*refs, n_aliased):
    (p_ref, cos_ref, sin_ref, dch_ref, gret_ref, wpool_ref, pscale_ref, s_ref,
     pool_ref) = refs[:DEC_RET_INPUTS]
    (mix_ref, mixp_ref, so_ref, poolo_ref, qr_scr, kt_scr, v_scr,
     qs_scr) = refs[DEC_RET_INPUTS + n_aliased:]
    if n_aliased == 0:
        for later in range(1, so_ref.shape[0]):
            so_ref[later] = jnp.zeros(so_ref.shape[1:], F32)
        so_ref = so_ref.at[0]
    step = pl.program_id(0)
    nb = p_ref.shape[0]
    cos2 = cos_ref[0:1, :]
    sin2 = sin_ref[0:1, :]

    def ret_qkv(h):
        lo, hi = h * RET_DIM, (h + 1) * RET_DIM
        qr = _rotate(p_ref[:, OFF_RQ + lo:OFF_RQ + hi], cos2, sin2)
        kr = _rotate(p_ref[:, OFF_RK + lo:OFF_RK + hi], cos2, sin2) * (RET_DIM ** -0.5)
        return qr, kr, p_ref[:, OFF_RV + lo:OFF_RV + hi]

    @pl.when(step == 0)
    def _():
        for h in range(RET_HEADS):
            qr, kr, v = ret_qkv(h)
            qr_scr[h] = qr
            kt_scr[h] = kr.T
            v_scr[h] = v.astype(BF16)

    lane = lax.broadcasted_iota(jnp.int32, (LANES, nb), 1)

    def body(bb, carry):
        b = step * DEC_BLOCK + bb
        sel = lane == b
        for h in range(RET_HEADS):
            s_old = s_ref[bb, h]
            q16 = jnp.broadcast_to(qr_scr[h, pl.ds(b, 1), :], (2 * SUBLANES, RET_DIM)).astype(BF16)
            qs_scr[h, pl.ds(b, 1), :] = _dot(q16, s_old.astype(BF16))[0:1, :]
            k_m = jnp.where(sel, kt_scr[h], 0.0).astype(BF16)
            so_ref[bb, h] = s_old * dch_ref[h:h + 1, :] + _dot(k_m, v_scr[h])
        return carry

    lax.fori_loop(0, DEC_BLOCK, body, 0)

    @pl.when(step == pl.num_programs(0) - 1)
    def _():
        for h in range(RET_HEADS):
            lo, hi = h * RET_DIM, (h + 1) * RET_DIM
            qr, kr, v = ret_qkv(h)
            g = p_ref[:, OFF_RG + lo:OFF_RG + hi]
            o = jnp.sum(qr * kr, axis=-1, keepdims=True) * v + qs_scr[h] * dch_ref[h:h + 1, :]
            out = _rms(o) * gret_ref[:, lo:hi] * (g * jax.nn.sigmoid(g))
            mix_ref[:, lo:hi] = out.astype(BF16)
        u = p_ref[:, OFF_PU:OFF_PU + POOL_WIDTH]
        for g, win in enumerate(POOL_WINDOWS):
            lo, hi = g * POOL_GROUP, (g + 1) * POOL_GROUP
            wsum = u[:, lo:hi]
            for s in range(1, win):
                wsum = wsum + pool_ref[POOL_BUF - s, :, lo:hi]
            z = wsum / float(win) - u[:, lo:hi]
            y = _dot(z.astype(BF16), wpool_ref[g]) * pscale_ref[:, lo:hi]
            mixp_ref[:, lo:hi] = y.astype(BF16)
        for r in range(POOL_BUF - 1):
            poolo_ref[r] = pool_ref[r + 1]
        poolo_ref[POOL_BUF - 1] = u


def _decode_ret_pool(proj, tables, gains, layer, s_ret, pool_t, stacked_prev):
    nb = proj.shape[0]
    const2 = lambda s: (0, 0)
    const3 = lambda s: (0, 0, 0)
    full2 = lambda shape: pl.BlockSpec(shape, const2)
    s_blk = pl.BlockSpec((None, DEC_BLOCK, RET_HEADS, RET_DIM, RET_DIM),
                         lambda s: (layer, s, 0, 0, 0))
    pool_blk = (POOL_BUF, nb, POOL_WIDTH)
    in_specs = [
        full2((nb, PROJ_PAD)),
        full2((SUBLANES, RET_DIM)),
        full2((SUBLANES, RET_DIM)),
        full2((SUBLANES, LANES)),
        full2((1, RET_WIDTH)),
        pl.BlockSpec((len(POOL_WINDOWS), POOL_GROUP, POOL_GROUP), const3),
        full2((1, POOL_WIDTH)),
        s_blk,
        pl.BlockSpec((None,) + pool_blk, lambda s: (layer, 0, 0, 0)),
    ]
    assert len(in_specs) == DEC_RET_INPUTS
    in_specs += [pl.BlockSpec(memory_space=pl.ANY)] * len(stacked_prev)
    if stacked_prev:
        so_blk = s_blk
    else:
        assert layer == 0
        so_blk = pl.BlockSpec((s_ret.shape[0],) + s_blk.block_shape[1:],
                              lambda s: (0, s, 0, 0, 0))
    out_shape = [
        jax.ShapeDtypeStruct((nb, RET_WIDTH), BF16),
        jax.ShapeDtypeStruct((nb, POOL_WIDTH), BF16),
        jax.ShapeDtypeStruct(s_ret.shape, F32),
        jax.ShapeDtypeStruct(pool_blk, F32),
    ]
    out_specs = [
        full2((nb, RET_WIDTH)),
        full2((nb, POOL_WIDTH)),
        so_blk,
        pl.BlockSpec(pool_blk, const3),
    ]
    scratch = [
        pltpu.VMEM((RET_HEADS, nb, RET_DIM), F32),
        pltpu.VMEM((RET_HEADS, RET_DIM, nb), F32),
        pltpu.VMEM((RET_HEADS, nb, RET_DIM), BF16),
        pltpu.VMEM((RET_HEADS, nb, RET_DIM), F32),
    ]
    return pl.pallas_call(
        functools.partial(_decode_ret_pool_kernel, n_aliased=len(stacked_prev)),
        grid=(nb // DEC_BLOCK,),
        in_specs=in_specs,
        out_specs=out_specs,
        out_shape=out_shape,
        scratch_shapes=scratch,
        input_output_aliases={DEC_RET_INPUTS + i: 2 + i for i in range(len(stacked_prev))},
        compiler_params=pltpu.CompilerParams(
            dimension_semantics=("arbitrary",), vmem_limit_bytes=VMEM_LIMIT),
        name="decode_ret_pool",
    )(proj, tables["cos_s"], tables["sin_s"], tables["dch"], gains["g_ret"], gains["w_pool"],
      gains["pool_scale"], s_ret, pool_t, *stacked_prev)


ML_DBLK = 32
DEC_ML_INPUTS = 6


def _decode_mlstm_kernel(*refs, n_aliased):
    pt_ref, gb_ref, gml_ref, c_ref, n_ref, m_ref = refs[:DEC_ML_INPUTS]
    mix_ref, co_ref, no_ref, mo_ref, rows_scr, wk_scr, qc_scr = refs[DEC_ML_INPUTS + n_aliased:]
    if n_aliased == 0:
        for later in range(1, co_ref.shape[0]):
            co_ref[later] = jnp.zeros(co_ref.shape[1:], F32)
        co_ref = co_ref.at[0]
    h = pl.program_id(0)
    j = pl.program_id(1)
    q_row0 = pl.multiple_of(OFF_MQ + h * ML_QK_PAD, ML_QK_PAD)
    k_row0 = pl.multiple_of(OFF_MK + h * ML_QK_PAD, ML_QK_PAD)
    v_row0 = pl.multiple_of(OFF_MV + h * ML_V_PAD, ML_V_PAD)
    o_row0 = pl.multiple_of(OFF_MO + h * ML_V_PAD, ML_V_PAD)

    @pl.when(j == 0)
    def _():
        i_pre = pt_ref[pl.ds(OFF_MQ + GATE_I_LANE + h, 1), :] + gb_ref[pl.ds(h, 1), :]
        lf = _log_sigmoid(pt_ref[pl.ds(OFF_MQ + GATE_F_LANE + h, 1), :]
                          + gb_ref[pl.ds(ML_HEADS + h, 1), :])
        inter = lf + m_ref[pl.ds(h, 1), :]
        m_new = jnp.maximum(inter, i_pre)
        w = jnp.exp(i_pre - m_new)
        rows_scr[0:1, :] = jnp.exp(inter - m_new)
        rows_scr[1:2, :] = w
        rows_scr[2:3, :] = m_new
        wk_scr[...] = w * (pt_ref[pl.ds(k_row0, ML_QK), :] * (ML_QK ** -0.5))
        qc_scr[...] = jnp.zeros_like(qc_scr)

    a = rows_scr[0:1, :]
    v_t = pt_ref[pl.ds(v_row0, ML_V), :]

    def body(dd, qc):
        d = j * ML_DBLK + dd
        c_old = c_ref[dd]
        co_ref[dd] = a * c_old + wk_scr[pl.ds(d, 1), :] * v_t
        return qc + pt_ref[pl.ds(q_row0 + d, 1), :] * c_old

    qc_scr[...] = lax.fori_loop(0, ML_DBLK, body, qc_scr[...])

    @pl.when(j == pl.num_programs(1) - 1)
    def _():
        w = rows_scr[1:2, :]
        m_new = rows_scr[2:3, :]
        q_t = pt_ref[pl.ds(q_row0, ML_QK), :]
        k_t = pt_ref[pl.ds(k_row0, ML_QK), :] * (ML_QK ** -0.5)
        n_old = n_ref[...]
        s = jnp.sum(q_t * k_t, axis=0, keepdims=True) * w
        num = s * v_t + a * qc_scr[...]
        den = s + a * jnp.sum(q_t * n_old, axis=0, keepdims=True)
        hh = num / jnp.maximum(jnp.abs(den), jnp.exp(-m_new))
        scale = lax.rsqrt(jnp.sum(hh * hh, axis=0, keepdims=True) / ML_V + EPS)
        out_t = (hh * scale * gml_ref[0:ML_V, :]
                 * jax.nn.sigmoid(pt_ref[pl.ds(o_row0, ML_V), :]))
        out_t = jnp.concatenate([out_t, jnp.zeros((ML_V_PAD - ML_V, out_t.shape[1]), F32)], axis=0)
        mix_ref[...] = out_t.T.astype(BF16)
        no_ref[...] = a * n_old + wk_scr[...]
        mo_ref[pl.ds(h, 1), :] = m_new


def _decode_mlstm(proj_t, gains, layer, c_t, n_t, m_t, stacked_prev):
    nb = proj_t.shape[1]
    c_blk = pl.BlockSpec((None, None, ML_DBLK, ML_V, nb), lambda h, j: (layer, h, j, 0, 0))
    in_specs = [
        pl.BlockSpec((PROJ_PAD, nb), lambda h, j: (0, 0)),
        pl.BlockSpec((SUBLANES, LANES), lambda h, j: (0, 0)),
        pl.BlockSpec((None, ML_V_PAD, nb), lambda h, j: (h, 0, 0)),
        c_blk,
        pl.BlockSpec((None, None, ML_QK, nb), lambda h, j: (layer, h, 0, 0)),
        pl.BlockSpec((None, ML_HEADS, nb), lambda h, j: (layer, 0, 0)),
    ]
    assert len(in_specs) == DEC_ML_INPUTS
    in_specs += [pl.BlockSpec(memory_space=pl.ANY)] * len(stacked_prev)
    if stacked_prev:
        co_blk = c_blk
    else:
        assert layer == 0
        co_blk = pl.BlockSpec((c_t.shape[0], None, ML_DBLK, ML_V, nb),
                              lambda h, j: (0, h, j, 0, 0))
    out_shape = [
        jax.ShapeDtypeStruct((ML_HEADS, nb, ML_V_PAD), BF16),
        jax.ShapeDtypeStruct(c_t.shape, F32),
        jax.ShapeDtypeStruct((ML_HEADS, ML_QK, nb), F32),
        jax.ShapeDtypeStruct((ML_HEADS, nb), F32),
    ]
    out_specs = [
        pl.BlockSpec((None, nb, ML_V_PAD), lambda h, j: (h, 0, 0)),
        co_blk,
        pl.BlockSpec((None, ML_QK, nb), lambda h, j: (h, 0, 0)),
        pl.BlockSpec((ML_HEADS, nb), lambda h, j: (0, 0)),
    ]
    scratch = [
        pltpu.VMEM((SUBLANES, nb), F32),
        pltpu.VMEM((ML_QK, nb), F32),
        pltpu.VMEM((ML_V, nb), F32),
    ]
    return pl.pallas_call(
        functools.partial(_decode_mlstm_kernel, n_aliased=len(stacked_prev)),
        grid=(ML_HEADS, ML_QK // ML_DBLK),
        in_specs=in_specs,
        out_specs=out_specs,
        out_shape=out_shape,
        scratch_shapes=scratch,
        input_output_aliases={DEC_ML_INPUTS + i: 1 + i for i in range(len(stacked_prev))},
        compiler_params=pltpu.CompilerParams(
            dimension_semantics=("arbitrary", "arbitrary"), vmem_limit_bytes=VMEM_LIMIT),
        name="decode_mlstm",
    )(proj_t, gains["gb_r"], gains["g_ml_t"], c_t, n_t, m_t, *stacked_prev)


def _tables(seq):
    half = RET_DIM // 2
    inv_freq = ROPE_THETA ** (-jnp.arange(half, dtype=F32) / half)

    def cos_sin(pos):
        ang = pos[:, None] * inv_freq[None, :]
        cos, sin = jnp.cos(ang), jnp.sin(ang)
        return jnp.concatenate([cos, cos], axis=-1), jnp.concatenate([-sin, sin], axis=-1)

    cos_p, sin_p = cos_sin(jnp.arange(seq, dtype=F32))
    cos_s, sin_s = cos_sin((PAST_LEN + jnp.arange(1)).astype(F32))
    log_gamma = jnp.log1p(-jnp.exp2(-5.0 - jnp.arange(RET_HEADS, dtype=F32)))
    idx = jnp.arange(CHUNK, dtype=F32)
    diff = idx[:, None] - idx[None, :]
    causal = diff >= 0
    lg = log_gamma[:, None, None]
    dintra = jnp.where(causal, jnp.exp(lg * jnp.where(causal, diff, 0.0)), 0.0)
    dq = jnp.exp(log_gamma[:, None] * (idx + 1.0))
    dk = jnp.exp(log_gamma[:, None] * (CHUNK - 1.0 - idx))

    def pad_rows(t, rows):
        return jnp.pad(t, ((0, rows - t.shape[0]), (0, 0)))

    def chunk_decay(c):
        return pad_rows(jnp.broadcast_to(jnp.exp(log_gamma * c)[:, None], (RET_HEADS, LANES)), SUBLANES)

    return {
        "cos": cos_p, "sin": sin_p,
        "cos_s": jnp.broadcast_to(cos_s, (SUBLANES, RET_DIM)),
        "sin_s": jnp.broadcast_to(sin_s, (SUBLANES, RET_DIM)),
        "dintra": dintra,
        "dq": jnp.pad(dq.T, ((0, 0), (0, LANES - RET_HEADS))),
        "dk": pad_rows(dk, SUBLANES),
        "dch": chunk_decay(float(CHUNK)),
        "dch_s": chunk_decay(1.0),
    }


def _pad_last(t, width):
    return jnp.pad(t, [(0, 0)] * (t.ndim - 1) + [(0, width - t.shape[-1])])


def _prep_gains(l, g_ret_norm, g_mlstm_norm, b_mlstm_i, b_mlstm_f, w_pool, pool_scale):
    gate_bias = jnp.concatenate([b_mlstm_i[l], b_mlstm_f[l]])
    g_ml = _pad_last(g_mlstm_norm[l].reshape(ML_HEADS, ML_V), ML_V_PAD)
    lanes = jnp.concatenate([jnp.zeros((GATE_I_LANE,), F32), _pad_last(gate_bias, LANES - GATE_I_LANE)])
    return {
        "g_ret": g_ret_norm[l][None, :],
        "g_ml": g_ml.reshape(1, -1),
        "g_ml_t": jnp.broadcast_to(g_ml[:, :, None], (ML_HEADS, ML_V_PAD, LANES)),
        "gb_r": jnp.broadcast_to(gate_bias[:, None], (SUBLANES, LANES)),
        "gb_c": jnp.broadcast_to(lanes[None, :], (SUBLANES, LANES)),
        "w_pool": w_pool[l].astype(BF16),
        "pool_scale": pool_scale[l][None, :],
    }


@jax.jit
def kernel(x_prompt, x_sample, state_ret, state_mlstm_c, state_mlstm_n, state_mlstm_m, state_pool,
           w_in, w_out, g_ret_norm, g_mlstm_norm, b_mlstm_i, b_mlstm_f, w_pool, pool_scale,
           g_pre_mix, g_post_mix, g_pre_ffn, g_post_ffn, w_ffn_gate, w_ffn_up, w_ffn_down):
    batch, seq, d = x_prompt.shape
    nb = x_sample.shape[0]
    depth = w_in.shape[0]
    tables = _tables(seq)
    tables_s = dict(tables, dch=tables["dch_s"])
    w_in_t = jnp.swapaxes(w_in, 1, 2)

    yp = x_prompt.reshape(batch * seq, d)
    ys = x_sample.reshape(nb, d)
    new_p, new_s = [], []
    stacked_ret, stacked_c = (), ()
    c_t = jnp.transpose(state_mlstm_c, (0, 2, 3, 4, 1))
    n_t = jnp.transpose(state_mlstm_n, (0, 2, 3, 1))
    m_t = jnp.transpose(state_mlstm_m, (0, 2, 1))
    pool_t = jnp.transpose(state_pool, (0, 2, 1, 3))

    t = _token_tiles(batch * seq)
    t_s = _token_tiles(nb)
    for l in range(depth):
        w_in_l = _prep_w_in(w_in_t, l)
        w_out_l = _prep_w_out(w_out, l)
        gains = _prep_gains(l, g_ret_norm, g_mlstm_norm, b_mlstm_i, b_mlstm_f, w_pool, pool_scale)
        row = lambda g: g[l][None, :]

        mixed, s_ret, s_c, s_n, s_m, s_pool = _prompt_front(
            yp.reshape(batch, seq, d), row(g_pre_mix), w_in_l, tables, gains)
        new_p.append((s_ret, s_c[:, :, :ML_QK, :ML_V], s_n[:, :ML_HEADS, :ML_QK],
                      s_m[:, :ML_HEADS, 0], s_pool[:, 1:, :]))

        proj, proj_t = _inproj(ys, row(g_pre_mix), w_in_l, tm=t_s["inproj_tm"], tn=t_s["inproj_tn"])
        mix_ret, mix_pool, s_ret, pool_new = _decode_ret_pool(
            proj, tables_s, gains, l, state_ret, pool_t, stacked_ret)
        mix_ml, s_c, n_new, m_new = _decode_mlstm(proj_t, gains, l, c_t, n_t, m_t, stacked_c)
        mixed_s = jnp.concatenate(
            [mix_ret, jnp.swapaxes(mix_ml, 0, 1).reshape(nb, ML_HEADS * ML_V_PAD), mix_pool], axis=1)
        stacked_ret, stacked_c = (s_ret,), (s_c,)
        new_s.append((n_new, m_new, pool_new))

        x1, x1_s = _outproj(mixed, mixed_s, w_out_l, yp, ys, row(g_post_mix),
                            tm=t["outproj_tm"], tk=t["outproj_tk"])
        yp, ys = _ffn(x1, x1_s, row(g_pre_ffn), w_ffn_gate, w_ffn_up, w_ffn_down, row(g_post_ffn),
                      l, tm=t["ffn_tm"], tf=t["ffn_tf"])

    stack = lambda states, i: jnp.stack([s[i] for s in states], axis=0)
    return ((yp.reshape(batch, seq, d), ys.reshape(nb, 1, d))
            + tuple(stack(new_p, i) for i in range(5))
            + (stacked_ret[0],
               jnp.transpose(stacked_c[0], (0, 4, 1, 2, 3)),
               jnp.transpose(stack(new_s, 0), (0, 3, 1, 2)),
               jnp.transpose(stack(new_s, 1), (0, 2, 1)),
               jnp.transpose(stack(new_s, 2), (0, 2, 1, 3))))
```

```python
import functools

import jax
import jax.numpy as jnp
from jax import lax
from jax.experimental import pallas as pl
from jax.experimental.pallas import tpu as pltpu

F32 = jnp.float32
BF16 = jnp.bfloat16

D_MODEL = 2048
PAST_LEN = 16384
RET_HEADS = 6
RET_DIM = 128
RET_WIDTH = RET_HEADS * RET_DIM
ML_HEADS = 4
ML_QK = 96
ML_V = 192
ML_WIDTH = ML_HEADS * ML_V
POOL_WIDTH = 512
POOL_WINDOWS = (2, 4, 8, 16)
POOL_GROUP = POOL_WIDTH // len(POOL_WINDOWS)
POOL_BUF = max(POOL_WINDOWS) - 1
D_FF = 5632
CHUNK = 128
ROPE_THETA = 10000.0
EPS = 1e-6

LANES = 128
SUBLANES = 8
VMEM_LIMIT = 56 * 1024 * 1024

ML_QK_PAD = LANES
ML_V_PAD = 2 * LANES
OFF_RQ = 0
OFF_RK = OFF_RQ + RET_WIDTH
OFF_RV = OFF_RK + RET_WIDTH
OFF_RG = OFF_RV + RET_WIDTH
OFF_MQ = OFF_RG + RET_WIDTH
OFF_MK = OFF_MQ + ML_HEADS * ML_QK_PAD
OFF_MV = OFF_MK + ML_HEADS * ML_QK_PAD
OFF_MO = OFF_MV + ML_HEADS * ML_V_PAD
OFF_PU = OFF_MO + ML_HEADS * ML_V_PAD
PROJ_PAD = OFF_PU + POOL_WIDTH
GATE_I_LANE = ML_QK
GATE_F_LANE = ML_QK + ML_HEADS
MIX_OFF_ML = RET_WIDTH
MIX_OFF_POOL = MIX_OFF_ML + ML_HEADS * ML_V_PAD
MIX_PAD = MIX_OFF_POOL + POOL_WIDTH

_HI = lax.Precision.HIGHEST


def _token_tiles(m):
    return {
        "inproj_tm": min(m, 1024), "inproj_tn": 512,
        "outproj_tm": min(m, 1024), "outproj_tk": 768,
        "ffn_tm": min(m, 1024), "ffn_tf": 256,
    }


def _dot(a, b):
    return jnp.dot(a, b, preferred_element_type=F32)


def _rms(x, width=None):
    n = x.shape[-1] if width is None else width
    return x * lax.rsqrt(jnp.sum(x * x, axis=-1, keepdims=True) / n + EPS)


def _log_sigmoid(x):
    return jnp.minimum(x, 0.0) - jnp.log1p(jnp.exp(-jnp.abs(x)))


def _rotate(x, cos2, sin2):
    return x * cos2 + pltpu.roll(x, RET_DIM // 2, 1) * sin2


def _inproj_kernel(x_ref, g_ref, w_ref, o_ref, ot_ref, xn_ref):
    @pl.when(pl.program_id(1) == 0)
    def _():
        xn_ref[...] = (_rms(x_ref[...]) * g_ref[...]).astype(BF16)

    res = _dot(xn_ref[...], w_ref[...])
    o_ref[...] = res
    ot_ref[...] = res.T


def _inproj(x, g, w, tm, tn):
    m = x.shape[0]
    return pl.pallas_call(
        _inproj_kernel,
        grid=(m // tm, PROJ_PAD // tn),
        in_specs=[
            pl.BlockSpec((tm, D_MODEL), lambda i, j: (i, 0)),
            pl.BlockSpec((1, D_MODEL), lambda i, j: (0, 0)),
            pl.BlockSpec((D_MODEL, tn), lambda i, j: (0, j)),
        ],
        out_specs=[
            pl.BlockSpec((tm, tn), lambda i, j: (i, j)),
            pl.BlockSpec((tn, tm), lambda i, j: (j, i)),
        ],
        out_shape=[
            jax.ShapeDtypeStruct((m, PROJ_PAD), F32),
            jax.ShapeDtypeStruct((PROJ_PAD, m), F32),
        ],
        scratch_shapes=[pltpu.VMEM((tm, D_MODEL), BF16)],
        compiler_params=pltpu.CompilerParams(
            dimension_semantics=("parallel", "arbitrary"), vmem_limit_bytes=VMEM_LIMIT),
        name="inproj",
    )(x, g, w)


def _outproj_kernel(m_ref, ms_ref, w_ref, x_ref, xs_ref, gpost_ref, x1_ref, x1s_ref):
    first_tile = pl.program_id(0) == 0
    k = pl.program_id(1)
    last = pl.num_programs(1) - 1

    @pl.when(k == 0)
    def _():
        x1_ref[...] = jnp.zeros_like(x1_ref)

    @pl.when(first_tile & (k == 0))
    def _():
        x1s_ref[...] = jnp.zeros_like(x1s_ref)

    @pl.when(first_tile)
    def _():
        x1s_ref[...] += _dot(ms_ref[...], w_ref[...])

    x1_ref[...] += _dot(m_ref[...], w_ref[...])

    @pl.when(k == last)
    def _():
        x1_ref[...] = x_ref[...] + _rms(x1_ref[...]) * gpost_ref[...]

    @pl.when(first_tile & (k == last))
    def _():
        x1s_ref[...] = xs_ref[...] + _rms(x1s_ref[...]) * gpost_ref[...]


def _outproj(mixed, mixed_s, w, x, x_s, g_post, tm, tk):
    m, ns = x.shape[0], x_s.shape[0]
    return pl.pallas_call(
        _outproj_kernel,
        grid=(m // tm, MIX_PAD // tk),
        in_specs=[
            pl.BlockSpec((tm, tk), lambda i, k: (i, k)),
            pl.BlockSpec((ns, tk), lambda i, k: (0, k)),
            pl.BlockSpec((tk, D_MODEL), lambda i, k: (k, 0)),
            pl.BlockSpec((tm, D_MODEL), lambda i, k: (i, 0)),
            pl.BlockSpec((ns, D_MODEL), lambda i, k: (0, 0)),
            pl.BlockSpec((1, D_MODEL), lambda i, k: (0, 0)),
        ],
        out_specs=[
            pl.BlockSpec((tm, D_MODEL), lambda i, k: (i, 0)),
            pl.BlockSpec((ns, D_MODEL), lambda i, k: (0, 0)),
        ],
        out_shape=[
            jax.ShapeDtypeStruct((m, D_MODEL), F32),
            jax.ShapeDtypeStruct((ns, D_MODEL), F32),
        ],
        compiler_params=pltpu.CompilerParams(
            dimension_semantics=("arbitrary", "arbitrary"), vmem_limit_bytes=VMEM_LIMIT),
        name="outproj",
    )(mixed, mixed_s, w, x, x_s, g_post)


FFN_ROW_CHUNK = 256


def _ffn_kernel(x1_ref, x1s_ref, gpre_ref, wg_ref, wu_ref, wd_ref, gpost_ref, o_ref, os_ref,
                hn_ref, hns_ref):
    first_tile = pl.program_id(0) == 0
    f = pl.program_id(1)
    last = pl.num_programs(1) - 1

    def swiglu_down(hn, wg, wu, wd):
        gate = _dot(hn, wg)
        act = (gate * jax.nn.sigmoid(gate) * _dot(hn, wu)).astype(BF16)
        return _dot(act, wd)

    def middle_step(h_ref, acc_ref):
        acc_ref[...] += swiglu_down(h_ref[...], wg_ref[...].astype(BF16),
                                    wu_ref[...].astype(BF16), wd_ref[...].astype(BF16))

    def row_chunks(ref):
        rows = ref.shape[0]
        chunk = min(rows, FFN_ROW_CHUNK)
        return [slice(r, r + chunk) for r in range(0, rows, chunk)]

    def first_step(x_ref, h_ref, acc_ref):
        wg, wu, wd = (w[...].astype(BF16) for w in (wg_ref, wu_ref, wd_ref))
        for rows in row_chunks(x_ref):
            hn = (_rms(x_ref[rows, :]) * gpre_ref[...]).astype(BF16)
            h_ref[rows, :] = hn
            acc_ref[rows, :] = swiglu_down(hn, wg, wu, wd)

    def last_step(x_ref, h_ref, acc_ref):
        wg, wu, wd = (w[...].astype(BF16) for w in (wg_ref, wu_ref, wd_ref))
        for rows in row_chunks(x_ref):
            acc = acc_ref[rows, :] + swiglu_down(h_ref[rows, :], wg, wu, wd)
            acc_ref[rows, :] = x_ref[rows, :] + _rms(acc) * gpost_ref[...]

    middle = (f > 0) & (f < last)
    for on_tile, refs in ((first_tile, (x1s_ref, hns_ref, os_ref)), (True, (x1_ref, hn_ref, o_ref))):
        pl.when(on_tile & (f == 0))(functools.partial(first_step, *refs))
        pl.when(on_tile & middle)(functools.partial(middle_step, *refs[1:]))
        pl.when(on_tile & (f == last))(functools.partial(last_step, *refs))


def _ffn(x1, x1_s, g_pre, wg, wu, wd, g_post, layer, tm, tf):
    m, ns = x1.shape[0], x1_s.shape[0]
    return pl.pallas_call(
        _ffn_kernel,
        grid=(m // tm, D_FF // tf),
        in_specs=[
            pl.BlockSpec((tm, D_MODEL), lambda i, f: (i, 0), pipeline_mode=pl.Buffered(1)),
            pl.BlockSpec((ns, D_MODEL), lambda i, f: (0, 0)),
            pl.BlockSpec((1, D_MODEL), lambda i, f: (0, 0)),
            pl.BlockSpec((None, D_MODEL, tf), lambda i, f: (layer, 0, f)),
            pl.BlockSpec((None, D_MODEL, tf), lambda i, f: (layer, 0, f)),
            pl.BlockSpec((None, tf, D_MODEL), lambda i, f: (layer, f, 0)),
            pl.BlockSpec((1, D_MODEL), lambda i, f: (0, 0)),
        ],
        out_specs=[
            pl.BlockSpec((tm, D_MODEL), lambda i, f: (i, 0)),
            pl.BlockSpec((ns, D_MODEL), lambda i, f: (0, 0)),
        ],
        out_shape=[
            jax.ShapeDtypeStruct((m, D_MODEL), F32),
            jax.ShapeDtypeStruct((ns, D_MODEL), F32),
        ],
        scratch_shapes=[pltpu.VMEM((tm, D_MODEL), BF16), pltpu.VMEM((ns, D_MODEL), BF16)],
        compiler_params=pltpu.CompilerParams(
            dimension_semantics=("arbitrary", "arbitrary"), vmem_limit_bytes=VMEM_LIMIT),
        name="ffn",
    )(x1, x1_s, g_pre, wg, wu, wd, g_post)


SRC_MQ = 4 * RET_WIDTH
SRC_MK = SRC_MQ + ML_HEADS * ML_QK
SRC_MV = SRC_MK + ML_HEADS * ML_QK
SRC_MO = SRC_MV + ML_WIDTH
SRC_MI = SRC_MO + ML_WIDTH
SRC_MF = SRC_MI + ML_HEADS
SRC_PU = SRC_MF + ML_HEADS
IN_WIDTH = SRC_PU + POOL_WIDTH


def _prep_w_in_kernel(w_ref, o_ref):
    tk = w_ref.shape[1]

    def put(dst, pieces, width):
        rows = [w_ref[src:src + n, :] for src, n in pieces]
        have = sum(n for _, n in pieces)
        if have < width:
            rows.append(jnp.zeros((width - have, tk), F32))
        block = rows[0] if len(rows) == 1 else jnp.concatenate(rows, axis=0)
        o_ref[:, dst:dst + width] = block.T.astype(BF16)

    for r in range(0, 4 * RET_WIDTH, 2 * LANES):
        put(r, [(r, 2 * LANES)], 2 * LANES)
    for h in range(ML_HEADS):
        q_rows = [(SRC_MQ + h * ML_QK, ML_QK)]
        if h == 0:
            q_rows.append((SRC_MI, 2 * ML_HEADS))
        put(OFF_MQ + h * ML_QK_PAD, q_rows, ML_QK_PAD)
        put(OFF_MK + h * ML_QK_PAD, [(SRC_MK + h * ML_QK, ML_QK)], ML_QK_PAD)
        put(OFF_MV + h * ML_V_PAD, [(SRC_MV + h * ML_V, ML_V)], ML_V_PAD)
        put(OFF_MO + h * ML_V_PAD, [(SRC_MO + h * ML_V, ML_V)], ML_V_PAD)
    for r in range(0, POOL_WIDTH, 2 * LANES):
        put(OFF_PU + r, [(SRC_PU + r, 2 * LANES)], 2 * LANES)


def _prep_w_in(w_t, layer, tk=256):
    d = w_t.shape[2]
    return pl.pallas_call(
        _prep_w_in_kernel,
        grid=(d // tk,),
        in_specs=[pl.BlockSpec((None, IN_WIDTH, tk), lambda i: (layer, 0, i))],
        out_specs=pl.BlockSpec((tk, PROJ_PAD), lambda i: (i, 0)),
        out_shape=jax.ShapeDtypeStruct((d, PROJ_PAD), BF16),
        compiler_params=pltpu.CompilerParams(
            dimension_semantics=("parallel",), vmem_limit_bytes=VMEM_LIMIT),
        name="prep_w_in",
    )(w_t)


def _prep_w_out_kernel(w_ref, o_ref):
    def put(dst, src, rows):
        o_ref[dst:dst + rows, :] = w_ref[src:src + rows, :].astype(BF16)

    put(0, 0, RET_WIDTH)
    for h in range(ML_HEADS):
        dst = MIX_OFF_ML + h * ML_V_PAD
        put(dst, RET_WIDTH + h * ML_V, ML_V)
        o_ref[dst + ML_V:dst + ML_V_PAD, :] = jnp.zeros((ML_V_PAD - ML_V, o_ref.shape[1]), BF16)
    put(MIX_OFF_POOL, RET_WIDTH + ML_WIDTH, POOL_WIDTH)


def _prep_w_out(w, layer, tn=512):
    _, k, n = w.shape
    return pl.pallas_call(
        _prep_w_out_kernel,
        grid=(n // tn,),
        in_specs=[pl.BlockSpec((None, k, tn), lambda j: (layer, 0, j))],
        out_specs=pl.BlockSpec((MIX_PAD, tn), lambda j: (0, j)),
        out_shape=jax.ShapeDtypeStruct((MIX_PAD, n), BF16),
        compiler_params=pltpu.CompilerParams(
            dimension_semantics=("parallel",), vmem_limit_bytes=VMEM_LIMIT),
        name="prep_w_out",
    )(w)


PROMPT_BLOCK = 2


def _ret_head(q, k, get_v, get_g, cos2, sin2, dintra, dq_col, dk_row, dch_row, g_norm, s_old,
              store):
    qb = _rotate(q, cos2, sin2).astype(BF16)
    kr_t = (_rotate(k, cos2, sin2) * (RET_DIM ** -0.5)).T
    kb = kr_t.astype(BF16)
    kdb = (kr_t * dk_row).astype(BF16)
    yield
    scores = (_dot(qb, kb) * dintra).astype(BF16)
    vb = get_v().astype(BF16)
    g = get_g()
    gate = g * jax.nn.sigmoid(g)
    yield
    o = _dot(scores, vb) + _dot(qb, s_old.astype(BF16)) * dq_col
    s_new = s_old * dch_row + _dot(kdb, vb)
    store(_rms(o) * g_norm * gate, s_new)


def _ml_head(q, k, vb, og, causal, b_c, i_c, b_r, i_r, g_norm, c_old, n_old, m_prev, store):
    k_t = k.T
    qb = q.astype(BF16)
    logw = jnp.where(causal, b_c - b_r + i_r, -jnp.inf)
    inter = b_c + m_prev
    m_t = jnp.maximum(inter, jnp.max(logw, axis=-1, keepdims=True))
    w = jnp.exp(logw - m_t)
    a = jnp.exp(inter - m_t)
    m_new = m_t[CHUNK - 1:CHUNK, :]
    b_last = b_c[CHUNK - 1:CHUNK, :]
    w_r = jnp.exp(b_last - b_r + i_r - m_new)
    w_c = jnp.exp(b_last - b_c + i_c - m_new)
    a_end = jnp.exp(b_last + m_prev - m_new)
    kwb = (k_t * w_r).astype(BF16)
    n_new = a_end * n_old + jnp.sum(k * w_c, axis=0, keepdims=True)
    qn = jnp.sum(q * n_old, axis=-1, keepdims=True)
    floor = jnp.exp(-m_t)
    gate = jax.nn.sigmoid(og)
    yield
    s = _dot(qb, k_t.astype(BF16)) * w
    den = jnp.sum(s, axis=-1, keepdims=True) + a * qn
    sb = s.astype(BF16)
    yield
    num = _dot(sb, vb) + a * _dot(qb, c_old.astype(BF16))
    c_new = a_end * c_old + _dot(kwb, vb)
    hh = num / jnp.maximum(jnp.abs(den), floor)
    store(_rms(hh, ML_V) * g_norm * gate, c_new, n_new, m_new)


def _prompt_front_kernel(x_ref, gpre_ref, w_ref, cos_ref, sin_ref, dintra_ref, dq_ref, dk_ref,
                         dch_ref, gret_ref, gml_ref, gb_r_ref, gb_c_ref, wpool_ref, pscale_ref,
                         mix_ref, sret_ref, sc_ref, sn_ref, sm_ref, spool_ref,
                         s_scr, c_scr, n_scr, m_scr, u_scr):
    bp = pl.program_id(0)
    c = pl.program_id(1)
    nseq = x_ref.shape[0]

    @pl.when(c == 0)
    def _():
        s_scr[...] = jnp.zeros_like(s_scr)
        c_scr[...] = jnp.zeros_like(c_scr)
        n_scr[...] = jnp.zeros_like(n_scr)
        m_scr[...] = jnp.zeros_like(m_scr)
        u_scr[:, 0:2 * SUBLANES, :] = jnp.zeros((nseq, 2 * SUBLANES, POOL_WIDTH), F32)

    xn = (_rms(x_ref[...].reshape(nseq * CHUNK, D_MODEL)) * gpre_ref[...]).astype(BF16)

    def proj(lo, width):
        return _dot(xn, w_ref[:, lo:lo + width])

    def seq_rows(t, i):
        return t[i * CHUNK:(i + 1) * CHUNK]

    cos2 = cos_ref[...]
    sin2 = sin_ref[...]
    pair = 2 * RET_DIM

    def ret_project(h0):
        return [functools.partial(proj, off + h0 * RET_DIM, pair)
                for off in (OFF_RQ, OFF_RK, OFF_RV, OFF_RG)]

    def ret_mix(h0, projected):
        pq, pk = projected[:2]
        instances = []
        for h in (h0, h0 + 1):
            lo, hi = (h - h0) * RET_DIM, (h - h0 + 1) * RET_DIM
            for i in range(nseq):
                def store(out, s_new, h=h, i=i):
                    s_scr[i, h] = s_new
                    mix_ref[i, :, h * RET_DIM:(h + 1) * RET_DIM] = out.astype(BF16)

                def late(n, i=i, lo=lo, hi=hi):
                    return seq_rows(projected[n], i)[:, lo:hi]

                instances.append(_ret_head(
                    seq_rows(pq, i)[:, lo:hi], seq_rows(pk, i)[:, lo:hi],
                    functools.partial(late, 2), functools.partial(late, 3),
                    cos2, sin2, dintra_ref[h], dq_ref[:, h:h + 1], dk_ref[h:h + 1, :],
                    dch_ref[h:h + 1, :], gret_ref[:, h * RET_DIM:(h + 1) * RET_DIM], s_scr[i, h],
                    store))
        return instances

    row = lax.broadcasted_iota(jnp.int32, (CHUNK, CHUNK), 0)
    col = lax.broadcasted_iota(jnp.int32, (CHUNK, CHUNK), 1)
    causal = col <= row
    upper = (row <= col).astype(F32)
    lower = causal.astype(F32)
    gates = [None] * nseq

    def ml_project(h0):
        return [functools.partial(proj, OFF_MQ + h0 * ML_QK_PAD, 2 * ML_QK_PAD),
                functools.partial(proj, OFF_MK + h0 * ML_QK_PAD, 2 * ML_QK_PAD),
                functools.partial(proj, OFF_MV + h0 * ML_V_PAD, 2 * ML_V_PAD),
                functools.partial(proj, OFF_MO + h0 * ML_V_PAD, 2 * ML_V_PAD)]

    def ml_mix(h0, projected):
        pq, pk, pv, po = projected
        if h0 == 0:
            for i in range(nseq):
                gq0 = seq_rows(pq, i)[:, 0:ML_QK_PAD]
                i_r8 = gq0.T[GATE_I_LANE:GATE_I_LANE + SUBLANES, :] + gb_r_ref[...]
                b_r8 = jnp.dot(_log_sigmoid(i_r8), upper, precision=_HI, preferred_element_type=F32)
                pre_c = gq0 + gb_c_ref[0:1, :]
                b_c_all = jnp.dot(lower, _log_sigmoid(pre_c), precision=_HI,
                                  preferred_element_type=F32)
                gates[i] = (i_r8, b_r8, pre_c, b_c_all)
        instances = []
        for h in (h0, h0 + 1):
            qlo = (h - h0) * ML_QK_PAD
            vlo = (h - h0) * ML_V_PAD
            for i in range(nseq):
                def store(out, c_new, n_new, m_new, h=h, i=i):
                    c_scr[i, h] = c_new
                    n_scr[i, h:h + 1, :] = n_new
                    m_scr[i, h:h + 1, :] = jnp.broadcast_to(m_new, (1, LANES))
                    mix_ref[i, :, MIX_OFF_ML + h * ML_V_PAD:MIX_OFF_ML + (h + 1) * ML_V_PAD] = (
                        out.astype(BF16))

                i_r8, b_r8, pre_c, b_c_all = gates[i]
                instances.append(_ml_head(
                    seq_rows(pq, i)[:, qlo:qlo + ML_QK_PAD],
                    seq_rows(pk, i)[:, qlo:qlo + ML_QK_PAD] * (ML_QK ** -0.5),
                    seq_rows(pv, i)[:, vlo:vlo + ML_V_PAD].astype(BF16),
                    seq_rows(po, i)[:, vlo:vlo + ML_V_PAD],
                    causal,
                    b_c_all[:, GATE_F_LANE + h:GATE_F_LANE + h + 1],
                    pre_c[:, GATE_I_LANE + h:GATE_I_LANE + h + 1],
                    b_r8[ML_HEADS + h:ML_HEADS + h + 1, :], i_r8[h:h + 1, :],
                    gml_ref[:, h * ML_V_PAD:(h + 1) * ML_V_PAD],
                    c_scr[i, h], n_scr[i, h:h + 1, :], m_scr[i, h:h + 1, 0:1], store))
        return instances

    base = 2 * SUBLANES
    pos = (c * CHUNK + lax.broadcasted_iota(jnp.int32, (CHUNK, 1), 0)).astype(F32)

    def pool_seq(u, i):
        u_scr[i, base:base + CHUNK, :] = u
        zs = []
        for g, win in enumerate(POOL_WINDOWS):
            lo, hi = g * POOL_GROUP, (g + 1) * POOL_GROUP
            wsum = u[:, lo:hi]
            for s in range(1, win):
                wsum = wsum + u_scr[i, base - s:base - s + CHUNK, lo:hi]
            zs.append((wsum / jnp.minimum(pos + 1.0, float(win)) - u[:, lo:hi]).astype(BF16))
        u_scr[i, 0:base, :] = u_scr[i, CHUNK:CHUNK + base, :]
        yield
        for g, z in enumerate(zs):
            lo, hi = g * POOL_GROUP, (g + 1) * POOL_GROUP
            y = _dot(z, wpool_ref[g]) * pscale_ref[:, lo:hi]
            mix_ref[i, :, MIX_OFF_POOL + lo:MIX_OFF_POOL + hi] = y.astype(BF16)

    def pool_mix(projected):
        pu, = projected
        return [pool_seq(seq_rows(pu, i), i) for i in range(nseq)]

    stages = [(ret_project(h0), functools.partial(ret_mix, h0)) for h0 in range(0, RET_HEADS, 2)]
    stages += [(ml_project(h0), functools.partial(ml_mix, h0)) for h0 in range(0, ML_HEADS, 2)]
    stages.append(([functools.partial(proj, OFF_PU, POOL_WIDTH)], pool_mix))
    early = 2
    projected = [dot() for dot in stages[0][0][:early]]
    late_dots = list(stages[0][0][early:])
    for n, (_, mix) in enumerate(stages):
        pending = list(stages[n + 1][0]) if n + 1 < len(stages) else []
        upcoming = []
        live = mix(projected)
        while live:
            for _ in range(max(1, (len(late_dots) + len(pending)) // 2)):
                if late_dots:
                    projected.append(late_dots.pop(0)())
                elif pending:
                    upcoming.append(pending.pop(0)())
            live = [inst for inst in live if next(inst, StopIteration) is not StopIteration]
        upcoming += [dot() for dot in pending]
        projected = upcoming

    @pl.when(c == pl.num_programs(1) - 1)
    def _():
        for i in range(nseq):
            b = bp * nseq + i
            sret_ref[b] = s_scr[i]
            sc_ref[b] = c_scr[i]
            sn_ref[b] = n_scr[i]
            sm_ref[b] = m_scr[i]
            spool_ref[b] = u_scr[i, 0:base, :]


def _prompt_front(x, g_pre, w, tables, gains):
    batch, seq, _ = x.shape
    nchunk = seq // CHUNK
    nseq = PROMPT_BLOCK
    const2 = lambda b, c: (0, 0)
    const3 = lambda b, c: (0, 0, 0)
    const4 = lambda b, c: (0, 0, 0, 0)
    in_specs = [
        pl.BlockSpec((nseq, CHUNK, D_MODEL), lambda b, c: (b, c, 0)),
        pl.BlockSpec((1, D_MODEL), const2),
        pl.BlockSpec((D_MODEL, PROJ_PAD), const2, pipeline_mode=pl.Buffered(1)),
        pl.BlockSpec((CHUNK, RET_DIM), lambda b, c: (c, 0)),
        pl.BlockSpec((CHUNK, RET_DIM), lambda b, c: (c, 0)),
        pl.BlockSpec((RET_HEADS, CHUNK, CHUNK), const3),
        pl.BlockSpec((CHUNK, LANES), const2),
        pl.BlockSpec((SUBLANES, CHUNK), const2),
        pl.BlockSpec((SUBLANES, LANES), const2),
        pl.BlockSpec((1, RET_WIDTH), const2),
        pl.BlockSpec((1, ML_HEADS * ML_V_PAD), const2),
        pl.BlockSpec((SUBLANES, LANES), const2),
        pl.BlockSpec((SUBLANES, LANES), const2),
        pl.BlockSpec((len(POOL_WINDOWS), POOL_GROUP, POOL_GROUP), const3),
        pl.BlockSpec((1, POOL_WIDTH), const2),
    ]
    out_shape = [
        jax.ShapeDtypeStruct((batch, seq, MIX_PAD), BF16),
        jax.ShapeDtypeStruct((batch, RET_HEADS, RET_DIM, RET_DIM), F32),
        jax.ShapeDtypeStruct((batch, ML_HEADS, ML_QK_PAD, ML_V_PAD), F32),
        jax.ShapeDtypeStruct((batch, SUBLANES, LANES), F32),
        jax.ShapeDtypeStruct((batch, SUBLANES, LANES), F32),
        jax.ShapeDtypeStruct((batch, 2 * SUBLANES, POOL_WIDTH), F32),
    ]
    out_specs = [
        pl.BlockSpec((nseq, CHUNK, MIX_PAD), lambda b, c: (b, c, 0)),
        pl.BlockSpec((batch, RET_HEADS, RET_DIM, RET_DIM), const4),
        pl.BlockSpec((batch, ML_HEADS, ML_QK_PAD, ML_V_PAD), const4),
        pl.BlockSpec((batch, SUBLANES, LANES), const3),
        pl.BlockSpec((batch, SUBLANES, LANES), const3),
        pl.BlockSpec((batch, 2 * SUBLANES, POOL_WIDTH), const3),
    ]
    scratch = [
        pltpu.VMEM((nseq, RET_HEADS, RET_DIM, RET_DIM), F32),
        pltpu.VMEM((nseq, ML_HEADS, ML_QK_PAD, ML_V_PAD), F32),
        pltpu.VMEM((nseq, SUBLANES, LANES), F32),
        pltpu.VMEM((nseq, SUBLANES, LANES), F32),
        pltpu.VMEM((nseq, 2 * SUBLANES + CHUNK, POOL_WIDTH), F32),
    ]
    mixed, *states = pl.pallas_call(
        _prompt_front_kernel,
        grid=(batch // nseq, nchunk),
        in_specs=in_specs,
        out_specs=out_specs,
        out_shape=out_shape,
        scratch_shapes=scratch,
        compiler_params=pltpu.CompilerParams(
            dimension_semantics=("arbitrary", "arbitrary"), vmem_limit_bytes=VMEM_LIMIT),
        name="prompt_front",
    )(x, g_pre, w, tables["cos"], tables["sin"], tables["dintra"], tables["dq"], tables["dk"],
      tables["dch"], gains["g_ret"], gains["g_ml"], gains["gb_r"], gains["gb_c"],
      gains["w_pool"], gains["pool_scale"])
    return (mixed.reshape(batch * seq, MIX_PAD), *states)


DEC_BLOCK = 8
DEC_RET_INPUTS = 9


def _decode_ret_pool_kernel(*refs, n_aliased):
    (p_ref, cos_ref, sin_ref, dch_ref, gret_ref, wpool_ref, pscale_ref, s_ref,
     pool_ref) = refs[:DEC_RET_INPUTS]
    (mix_ref, mixp_ref, so_ref, poolo_ref, qr_scr, kt_scr, v_scr,
     qs_scr) = refs[DEC_RET_INPUTS + n_aliased:]
    if n_aliased == 0:
        for later in range(1, so_ref.shape[0]):
            so_ref[later] = jnp.zeros(so_ref.shape[1:], F32)
        so_ref = so_ref.at[0]
    step = pl.program_id(0)
    nb = p_ref.shape[0]
    cos2 = cos_ref[0:1, :]
    sin2 = sin_ref[0:1, :]

    def ret_qkv(h):
        lo, hi = h * RET_DIM, (h + 1) * RET_DIM
        qr = _rotate(p_ref[:, OFF_RQ + lo:OFF_RQ + hi], cos2, sin2)
        kr = _rotate(p_ref[:, OFF_RK + lo:OFF_RK + hi], cos2, sin2) * (RET_DIM ** -0.5)
        return qr, kr, p_ref[:, OFF_RV + lo:OFF_RV + hi]

    @pl.when(step == 0)
    def _():
        for h in range(RET_HEADS):
            qr, kr, v = ret_qkv(h)
            qr_scr[h] = qr
            kt_scr[h] = kr.T
            v_scr[h] = v.astype(BF16)

    lane = lax.broadcasted_iota(jnp.int32, (LANES, nb), 1)

    def body(bb, carry):
        b = step * DEC_BLOCK + bb
        sel = lane == b
        for h in range(RET_HEADS):
            s_old = s_ref[bb, h]
            q16 = jnp.broadcast_to(qr_scr[h, pl.ds(b, 1), :], (2 * SUBLANES, RET_DIM)).astype(BF16)
            qs_scr[h, pl.ds(b, 1), :] = _dot(q16, s_old.astype(BF16))[0:1, :]
            k_m = jnp.where(sel, kt_scr[h], 0.0).astype(BF16)
            so_ref[bb, h] = s_old * dch_ref[h:h + 1, :] + _dot(k_m, v_scr[h])
        return carry

    lax.fori_loop(0, DEC_BLOCK, body, 0)

    @pl.when(step == pl.num_programs(0) - 1)
    def _():
        for h in range(RET_HEADS):
            lo, hi = h * RET_DIM, (h + 1) * RET_DIM
            qr, kr, v = ret_qkv(h)
            g = p_ref[:, OFF_RG + lo:OFF_RG + hi]
            o = jnp.sum(qr * kr, axis=-1, keepdims=True) * v + qs_scr[h] * dch_ref[h:h + 1, :]
            out = _rms(o) * gret_ref[:, lo:hi] * (g * jax.nn.sigmoid(g))
            mix_ref[:, lo:hi] = out.astype(BF16)
        u = p_ref[:, OFF_PU:OFF_PU + POOL_WIDTH]
        for g, win in enumerate(POOL_WINDOWS):
            lo, hi = g * POOL_GROUP, (g + 1) * POOL_GROUP
            wsum = u[:, lo:hi]
            for s in range(1, win):
                wsum = wsum + pool_ref[POOL_BUF - s, :, lo:hi]
            z = wsum / float(win) - u[:, lo:hi]
            y = _dot(z.astype(BF16), wpool_ref[g]) * pscale_ref[:, lo:hi]
            mixp_ref[:, lo:hi] = y.astype(BF16)
        for r in range(POOL_BUF - 1):
            poolo_ref[r] = pool_ref[r + 1]
        poolo_ref[POOL_BUF - 1] = u


def _decode_ret_pool(proj, tables, gains, layer, s_ret, pool_t, stacked_prev):
    nb = proj.shape[0]
    const2 = lambda s: (0, 0)
    const3 = lambda s: (0, 0, 0)
    full2 = lambda shape: pl.BlockSpec(shape, const2)
    s_blk = pl.BlockSpec((None, DEC_BLOCK, RET_HEADS, RET_DIM, RET_DIM),
                         lambda s: (layer, s, 0, 0, 0))
    pool_blk = (POOL_BUF, nb, POOL_WIDTH)
    in_specs = [
        full2((nb, PROJ_PAD)),
        full2((SUBLANES, RET_DIM)),
        full2((SUBLANES, RET_DIM)),
        full2((SUBLANES, LANES)),
        full2((1, RET_WIDTH)),
        pl.BlockSpec((len(POOL_WINDOWS), POOL_GROUP, POOL_GROUP), const3),
        full2((1, POOL_WIDTH)),
        s_blk,
        pl.BlockSpec((None,) + pool_blk, lambda s: (layer, 0, 0, 0)),
    ]
    assert len(in_specs) == DEC_RET_INPUTS
    in_specs += [pl.BlockSpec(memory_space=pl.ANY)] * len(stacked_prev)
    if stacked_prev:
        so_blk = s_blk
    else:
        assert layer == 0
        so_blk = pl.BlockSpec((s_ret.shape[0],) + s_blk.block_shape[1:],
                              lambda s: (0, s, 0, 0, 0))
    out_shape = [
        jax.ShapeDtypeStruct((nb, RET_WIDTH), BF16),
        jax.ShapeDtypeStruct((nb, POOL_WIDTH), BF16),
        jax.ShapeDtypeStruct(s_ret.shape, F32),
        jax.ShapeDtypeStruct(pool_blk, F32),
    ]
    out_specs = [
        full2((nb, RET_WIDTH)),
        full2((nb, POOL_WIDTH)),
        so_blk,
        pl.BlockSpec(pool_blk, const3),
    ]
    scratch = [
        pltpu.VMEM((RET_HEADS, nb, RET_DIM), F32),
        pltpu.VMEM((RET_HEADS, RET_DIM, nb), F32),
        pltpu.VMEM((RET_HEADS, nb, RET_DIM), BF16),
        pltpu.VMEM((RET_HEADS, nb, RET_DIM), F32),
    ]
    return pl.pallas_call(
        functools.partial(_decode_ret_pool_kernel, n_aliased=len(stacked_prev)),
        grid=(nb // DEC_BLOCK,),
        in_specs=in_specs,
        out_specs=out_specs,
        out_shape=out_shape,
        scratch_shapes=scratch,
        input_output_aliases={DEC_RET_INPUTS + i: 2 + i for i in range(len(stacked_prev))},
        compiler_params=pltpu.CompilerParams(
            dimension_semantics=("arbitrary",), vmem_limit_bytes=VMEM_LIMIT),
        name="decode_ret_pool",
    )(proj, tables["cos_s"], tables["sin_s"], tables["dch"], gains["g_ret"], gains["w_pool"],
      gains["pool_scale"], s_ret, pool_t, *stacked_prev)


ML_DBLK = 32
DEC_ML_INPUTS = 6


def _decode_mlstm_kernel(*refs, n_aliased):
    pt_ref, gb_ref, gml_ref, c_ref, n_ref, m_ref = refs[:DEC_ML_INPUTS]
    mix_ref, co_ref, no_ref, mo_ref, rows_scr, wk_scr, qc_scr = refs[DEC_ML_INPUTS + n_aliased:]
    if n_aliased == 0:
        for later in range(1, co_ref.shape[0]):
            co_ref[later] = jnp.zeros(co_ref.shape[1:], F32)
        co_ref = co_ref.at[0]
    h = pl.program_id(0)
    j = pl.program_id(1)
    q_row0 = pl.multiple_of(OFF_MQ + h * ML_QK_PAD, ML_QK_PAD)
    k_row0 = pl.multiple_of(OFF_MK + h * ML_QK_PAD, ML_QK_PAD)
    v_row0 = pl.multiple_of(OFF_MV + h * ML_V_PAD, ML_V_PAD)
    o_row0 = pl.multiple_of(OFF_MO + h * ML_V_PAD, ML_V_PAD)

    @pl.when(j == 0)
    def _():
        i_pre = pt_ref[pl.ds(OFF_MQ + GATE_I_LANE + h, 1), :] + gb_ref[pl.ds(h, 1), :]
        lf = _log_sigmoid(pt_ref[pl.ds(OFF_MQ + GATE_F_LANE + h, 1), :]
                          + gb_ref[pl.ds(ML_HEADS + h, 1), :])
        inter = lf + m_ref[pl.ds(h, 1), :]
        m_new = jnp.maximum(inter, i_pre)
        w = jnp.exp(i_pre - m_new)
        rows_scr[0:1, :] = jnp.exp(inter - m_new)
        rows_scr[1:2, :] = w
        rows_scr[2:3, :] = m_new
        wk_scr[...] = w * (pt_ref[pl.ds(k_row0, ML_QK), :] * (ML_QK ** -0.5))
        qc_scr[...] = jnp.zeros_like(qc_scr)

    a = rows_scr[0:1, :]
    v_t = pt_ref[pl.ds(v_row0, ML_V), :]

    def body(dd, qc):
        d = j * ML_DBLK + dd
        c_old = c_ref[dd]
        co_ref[dd] = a * c_old + wk_scr[pl.ds(d, 1), :] * v_t
        return qc + pt_ref[pl.ds(q_row0 + d, 1), :] * c_old

    qc_scr[...] = lax.fori_loop(0, ML_DBLK, body, qc_scr[...])

    @pl.when(j == pl.num_programs(1) - 1)
    def _():
        w = rows_scr[1:2, :]
        m_new = rows_scr[2:3, :]
        q_t = pt_ref[pl.ds(q_row0, ML_QK), :]
        k_t = pt_ref[pl.ds(k_row0, ML_QK), :] * (ML_QK ** -0.5)
        n_old = n_ref[...]
        s = jnp.sum(q_t * k_t, axis=0, keepdims=True) * w
        num = s * v_t + a * qc_scr[...]
        den = s + a * jnp.sum(q_t * n_old, axis=0, keepdims=True)
        hh = num / jnp.maximum(jnp.abs(den), jnp.exp(-m_new))
        scale = lax.rsqrt(jnp.sum(hh * hh, axis=0, keepdims=True) / ML_V + EPS)
        out_t = (hh * scale * gml_ref[0:ML_V, :]
                 * jax.nn.sigmoid(pt_ref[pl.ds(o_row0, ML_V), :]))
        out_t = jnp.concatenate([out_t, jnp.zeros((ML_V_PAD - ML_V, out_t.shape[1]), F32)], axis=0)
        mix_ref[...] = out_t.T.astype(BF16)
        no_ref[...] = a * n_old + wk_scr[...]
        mo_ref[pl.ds(h, 1), :] = m_new


def _decode_mlstm(proj_t, gains, layer, c_t, n_t, m_t, stacked_prev):
    nb = proj_t.shape[1]
    c_blk = pl.BlockSpec((None, None, ML_DBLK, ML_V, nb), lambda h, j: (layer, h, j, 0, 0))
    in_specs = [
        pl.BlockSpec((PROJ_PAD, nb), lambda h, j: (0, 0)),
        pl.BlockSpec((SUBLANES, LANES), lambda h, j: (0, 0)),
        pl.BlockSpec((None, ML_V_PAD, nb), lambda h, j: (h, 0, 0)),
        c_blk,
        pl.BlockSpec((None, None, ML_QK, nb), lambda h, j: (layer, h, 0, 0)),
        pl.BlockSpec((None, ML_HEADS, nb), lambda h, j: (layer, 0, 0)),
    ]
    assert len(in_specs) == DEC_ML_INPUTS
    in_specs += [pl.BlockSpec(memory_space=pl.ANY)] * len(stacked_prev)
    if stacked_prev:
        co_blk = c_blk
    else:
        assert layer == 0
        co_blk = pl.BlockSpec((c_t.shape[0], None, ML_DBLK, ML_V, nb),
                              lambda h, j: (0, h, j, 0, 0))
    out_shape = [
        jax.ShapeDtypeStruct((ML_HEADS, nb, ML_V_PAD), BF16),
        jax.ShapeDtypeStruct(c_t.shape, F32),
        jax.ShapeDtypeStruct((ML_HEADS, ML_QK, nb), F32),
        jax.ShapeDtypeStruct((ML_HEADS, nb), F32),
    ]
    out_specs = [
        pl.BlockSpec((None, nb, ML_V_PAD), lambda h, j: (h, 0, 0)),
        co_blk,
        pl.BlockSpec((None, ML_QK, nb), lambda h, j: (h, 0, 0)),
        pl.BlockSpec((ML_HEADS, nb), lambda h, j: (0, 0)),
    ]
    scratch = [
        pltpu.VMEM((SUBLANES, nb), F32),
        pltpu.VMEM((ML_QK, nb), F32),
        pltpu.VMEM((ML_V, nb), F32),
    ]
    return pl.pallas_call(
        functools.partial(_decode_mlstm_kernel, n_aliased=len(stacked_prev)),
        grid=(ML_HEADS, ML_QK // ML_DBLK),
        in_specs=in_specs,
        out_specs=out_specs,
        out_shape=out_shape,
        scratch_shapes=scratch,
        input_output_aliases={DEC_ML_INPUTS + i: 1 + i for i in range(len(stacked_prev))},
        compiler_params=pltpu.CompilerParams(
            dimension_semantics=("arbitrary", "arbitrary"), vmem_limit_bytes=VMEM_LIMIT),
        name="decode_mlstm",
    )(proj_t, gains["gb_r"], gains["g_ml_t"], c_t, n_t, m_t, *stacked_prev)


def _tables(seq):
    half = RET_DIM // 2
    inv_freq = ROPE_THETA ** (-jnp.arange(half, dtype=F32) / half)

    def cos_sin(pos):
        ang = pos[:, None] * inv_freq[None, :]
        cos, sin = jnp.cos(ang), jnp.sin(ang)
        return jnp.concatenate([cos, cos], axis=-1), jnp.concatenate([-sin, sin], axis=-1)

    cos_p, sin_p = cos_sin(jnp.arange(seq, dtype=F32))
    cos_s, sin_s = cos_sin((PAST_LEN + jnp.arange(1)).astype(F32))
    log_gamma = jnp.log1p(-jnp.exp2(-5.0 - jnp.arange(RET_HEADS, dtype=F32)))
    idx = jnp.arange(CHUNK, dtype=F32)
    diff = idx[:, None] - idx[None, :]
    causal = diff >= 0
    lg = log_gamma[:, None, None]
    dintra = jnp.where(causal, jnp.exp(lg * jnp.where(causal, diff, 0.0)), 0.0)
    dq = jnp.exp(log_gamma[:, None] * (idx + 1.0))
    dk = jnp.exp(log_gamma[:, None] * (CHUNK - 1.0 - idx))

    def pad_rows(t, rows):
        return jnp.pad(t, ((0, rows - t.shape[0]), (0, 0)))

    def chunk_decay(c):
        return pad_rows(jnp.broadcast_to(jnp.exp(log_gamma * c)[:, None], (RET_HEADS, LANES)), SUBLANES)

    return {
        "cos": cos_p, "sin": sin_p,
        "cos_s": jnp.broadcast_to(cos_s, (SUBLANES, RET_DIM)),
        "sin_s": jnp.broadcast_to(sin_s, (SUBLANES, RET_DIM)),
        "dintra": dintra,
        "dq": jnp.pad(dq.T, ((0, 0), (0, LANES - RET_HEADS))),
        "dk": pad_rows(dk, SUBLANES),
        "dch": chunk_decay(float(CHUNK)),
        "dch_s": chunk_decay(1.0),
    }


def _pad_last(t, width):
    return jnp.pad(t, [(0, 0)] * (t.ndim - 1) + [(0, width - t.shape[-1])])


def _prep_gains(l, g_ret_norm, g_mlstm_norm, b_mlstm_i, b_mlstm_f, w_pool, pool_scale):
    gate_bias = jnp.concatenate([b_mlstm_i[l], b_mlstm_f[l]])
    g_ml = _pad_last(g_mlstm_norm[l].reshape(ML_HEADS, ML_V), ML_V_PAD)
    lanes = jnp.concatenate([jnp.zeros((GATE_I_LANE,), F32), _pad_last(gate_bias, LANES - GATE_I_LANE)])
    return {
        "g_ret": g_ret_norm[l][None, :],
        "g_ml": g_ml.reshape(1, -1),
        "g_ml_t": jnp.broadcast_to(g_ml[:, :, None], (ML_HEADS, ML_V_PAD, LANES)),
        "gb_r": jnp.broadcast_to(gate_bias[:, None], (SUBLANES, LANES)),
        "gb_c": jnp.broadcast_to(lanes[None, :], (SUBLANES, LANES)),
        "w_pool": w_pool[l].astype(BF16),
        "pool_scale": pool_scale[l][None, :],
    }


@jax.jit
def kernel(x_prompt, x_sample, state_ret, state_mlstm_c, state_mlstm_n, state_mlstm_m, state_pool,
           w_in, w_out, g_ret_norm, g_mlstm_norm, b_mlstm_i, b_mlstm_f, w_pool, pool_scale,
           g_pre_mix, g_post_mix, g_pre_ffn, g_post_ffn, w_ffn_gate, w_ffn_up, w_ffn_down):
    batch, seq, d = x_prompt.shape
    nb = x_sample.shape[0]
    depth = w_in.shape[0]
    tables = _tables(seq)
    tables_s = dict(tables, dch=tables["dch_s"])
    w_in_t = jnp.swapaxes(w_in, 1, 2)

    yp = x_prompt.reshape(batch * seq, d)
    ys = x_sample.reshape(nb, d)
    new_p, new_s = [], []
    stacked_ret, stacked_c = (), ()
    c_t = jnp.transpose(state_mlstm_c, (0, 2, 3, 4, 1))
    n_t = jnp.transpose(state_mlstm_n, (0, 2, 3, 1))
    m_t = jnp.transpose(state_mlstm_m, (0, 2, 1))
    pool_t = jnp.transpose(state_pool, (0, 2, 1, 3))

    t = _token_tiles(batch * seq)
    t_s = _token_tiles(nb)
    for l in range(depth):
        w_in_l = _prep_w_in(w_in_t, l)
        w_out_l = _prep_w_out(w_out, l)
        gains = _prep_gains(l, g_ret_norm, g_mlstm_norm, b_mlstm_i, b_mlstm_f, w_pool, pool_scale)
        row = lambda g: g[l][None, :]

        mixed, s_ret, s_c, s_n, s_m, s_pool = _prompt_front(
            yp.reshape(batch, seq, d), row(g_pre_mix), w_in_l, tables, gains)
        new_p.append((s_ret, s_c[:, :, :ML_QK, :ML_V], s_n[:, :ML_HEADS, :ML_QK],
                      s_m[:, :ML_HEADS, 0], s_pool[:, 1:, :]))

        proj, proj_t = _inproj(ys, row(g_pre_mix), w_in_l, tm=t_s["inproj_tm"], tn=t_s["inproj_tn"])
        mix_ret, mix_pool, s_ret, pool_new = _decode_ret_pool(
            proj, tables_s, gains, l, state_ret, pool_t, stacked_ret)
        mix_ml, s_c, n_new, m_new = _decode_mlstm(proj_t, gains, l, c_t, n_t, m_t, stacked_c)
        mixed_s = jnp.concatenate(
            [mix_ret, jnp.swapaxes(mix_ml, 0, 1).reshape(nb, ML_HEADS * ML_V_PAD), mix_pool], axis=1)
        stacked_ret, stacked_c = (s_ret,), (s_c,)
        new_s.append((n_new, m_new, pool_new))

        x1, x1_s = _outproj(mixed, mixed_s, w_out_l, yp, ys, row(g_post_mix),
                            tm=t["outproj_tm"], tk=t["outproj_tk"])
        yp, ys = _ffn(x1, x1_s, row(g_pre_ffn), w_ffn_gate, w_ffn_up, w_ffn_down, row(g_post_ffn),
                      l, tm=t["ffn_tm"], tf=t["ffn_tf"])

    stack = lambda states, i: jnp.stack([s[i] for s in states], axis=0)
    return ((yp.reshape(batch, seq, d), ys.reshape(nb, 1, d))
            + tuple(stack(new_p, i) for i in range(5))
            + (stacked_ret[0],
               jnp.transpose(stacked_c[0], (0, 4, 1, 2, 3)),
               jnp.transpose(stack(new_s, 0), (0, 3, 1, 2)),
               jnp.transpose(stack(new_s, 1), (0, 2, 1)),
               jnp.transpose(stack(new_s, 2), (0, 2, 1, 3))))
```

```python
import functools

import jax
import jax.numpy as jnp
from jax import lax
from jax.experimental import pallas as pl
from jax.experimental.pallas import tpu as pltpu

F32 = jnp.float32
BF16 = jnp.bfloat16

D_MODEL = 2048
PAST_LEN = 16384
RET_HEADS = 6
RET_DIM = 128
RET_WIDTH = RET_HEADS * RET_DIM
ML_HEADS = 4
ML_QK = 96
ML_V = 192
ML_WIDTH = ML_HEADS * ML_V
POOL_WIDTH = 512
POOL_WINDOWS = (2, 4, 8, 16)
POOL_GROUP = POOL_WIDTH // len(POOL_WINDOWS)
POOL_BUF = max(POOL_WINDOWS) - 1
D_FF = 5632
CHUNK = 128
ROPE_THETA = 10000.0
EPS = 1e-6

LANES = 128
SUBLANES = 8
VMEM_LIMIT = 56 * 1024 * 1024

ML_QK_PAD = LANES
ML_V_PAD = 2 * LANES
OFF_RQ = 0
OFF_RK = OFF_RQ + RET_WIDTH
OFF_RV = OFF_RK + RET_WIDTH
OFF_RG = OFF_RV + RET_WIDTH
OFF_MQ = OFF_RG + RET_WIDTH
OFF_MK = OFF_MQ + ML_HEADS * ML_QK_PAD
OFF_MV = OFF_MK + ML_HEADS * ML_QK_PAD
OFF_MO = OFF_MV + ML_HEADS * ML_V_PAD
OFF_PU = OFF_MO + ML_HEADS * ML_V_PAD
PROJ_PAD = OFF_PU + POOL_WIDTH
GATE_I_LANE = ML_QK
GATE_F_LANE = ML_QK + ML_HEADS
MIX_OFF_ML = RET_WIDTH
MIX_OFF_POOL = MIX_OFF_ML + ML_HEADS * ML_V_PAD
MIX_PAD = MIX_OFF_POOL + POOL_WIDTH

_HI = lax.Precision.HIGHEST


def _token_tiles(m):
    return {
        "inproj_tm": min(m, 1024), "inproj_tn": 512,
        "outproj_tm": min(m, 1024), "outproj_tk": 768,
        "ffn_tm": min(m, 1024), "ffn_tf": 256,
    }


def _dot(a, b):
    return jnp.dot(a, b, preferred_element_type=F32)


def _rms(x, width=None):
    n = x.shape[-1] if width is None else width
    return x * lax.rsqrt(jnp.sum(x * x, axis=-1, keepdims=True) / n + EPS)


def _log_sigmoid(x):
    return jnp.minimum(x, 0.0) - jnp.log1p(jnp.exp(-jnp.abs(x)))


def _rotate(x, cos2, sin2):
    return x * cos2 + pltpu.roll(x, RET_DIM // 2, 1) * sin2


def _inproj_kernel(x_ref, g_ref, w_ref, o_ref, ot_ref, xn_ref):
    @pl.when(pl.program_id(1) == 0)
    def _():
        xn_ref[...] = (_rms(x_ref[...]) * g_ref[...]).astype(BF16)

    res = _dot(xn_ref[...], w_ref[...])
    o_ref[...] = res
    ot_ref[...] = res.T


def _inproj(x, g, w, tm, tn):
    m = x.shape[0]
    return pl.pallas_call(
        _inproj_kernel,
        grid=(m // tm, PROJ_PAD // tn),
        in_specs=[
            pl.BlockSpec((tm, D_MODEL), lambda i, j: (i, 0)),
            pl.BlockSpec((1, D_MODEL), lambda i, j: (0, 0)),
            pl.BlockSpec((D_MODEL, tn), lambda i, j: (0, j)),
        ],
        out_specs=[
            pl.BlockSpec((tm, tn), lambda i, j: (i, j)),
            pl.BlockSpec((tn, tm), lambda i, j: (j, i)),
        ],
        out_shape=[
            jax.ShapeDtypeStruct((m, PROJ_PAD), F32),
            jax.ShapeDtypeStruct((PROJ_PAD, m), F32),
        ],
        scratch_shapes=[pltpu.VMEM((tm, D_MODEL), BF16)],
        compiler_params=pltpu.CompilerParams(
            dimension_semantics=("parallel", "arbitrary"), vmem_limit_bytes=VMEM_LIMIT),
        name="inproj",
    )(x, g, w)


def _outproj_kernel(m_ref, ms_ref, w_ref, x_ref, xs_ref, gpost_ref, x1_ref, x1s_ref):
    first_tile = pl.program_id(0) == 0
    k = pl.program_id(1)
    last = pl.num_programs(1) - 1

    @pl.when(k == 0)
    def _():
        x1_ref[...] = jnp.zeros_like(x1_ref)

    @pl.when(first_tile & (k == 0))
    def _():
        x1s_ref[...] = jnp.zeros_like(x1s_ref)

    @pl.when(first_tile)
    def _():
        x1s_ref[...] += _dot(ms_ref[...], w_ref[...])

    x1_ref[...] += _dot(m_ref[...], w_ref[...])

    @pl.when(k == last)
    def _():
        x1_ref[...] = x_ref[...] + _rms(x1_ref[...]) * gpost_ref[...]

    @pl.when(first_tile & (k == last))
    def _():
        x1s_ref[...] = xs_ref[...] + _rms(x1s_ref[...]) * gpost_ref[...]


def _outproj(mixed, mixed_s, w, x, x_s, g_post, tm, tk):
    m, ns = x.shape[0], x_s.shape[0]
    return pl.pallas_call(
        _outproj_kernel,
        grid=(m // tm, MIX_PAD // tk),
        in_specs=[
            pl.BlockSpec((tm, tk), lambda i, k: (i, k)),
            pl.BlockSpec((ns, tk), lambda i, k: (0, k)),
            pl.BlockSpec((tk, D_MODEL), lambda i, k: (k, 0)),
            pl.BlockSpec((tm, D_MODEL), lambda i, k: (i, 0)),
            pl.BlockSpec((ns, D_MODEL), lambda i, k: (0, 0)),
            pl.BlockSpec((1, D_MODEL), lambda i, k: (0, 0)),
        ],
        out_specs=[
            pl.BlockSpec((tm, D_MODEL), lambda i, k: (i, 0)),
            pl.BlockSpec((ns, D_MODEL), lambda i, k: (0, 0)),
        ],
        out_shape=[
            jax.ShapeDtypeStruct((m, D_MODEL), F32),
            jax.ShapeDtypeStruct((ns, D_MODEL), F32),
        ],
        compiler_params=pltpu.CompilerParams(
            dimension_semantics=("arbitrary", "arbitrary"), vmem_limit_bytes=VMEM_LIMIT),
        name="outproj",
    )(mixed, mixed_s, w, x, x_s, g_post)


FFN_ROW_CHUNK = 256


def _ffn_kernel(x1_ref, x1s_ref, gpre_ref, wg_ref, wu_ref, wd_ref, gpost_ref, o_ref, os_ref,
                hn_ref, hns_ref):
    first_tile = pl.program_id(0) == 0
    f = pl.program_id(1)
    last = pl.num_programs(1) - 1

    def swiglu_down(hn, wg, wu, wd):
        gate = _dot(hn, wg)
        act = (gate * jax.nn.sigmoid(gate) * _dot(hn, wu)).astype(BF16)
        return _dot(act, wd)

    def middle_step(h_ref, acc_ref):
        acc_ref[...] += swiglu_down(h_ref[...], wg_ref[...].astype(BF16),
                                    wu_ref[...].astype(BF16), wd_ref[...].astype(BF16))

    def row_chunks(ref):
        rows = ref.shape[0]
        chunk = min(rows, FFN_ROW_CHUNK)
        return [slice(r, r + chunk) for r in range(0, rows, chunk)]

    def first_step(x_ref, h_ref, acc_ref):
        wg, wu, wd = (w[...].astype(BF16) for w in (wg_ref, wu_ref, wd_ref))
        for rows in row_chunks(x_ref):
            hn = (_rms(x_ref[rows, :]) * gpre_ref[...]).astype(BF16)
            h_ref[rows, :] = hn
            acc_ref[rows, :] = swiglu_down(hn, wg, wu, wd)

    def last_step(x_ref, h_ref, acc_ref):
        wg, wu, wd = (w[...].astype(BF16) for w in (wg_ref, wu_ref, wd_ref))
        for rows in row_chunks(x_ref):
            acc = acc_ref[rows, :] + swiglu_down(h_ref[rows, :], wg, wu, wd)
            acc_ref[rows, :] = x_ref[rows, :] + _rms(acc) * gpost_ref[...]

    middle = (f > 0) & (f < last)
    for on_tile, refs in ((first_tile, (x1s_ref, hns_ref, os_ref)), (True, (x1_ref, hn_ref, o_ref))):
        pl.when(on_tile & (f == 0))(functools.partial(first_step, *refs))
        pl.when(on_tile & middle)(functools.partial(middle_step, *refs[1:]))
        pl.when(on_tile & (f == last))(functools.partial(last_step, *refs))


def _ffn(x1, x1_s, g_pre, wg, wu, wd, g_post, layer, tm, tf):
    m, ns = x1.shape[0], x1_s.shape[0]
    return pl.pallas_call(
        _ffn_kernel,
        grid=(m // tm, D_FF // tf),
        in_specs=[
            pl.BlockSpec((tm, D_MODEL), lambda i, f: (i, 0)),
            pl.BlockSpec((ns, D_MODEL), lambda i, f: (0, 0)),
            pl.BlockSpec((1, D_MODEL), lambda i, f: (0, 0)),
            pl.BlockSpec((None, D_MODEL, tf), lambda i, f: (layer, 0, f)),
            pl.BlockSpec((None, D_MODEL, tf), lambda i, f: (layer, 0, f)),
            pl.BlockSpec((None, tf, D_MODEL), lambda i, f: (layer, f, 0)),
            pl.BlockSpec((1, D_MODEL), lambda i, f: (0, 0)),
        ],
        out_specs=[
            pl.BlockSpec((tm, D_MODEL), lambda i, f: (i, 0)),
            pl.BlockSpec((ns, D_MODEL), lambda i, f: (0, 0)),
        ],
        out_shape=[
            jax.ShapeDtypeStruct((m, D_MODEL), F32),
            jax.ShapeDtypeStruct((ns, D_MODEL), F32),
        ],
        scratch_shapes=[pltpu.VMEM((tm, D_MODEL), BF16), pltpu.VMEM((ns, D_MODEL), BF16)],
        compiler_params=pltpu.CompilerParams(
            dimension_semantics=("arbitrary", "arbitrary"), vmem_limit_bytes=VMEM_LIMIT),
        name="ffn",
    )(x1, x1_s, g_pre, wg, wu, wd, g_post)


SRC_MQ = 4 * RET_WIDTH
SRC_MK = SRC_MQ + ML_HEADS * ML_QK
SRC_MV = SRC_MK + ML_HEADS * ML_QK
SRC_MO = SRC_MV + ML_WIDTH
SRC_MI = SRC_MO + ML_WIDTH
SRC_MF = SRC_MI + ML_HEADS
SRC_PU = SRC_MF + ML_HEADS
IN_WIDTH = SRC_PU + POOL_WIDTH


def _prep_w_in_kernel(w_ref, o_ref):
    tk = w_ref.shape[1]

    def put(dst, pieces, width):
        rows = [w_ref[src:src + n, :] for src, n in pieces]
        have = sum(n for _, n in pieces)
        if have < width:
            rows.append(jnp.zeros((width - have, tk), F32))
        block = rows[0] if len(rows) == 1 else jnp.concatenate(rows, axis=0)
        o_ref[:, dst:dst + width] = block.T.astype(BF16)

    for r in range(0, 4 * RET_WIDTH, 2 * LANES):
        put(r, [(r, 2 * LANES)], 2 * LANES)
    for h in range(ML_HEADS):
        q_rows = [(SRC_MQ + h * ML_QK, ML_QK)]
        if h == 0:
            q_rows.append((SRC_MI, 2 * ML_HEADS))
        put(OFF_MQ + h * ML_QK_PAD, q_rows, ML_QK_PAD)
        put(OFF_MK + h * ML_QK_PAD, [(SRC_MK + h * ML_QK, ML_QK)], ML_QK_PAD)
        put(OFF_MV + h * ML_V_PAD, [(SRC_MV + h * ML_V, ML_V)], ML_V_PAD)
        put(OFF_MO + h * ML_V_PAD, [(SRC_MO + h * ML_V, ML_V)], ML_V_PAD)
    for r in range(0, POOL_WIDTH, 2 * LANES):
        put(OFF_PU + r, [(SRC_PU + r, 2 * LANES)], 2 * LANES)


def _prep_w_in(w_t, layer, tk=256):
    d = w_t.shape[2]
    return pl.pallas_call(
        _prep_w_in_kernel,
        grid=(d // tk,),
        in_specs=[pl.BlockSpec((None, IN_WIDTH, tk), lambda i: (layer, 0, i))],
        out_specs=pl.BlockSpec((tk, PROJ_PAD), lambda i: (i, 0)),
        out_shape=jax.ShapeDtypeStruct((d, PROJ_PAD), BF16),
        compiler_params=pltpu.CompilerParams(
            dimension_semantics=("parallel",), vmem_limit_bytes=VMEM_LIMIT),
        name="prep_w_in",
    )(w_t)


def _prep_w_out_kernel(w_ref, o_ref):
    def put(dst, src, rows):
        o_ref[dst:dst + rows, :] = w_ref[src:src + rows, :].astype(BF16)

    put(0, 0, RET_WIDTH)
    for h in range(ML_HEADS):
        dst = MIX_OFF_ML + h * ML_V_PAD
        put(dst, RET_WIDTH + h * ML_V, ML_V)
        o_ref[dst + ML_V:dst + ML_V_PAD, :] = jnp.zeros((ML_V_PAD - ML_V, o_ref.shape[1]), BF16)
    put(MIX_OFF_POOL, RET_WIDTH + ML_WIDTH, POOL_WIDTH)


def _prep_w_out(w, layer, tn=512):
    _, k, n = w.shape
    return pl.pallas_call(
        _prep_w_out_kernel,
        grid=(n // tn,),
        in_specs=[pl.BlockSpec((None, k, tn), lambda j: (layer, 0, j))],
        out_specs=pl.BlockSpec((MIX_PAD, tn), lambda j: (0, j)),
        out_shape=jax.ShapeDtypeStruct((MIX_PAD, n), BF16),
        compiler_params=pltpu.CompilerParams(
            dimension_semantics=("parallel",), vmem_limit_bytes=VMEM_LIMIT),
        name="prep_w_out",
    )(w)


PROMPT_BLOCK = 2


def _ret_head(q, k, get_v, get_g, cos2, sin2, dintra, dq_col, dk_row, dch_row, g_norm, s_old,
              store):
    qb = _rotate(q, cos2, sin2).astype(BF16)
    kr_t = (_rotate(k, cos2, sin2) * (RET_DIM ** -0.5)).T
    kb = kr_t.astype(BF16)
    kdb = (kr_t * dk_row).astype(BF16)
    yield
    scores = (_dot(qb, kb) * dintra).astype(BF16)
    vb = get_v().astype(BF16)
    g = get_g()
    gate = g * jax.nn.sigmoid(g)
    yield
    o = _dot(scores, vb) + _dot(qb, s_old.astype(BF16)) * dq_col
    s_new = s_old * dch_row + _dot(kdb, vb)
    store(_rms(o) * g_norm * gate, s_new)


def _ml_head(q, k, vb, og, causal, b_c, i_c, b_r, i_r, g_norm, c_old, n_old, m_prev, store):
    k_t = k.T
    qb = q.astype(BF16)
    logw = jnp.where(causal, b_c - b_r + i_r, -jnp.inf)
    inter = b_c + m_prev
    m_t = jnp.maximum(inter, jnp.max(logw, axis=-1, keepdims=True))
    w = jnp.exp(logw - m_t)
    a = jnp.exp(inter - m_t)
    m_new = m_t[CHUNK - 1:CHUNK, :]
    b_last = b_c[CHUNK - 1:CHUNK, :]
    w_r = jnp.exp(b_last - b_r + i_r - m_new)
    w_c = jnp.exp(b_last - b_c + i_c - m_new)
    a_end = jnp.exp(b_last + m_prev - m_new)
    kwb = (k_t * w_r).astype(BF16)
    n_new = a_end * n_old + jnp.sum(k * w_c, axis=0, keepdims=True)
    qn = jnp.sum(q * n_old, axis=-1, keepdims=True)
    floor = jnp.exp(-m_t)
    gate = jax.nn.sigmoid(og)
    yield
    s = _dot(qb, k_t.astype(BF16)) * w
    den = jnp.sum(s, axis=-1, keepdims=True) + a * qn
    sb = s.astype(BF16)
    yield
    num = _dot(sb, vb) + a * _dot(qb, c_old.astype(BF16))
    c_new = a_end * c_old + _dot(kwb, vb)
    hh = num / jnp.maximum(jnp.abs(den), floor)
    store(_rms(hh, ML_V) * g_norm * gate, c_new, n_new, m_new)


def _prompt_front_kernel(x_ref, gpre_ref, w_ref, cos_ref, sin_ref, dintra_ref, dq_ref, dk_ref,
                         dch_ref, gret_ref, gml_ref, gb_r_ref, gb_c_ref, wpool_ref, pscale_ref,
                         mix_ref, sret_ref, sc_ref, sn_ref, sm_ref, spool_ref,
                         s_scr, c_scr, n_scr, m_scr, u_scr):
    bp = pl.program_id(0)
    c = pl.program_id(1)
    nseq = x_ref.shape[0]

    @pl.when(c == 0)
    def _():
        s_scr[...] = jnp.zeros_like(s_scr)
        c_scr[...] = jnp.zeros_like(c_scr)
        n_scr[...] = jnp.zeros_like(n_scr)
        m_scr[...] = jnp.zeros_like(m_scr)
        u_scr[:, 0:2 * SUBLANES, :] = jnp.zeros((nseq, 2 * SUBLANES, POOL_WIDTH), F32)

    xn = (_rms(x_ref[...].reshape(nseq * CHUNK, D_MODEL)) * gpre_ref[...]).astype(BF16)

    def proj(lo, width):
        return _dot(xn, w_ref[:, lo:lo + width])

    def seq_rows(t, i):
        return t[i * CHUNK:(i + 1) * CHUNK]

    cos2 = cos_ref[...]
    sin2 = sin_ref[...]
    pair = 2 * RET_DIM

    def ret_project(h0):
        return [functools.partial(proj, off + h0 * RET_DIM, pair)
                for off in (OFF_RQ, OFF_RK, OFF_RV, OFF_RG)]

    def ret_mix(h0, projected):
        pq, pk = projected[:2]
        instances = []
        for h in (h0, h0 + 1):
            lo, hi = (h - h0) * RET_DIM, (h - h0 + 1) * RET_DIM
            for i in range(nseq):
                def store(out, s_new, h=h, i=i):
                    s_scr[i, h] = s_new
                    mix_ref[i, :, h * RET_DIM:(h + 1) * RET_DIM] = out.astype(BF16)

                def late(n, i=i, lo=lo, hi=hi):
                    return seq_rows(projected[n], i)[:, lo:hi]

                instances.append(_ret_head(
                    seq_rows(pq, i)[:, lo:hi], seq_rows(pk, i)[:, lo:hi],
                    functools.partial(late, 2), functools.partial(late, 3),
                    cos2, sin2, dintra_ref[h], dq_ref[:, h:h + 1], dk_ref[h:h + 1, :],
                    dch_ref[h:h + 1, :], gret_ref[:, h * RET_DIM:(h + 1) * RET_DIM], s_scr[i, h],
                    store))
        return instances

    row = lax.broadcasted_iota(jnp.int32, (CHUNK, CHUNK), 0)
    col = lax.broadcasted_iota(jnp.int32, (CHUNK, CHUNK), 1)
    causal = col <= row
    upper = (row <= col).astype(F32)
    lower = causal.astype(F32)
    gates = [None] * nseq

    def ml_project(h0):
        return [functools.partial(proj, OFF_MQ + h0 * ML_QK_PAD, 2 * ML_QK_PAD),
                functools.partial(proj, OFF_MK + h0 * ML_QK_PAD, 2 * ML_QK_PAD),
                functools.partial(proj, OFF_MV + h0 * ML_V_PAD, 2 * ML_V_PAD),
                functools.partial(proj, OFF_MO + h0 * ML_V_PAD, 2 * ML_V_PAD)]

    def ml_mix(h0, projected):
        pq, pk, pv, po = projected
        if h0 == 0:
            for i in range(nseq):
                gq0 = seq_rows(pq, i)[:, 0:ML_QK_PAD]
                i_r8 = gq0.T[GATE_I_LANE:GATE_I_LANE + SUBLANES, :] + gb_r_ref[...]
                b_r8 = jnp.dot(_log_sigmoid(i_r8), upper, precision=_HI, preferred_element_type=F32)
                pre_c = gq0 + gb_c_ref[0:1, :]
                b_c_all = jnp.dot(lower, _log_sigmoid(pre_c), precision=_HI,
                                  preferred_element_type=F32)
                gates[i] = (i_r8, b_r8, pre_c, b_c_all)
        instances = []
        for h in (h0, h0 + 1):
            qlo = (h - h0) * ML_QK_PAD
            vlo = (h - h0) * ML_V_PAD
            for i in range(nseq):
                def store(out, c_new, n_new, m_new, h=h, i=i):
                    c_scr[i, h] = c_new
                    n_scr[i, h:h + 1, :] = n_new
                    m_scr[i, h:h + 1, :] = jnp.broadcast_to(m_new, (1, LANES))
                    mix_ref[i, :, MIX_OFF_ML + h * ML_V_PAD:MIX_OFF_ML + (h + 1) * ML_V_PAD] = (
                        out.astype(BF16))

                i_r8, b_r8, pre_c, b_c_all = gates[i]
                instances.append(_ml_head(
                    seq_rows(pq, i)[:, qlo:qlo + ML_QK_PAD],
                    seq_rows(pk, i)[:, qlo:qlo + ML_QK_PAD] * (ML_QK ** -0.5),
                    seq_rows(pv, i)[:, vlo:vlo + ML_V_PAD].astype(BF16),
                    seq_rows(po, i)[:, vlo:vlo + ML_V_PAD],
                    causal,
                    b_c_all[:, GATE_F_LANE + h:GATE_F_LANE + h + 1],
                    pre_c[:, GATE_I_LANE + h:GATE_I_LANE + h + 1],
                    b_r8[ML_HEADS + h:ML_HEADS + h + 1, :], i_r8[h:h + 1, :],
                    gml_ref[:, h * ML_V_PAD:(h + 1) * ML_V_PAD],
                    c_scr[i, h], n_scr[i, h:h + 1, :], m_scr[i, h:h + 1, 0:1], store))
        return instances

    base = 2 * SUBLANES
    pos = (c * CHUNK + lax.broadcasted_iota(jnp.int32, (CHUNK, 1), 0)).astype(F32)

    def pool_seq(u, i):
        u_scr[i, base:base + CHUNK, :] = u
        zs = []
        for g, win in enumerate(POOL_WINDOWS):
            lo, hi = g * POOL_GROUP, (g + 1) * POOL_GROUP
            wsum = u[:, lo:hi]
            for s in range(1, win):
                wsum = wsum + u_scr[i, base - s:base - s + CHUNK, lo:hi]
            zs.append((wsum / jnp.minimum(pos + 1.0, float(win)) - u[:, lo:hi]).astype(BF16))
        u_scr[i, 0:base, :] = u_scr[i, CHUNK:CHUNK + base, :]
        yield
        for g, z in enumerate(zs):
            lo, hi = g * POOL_GROUP, (g + 1) * POOL_GROUP
            y = _dot(z, wpool_ref[g]) * pscale_ref[:, lo:hi]
            mix_ref[i, :, MIX_OFF_POOL + lo:MIX_OFF_POOL + hi] = y.astype(BF16)

    def pool_mix(projected):
        pu, = projected
        return [pool_seq(seq_rows(pu, i), i) for i in range(nseq)]

    stages = [(ret_project(h0), functools.partial(ret_mix, h0)) for h0 in range(0, RET_HEADS, 2)]
    stages += [(ml_project(h0), functools.partial(ml_mix, h0)) for h0 in range(0, ML_HEADS, 2)]
    stages.append(([functools.partial(proj, OFF_PU, POOL_WIDTH)], pool_mix))
    early = 2
    projected = [dot() for dot in stages[0][0][:early]]
    late_dots = list(stages[0][0][early:])
    for n, (_, mix) in enumerate(stages):
        pending = list(stages[n + 1][0]) if n + 1 < len(stages) else []
        upcoming = []
        live = mix(projected)
        while live:
            for _ in range(max(1, (len(late_dots) + len(pending)) // 2)):
                if late_dots:
                    projected.append(late_dots.pop(0)())
                elif pending:
                    upcoming.append(pending.pop(0)())
            live = [inst for inst in live if next(inst, StopIteration) is not StopIteration]
        upcoming += [dot() for dot in pending]
        projected = upcoming

    @pl.when(c == pl.num_programs(1) - 1)
    def _():
        for i in range(nseq):
            b = bp * nseq + i
            sret_ref[b] = s_scr[i]
            sc_ref[b] = c_scr[i]
            sn_ref[b] = n_scr[i]
            sm_ref[b] = m_scr[i]
            spool_ref[b] = u_scr[i, 0:base, :]


def _prompt_front(x, g_pre, w, tables, gains):
    batch, seq, _ = x.shape
    nchunk = seq // CHUNK
    nseq = PROMPT_BLOCK
    const2 = lambda b, c: (0, 0)
    const3 = lambda b, c: (0, 0, 0)
    const4 = lambda b, c: (0, 0, 0, 0)
    in_specs = [
        pl.BlockSpec((nseq, CHUNK, D_MODEL), lambda b, c: (b, c, 0)),
        pl.BlockSpec((1, D_MODEL), const2),
        pl.BlockSpec((D_MODEL, PROJ_PAD), const2, pipeline_mode=pl.Buffered(1)),
        pl.BlockSpec((CHUNK, RET_DIM), lambda b, c: (c, 0)),
        pl.BlockSpec((CHUNK, RET_DIM), lambda b, c: (c, 0)),
        pl.BlockSpec((RET_HEADS, CHUNK, CHUNK), const3),
        pl.BlockSpec((CHUNK, LANES), const2),
        pl.BlockSpec((SUBLANES, CHUNK), const2),
        pl.BlockSpec((SUBLANES, LANES), const2),
        pl.BlockSpec((1, RET_WIDTH), const2),
        pl.BlockSpec((1, ML_HEADS * ML_V_PAD), const2),
        pl.BlockSpec((SUBLANES, LANES), const2),
        pl.BlockSpec((SUBLANES, LANES), const2),
        pl.BlockSpec((len(POOL_WINDOWS), POOL_GROUP, POOL_GROUP), const3),
        pl.BlockSpec((1, POOL_WIDTH), const2),
    ]
    out_shape = [
        jax.ShapeDtypeStruct((batch, seq, MIX_PAD), BF16),
        jax.ShapeDtypeStruct((batch, RET_HEADS, RET_DIM, RET_DIM), F32),
        jax.ShapeDtypeStruct((batch, ML_HEADS, ML_QK_PAD, ML_V_PAD), F32),
        jax.ShapeDtypeStruct((batch, SUBLANES, LANES), F32),
        jax.ShapeDtypeStruct((batch, SUBLANES, LANES), F32),
        jax.ShapeDtypeStruct((batch, 2 * SUBLANES, POOL_WIDTH), F32),
    ]
    out_specs = [
        pl.BlockSpec((nseq, CHUNK, MIX_PAD), lambda b, c: (b, c, 0)),
        pl.BlockSpec((batch, RET_HEADS, RET_DIM, RET_DIM), const4),
        pl.BlockSpec((batch, ML_HEADS, ML_QK_PAD, ML_V_PAD), const4),
        pl.BlockSpec((batch, SUBLANES, LANES), const3),
        pl.BlockSpec((batch, SUBLANES, LANES), const3),
        pl.BlockSpec((batch, 2 * SUBLANES, POOL_WIDTH), const3),
    ]
    scratch = [
        pltpu.VMEM((nseq, RET_HEADS, RET_DIM, RET_DIM), F32),
        pltpu.VMEM((nseq, ML_HEADS, ML_QK_PAD, ML_V_PAD), F32),
        pltpu.VMEM((nseq, SUBLANES, LANES), F32),
        pltpu.VMEM((nseq, SUBLANES, LANES), F32),
        pltpu.VMEM((nseq, 2 * SUBLANES + CHUNK, POOL_WIDTH), F32),
    ]
    mixed, *states = pl.pallas_call(
        _prompt_front_kernel,
        grid=(batch // nseq, nchunk),
        in_specs=in_specs,
        out_specs=out_specs,
        out_shape=out_shape,
        scratch_shapes=scratch,
        compiler_params=pltpu.CompilerParams(
            dimension_semantics=("arbitrary", "arbitrary"), vmem_limit_bytes=VMEM_LIMIT),
        name="prompt_front",
    )(x, g_pre, w, tables["cos"], tables["sin"], tables["dintra"], tables["dq"], tables["dk"],
      tables["dch"], gains["g_ret"], gains["g_ml"], gains["gb_r"], gains["gb_c"],
      gains["w_pool"], gains["pool_scale"])
    return (mixed.reshape(batch * seq, MIX_PAD), *states)


DEC_BLOCK = 8
DEC_RET_INPUTS = 9


def _decode_ret_pool_kernel(*refs, n_aliased):
    (p_ref, cos_ref, sin_ref, dch_ref, gret_ref, wpool_ref, pscale_ref, s_ref,
     pool_ref) = refs[:DEC_RET_INPUTS]
    (mix_ref, mixp_ref, so_ref, poolo_ref, qr_scr, kt_scr, v_scr,
     qs_scr) = refs[DEC_RET_INPUTS + n_aliased:]
    if n_aliased == 0:
        for later in range(1, so_ref.shape[0]):
            so_ref[later] = jnp.zeros(so_ref.shape[1:], F32)
        so_ref = so_ref.at[0]
    step = pl.program_id(0)
    nb = p_ref.shape[0]
    cos2 = cos_ref[0:1, :]
    sin2 = sin_ref[0:1, :]

    def ret_qkv(h):
        lo, hi = h * RET_DIM, (h + 1) * RET_DIM
        qr = _rotate(p_ref[:, OFF_RQ + lo:OFF_RQ + hi], cos2, sin2)
        kr = _rotate(p_ref[:, OFF_RK + lo:OFF_RK + hi], cos2, sin2) * (RET_DIM ** -0.5)
        return qr, kr, p_ref[:, OFF_RV + lo:OFF_RV + hi]

    @pl.when(step == 0)
    def _():
        for h in range(RET_HEADS):
            qr, kr, v = ret_qkv(h)
            qr_scr[h] = qr
            kt_scr[h] = kr.T
            v_scr[h] = v.astype(BF16)

    lane = lax.broadcasted_iota(jnp.int32, (LANES, nb), 1)

    def body(bb, carry):
        b = step * DEC_BLOCK + bb
        sel = lane == b
        for h in range(RET_HEADS):
            s_old = s_ref[bb, h]
            q16 = jnp.broadcast_to(qr_scr[h, pl.ds(b, 1), :], (2 * SUBLANES, RET_DIM)).astype(BF16)
            qs_scr[h, pl.ds(b, 1), :] = _dot(q16, s_old.astype(BF16))[0:1, :]
            k_m = jnp.where(sel, kt_scr[h], 0.0).astype(BF16)
            so_ref[bb, h] = s_old * dch_ref[h:h + 1, :] + _dot(k_m, v_scr[h])
        return carry

    lax.fori_loop(0, DEC_BLOCK, body, 0)

    @pl.when(step == pl.num_programs(0) - 1)
    def _():
        for h in range(RET_HEADS):
            lo, hi = h * RET_DIM, (h + 1) * RET_DIM
            qr, kr, v = ret_qkv(h)
            g = p_ref[:, OFF_RG + lo:OFF_RG + hi]
            o = jnp.sum(qr * kr, axis=-1, keepdims=True) * v + qs_scr[h] * dch_ref[h:h + 1, :]
            out = _rms(o) * gret_ref[:, lo:hi] * (g * jax.nn.sigmoid(g))
            mix_ref[:, lo:hi] = out.astype(BF16)
        u = p_ref[:, OFF_PU:OFF_PU + POOL_WIDTH]
        for g, win in enumerate(POOL_WINDOWS):
            lo, hi = g * POOL_GROUP, (g + 1) * POOL_GROUP
            wsum = u[:, lo:hi]
            for s in range(1, win):
                wsum = wsum + pool_ref[POOL_BUF - s, :, lo:hi]
            z = wsum / float(win) - u[:, lo:hi]
            y = _dot(z.astype(BF16), wpool_ref[g]) * pscale_ref[:, lo:hi]
            mixp_ref[:, lo:hi] = y.astype(BF16)
        for r in range(POOL_BUF - 1):
            poolo_ref[r] = pool_ref[r + 1]
        poolo_ref[POOL_BUF - 1] = u


def _decode_ret_pool(proj, tables, gains, layer, s_ret, pool_t, stacked_prev):
    nb = proj.shape[0]
    const2 = lambda s: (0, 0)
    const3 = lambda s: (0, 0, 0)
    full2 = lambda shape: pl.BlockSpec(shape, const2)
    s_blk = pl.BlockSpec((None, DEC_BLOCK, RET_HEADS, RET_DIM, RET_DIM),
                         lambda s: (layer, s, 0, 0, 0))
    pool_blk = (POOL_BUF, nb, POOL_WIDTH)
    in_specs = [
        full2((nb, PROJ_PAD)),
        full2((SUBLANES, RET_DIM)),
        full2((SUBLANES, RET_DIM)),
        full2((SUBLANES, LANES)),
        full2((1, RET_WIDTH)),
        pl.BlockSpec((len(POOL_WINDOWS), POOL_GROUP, POOL_GROUP), const3),
        full2((1, POOL_WIDTH)),
        s_blk,
        pl.BlockSpec((None,) + pool_blk, lambda s: (layer, 0, 0, 0)),
    ]
    assert len(in_specs) == DEC_RET_INPUTS
    in_specs += [pl.BlockSpec(memory_space=pl.ANY)] * len(stacked_prev)
    if stacked_prev:
        so_blk = s_blk
    else:
        assert layer == 0
        so_blk = pl.BlockSpec((s_ret.shape[0],) + s_blk.block_shape[1:],
                              lambda s: (0, s, 0, 0, 0))
    out_shape = [
        jax.ShapeDtypeStruct((nb, RET_WIDTH), BF16),
        jax.ShapeDtypeStruct((nb, POOL_WIDTH), BF16),
        jax.ShapeDtypeStruct(s_ret.shape, F32),
        jax.ShapeDtypeStruct(pool_blk, F32),
    ]
    out_specs = [
        full2((nb, RET_WIDTH)),
        full2((nb, POOL_WIDTH)),
        so_blk,
        pl.BlockSpec(pool_blk, const3),
    ]
    scratch = [
        pltpu.VMEM((RET_HEADS, nb, RET_DIM), F32),
        pltpu.VMEM((RET_HEADS, RET_DIM, nb), F32),
        pltpu.VMEM((RET_HEADS, nb, RET_DIM), BF16),
        pltpu.VMEM((RET_HEADS, nb, RET_DIM), F32),
    ]
    return pl.pallas_call(
        functools.partial(_decode_ret_pool_kernel, n_aliased=len(stacked_prev)),
        grid=(nb // DEC_BLOCK,),
        in_specs=in_specs,
        out_specs=out_specs,
        out_shape=out_shape,
        scratch_shapes=scratch,
        input_output_aliases={DEC_RET_INPUTS + i: 2 + i for i in range(len(stacked_prev))},
        compiler_params=pltpu.CompilerParams(
            dimension_semantics=("arbitrary",), vmem_limit_bytes=VMEM_LIMIT),
        name="decode_ret_pool",
    )(proj, tables["cos_s"], tables["sin_s"], tables["dch"], gains["g_ret"], gains["w_pool"],
      gains["pool_scale"], s_ret, pool_t, *stacked_prev)


ML_DBLK = 32
DEC_ML_INPUTS = 6


def _decode_mlstm_kernel(*refs, n_aliased):
    pt_ref, gb_ref, gml_ref, c_ref, n_ref, m_ref = refs[:DEC_ML_INPUTS]
    mix_ref, co_ref, no_ref, mo_ref, rows_scr, wk_scr, qc_scr = refs[DEC_ML_INPUTS + n_aliased:]
    if n_aliased == 0:
        for later in range(1, co_ref.shape[0]):
            co_ref[later] = jnp.zeros(co_ref.shape[1:], F32)
        co_ref = co_ref.at[0]
    h = pl.program_id(0)
    j = pl.program_id(1)
    q_row0 = pl.multiple_of(OFF_MQ + h * ML_QK_PAD, ML_QK_PAD)
    k_row0 = pl.multiple_of(OFF_MK + h * ML_QK_PAD, ML_QK_PAD)
    v_row0 = pl.multiple_of(OFF_MV + h * ML_V_PAD, ML_V_PAD)
    o_row0 = pl.multiple_of(OFF_MO + h * ML_V_PAD, ML_V_PAD)

    @pl.when(j == 0)
    def _():
        i_pre = pt_ref[pl.ds(OFF_MQ + GATE_I_LANE + h, 1), :] + gb_ref[pl.ds(h, 1), :]
        lf = _log_sigmoid(pt_ref[pl.ds(OFF_MQ + GATE_F_LANE + h, 1), :]
                          + gb_ref[pl.ds(ML_HEADS + h, 1), :])
        inter = lf + m_ref[pl.ds(h, 1), :]
        m_new = jnp.maximum(inter, i_pre)
        w = jnp.exp(i_pre - m_new)
        rows_scr[0:1, :] = jnp.exp(inter - m_new)
        rows_scr[1:2, :] = w
        rows_scr[2:3, :] = m_new
        wk_scr[...] = w * (pt_ref[pl.ds(k_row0, ML_QK), :] * (ML_QK ** -0.5))
        qc_scr[...] = jnp.zeros_like(qc_scr)

    a = rows_scr[0:1, :]
    v_t = pt_ref[pl.ds(v_row0, ML_V), :]

    def body(dd, qc):
        d = j * ML_DBLK + dd
        c_old = c_ref[dd]
        co_ref[dd] = a * c_old + wk_scr[pl.ds(d, 1), :] * v_t
        return qc + pt_ref[pl.ds(q_row0 + d, 1), :] * c_old

    qc_scr[...] = lax.fori_loop(0, ML_DBLK, body, qc_scr[...])

    @pl.when(j == pl.num_programs(1) - 1)
    def _():
        w = rows_scr[1:2, :]
        m_new = rows_scr[2:3, :]
        q_t = pt_ref[pl.ds(q_row0, ML_QK), :]
        k_t = pt_ref[pl.ds(k_row0, ML_QK), :] * (ML_QK ** -0.5)
        n_old = n_ref[...]
        s = jnp.sum(q_t * k_t, axis=0, keepdims=True) * w
        num = s * v_t + a * qc_scr[...]
        den = s + a * jnp.sum(q_t * n_old, axis=0, keepdims=True)
        hh = num / jnp.maximum(jnp.abs(den), jnp.exp(-m_new))
        scale = lax.rsqrt(jnp.sum(hh * hh, axis=0, keepdims=True) / ML_V + EPS)
        out_t = (hh * scale * gml_ref[0:ML_V, :]
                 * jax.nn.sigmoid(pt_ref[pl.ds(o_row0, ML_V), :]))
        out_t = jnp.concatenate([out_t, jnp.zeros((ML_V_PAD - ML_V, out_t.shape[1]), F32)], axis=0)
        mix_ref[...] = out_t.T.astype(BF16)
        no_ref[...] = a * n_old + wk_scr[...]
        mo_ref[pl.ds(h, 1), :] = m_new


def _decode_mlstm(proj_t, gains, layer, c_t, n_t, m_t, stacked_prev):
    nb = proj_t.shape[1]
    c_blk = pl.BlockSpec((None, None, ML_DBLK, ML_V, nb), lambda h, j: (layer, h, j, 0, 0))
    in_specs = [
        pl.BlockSpec((PROJ_PAD, nb), lambda h, j: (0, 0)),
        pl.BlockSpec((SUBLANES, LANES), lambda h, j: (0, 0)),
        pl.BlockSpec((None, ML_V_PAD, nb), lambda h, j: (h, 0, 0)),
        c_blk,
        pl.BlockSpec((None, None, ML_QK, nb), lambda h, j: (layer, h, 0, 0)),
        pl.BlockSpec((None, ML_HEADS, nb), lambda h, j: (layer, 0, 0)),
    ]
    assert len(in_specs) == DEC_ML_INPUTS
    in_specs += [pl.BlockSpec(memory_space=pl.ANY)] * len(stacked_prev)
    if stacked_prev:
        co_blk = c_blk
    else:
        assert layer == 0
        co_blk = pl.BlockSpec((c_t.shape[0], None, ML_DBLK, ML_V, nb),
                              lambda h, j: (0, h, j, 0, 0))
    out_shape = [
        jax.ShapeDtypeStruct((ML_HEADS, nb, ML_V_PAD), BF16),
        jax.ShapeDtypeStruct(c_t.shape, F32),
        jax.ShapeDtypeStruct((ML_HEADS, ML_QK, nb), F32),
        jax.ShapeDtypeStruct((ML_HEADS, nb), F32),
    ]
    out_specs = [
        pl.BlockSpec((None, nb, ML_V_PAD), lambda h, j: (h, 0, 0)),
        co_blk,
        pl.BlockSpec((None, ML_QK, nb), lambda h, j: (h, 0, 0)),
        pl.BlockSpec((ML_HEADS, nb), lambda h, j: (0, 0)),
    ]
    scratch = [
        pltpu.VMEM((SUBLANES, nb), F32),
        pltpu.VMEM((ML_QK, nb), F32),
        pltpu.VMEM((ML_V, nb), F32),
    ]
    return pl.pallas_call(
        functools.partial(_decode_mlstm_kernel, n_aliased=len(stacked_prev)),
        grid=(ML_HEADS, ML_QK // ML_DBLK),
        in_specs=in_specs,
        out_specs=out_specs,
        out_shape=out_shape,
        scratch_shapes=scratch,
        input_output_aliases={DEC_ML_INPUTS + i: 1 + i for i in range(len(stacked_prev))},
        compiler_params=pltpu.CompilerParams(
            dimension_semantics=("arbitrary", "arbitrary"), vmem_limit_bytes=VMEM_LIMIT),
        name="decode_mlstm",
    )(proj_t, gains["gb_r"], gains["g_ml_t"], c_t, n_t, m_t, *stacked_prev)


def _tables(seq):
    half = RET_DIM // 2
    inv_freq = ROPE_THETA ** (-jnp.arange(half, dtype=F32) / half)

    def cos_sin(pos):
        ang = pos[:, None] * inv_freq[None, :]
        cos, sin = jnp.cos(ang), jnp.sin(ang)
        return jnp.concatenate([cos, cos], axis=-1), jnp.concatenate([-sin, sin], axis=-1)

    cos_p, sin_p = cos_sin(jnp.arange(seq, dtype=F32))
    cos_s, sin_s = cos_sin((PAST_LEN + jnp.arange(1)).astype(F32))
    log_gamma = jnp.log1p(-jnp.exp2(-5.0 - jnp.arange(RET_HEADS, dtype=F32)))
    idx = jnp.arange(CHUNK, dtype=F32)
    diff = idx[:, None] - idx[None, :]
    causal = diff >= 0
    lg = log_gamma[:, None, None]
    dintra = jnp.where(causal, jnp.exp(lg * jnp.where(causal, diff, 0.0)), 0.0)
    dq = jnp.exp(log_gamma[:, None] * (idx + 1.0))
    dk = jnp.exp(log_gamma[:, None] * (CHUNK - 1.0 - idx))

    def pad_rows(t, rows):
        return jnp.pad(t, ((0, rows - t.shape[0]), (0, 0)))

    def chunk_decay(c):
        return pad_rows(jnp.broadcast_to(jnp.exp(log_gamma * c)[:, None], (RET_HEADS, LANES)), SUBLANES)

    return {
        "cos": cos_p, "sin": sin_p,
        "cos_s": jnp.broadcast_to(cos_s, (SUBLANES, RET_DIM)),
        "sin_s": jnp.broadcast_to(sin_s, (SUBLANES, RET_DIM)),
        "dintra": dintra,
        "dq": jnp.pad(dq.T, ((0, 0), (0, LANES - RET_HEADS))),
        "dk": pad_rows(dk, SUBLANES),
        "dch": chunk_decay(float(CHUNK)),
        "dch_s": chunk_decay(1.0),
    }


def _pad_last(t, width):
    return jnp.pad(t, [(0, 0)] * (t.ndim - 1) + [(0, width - t.shape[-1])])


def _prep_gains(l, g_ret_norm, g_mlstm_norm, b_mlstm_i, b_mlstm_f, w_pool, pool_scale):
    gate_bias = jnp.concatenate([b_mlstm_i[l], b_mlstm_f[l]])
    g_ml = _pad_last(g_mlstm_norm[l].reshape(ML_HEADS, ML_V), ML_V_PAD)
    lanes = jnp.concatenate([jnp.zeros((GATE_I_LANE,), F32), _pad_last(gate_bias, LANES - GATE_I_LANE)])
    return {
        "g_ret": g_ret_norm[l][None, :],
        "g_ml": g_ml.reshape(1, -1),
        "g_ml_t": jnp.broadcast_to(g_ml[:, :, None], (ML_HEADS, ML_V_PAD, LANES)),
        "gb_r": jnp.broadcast_to(gate_bias[:, None], (SUBLANES, LANES)),
        "gb_c": jnp.broadcast_to(lanes[None, :], (SUBLANES, LANES)),
        "w_pool": w_pool[l].astype(BF16),
        "pool_scale": pool_scale[l][None, :],
    }


@jax.jit
def kernel(x_prompt, x_sample, state_ret, state_mlstm_c, state_mlstm_n, state_mlstm_m, state_pool,
           w_in, w_out, g_ret_norm, g_mlstm_norm, b_mlstm_i, b_mlstm_f, w_pool, pool_scale,
           g_pre_mix, g_post_mix, g_pre_ffn, g_post_ffn, w_ffn_gate, w_ffn_up, w_ffn_down):
    batch, seq, d = x_prompt.shape
    nb = x_sample.shape[0]
    depth = w_in.shape[0]
    tables = _tables(seq)
    tables_s = dict(tables, dch=tables["dch_s"])
    w_in_t = jnp.swapaxes(w_in, 1, 2)

    yp = x_prompt.reshape(batch * seq, d)
    ys = x_sample.reshape(nb, d)
    new_p, new_s = [], []
    stacked_ret, stacked_c = (), ()
    c_t = jnp.transpose(state_mlstm_c, (0, 2, 3, 4, 1))
    n_t = jnp.transpose(state_mlstm_n, (0, 2, 3, 1))
    m_t = jnp.transpose(state_mlstm_m, (0, 2, 1))
    pool_t = jnp.transpose(state_pool, (0, 2, 1, 3))

    t = _token_tiles(batch * seq)
    t_s = _token_tiles(nb)
    for l in range(depth):
        w_in_l = _prep_w_in(w_in_t, l)
        w_out_l = _prep_w_out(w_out, l)
        gains = _prep_gains(l, g_ret_norm, g_mlstm_norm, b_mlstm_i, b_mlstm_f, w_pool, pool_scale)
        row = lambda g: g[l][None, :]

        mixed, s_ret, s_c, s_n, s_m, s_pool = _prompt_front(
            yp.reshape(batch, seq, d), row(g_pre_mix), w_in_l, tables, gains)
        new_p.append((s_ret, s_c[:, :, :ML_QK, :ML_V], s_n[:, :ML_HEADS, :ML_QK],
                      s_m[:, :ML_HEADS, 0], s_pool[:, 1:, :]))

        proj, proj_t = _inproj(ys, row(g_pre_mix), w_in_l, tm=t_s["inproj_tm"], tn=t_s["inproj_tn"])
        mix_ret, mix_pool, s_ret, pool_new = _decode_ret_pool(
            proj, tables_s, gains, l, state_ret, pool_t, stacked_ret)
        mix_ml, s_c, n_new, m_new = _decode_mlstm(proj_t, gains, l, c_t, n_t, m_t, stacked_c)
        mixed_s = jnp.concatenate(
            [mix_ret, jnp.swapaxes(mix_ml, 0, 1).reshape(nb, ML_HEADS * ML_V_PAD), mix_pool], axis=1)
        stacked_ret, stacked_c = (s_ret,), (s_c,)
        new_s.append((n_new, m_new, pool_new))

        x1, x1_s = _outproj(mixed, mixed_s, w_out_l, yp, ys, row(g_post_mix),
                            tm=t["outproj_tm"], tk=t["outproj_tk"])
        yp, ys = _ffn(x1, x1_s, row(g_pre_ffn), w_ffn_gate, w_ffn_up, w_ffn_down, row(g_post_ffn),
                      l, tm=t["ffn_tm"], tf=t["ffn_tf"])

    stack = lambda states, i: jnp.stack([s[i] for s in states], axis=0)
    return ((yp.reshape(batch, seq, d), ys.reshape(nb, 1, d))
            + tuple(stack(new_p, i) for i in range(5))
            + (stacked_ret[0],
               jnp.transpose(stacked_c[0], (0, 4, 1, 2, 3)),
               jnp.transpose(stack(new_s, 0), (0, 3, 1, 2)),
               jnp.transpose(stack(new_s, 1), (0, 2, 1)),
               jnp.transpose(stack(new_s, 2), (0, 2, 1, 3))))
```

```python
import functools

import jax
import jax.numpy as jnp
from jax import lax
from jax.experimental import pallas as pl
from jax.experimental.pallas import tpu as pltpu

F32 = jnp.float32
BF16 = jnp.bfloat16

D_MODEL = 2048
PAST_LEN = 16384
RET_HEADS = 6
RET_DIM = 128
RET_WIDTH = RET_HEADS * RET_DIM
ML_HEADS = 4
ML_QK = 96
ML_V = 192
ML_WIDTH = ML_HEADS * ML_V
POOL_WIDTH = 512
POOL_WINDOWS = (2, 4, 8, 16)
POOL_GROUP = POOL_WIDTH // len(POOL_WINDOWS)
POOL_BUF = max(POOL_WINDOWS) - 1
D_FF = 5632
CHUNK = 128
ROPE_THETA = 10000.0
EPS = 1e-6

LANES = 128
SUBLANES = 8
VMEM_LIMIT = 56 * 1024 * 1024

ML_QK_PAD = LANES
ML_V_PAD = 2 * LANES
OFF_RQ = 0
OFF_RK = OFF_RQ + RET_WIDTH
OFF_RV = OFF_RK + RET_WIDTH
OFF_RG = OFF_RV + RET_WIDTH
OFF_MQ = OFF_RG + RET_WIDTH
OFF_MK = OFF_MQ + ML_HEADS * ML_QK_PAD
OFF_MV = OFF_MK + ML_HEADS * ML_QK_PAD
OFF_MO = OFF_MV + ML_HEADS * ML_V_PAD
OFF_PU = OFF_MO + ML_HEADS * ML_V_PAD
PROJ_PAD = OFF_PU + POOL_WIDTH
GATE_I_LANE = ML_QK
GATE_F_LANE = ML_QK + ML_HEADS
MIX_OFF_ML = RET_WIDTH
MIX_OFF_POOL = MIX_OFF_ML + ML_HEADS * ML_V_PAD
MIX_PAD = MIX_OFF_POOL + POOL_WIDTH

_HI = lax.Precision.HIGHEST


def _token_tiles(m):
    return {
        "inproj_tm": min(m, 1024), "inproj_tn": 512,
        "outproj_tm": min(m, 512),
        "ffn_tm": min(m, 1024), "ffn_tf": 256,
    }


def _dot(a, b):
    return jnp.dot(a, b, preferred_element_type=F32)


def _rms(x, width=None):
    n = x.shape[-1] if width is None else width
    return x * lax.rsqrt(jnp.sum(x * x, axis=-1, keepdims=True) / n + EPS)


def _log_sigmoid(x):
    return jnp.minimum(x, 0.0) - jnp.log1p(jnp.exp(-jnp.abs(x)))


def _rotate(x, cos2, sin2):
    return x * cos2 + pltpu.roll(x, RET_DIM // 2, 1) * sin2


def _inproj_kernel(x_ref, g_ref, w_ref, o_ref, ot_ref, xn_ref):
    @pl.when(pl.program_id(1) == 0)
    def _():
        xn_ref[...] = (_rms(x_ref[...]) * g_ref[...]).astype(BF16)

    res = _dot(xn_ref[...], w_ref[...])
    o_ref[...] = res
    ot_ref[...] = res.T


def _inproj(x, g, w, tm, tn):
    m = x.shape[0]
    return pl.pallas_call(
        _inproj_kernel,
        grid=(m // tm, PROJ_PAD // tn),
        in_specs=[
            pl.BlockSpec((tm, D_MODEL), lambda i, j: (i, 0)),
            pl.BlockSpec((1, D_MODEL), lambda i, j: (0, 0)),
            pl.BlockSpec((D_MODEL, tn), lambda i, j: (0, j)),
        ],
        out_specs=[
            pl.BlockSpec((tm, tn), lambda i, j: (i, j)),
            pl.BlockSpec((tn, tm), lambda i, j: (j, i)),
        ],
        out_shape=[
            jax.ShapeDtypeStruct((m, PROJ_PAD), F32),
            jax.ShapeDtypeStruct((PROJ_PAD, m), F32),
        ],
        scratch_shapes=[pltpu.VMEM((tm, D_MODEL), BF16)],
        compiler_params=pltpu.CompilerParams(
            dimension_semantics=("parallel", "arbitrary"), vmem_limit_bytes=VMEM_LIMIT),
        name="inproj",
    )(x, g, w)


def _outproj_kernel(m_ref, ms_ref, w_ref, x_ref, xs_ref, gpost_ref, x1_ref, x1s_ref):
    @pl.when(pl.program_id(0) == 0)
    def _():
        x1s_ref[...] = xs_ref[...] + _rms(_dot(ms_ref[...], w_ref[...])) * gpost_ref[...]

    x1_ref[...] = x_ref[...] + _rms(_dot(m_ref[...], w_ref[...])) * gpost_ref[...]


def _outproj(mixed, mixed_s, w, x, x_s, g_post, tm):
    m, ns = x.shape[0], x_s.shape[0]
    return pl.pallas_call(
        _outproj_kernel,
        grid=(m // tm,),
        in_specs=[
            pl.BlockSpec((tm, MIX_PAD), lambda i: (i, 0)),
            pl.BlockSpec((ns, MIX_PAD), lambda i: (0, 0)),
            pl.BlockSpec((MIX_PAD, D_MODEL), lambda i: (0, 0), pipeline_mode=pl.Buffered(1)),
            pl.BlockSpec((tm, D_MODEL), lambda i: (i, 0)),
            pl.BlockSpec((ns, D_MODEL), lambda i: (0, 0)),
            pl.BlockSpec((1, D_MODEL), lambda i: (0, 0)),
        ],
        out_specs=[
            pl.BlockSpec((tm, D_MODEL), lambda i: (i, 0)),
            pl.BlockSpec((ns, D_MODEL), lambda i: (0, 0)),
        ],
        out_shape=[
            jax.ShapeDtypeStruct((m, D_MODEL), F32),
            jax.ShapeDtypeStruct((ns, D_MODEL), F32),
        ],
        compiler_params=pltpu.CompilerParams(
            dimension_semantics=("arbitrary",), vmem_limit_bytes=VMEM_LIMIT),
        name="outproj",
    )(mixed, mixed_s, w, x, x_s, g_post)


FFN_ROW_CHUNK = 256


def _ffn_kernel(x1_ref, x1s_ref, gpre_ref, wg_ref, wu_ref, wd_ref, gpost_ref, o_ref, os_ref,
                hn_ref, hns_ref):
    first_tile = pl.program_id(0) == 0
    f = pl.program_id(1)
    last = pl.num_programs(1) - 1

    def swiglu_down(hn, wg, wu, wd):
        gate = _dot(hn, wg)
        act = (gate * jax.nn.sigmoid(gate) * _dot(hn, wu)).astype(BF16)
        return _dot(act, wd)

    def middle_step(h_ref, acc_ref):
        acc_ref[...] += swiglu_down(h_ref[...], wg_ref[...].astype(BF16),
                                    wu_ref[...].astype(BF16), wd_ref[...].astype(BF16))

    def row_chunks(ref):
        rows = ref.shape[0]
        chunk = min(rows, FFN_ROW_CHUNK)
        return [slice(r, r + chunk) for r in range(0, rows, chunk)]

    def first_step(x_ref, h_ref, acc_ref):
        wg, wu, wd = (w[...].astype(BF16) for w in (wg_ref, wu_ref, wd_ref))
        for rows in row_chunks(x_ref):
            hn = (_rms(x_ref[rows, :]) * gpre_ref[...]).astype(BF16)
            h_ref[rows, :] = hn
            acc_ref[rows, :] = swiglu_down(hn, wg, wu, wd)

    def last_step(x_ref, h_ref, acc_ref):
        wg, wu, wd = (w[...].astype(BF16) for w in (wg_ref, wu_ref, wd_ref))
        for rows in row_chunks(x_ref):
            acc = acc_ref[rows, :] + swiglu_down(h_ref[rows, :], wg, wu, wd)
            acc_ref[rows, :] = x_ref[rows, :] + _rms(acc) * gpost_ref[...]

    middle = (f > 0) & (f < last)
    for on_tile, refs in ((first_tile, (x1s_ref, hns_ref, os_ref)), (True, (x1_ref, hn_ref, o_ref))):
        pl.when(on_tile & (f == 0))(functools.partial(first_step, *refs))
        pl.when(on_tile & middle)(functools.partial(middle_step, *refs[1:]))
        pl.when(on_tile & (f == last))(functools.partial(last_step, *refs))


def _ffn(x1, x1_s, g_pre, wg, wu, wd, g_post, layer, tm, tf):
    m, ns = x1.shape[0], x1_s.shape[0]
    return pl.pallas_call(
        _ffn_kernel,
        grid=(m // tm, D_FF // tf),
        in_specs=[
            pl.BlockSpec((tm, D_MODEL), lambda i, f: (i, 0)),
            pl.BlockSpec((ns, D_MODEL), lambda i, f: (0, 0)),
            pl.BlockSpec((1, D_MODEL), lambda i, f: (0, 0)),
            pl.BlockSpec((None, D_MODEL, tf), lambda i, f: (layer, 0, f)),
            pl.BlockSpec((None, D_MODEL, tf), lambda i, f: (layer, 0, f)),
            pl.BlockSpec((None, tf, D_MODEL), lambda i, f: (layer, f, 0)),
            pl.BlockSpec((1, D_MODEL), lambda i, f: (0, 0)),
        ],
        out_specs=[
            pl.BlockSpec((tm, D_MODEL), lambda i, f: (i, 0)),
            pl.BlockSpec((ns, D_MODEL), lambda i, f: (0, 0)),
        ],
        out_shape=[
            jax.ShapeDtypeStruct((m, D_MODEL), F32),
            jax.ShapeDtypeStruct((ns, D_MODEL), F32),
        ],
        scratch_shapes=[pltpu.VMEM((tm, D_MODEL), BF16), pltpu.VMEM((ns, D_MODEL), BF16)],
        compiler_params=pltpu.CompilerParams(
            dimension_semantics=("arbitrary", "arbitrary"), vmem_limit_bytes=VMEM_LIMIT),
        name="ffn",
    )(x1, x1_s, g_pre, wg, wu, wd, g_post)


SRC_MQ = 4 * RET_WIDTH
SRC_MK = SRC_MQ + ML_HEADS * ML_QK
SRC_MV = SRC_MK + ML_HEADS * ML_QK
SRC_MO = SRC_MV + ML_WIDTH
SRC_MI = SRC_MO + ML_WIDTH
SRC_MF = SRC_MI + ML_HEADS
SRC_PU = SRC_MF + ML_HEADS
IN_WIDTH = SRC_PU + POOL_WIDTH


def _prep_w_in_kernel(w_ref, o_ref):
    tk = w_ref.shape[1]

    def put(dst, pieces, width):
        rows = [w_ref[src:src + n, :] for src, n in pieces]
        have = sum(n for _, n in pieces)
        if have < width:
            rows.append(jnp.zeros((width - have, tk), F32))
        block = rows[0] if len(rows) == 1 else jnp.concatenate(rows, axis=0)
        o_ref[:, dst:dst + width] = block.T.astype(BF16)

    for r in range(0, 4 * RET_WIDTH, 2 * LANES):
        put(r, [(r, 2 * LANES)], 2 * LANES)
    for h in range(ML_HEADS):
        q_rows = [(SRC_MQ + h * ML_QK, ML_QK)]
        if h == 0:
            q_rows.append((SRC_MI, 2 * ML_HEADS))
        put(OFF_MQ + h * ML_QK_PAD, q_rows, ML_QK_PAD)
        put(OFF_MK + h * ML_QK_PAD, [(SRC_MK + h * ML_QK, ML_QK)], ML_QK_PAD)
        put(OFF_MV + h * ML_V_PAD, [(SRC_MV + h * ML_V, ML_V)], ML_V_PAD)
        put(OFF_MO + h * ML_V_PAD, [(SRC_MO + h * ML_V, ML_V)], ML_V_PAD)
    for r in range(0, POOL_WIDTH, 2 * LANES):
        put(OFF_PU + r, [(SRC_PU + r, 2 * LANES)], 2 * LANES)


def _prep_w_in(w_t, layer, tk=256):
    d = w_t.shape[2]
    return pl.pallas_call(
        _prep_w_in_kernel,
        grid=(d // tk,),
        in_specs=[pl.BlockSpec((None, IN_WIDTH, tk), lambda i: (layer, 0, i))],
        out_specs=pl.BlockSpec((tk, PROJ_PAD), lambda i: (i, 0)),
        out_shape=jax.ShapeDtypeStruct((d, PROJ_PAD), BF16),
        compiler_params=pltpu.CompilerParams(
            dimension_semantics=("parallel",), vmem_limit_bytes=VMEM_LIMIT),
        name="prep_w_in",
    )(w_t)


def _prep_w_out_kernel(w_ref, o_ref):
    def put(dst, src, rows):
        o_ref[dst:dst + rows, :] = w_ref[src:src + rows, :].astype(BF16)

    put(0, 0, RET_WIDTH)
    for h in range(ML_HEADS):
        dst = MIX_OFF_ML + h * ML_V_PAD
        put(dst, RET_WIDTH + h * ML_V, ML_V)
        o_ref[dst + ML_V:dst + ML_V_PAD, :] = jnp.zeros((ML_V_PAD - ML_V, o_ref.shape[1]), BF16)
    put(MIX_OFF_POOL, RET_WIDTH + ML_WIDTH, POOL_WIDTH)


def _prep_w_out(w, layer, tn=512):
    _, k, n = w.shape
    return pl.pallas_call(
        _prep_w_out_kernel,
        grid=(n // tn,),
        in_specs=[pl.BlockSpec((None, k, tn), lambda j: (layer, 0, j))],
        out_specs=pl.BlockSpec((MIX_PAD, tn), lambda j: (0, j)),
        out_shape=jax.ShapeDtypeStruct((MIX_PAD, n), BF16),
        compiler_params=pltpu.CompilerParams(
            dimension_semantics=("parallel",), vmem_limit_bytes=VMEM_LIMIT),
        name="prep_w_out",
    )(w)


PROMPT_BLOCK = 2


def _ret_head(q, k, get_v, get_g, cos2, sin2, dintra, dq_col, dk_row, dch_row, g_norm, s_old,
              store):
    qb = _rotate(q, cos2, sin2).astype(BF16)
    kr_t = (_rotate(k, cos2, sin2) * (RET_DIM ** -0.5)).T
    kb = kr_t.astype(BF16)
    kdb = (kr_t * dk_row).astype(BF16)
    yield
    scores = (_dot(qb, kb) * dintra).astype(BF16)
    vb = get_v().astype(BF16)
    g = get_g()
    gate = g * jax.nn.sigmoid(g)
    yield
    o = _dot(scores, vb) + _dot(qb, s_old.astype(BF16)) * dq_col
    s_new = s_old * dch_row + _dot(kdb, vb)
    store(_rms(o) * g_norm * gate, s_new)


def _ml_head(q, k, vb, og, causal, b_c, i_c, b_r, i_r, g_norm, c_old, n_old, m_prev, store):
    k_t = k.T
    qb = q.astype(BF16)
    logw = jnp.where(causal, b_c - b_r + i_r, -jnp.inf)
    inter = b_c + m_prev
    m_t = jnp.maximum(inter, jnp.max(logw, axis=-1, keepdims=True))
    w = jnp.exp(logw - m_t)
    a = jnp.exp(inter - m_t)
    m_new = m_t[CHUNK - 1:CHUNK, :]
    b_last = b_c[CHUNK - 1:CHUNK, :]
    w_r = jnp.exp(b_last - b_r + i_r - m_new)
    w_c = jnp.exp(b_last - b_c + i_c - m_new)
    a_end = jnp.exp(b_last + m_prev - m_new)
    kwb = (k_t * w_r).astype(BF16)
    n_new = a_end * n_old + jnp.sum(k * w_c, axis=0, keepdims=True)
    qn = jnp.sum(q * n_old, axis=-1, keepdims=True)
    floor = jnp.exp(-m_t)
    gate = jax.nn.sigmoid(og)
    yield
    s = _dot(qb, k_t.astype(BF16)) * w
    den = jnp.sum(s, axis=-1, keepdims=True) + a * qn
    sb = s.astype(BF16)
    yield
    num = _dot(sb, vb) + a * _dot(qb, c_old.astype(BF16))
    c_new = a_end * c_old + _dot(kwb, vb)
    hh = num / jnp.maximum(jnp.abs(den), floor)
    store(_rms(hh, ML_V) * g_norm * gate, c_new, n_new, m_new)


def _prompt_front_kernel(x_ref, gpre_ref, w_ref, cos_ref, sin_ref, dintra_ref, dq_ref, dk_ref,
                         dch_ref, gret_ref, gml_ref, gb_r_ref, gb_c_ref, wpool_ref, pscale_ref,
                         mix_ref, sret_ref, sc_ref, sn_ref, sm_ref, spool_ref,
                         s_scr, c_scr, n_scr, m_scr, u_scr):
    bp = pl.program_id(0)
    c = pl.program_id(1)
    nseq = x_ref.shape[0]

    @pl.when(c == 0)
    def _():
        s_scr[...] = jnp.zeros_like(s_scr)
        c_scr[...] = jnp.zeros_like(c_scr)
        n_scr[...] = jnp.zeros_like(n_scr)
        m_scr[...] = jnp.zeros_like(m_scr)
        u_scr[:, 0:2 * SUBLANES, :] = jnp.zeros((nseq, 2 * SUBLANES, POOL_WIDTH), F32)

    xn = (_rms(x_ref[...].reshape(nseq * CHUNK, D_MODEL)) * gpre_ref[...]).astype(BF16)

    def proj(lo, width):
        return _dot(xn, w_ref[:, lo:lo + width])

    def seq_rows(t, i):
        return t[i * CHUNK:(i + 1) * CHUNK]

    cos2 = cos_ref[...]
    sin2 = sin_ref[...]
    pair = 2 * RET_DIM

    def ret_project(h0):
        return [functools.partial(proj, off + h0 * RET_DIM, pair)
                for off in (OFF_RQ, OFF_RK, OFF_RV, OFF_RG)]

    def ret_mix(h0, projected):
        pq, pk = projected[:2]
        instances = []
        for h in (h0, h0 + 1):
            lo, hi = (h - h0) * RET_DIM, (h - h0 + 1) * RET_DIM
            for i in range(nseq):
                def store(out, s_new, h=h, i=i):
                    s_scr[i, h] = s_new
                    mix_ref[i, :, h * RET_DIM:(h + 1) * RET_DIM] = out.astype(BF16)

                def late(n, i=i, lo=lo, hi=hi):
                    return seq_rows(projected[n], i)[:, lo:hi]

                instances.append(_ret_head(
                    seq_rows(pq, i)[:, lo:hi], seq_rows(pk, i)[:, lo:hi],
                    functools.partial(late, 2), functools.partial(late, 3),
                    cos2, sin2, dintra_ref[h], dq_ref[:, h:h + 1], dk_ref[h:h + 1, :],
                    dch_ref[h:h + 1, :], gret_ref[:, h * RET_DIM:(h + 1) * RET_DIM], s_scr[i, h],
                    store))
        return instances

    row = lax.broadcasted_iota(jnp.int32, (CHUNK, CHUNK), 0)
    col = lax.broadcasted_iota(jnp.int32, (CHUNK, CHUNK), 1)
    causal = col <= row
    upper = (row <= col).astype(F32)
    lower = causal.astype(F32)
    gates = [None] * nseq

    def ml_project(h0):
        return [functools.partial(proj, OFF_MQ + h0 * ML_QK_PAD, 2 * ML_QK_PAD),
                functools.partial(proj, OFF_MK + h0 * ML_QK_PAD, 2 * ML_QK_PAD),
                functools.partial(proj, OFF_MV + h0 * ML_V_PAD, 2 * ML_V_PAD),
                functools.partial(proj, OFF_MO + h0 * ML_V_PAD, 2 * ML_V_PAD)]

    def ml_mix(h0, projected):
        pq, pk, pv, po = projected
        if h0 == 0:
            for i in range(nseq):
                gq0 = seq_rows(pq, i)[:, 0:ML_QK_PAD]
                i_r8 = gq0.T[GATE_I_LANE:GATE_I_LANE + SUBLANES, :] + gb_r_ref[...]
                b_r8 = jnp.dot(_log_sigmoid(i_r8), upper, precision=_HI, preferred_element_type=F32)
                pre_c = gq0 + gb_c_ref[0:1, :]
                b_c_all = jnp.dot(lower, _log_sigmoid(pre_c), precision=_HI,
                                  preferred_element_type=F32)
                gates[i] = (i_r8, b_r8, pre_c, b_c_all)
        instances = []
        for h in (h0, h0 + 1):
            qlo = (h - h0) * ML_QK_PAD
            vlo = (h - h0) * ML_V_PAD
            for i in range(nseq):
                def store(out, c_new, n_new, m_new, h=h, i=i):
                    c_scr[i, h] = c_new
                    n_scr[i, h:h + 1, :] = n_new
                    m_scr[i, h:h + 1, :] = jnp.broadcast_to(m_new, (1, LANES))
                    mix_ref[i, :, MIX_OFF_ML + h * ML_V_PAD:MIX_OFF_ML + (h + 1) * ML_V_PAD] = (
                        out.astype(BF16))

                i_r8, b_r8, pre_c, b_c_all = gates[i]
                instances.append(_ml_head(
                    seq_rows(pq, i)[:, qlo:qlo + ML_QK_PAD],
                    seq_rows(pk, i)[:, qlo:qlo + ML_QK_PAD] * (ML_QK ** -0.5),
                    seq_rows(pv, i)[:, vlo:vlo + ML_V_PAD].astype(BF16),
                    seq_rows(po, i)[:, vlo:vlo + ML_V_PAD],
                    causal,
                    b_c_all[:, GATE_F_LANE + h:GATE_F_LANE + h + 1],
                    pre_c[:, GATE_I_LANE + h:GATE_I_LANE + h + 1],
                    b_r8[ML_HEADS + h:ML_HEADS + h + 1, :], i_r8[h:h + 1, :],
                    gml_ref[:, h * ML_V_PAD:(h + 1) * ML_V_PAD],
                    c_scr[i, h], n_scr[i, h:h + 1, :], m_scr[i, h:h + 1, 0:1], store))
        return instances

    base = 2 * SUBLANES
    pos = (c * CHUNK + lax.broadcasted_iota(jnp.int32, (CHUNK, 1), 0)).astype(F32)

    def pool_seq(u, i):
        u_scr[i, base:base + CHUNK, :] = u
        zs = []
        for g, win in enumerate(POOL_WINDOWS):
            lo, hi = g * POOL_GROUP, (g + 1) * POOL_GROUP
            wsum = u[:, lo:hi]
            for s in range(1, win):
                wsum = wsum + u_scr[i, base - s:base - s + CHUNK, lo:hi]
            zs.append((wsum / jnp.minimum(pos + 1.0, float(win)) - u[:, lo:hi]).astype(BF16))
        u_scr[i, 0:base, :] = u_scr[i, CHUNK:CHUNK + base, :]
        yield
        for g, z in enumerate(zs):
            lo, hi = g * POOL_GROUP, (g + 1) * POOL_GROUP
            y = _dot(z, wpool_ref[g]) * pscale_ref[:, lo:hi]
            mix_ref[i, :, MIX_OFF_POOL + lo:MIX_OFF_POOL + hi] = y.astype(BF16)

    def pool_mix(projected):
        pu, = projected
        return [pool_seq(seq_rows(pu, i), i) for i in range(nseq)]

    stages = [(ret_project(h0), functools.partial(ret_mix, h0)) for h0 in range(0, RET_HEADS, 2)]
    stages += [(ml_project(h0), functools.partial(ml_mix, h0)) for h0 in range(0, ML_HEADS, 2)]
    stages.append(([functools.partial(proj, OFF_PU, POOL_WIDTH)], pool_mix))
    early = 2
    projected = [dot() for dot in stages[0][0][:early]]
    late_dots = list(stages[0][0][early:])
    for n, (_, mix) in enumerate(stages):
        pending = list(stages[n + 1][0]) if n + 1 < len(stages) else []
        upcoming = []
        live = mix(projected)
        while live:
            for _ in range(max(1, (len(late_dots) + len(pending)) // 2)):
                if late_dots:
                    projected.append(late_dots.pop(0)())
                elif pending:
                    upcoming.append(pending.pop(0)())
            live = [inst for inst in live if next(inst, StopIteration) is not StopIteration]
        upcoming += [dot() for dot in pending]
        projected = upcoming

    @pl.when(c == pl.num_programs(1) - 1)
    def _():
        for i in range(nseq):
            b = bp * nseq + i
            sret_ref[b] = s_scr[i]
            sc_ref[b] = c_scr[i]
            sn_ref[b] = n_scr[i]
            sm_ref[b] = m_scr[i]
            spool_ref[b] = u_scr[i, 0:base, :]


def _prompt_front(x, g_pre, w, tables, gains):
    batch, seq, _ = x.shape
    nchunk = seq // CHUNK
    nseq = PROMPT_BLOCK
    const2 = lambda b, c: (0, 0)
    const3 = lambda b, c: (0, 0, 0)
    const4 = lambda b, c: (0, 0, 0, 0)
    in_specs = [
        pl.BlockSpec((nseq, CHUNK, D_MODEL), lambda b, c: (b, c, 0)),
        pl.BlockSpec((1, D_MODEL), const2),
        pl.BlockSpec((D_MODEL, PROJ_PAD), const2, pipeline_mode=pl.Buffered(1)),
        pl.BlockSpec((CHUNK, RET_DIM), lambda b, c: (c, 0)),
        pl.BlockSpec((CHUNK, RET_DIM), lambda b, c: (c, 0)),
        pl.BlockSpec((RET_HEADS, CHUNK, CHUNK), const3),
        pl.BlockSpec((CHUNK, LANES), const2),
        pl.BlockSpec((SUBLANES, CHUNK), const2),
        pl.BlockSpec((SUBLANES, LANES), const2),
        pl.BlockSpec((1, RET_WIDTH), const2),
        pl.BlockSpec((1, ML_HEADS * ML_V_PAD), const2),
        pl.BlockSpec((SUBLANES, LANES), const2),
        pl.BlockSpec((SUBLANES, LANES), const2),
        pl.BlockSpec((len(POOL_WINDOWS), POOL_GROUP, POOL_GROUP), const3),
        pl.BlockSpec((1, POOL_WIDTH), const2),
    ]
    out_shape = [
        jax.ShapeDtypeStruct((batch, seq, MIX_PAD), BF16),
        jax.ShapeDtypeStruct((batch, RET_HEADS, RET_DIM, RET_DIM), F32),
        jax.ShapeDtypeStruct((batch, ML_HEADS, ML_QK_PAD, ML_V_PAD), F32),
        jax.ShapeDtypeStruct((batch, SUBLANES, LANES), F32),
        jax.ShapeDtypeStruct((batch, SUBLANES, LANES), F32),
        jax.ShapeDtypeStruct((batch, 2 * SUBLANES, POOL_WIDTH), F32),
    ]
    out_specs = [
        pl.BlockSpec((nseq, CHUNK, MIX_PAD), lambda b, c: (b, c, 0)),
        pl.BlockSpec((batch, RET_HEADS, RET_DIM, RET_DIM), const4),
        pl.BlockSpec((batch, ML_HEADS, ML_QK_PAD, ML_V_PAD), const4),
        pl.BlockSpec((batch, SUBLANES, LANES), const3),
        pl.BlockSpec((batch, SUBLANES, LANES), const3),
        pl.BlockSpec((batch, 2 * SUBLANES, POOL_WIDTH), const3),
    ]
    scratch = [
        pltpu.VMEM((nseq, RET_HEADS, RET_DIM, RET_DIM), F32),
        pltpu.VMEM((nseq, ML_HEADS, ML_QK_PAD, ML_V_PAD), F32),
        pltpu.VMEM((nseq, SUBLANES, LANES), F32),
        pltpu.VMEM((nseq, SUBLANES, LANES), F32),
        pltpu.VMEM((nseq, 2 * SUBLANES + CHUNK, POOL_WIDTH), F32),
    ]
    mixed, *states = pl.pallas_call(
        _prompt_front_kernel,
        grid=(batch // nseq, nchunk),
        in_specs=in_specs,
        out_specs=out_specs,
        out_shape=out_shape,
        scratch_shapes=scratch,
        compiler_params=pltpu.CompilerParams(
            dimension_semantics=("arbitrary", "arbitrary"), vmem_limit_bytes=VMEM_LIMIT),
        name="prompt_front",
    )(x, g_pre, w, tables["cos"], tables["sin"], tables["dintra"], tables["dq"], tables["dk"],
      tables["dch"], gains["g_ret"], gains["g_ml"], gains["gb_r"], gains["gb_c"],
      gains["w_pool"], gains["pool_scale"])
    return (mixed.reshape(batch * seq, MIX_PAD), *states)


DEC_BLOCK = 8
DEC_RET_INPUTS = 9


def _decode_ret_pool_kernel(*refs, n_aliased):
    (p_ref, cos_ref, sin_ref, dch_ref, gret_ref, wpool_ref, pscale_ref, s_ref,
     pool_ref) = refs[:DEC_RET_INPUTS]
    (mix_ref, mixp_ref, so_ref, poolo_ref, qr_scr, kt_scr, v_scr,
     qs_scr) = refs[DEC_RET_INPUTS + n_aliased:]
    if n_aliased == 0:
        for later in range(1, so_ref.shape[0]):
            so_ref[later] = jnp.zeros(so_ref.shape[1:], F32)
        so_ref = so_ref.at[0]
    step = pl.program_id(0)
    nb = p_ref.shape[0]
    cos2 = cos_ref[0:1, :]
    sin2 = sin_ref[0:1, :]

    def ret_qkv(h):
        lo, hi = h * RET_DIM, (h + 1) * RET_DIM
        qr = _rotate(p_ref[:, OFF_RQ + lo:OFF_RQ + hi], cos2, sin2)
        kr = _rotate(p_ref[:, OFF_RK + lo:OFF_RK + hi], cos2, sin2) * (RET_DIM ** -0.5)
        return qr, kr, p_ref[:, OFF_RV + lo:OFF_RV + hi]

    @pl.when(step == 0)
    def _():
        for h in range(RET_HEADS):
            qr, kr, v = ret_qkv(h)
            qr_scr[h] = qr
            kt_scr[h] = kr.T
            v_scr[h] = v.astype(BF16)

    lane = lax.broadcasted_iota(jnp.int32, (LANES, nb), 1)

    def body(bb, carry):
        b = step * DEC_BLOCK + bb
        sel = lane == b
        for h in range(RET_HEADS):
            s_old = s_ref[bb, h]
            q16 = jnp.broadcast_to(qr_scr[h, pl.ds(b, 1), :], (2 * SUBLANES, RET_DIM)).astype(BF16)
            qs_scr[h, pl.ds(b, 1), :] = _dot(q16, s_old.astype(BF16))[0:1, :]
            k_m = jnp.where(sel, kt_scr[h], 0.0).astype(BF16)
            so_ref[bb, h] = s_old * dch_ref[h:h + 1, :] + _dot(k_m, v_scr[h])
        return carry

    lax.fori_loop(0, DEC_BLOCK, body, 0)

    @pl.when(step == pl.num_programs(0) - 1)
    def _():
        for h in range(RET_HEADS):
            lo, hi = h * RET_DIM, (h + 1) * RET_DIM
            qr, kr, v = ret_qkv(h)
            g = p_ref[:, OFF_RG + lo:OFF_RG + hi]
            o = jnp.sum(qr * kr, axis=-1, keepdims=True) * v + qs_scr[h] * dch_ref[h:h + 1, :]
            out = _rms(o) * gret_ref[:, lo:hi] * (g * jax.nn.sigmoid(g))
            mix_ref[:, lo:hi] = out.astype(BF16)
        u = p_ref[:, OFF_PU:OFF_PU + POOL_WIDTH]
        for g, win in enumerate(POOL_WINDOWS):
            lo, hi = g * POOL_GROUP, (g + 1) * POOL_GROUP
            wsum = u[:, lo:hi]
            for s in range(1, win):
                wsum = wsum + pool_ref[POOL_BUF - s, :, lo:hi]
            z = wsum / float(win) - u[:, lo:hi]
            y = _dot(z.astype(BF16), wpool_ref[g]) * pscale_ref[:, lo:hi]
            mixp_ref[:, lo:hi] = y.astype(BF16)
        for r in range(POOL_BUF - 1):
            poolo_ref[r] = pool_ref[r + 1]
        poolo_ref[POOL_BUF - 1] = u


def _decode_ret_pool(proj, tables, gains, layer, s_ret, pool_t, stacked_prev):
    nb = proj.shape[0]
    const2 = lambda s: (0, 0)
    const3 = lambda s: (0, 0, 0)
    full2 = lambda shape: pl.BlockSpec(shape, const2)
    s_blk = pl.BlockSpec((None, DEC_BLOCK, RET_HEADS, RET_DIM, RET_DIM),
                         lambda s: (layer, s, 0, 0, 0))
    pool_blk = (POOL_BUF, nb, POOL_WIDTH)
    in_specs = [
        full2((nb, PROJ_PAD)),
        full2((SUBLANES, RET_DIM)),
        full2((SUBLANES, RET_DIM)),
        full2((SUBLANES, LANES)),
        full2((1, RET_WIDTH)),
        pl.BlockSpec((len(POOL_WINDOWS), POOL_GROUP, POOL_GROUP), const3),
        full2((1, POOL_WIDTH)),
        s_blk,
        pl.BlockSpec((None,) + pool_blk, lambda s: (layer, 0, 0, 0)),
    ]
    assert len(in_specs) == DEC_RET_INPUTS
    in_specs += [pl.BlockSpec(memory_space=pl.ANY)] * len(stacked_prev)
    if stacked_prev:
        so_blk = s_blk
    else:
        assert layer == 0
        so_blk = pl.BlockSpec((s_ret.shape[0],) + s_blk.block_shape[1:],
                              lambda s: (0, s, 0, 0, 0))
    out_shape = [
        jax.ShapeDtypeStruct((nb, RET_WIDTH), BF16),
        jax.ShapeDtypeStruct((nb, POOL_WIDTH), BF16),
        jax.ShapeDtypeStruct(s_ret.shape, F32),
        jax.ShapeDtypeStruct(pool_blk, F32),
    ]
    out_specs = [
        full2((nb, RET_WIDTH)),
        full2((nb, POOL_WIDTH)),
        so_blk,
        pl.BlockSpec(pool_blk, const3),
    ]
    scratch = [
        pltpu.VMEM((RET_HEADS, nb, RET_DIM), F32),
        pltpu.VMEM((RET_HEADS, RET_DIM, nb), F32),
        pltpu.VMEM((RET_HEADS, nb, RET_DIM), BF16),
        pltpu.VMEM((RET_HEADS, nb, RET_DIM), F32),
    ]
    return pl.pallas_call(
        functools.partial(_decode_ret_pool_kernel, n_aliased=len(stacked_prev)),
        grid=(nb // DEC_BLOCK,),
        in_specs=in_specs,
        out_specs=out_specs,
        out_shape=out_shape,
        scratch_shapes=scratch,
        input_output_aliases={DEC_RET_INPUTS + i: 2 + i for i in range(len(stacked_prev))},
        compiler_params=pltpu.CompilerParams(
            dimension_semantics=("arbitrary",), vmem_limit_bytes=VMEM_LIMIT),
        name="decode_ret_pool",
    )(proj, tables["cos_s"], tables["sin_s"], tables["dch"], gains["g_ret"], gains["w_pool"],
      gains["pool_scale"], s_ret, pool_t, *stacked_prev)


ML_DBLK = 32
DEC_ML_INPUTS = 6


def _decode_mlstm_kernel(*refs, n_aliased):
    pt_ref, gb_ref, gml_ref, c_ref, n_ref, m_ref = refs[:DEC_ML_INPUTS]
    mix_ref, co_ref, no_ref, mo_ref, rows_scr, wk_scr, qc_scr = refs[DEC_ML_INPUTS + n_aliased:]
    if n_aliased == 0:
        for later in range(1, co_ref.shape[0]):
            co_ref[later] = jnp.zeros(co_ref.shape[1:], F32)
        co_ref = co_ref.at[0]
    h = pl.program_id(0)
    j = pl.program_id(1)
    q_row0 = pl.multiple_of(OFF_MQ + h * ML_QK_PAD, ML_QK_PAD)
    k_row0 = pl.multiple_of(OFF_MK + h * ML_QK_PAD, ML_QK_PAD)
    v_row0 = pl.multiple_of(OFF_MV + h * ML_V_PAD, ML_V_PAD)
    o_row0 = pl.multiple_of(OFF_MO + h * ML_V_PAD, ML_V_PAD)

    @pl.when(j == 0)
    def _():
        i_pre = pt_ref[pl.ds(OFF_MQ + GATE_I_LANE + h, 1), :] + gb_ref[pl.ds(h, 1), :]
        lf = _log_sigmoid(pt_ref[pl.ds(OFF_MQ + GATE_F_LANE + h, 1), :]
                          + gb_ref[pl.ds(ML_HEADS + h, 1), :])
        inter = lf + m_ref[pl.ds(h, 1), :]
        m_new = jnp.maximum(inter, i_pre)
        w = jnp.exp(i_pre - m_new)
        rows_scr[0:1, :] = jnp.exp(inter - m_new)
        rows_scr[1:2, :] = w
        rows_scr[2:3, :] = m_new
        wk_scr[...] = w * (pt_ref[pl.ds(k_row0, ML_QK), :] * (ML_QK ** -0.5))
        qc_scr[...] = jnp.zeros_like(qc_scr)

    a = rows_scr[0:1, :]
    v_t = pt_ref[pl.ds(v_row0, ML_V), :]

    def body(dd, qc):
        d = j * ML_DBLK + dd
        c_old = c_ref[dd]
        co_ref[dd] = a * c_old + wk_scr[pl.ds(d, 1), :] * v_t
        return qc + pt_ref[pl.ds(q_row0 + d, 1), :] * c_old

    qc_scr[...] = lax.fori_loop(0, ML_DBLK, body, qc_scr[...])

    @pl.when(j == pl.num_programs(1) - 1)
    def _():
        w = rows_scr[1:2, :]
        m_new = rows_scr[2:3, :]
        q_t = pt_ref[pl.ds(q_row0, ML_QK), :]
        k_t = pt_ref[pl.ds(k_row0, ML_QK), :] * (ML_QK ** -0.5)
        n_old = n_ref[...]
        s = jnp.sum(q_t * k_t, axis=0, keepdims=True) * w
        num = s * v_t + a * qc_scr[...]
        den = s + a * jnp.sum(q_t * n_old, axis=0, keepdims=True)
        hh = num / jnp.maximum(jnp.abs(den), jnp.exp(-m_new))
        scale = lax.rsqrt(jnp.sum(hh * hh, axis=0, keepdims=True) / ML_V + EPS)
        out_t = (hh * scale * gml_ref[0:ML_V, :]
                 * jax.nn.sigmoid(pt_ref[pl.ds(o_row0, ML_V), :]))
        out_t = jnp.concatenate([out_t, jnp.zeros((ML_V_PAD - ML_V, out_t.shape[1]), F32)], axis=0)
        mix_ref[...] = out_t.T.astype(BF16)
        no_ref[...] = a * n_old + wk_scr[...]
        mo_ref[pl.ds(h, 1), :] = m_new


def _decode_mlstm(proj_t, gains, layer, c_t, n_t, m_t, stacked_prev):
    nb = proj_t.shape[1]
    c_blk = pl.BlockSpec((None, None, ML_DBLK, ML_V, nb), lambda h, j: (layer, h, j, 0, 0))
    in_specs = [
        pl.BlockSpec((PROJ_PAD, nb), lambda h, j: (0, 0)),
        pl.BlockSpec((SUBLANES, LANES), lambda h, j: (0, 0)),
        pl.BlockSpec((None, ML_V_PAD, nb), lambda h, j: (h, 0, 0)),
        c_blk,
        pl.BlockSpec((None, None, ML_QK, nb), lambda h, j: (layer, h, 0, 0)),
        pl.BlockSpec((None, ML_HEADS, nb), lambda h, j: (layer, 0, 0)),
    ]
    assert len(in_specs) == DEC_ML_INPUTS
    in_specs += [pl.BlockSpec(memory_space=pl.ANY)] * len(stacked_prev)
    if stacked_prev:
        co_blk = c_blk
    else:
        assert layer == 0
        co_blk = pl.BlockSpec((c_t.shape[0], None, ML_DBLK, ML_V, nb),
                              lambda h, j: (0, h, j, 0, 0))
    out_shape = [
        jax.ShapeDtypeStruct((ML_HEADS, nb, ML_V_PAD), BF16),
        jax.ShapeDtypeStruct(c_t.shape, F32),
        jax.ShapeDtypeStruct((ML_HEADS, ML_QK, nb), F32),
        jax.ShapeDtypeStruct((ML_HEADS, nb), F32),
    ]
    out_specs = [
        pl.BlockSpec((None, nb, ML_V_PAD), lambda h, j: (h, 0, 0)),
        co_blk,
        pl.BlockSpec((None, ML_QK, nb), lambda h, j: (h, 0, 0)),
        pl.BlockSpec((ML_HEADS, nb), lambda h, j: (0, 0)),
    ]
    scratch = [
        pltpu.VMEM((SUBLANES, nb), F32),
        pltpu.VMEM((ML_QK, nb), F32),
        pltpu.VMEM((ML_V, nb), F32),
    ]
    return pl.pallas_call(
        functools.partial(_decode_mlstm_kernel, n_aliased=len(stacked_prev)),
        grid=(ML_HEADS, ML_QK // ML_DBLK),
        in_specs=in_specs,
        out_specs=out_specs,
        out_shape=out_shape,
        scratch_shapes=scratch,
        input_output_aliases={DEC_ML_INPUTS + i: 1 + i for i in range(len(stacked_prev))},
        compiler_params=pltpu.CompilerParams(
            dimension_semantics=("arbitrary", "arbitrary"), vmem_limit_bytes=VMEM_LIMIT),
        name="decode_mlstm",
    )(proj_t, gains["gb_r"], gains["g_ml_t"], c_t, n_t, m_t, *stacked_prev)


def _tables(seq):
    half = RET_DIM // 2
    inv_freq = ROPE_THETA ** (-jnp.arange(half, dtype=F32) / half)

    def cos_sin(pos):
        ang = pos[:, None] * inv_freq[None, :]
        cos, sin = jnp.cos(ang), jnp.sin(ang)
        return jnp.concatenate([cos, cos], axis=-1), jnp.concatenate([-sin, sin], axis=-1)

    cos_p, sin_p = cos_sin(jnp.arange(seq, dtype=F32))
    cos_s, sin_s = cos_sin((PAST_LEN + jnp.arange(1)).astype(F32))
    log_gamma = jnp.log1p(-jnp.exp2(-5.0 - jnp.arange(RET_HEADS, dtype=F32)))
    idx = jnp.arange(CHUNK, dtype=F32)
    diff = idx[:, None] - idx[None, :]
    causal = diff >= 0
    lg = log_gamma[:, None, None]
    dintra = jnp.where(causal, jnp.exp(lg * jnp.where(causal, diff, 0.0)), 0.0)
    dq = jnp.exp(log_gamma[:, None] * (idx + 1.0))
    dk = jnp.exp(log_gamma[:, None] * (CHUNK - 1.0 - idx))

    def pad_rows(t, rows):
        return jnp.pad(t, ((0, rows - t.shape[0]), (0, 0)))

    def chunk_decay(c):
        return pad_rows(jnp.broadcast_to(jnp.exp(log_gamma * c)[:, None], (RET_HEADS, LANES)), SUBLANES)

    return {
        "cos": cos_p, "sin": sin_p,
        "cos_s": jnp.broadcast_to(cos_s, (SUBLANES, RET_DIM)),
        "sin_s": jnp.broadcast_to(sin_s, (SUBLANES, RET_DIM)),
        "dintra": dintra,
        "dq": jnp.pad(dq.T, ((0, 0), (0, LANES - RET_HEADS))),
        "dk": pad_rows(dk, SUBLANES),
        "dch": chunk_decay(float(CHUNK)),
        "dch_s": chunk_decay(1.0),
    }


def _pad_last(t, width):
    return jnp.pad(t, [(0, 0)] * (t.ndim - 1) + [(0, width - t.shape[-1])])


def _prep_gains(l, g_ret_norm, g_mlstm_norm, b_mlstm_i, b_mlstm_f, w_pool, pool_scale):
    gate_bias = jnp.concatenate([b_mlstm_i[l], b_mlstm_f[l]])
    g_ml = _pad_last(g_mlstm_norm[l].reshape(ML_HEADS, ML_V), ML_V_PAD)
    lanes = jnp.concatenate([jnp.zeros((GATE_I_LANE,), F32), _pad_last(gate_bias, LANES - GATE_I_LANE)])
    return {
        "g_ret": g_ret_norm[l][None, :],
        "g_ml": g_ml.reshape(1, -1),
        "g_ml_t": jnp.broadcast_to(g_ml[:, :, None], (ML_HEADS, ML_V_PAD, LANES)),
        "gb_r": jnp.broadcast_to(gate_bias[:, None], (SUBLANES, LANES)),
        "gb_c": jnp.broadcast_to(lanes[None, :], (SUBLANES, LANES)),
        "w_pool": w_pool[l].astype(BF16),
        "pool_scale": pool_scale[l][None, :],
    }


@jax.jit
def kernel(x_prompt, x_sample, state_ret, state_mlstm_c, state_mlstm_n, state_mlstm_m, state_pool,
           w_in, w_out, g_ret_norm, g_mlstm_norm, b_mlstm_i, b_mlstm_f, w_pool, pool_scale,
           g_pre_mix, g_post_mix, g_pre_ffn, g_post_ffn, w_ffn_gate, w_ffn_up, w_ffn_down):
    batch, seq, d = x_prompt.shape
    nb = x_sample.shape[0]
    depth = w_in.shape[0]
    tables = _tables(seq)
    tables_s = dict(tables, dch=tables["dch_s"])
    w_in_t = jnp.swapaxes(w_in, 1, 2)

    yp = x_prompt.reshape(batch * seq, d)
    ys = x_sample.reshape(nb, d)
    new_p, new_s = [], []
    stacked_ret, stacked_c = (), ()
    c_t = jnp.transpose(state_mlstm_c, (0, 2, 3, 4, 1))
    n_t = jnp.transpose(state_mlstm_n, (0, 2, 3, 1))
    m_t = jnp.transpose(state_mlstm_m, (0, 2, 1))
    pool_t = jnp.transpose(state_pool, (0, 2, 1, 3))

    t = _token_tiles(batch * seq)
    t_s = _token_tiles(nb)
    for l in range(depth):
        w_in_l = _prep_w_in(w_in_t, l)
        w_out_l = _prep_w_out(w_out, l)
        gains = _prep_gains(l, g_ret_norm, g_mlstm_norm, b_mlstm_i, b_mlstm_f, w_pool, pool_scale)
        row = lambda g: g[l][None, :]

        mixed, s_ret, s_c, s_n, s_m, s_pool = _prompt_front(
            yp.reshape(batch, seq, d), row(g_pre_mix), w_in_l, tables, gains)
        new_p.append((s_ret, s_c[:, :, :ML_QK, :ML_V], s_n[:, :ML_HEADS, :ML_QK],
                      s_m[:, :ML_HEADS, 0], s_pool[:, 1:, :]))

        proj, proj_t = _inproj(ys, row(g_pre_mix), w_in_l, tm=t_s["inproj_tm"], tn=t_s["inproj_tn"])
        mix_ret, mix_pool, s_ret, pool_new = _decode_ret_pool(
            proj, tables_s, gains, l, state_ret, pool_t, stacked_ret)
        mix_ml, s_c, n_new, m_new = _decode_mlstm(proj_t, gains, l, c_t, n_t, m_t, stacked_c)
        mixed_s = jnp.concatenate(
            [mix_ret, jnp.swapaxes(mix_ml, 0, 1).reshape(nb, ML_HEADS * ML_V_PAD), mix_pool], axis=1)
        stacked_ret, stacked_c = (s_ret,), (s_c,)
        new_s.append((n_new, m_new, pool_new))

        x1, x1_s = _outproj(mixed, mixed_s, w_out_l, yp, ys, row(g_post_mix),
                            tm=t["outproj_tm"])
        yp, ys = _ffn(x1, x1_s, row(g_pre_ffn), w_ffn_gate, w_ffn_up, w_ffn_down, row(g_post_ffn),
                      l, tm=t["ffn_tm"], tf=t["ffn_tf"])

    stack = lambda states, i: jnp.stack([s[i] for s in states], axis=0)
    return ((yp.reshape(batch, seq, d), ys.reshape(nb, 1, d))
            + tuple(stack(new_p, i) for i in range(5))
            + (stacked_ret[0],
               jnp.transpose(stacked_c[0], (0, 4, 1, 2, 3)),
               jnp.transpose(stack(new_s, 0), (0, 3, 1, 2)),
               jnp.transpose(stack(new_s, 1), (0, 2, 1)),
               jnp.transpose(stack(new_s, 2), (0, 2, 1, 3))))
```

```python
import functools

import jax
import jax.numpy as jnp
from jax import lax
from jax.experimental import pallas as pl
from jax.experimental.pallas import tpu as pltpu

F32 = jnp.float32
BF16 = jnp.bfloat16

D_MODEL = 2048
PAST_LEN = 16384
RET_HEADS = 6
RET_DIM = 128
RET_WIDTH = RET_HEADS * RET_DIM
ML_HEADS = 4
ML_QK = 96
ML_V = 192
ML_WIDTH = ML_HEADS * ML_V
POOL_WIDTH = 512
POOL_WINDOWS = (2, 4, 8, 16)
POOL_GROUP = POOL_WIDTH // len(POOL_WINDOWS)
POOL_BUF = max(POOL_WINDOWS) - 1
D_FF = 5632
CHUNK = 128
ROPE_THETA = 10000.0
EPS = 1e-6

LANES = 128
SUBLANES = 8
VMEM_LIMIT = 56 * 1024 * 1024

ML_QK_PAD = LANES
ML_V_PAD = 2 * LANES
OFF_RQ = 0
OFF_RK = OFF_RQ + RET_WIDTH
OFF_RV = OFF_RK + RET_WIDTH
OFF_RG = OFF_RV + RET_WIDTH
OFF_MQ = OFF_RG + RET_WIDTH
OFF_MK = OFF_MQ + ML_HEADS * ML_QK_PAD
OFF_MV = OFF_MK + ML_HEADS * ML_QK_PAD
OFF_MO = OFF_MV + ML_HEADS * ML_V_PAD
OFF_PU = OFF_MO + ML_HEADS * ML_V_PAD
PROJ_PAD = OFF_PU + POOL_WIDTH
GATE_I_LANE = ML_QK
GATE_F_LANE = ML_QK + ML_HEADS
MIX_OFF_ML = RET_WIDTH
MIX_OFF_POOL = MIX_OFF_ML + ML_HEADS * ML_V_PAD
MIX_PAD = MIX_OFF_POOL + POOL_WIDTH

_HI = lax.Precision.HIGHEST


def _token_tiles(m):
    return {
        "inproj_tm": min(m, 1024), "inproj_tn": 512,
        "outproj_tm": min(m, 512),
        "ffn_tm": min(m, 1024), "ffn_tf": 256,
    }


def _dot(a, b):
    return jnp.dot(a, b, preferred_element_type=F32)


def _rms(x, width=None):
    n = x.shape[-1] if width is None else width
    return x * lax.rsqrt(jnp.sum(x * x, axis=-1, keepdims=True) / n + EPS)


def _log_sigmoid(x):
    return jnp.minimum(x, 0.0) - jnp.log1p(jnp.exp(-jnp.abs(x)))


def _rotate(x, cos2, sin2):
    return x * cos2 + pltpu.roll(x, RET_DIM // 2, 1) * sin2


def _inproj_kernel(x_ref, g_ref, w_ref, o_ref, ot_ref, xn_ref):
    @pl.when(pl.program_id(1) == 0)
    def _():
        xn_ref[...] = (_rms(x_ref[...]) * g_ref[...]).astype(BF16)

    res = _dot(xn_ref[...], w_ref[...])
    o_ref[...] = res
    ot_ref[...] = res.T


def _inproj(x, g, w, tm, tn):
    m = x.shape[0]
    return pl.pallas_call(
        _inproj_kernel,
        grid=(m // tm, PROJ_PAD // tn),
        in_specs=[
            pl.BlockSpec((tm, D_MODEL), lambda i, j: (i, 0)),
            pl.BlockSpec((1, D_MODEL), lambda i, j: (0, 0)),
            pl.BlockSpec((D_MODEL, tn), lambda i, j: (0, j)),
        ],
        out_specs=[
            pl.BlockSpec((tm, tn), lambda i, j: (i, j)),
            pl.BlockSpec((tn, tm), lambda i, j: (j, i)),
        ],
        out_shape=[
            jax.ShapeDtypeStruct((m, PROJ_PAD), F32),
            jax.ShapeDtypeStruct((PROJ_PAD, m), F32),
        ],
        scratch_shapes=[pltpu.VMEM((tm, D_MODEL), BF16)],
        compiler_params=pltpu.CompilerParams(
            dimension_semantics=("parallel", "arbitrary"), vmem_limit_bytes=VMEM_LIMIT),
        name="inproj",
    )(x, g, w)


def _outproj_kernel(m_ref, ms_ref, w_ref, x_ref, xs_ref, gpost_ref, x1_ref, x1s_ref):
    @pl.when(pl.program_id(0) == 0)
    def _():
        x1s_ref[...] = xs_ref[...] + _rms(_dot(ms_ref[...], w_ref[...])) * gpost_ref[...]

    x1_ref[...] = x_ref[...] + _rms(_dot(m_ref[...], w_ref[...])) * gpost_ref[...]


def _outproj(mixed, mixed_s, w, x, x_s, g_post, tm):
    m, ns = x.shape[0], x_s.shape[0]
    return pl.pallas_call(
        _outproj_kernel,
        grid=(m // tm,),
        in_specs=[
            pl.BlockSpec((tm, MIX_PAD), lambda i: (i, 0)),
            pl.BlockSpec((ns, MIX_PAD), lambda i: (0, 0)),
            pl.BlockSpec((MIX_PAD, D_MODEL), lambda i: (0, 0), pipeline_mode=pl.Buffered(1)),
            pl.BlockSpec((tm, D_MODEL), lambda i: (i, 0)),
            pl.BlockSpec((ns, D_MODEL), lambda i: (0, 0)),
            pl.BlockSpec((1, D_MODEL), lambda i: (0, 0)),
        ],
        out_specs=[
            pl.BlockSpec((tm, D_MODEL), lambda i: (i, 0)),
            pl.BlockSpec((ns, D_MODEL), lambda i: (0, 0)),
        ],
        out_shape=[
            jax.ShapeDtypeStruct((m, D_MODEL), F32),
            jax.ShapeDtypeStruct((ns, D_MODEL), F32),
        ],
        compiler_params=pltpu.CompilerParams(
            dimension_semantics=("arbitrary",), vmem_limit_bytes=VMEM_LIMIT),
        name="outproj",
    )(mixed, mixed_s, w, x, x_s, g_post)


FFN_ROW_CHUNK = 256


def _ffn_kernel(x1_ref, x1s_ref, gpre_ref, wg_ref, wu_ref, wd_ref, gpost_ref, o_ref, os_ref,
                hn_ref, hns_ref):
    first_tile = pl.program_id(0) == 0
    f = pl.program_id(1)
    last = pl.num_programs(1) - 1

    def swiglu_down(hn, wg, wu, wd):
        gate = _dot(hn, wg)
        act = (gate * jax.nn.sigmoid(gate) * _dot(hn, wu)).astype(BF16)
        return _dot(act, wd)

    def middle_step(h_ref, acc_ref):
        acc_ref[...] += swiglu_down(h_ref[...], wg_ref[...].astype(BF16),
                                    wu_ref[...].astype(BF16), wd_ref[...].astype(BF16))

    def row_chunks(ref):
        rows = ref.shape[0]
        chunk = min(rows, FFN_ROW_CHUNK)
        return [slice(r, r + chunk) for r in range(0, rows, chunk)]

    def first_step(x_ref, h_ref, acc_ref):
        wg, wu, wd = (w[...].astype(BF16) for w in (wg_ref, wu_ref, wd_ref))
        for rows in row_chunks(x_ref):
            hn = (_rms(x_ref[rows, :]) * gpre_ref[...]).astype(BF16)
            h_ref[rows, :] = hn
            acc_ref[rows, :] = swiglu_down(hn, wg, wu, wd)

    def last_step(x_ref, h_ref, acc_ref):
        wg, wu, wd = (w[...].astype(BF16) for w in (wg_ref, wu_ref, wd_ref))
        for rows in row_chunks(x_ref):
            acc = acc_ref[rows, :] + swiglu_down(h_ref[rows, :], wg, wu, wd)
            acc_ref[rows, :] = x_ref[rows, :] + _rms(acc) * gpost_ref[...]

    middle = (f > 0) & (f < last)
    for on_tile, refs in ((first_tile, (x1s_ref, hns_ref, os_ref)), (True, (x1_ref, hn_ref, o_ref))):
        pl.when(on_tile & (f == 0))(functools.partial(first_step, *refs))
        pl.when(on_tile & middle)(functools.partial(middle_step, *refs[1:]))
        pl.when(on_tile & (f == last))(functools.partial(last_step, *refs))


def _ffn(x1, x1_s, g_pre, wg, wu, wd, g_post, layer, tm, tf):
    m, ns = x1.shape[0], x1_s.shape[0]
    return pl.pallas_call(
        _ffn_kernel,
        grid=(m // tm, D_FF // tf),
        in_specs=[
            pl.BlockSpec((tm, D_MODEL), lambda i, f: (i, 0)),
            pl.BlockSpec((ns, D_MODEL), lambda i, f: (0, 0)),
            pl.BlockSpec((1, D_MODEL), lambda i, f: (0, 0)),
            pl.BlockSpec((None, D_MODEL, tf), lambda i, f: (layer, 0, f)),
            pl.BlockSpec((None, D_MODEL, tf), lambda i, f: (layer, 0, f)),
            pl.BlockSpec((None, tf, D_MODEL), lambda i, f: (layer, f, 0)),
            pl.BlockSpec((1, D_MODEL), lambda i, f: (0, 0)),
        ],
        out_specs=[
            pl.BlockSpec((tm, D_MODEL), lambda i, f: (i, 0)),
            pl.BlockSpec((ns, D_MODEL), lambda i, f: (0, 0)),
        ],
        out_shape=[
            jax.ShapeDtypeStruct((m, D_MODEL), F32),
            jax.ShapeDtypeStruct((ns, D_MODEL), F32),
        ],
        scratch_shapes=[pltpu.VMEM((tm, D_MODEL), BF16), pltpu.VMEM((ns, D_MODEL), BF16)],
        compiler_params=pltpu.CompilerParams(
            dimension_semantics=("arbitrary", "arbitrary"), vmem_limit_bytes=VMEM_LIMIT),
        name="ffn",
    )(x1, x1_s, g_pre, wg, wu, wd, g_post)


SRC_MQ = 4 * RET_WIDTH
SRC_MK = SRC_MQ + ML_HEADS * ML_QK
SRC_MV = SRC_MK + ML_HEADS * ML_QK
SRC_MO = SRC_MV + ML_WIDTH
SRC_MI = SRC_MO + ML_WIDTH
SRC_MF = SRC_MI + ML_HEADS
SRC_PU = SRC_MF + ML_HEADS
IN_WIDTH = SRC_PU + POOL_WIDTH


def _prep_w_in_kernel(w_ref, o_ref):
    tk = w_ref.shape[1]

    def put(dst, pieces, width):
        rows = [w_ref[src:src + n, :] for src, n in pieces]
        have = sum(n for _, n in pieces)
        if have < width:
            rows.append(jnp.zeros((width - have, tk), F32))
        block = rows[0] if len(rows) == 1 else jnp.concatenate(rows, axis=0)
        o_ref[:, dst:dst + width] = block.T.astype(BF16)

    for r in range(0, 4 * RET_WIDTH, 2 * LANES):
        put(r, [(r, 2 * LANES)], 2 * LANES)
    for h in range(ML_HEADS):
        q_rows = [(SRC_MQ + h * ML_QK, ML_QK)]
        if h == 0:
            q_rows.append((SRC_MI, 2 * ML_HEADS))
        put(OFF_MQ + h * ML_QK_PAD, q_rows, ML_QK_PAD)
        put(OFF_MK + h * ML_QK_PAD, [(SRC_MK + h * ML_QK, ML_QK)], ML_QK_PAD)
        put(OFF_MV + h * ML_V_PAD, [(SRC_MV + h * ML_V, ML_V)], ML_V_PAD)
        put(OFF_MO + h * ML_V_PAD, [(SRC_MO + h * ML_V, ML_V)], ML_V_PAD)
    for r in range(0, POOL_WIDTH, 2 * LANES):
        put(OFF_PU + r, [(SRC_PU + r, 2 * LANES)], 2 * LANES)


def _prep_w_in(w_t, layer, tk=256):
    d = w_t.shape[2]
    return pl.pallas_call(
        _prep_w_in_kernel,
        grid=(d // tk,),
        in_specs=[pl.BlockSpec((None, IN_WIDTH, tk), lambda i: (layer, 0, i))],
        out_specs=pl.BlockSpec((tk, PROJ_PAD), lambda i: (i, 0)),
        out_shape=jax.ShapeDtypeStruct((d, PROJ_PAD), BF16),
        compiler_params=pltpu.CompilerParams(
            dimension_semantics=("parallel",), vmem_limit_bytes=VMEM_LIMIT),
        name="prep_w_in",
    )(w_t)


def _prep_w_out_kernel(w_ref, o_ref):
    def put(dst, src, rows):
        o_ref[dst:dst + rows, :] = w_ref[src:src + rows, :].astype(BF16)

    put(0, 0, RET_WIDTH)
    for h in range(ML_HEADS):
        dst = MIX_OFF_ML + h * ML_V_PAD
        put(dst, RET_WIDTH + h * ML_V, ML_V)
        o_ref[dst + ML_V:dst + ML_V_PAD, :] = jnp.zeros((ML_V_PAD - ML_V, o_ref.shape[1]), BF16)
    put(MIX_OFF_POOL, RET_WIDTH + ML_WIDTH, POOL_WIDTH)


def _prep_w_out(w, layer, tn=512):
    _, k, n = w.shape
    return pl.pallas_call(
        _prep_w_out_kernel,
        grid=(n // tn,),
        in_specs=[pl.BlockSpec((None, k, tn), lambda j: (layer, 0, j))],
        out_specs=pl.BlockSpec((MIX_PAD, tn), lambda j: (0, j)),
        out_shape=jax.ShapeDtypeStruct((MIX_PAD, n), BF16),
        compiler_params=pltpu.CompilerParams(
            dimension_semantics=("parallel",), vmem_limit_bytes=VMEM_LIMIT),
        name="prep_w_out",
    )(w)


PROMPT_BLOCK = 2


def _ret_head(q, k, get_v, get_g, cos2, sin2, dintra, dq_col, dk_row, dch_row, g_norm, s_old,
              store):
    qb = _rotate(q, cos2, sin2).astype(BF16)
    kr_t = (_rotate(k, cos2, sin2) * (RET_DIM ** -0.5)).T
    kb = kr_t.astype(BF16)
    kdb = (kr_t * dk_row).astype(BF16)
    yield
    scores = (_dot(qb, kb) * dintra).astype(BF16)
    vb = get_v().astype(BF16)
    g = get_g()
    gate = g * jax.nn.sigmoid(g)
    yield
    o = _dot(scores, vb) + _dot(qb, s_old.astype(BF16)) * dq_col
    s_new = s_old * dch_row + _dot(kdb, vb)
    store(_rms(o) * g_norm * gate, s_new)


def _ml_head(q, k, vb, og, causal, b_c, i_c, b_r, i_r, g_norm, c_old, n_old, m_prev, store):
    k_t = k.T
    qb = q.astype(BF16)
    logw = jnp.where(causal, b_c - b_r + i_r, -jnp.inf)
    inter = b_c + m_prev
    m_t = jnp.maximum(inter, jnp.max(logw, axis=-1, keepdims=True))
    w = jnp.exp(logw - m_t)
    a = jnp.exp(inter - m_t)
    m_new = m_t[CHUNK - 1:CHUNK, :]
    b_last = b_c[CHUNK - 1:CHUNK, :]
    w_r = jnp.exp(b_last - b_r + i_r - m_new)
    w_c = jnp.exp(b_last - b_c + i_c - m_new)
    a_end = jnp.exp(b_last + m_prev - m_new)
    kwb = (k_t * w_r).astype(BF16)
    n_new = a_end * n_old + jnp.sum(k * w_c, axis=0, keepdims=True)
    qn = jnp.sum(q * n_old, axis=-1, keepdims=True)
    floor = jnp.exp(-m_t)
    gate = jax.nn.sigmoid(og)
    yield
    s = _dot(qb, k_t.astype(BF16)) * w
    den = jnp.sum(s, axis=-1, keepdims=True) + a * qn
    sb = s.astype(BF16)
    yield
    num = _dot(sb, vb) + a * _dot(qb, c_old.astype(BF16))
    c_new = a_end * c_old + _dot(kwb, vb)
    hh = num / jnp.maximum(jnp.abs(den), floor)
    store(_rms(hh, ML_V) * g_norm * gate, c_new, n_new, m_new)


FRONT_INPUTS = 20


def _prompt_front_kernel(*refs, n_aliased):
    (x_ref, gpre_ref, w_ref, cos_ref, sin_ref, dintra_ref, dq_ref, dk_ref, dch_ref, gret_ref,
     gml_ref, gb_r_ref, gb_c_ref, wpool_ref, pscale_ref,
     pt_ref, gmlt_ref, cs_ref, ns_ref, ms_ref) = refs[:FRONT_INPUTS]
    (mix_ref, sret_ref, sc_ref, sn_ref, sm_ref, spool_ref, mixs_ref, cso_ref, nso_ref, mso_ref,
     s_scr, c_scr, n_scr, m_scr, u_scr, rows_scr, wk_scr, qc_scr) = refs[FRONT_INPUTS + n_aliased:]
    bp = pl.program_id(0)
    c = pl.program_id(1)
    nseq = x_ref.shape[0]

    if n_aliased == 0:
        for later in range(1, cso_ref.shape[0]):
            cso_ref[later] = jnp.zeros(cso_ref.shape[1:], F32)
        cso_ref = cso_ref.at[0]
    blocks_per_head = ML_QK // cs_ref.shape[0]
    step = bp * pl.num_programs(1) + c
    _sample_mlstm_update(step // blocks_per_head, step % blocks_per_head, blocks_per_head - 1,
                         pt_ref, gb_r_ref, gmlt_ref, cs_ref, ns_ref, ms_ref,
                         mixs_ref, cso_ref, nso_ref, mso_ref, rows_scr, wk_scr, qc_scr)

    @pl.when(c == 0)
    def _():
        s_scr[...] = jnp.zeros_like(s_scr)
        c_scr[...] = jnp.zeros_like(c_scr)
        n_scr[...] = jnp.zeros_like(n_scr)
        m_scr[...] = jnp.zeros_like(m_scr)
        u_scr[:, 0:2 * SUBLANES, :] = jnp.zeros((nseq, 2 * SUBLANES, POOL_WIDTH), F32)

    xn = (_rms(x_ref[...].reshape(nseq * CHUNK, D_MODEL)) * gpre_ref[...]).astype(BF16)

    def proj(lo, width):
        return _dot(xn, w_ref[:, lo:lo + width])

    def seq_rows(t, i):
        return t[i * CHUNK:(i + 1) * CHUNK]

    cos2 = cos_ref[...]
    sin2 = sin_ref[...]
    pair = 2 * RET_DIM

    def ret_project(h0):
        return [functools.partial(proj, off + h0 * RET_DIM, pair)
                for off in (OFF_RQ, OFF_RK, OFF_RV, OFF_RG)]

    def ret_mix(h0, projected):
        pq, pk = projected[:2]
        instances = []
        for h in (h0, h0 + 1):
            lo, hi = (h - h0) * RET_DIM, (h - h0 + 1) * RET_DIM
            for i in range(nseq):
                def store(out, s_new, h=h, i=i):
                    s_scr[i, h] = s_new
                    mix_ref[i, :, h * RET_DIM:(h + 1) * RET_DIM] = out.astype(BF16)

                def late(n, i=i, lo=lo, hi=hi):
                    return seq_rows(projected[n], i)[:, lo:hi]

                instances.append(_ret_head(
                    seq_rows(pq, i)[:, lo:hi], seq_rows(pk, i)[:, lo:hi],
                    functools.partial(late, 2), functools.partial(late, 3),
                    cos2, sin2, dintra_ref[h], dq_ref[:, h:h + 1], dk_ref[h:h + 1, :],
                    dch_ref[h:h + 1, :], gret_ref[:, h * RET_DIM:(h + 1) * RET_DIM], s_scr[i, h],
                    store))
        return instances

    row = lax.broadcasted_iota(jnp.int32, (CHUNK, CHUNK), 0)
    col = lax.broadcasted_iota(jnp.int32, (CHUNK, CHUNK), 1)
    causal = col <= row
    upper = (row <= col).astype(F32)
    lower = causal.astype(F32)
    gates = [None] * nseq

    def ml_project(h0):
        return [functools.partial(proj, OFF_MQ + h0 * ML_QK_PAD, 2 * ML_QK_PAD),
                functools.partial(proj, OFF_MK + h0 * ML_QK_PAD, 2 * ML_QK_PAD),
                functools.partial(proj, OFF_MV + h0 * ML_V_PAD, 2 * ML_V_PAD),
                functools.partial(proj, OFF_MO + h0 * ML_V_PAD, 2 * ML_V_PAD)]

    def ml_mix(h0, projected):
        pq, pk, pv, po = projected
        if h0 == 0:
            for i in range(nseq):
                gq0 = seq_rows(pq, i)[:, 0:ML_QK_PAD]
                i_r8 = gq0.T[GATE_I_LANE:GATE_I_LANE + SUBLANES, :] + gb_r_ref[...]
                b_r8 = jnp.dot(_log_sigmoid(i_r8), upper, precision=_HI, preferred_element_type=F32)
                pre_c = gq0 + gb_c_ref[0:1, :]
                b_c_all = jnp.dot(lower, _log_sigmoid(pre_c), precision=_HI,
                                  preferred_element_type=F32)
                gates[i] = (i_r8, b_r8, pre_c, b_c_all)
        instances = []
        for h in (h0, h0 + 1):
            qlo = (h - h0) * ML_QK_PAD
            vlo = (h - h0) * ML_V_PAD
            for i in range(nseq):
                def store(out, c_new, n_new, m_new, h=h, i=i):
                    c_scr[i, h] = c_new
                    n_scr[i, h:h + 1, :] = n_new
                    m_scr[i, h:h + 1, :] = jnp.broadcast_to(m_new, (1, LANES))
                    mix_ref[i, :, MIX_OFF_ML + h * ML_V_PAD:MIX_OFF_ML + (h + 1) * ML_V_PAD] = (
                        out.astype(BF16))

                i_r8, b_r8, pre_c, b_c_all = gates[i]
                instances.append(_ml_head(
                    seq_rows(pq, i)[:, qlo:qlo + ML_QK_PAD],
                    seq_rows(pk, i)[:, qlo:qlo + ML_QK_PAD] * (ML_QK ** -0.5),
                    seq_rows(pv, i)[:, vlo:vlo + ML_V_PAD].astype(BF16),
                    seq_rows(po, i)[:, vlo:vlo + ML_V_PAD],
                    causal,
                    b_c_all[:, GATE_F_LANE + h:GATE_F_LANE + h + 1],
                    pre_c[:, GATE_I_LANE + h:GATE_I_LANE + h + 1],
                    b_r8[ML_HEADS + h:ML_HEADS + h + 1, :], i_r8[h:h + 1, :],
                    gml_ref[:, h * ML_V_PAD:(h + 1) * ML_V_PAD],
                    c_scr[i, h], n_scr[i, h:h + 1, :], m_scr[i, h:h + 1, 0:1], store))
        return instances

    base = 2 * SUBLANES
    pos = (c * CHUNK + lax.broadcasted_iota(jnp.int32, (CHUNK, 1), 0)).astype(F32)

    def pool_seq(u, i):
        u_scr[i, base:base + CHUNK, :] = u
        zs = []
        for g, win in enumerate(POOL_WINDOWS):
            lo, hi = g * POOL_GROUP, (g + 1) * POOL_GROUP
            wsum = u[:, lo:hi]
            for s in range(1, win):
                wsum = wsum + u_scr[i, base - s:base - s + CHUNK, lo:hi]
            zs.append((wsum / jnp.minimum(pos + 1.0, float(win)) - u[:, lo:hi]).astype(BF16))
        u_scr[i, 0:base, :] = u_scr[i, CHUNK:CHUNK + base, :]
        yield
        for g, z in enumerate(zs):
            lo, hi = g * POOL_GROUP, (g + 1) * POOL_GROUP
            y = _dot(z, wpool_ref[g]) * pscale_ref[:, lo:hi]
            mix_ref[i, :, MIX_OFF_POOL + lo:MIX_OFF_POOL + hi] = y.astype(BF16)

    def pool_mix(projected):
        pu, = projected
        return [pool_seq(seq_rows(pu, i), i) for i in range(nseq)]

    stages = [(ret_project(h0), functools.partial(ret_mix, h0)) for h0 in range(0, RET_HEADS, 2)]
    stages += [(ml_project(h0), functools.partial(ml_mix, h0)) for h0 in range(0, ML_HEADS, 2)]
    stages.append(([functools.partial(proj, OFF_PU, POOL_WIDTH)], pool_mix))
    early = 2
    projected = [dot() for dot in stages[0][0][:early]]
    late_dots = list(stages[0][0][early:])
    for n, (_, mix) in enumerate(stages):
        pending = list(stages[n + 1][0]) if n + 1 < len(stages) else []
        upcoming = []
        live = mix(projected)
        while live:
            for _ in range(max(1, (len(late_dots) + len(pending)) // 2)):
                if late_dots:
                    projected.append(late_dots.pop(0)())
                elif pending:
                    upcoming.append(pending.pop(0)())
            live = [inst for inst in live if next(inst, StopIteration) is not StopIteration]
        upcoming += [dot() for dot in pending]
        projected = upcoming

    @pl.when(c == pl.num_programs(1) - 1)
    def _():
        for i in range(nseq):
            b = bp * nseq + i
            sret_ref[b] = s_scr[i]
            sc_ref[b] = c_scr[i]
            sn_ref[b] = n_scr[i]
            sm_ref[b] = m_scr[i]
            spool_ref[b] = u_scr[i, 0:base, :]


def _prompt_front(x, g_pre, w, tables, gains, layer, proj_t, c_t, n_t, m_t, stacked_prev):
    batch, seq, _ = x.shape
    nb = proj_t.shape[1]
    nchunk = seq // CHUNK
    nseq = PROMPT_BLOCK
    steps = (batch // nseq) * nchunk
    dblk = ML_HEADS * ML_QK // steps
    assert dblk * steps == ML_HEADS * ML_QK and ML_QK % dblk == 0
    per_head = ML_QK // dblk
    head = lambda b, c: (b * nchunk + c) // per_head
    c_blk = pl.BlockSpec((None, None, dblk, ML_V, nb),
                         lambda b, c: (layer, head(b, c), (b * nchunk + c) % per_head, 0, 0))
    if stacked_prev:
        co_blk = c_blk
    else:
        assert layer == 0
        co_blk = pl.BlockSpec((c_t.shape[0], None, dblk, ML_V, nb),
                              lambda b, c: (0, head(b, c), (b * nchunk + c) % per_head, 0, 0))
    ml_rows = OFF_PU - OFF_MQ
    assert OFF_MQ % ml_rows == 0
    const2 = lambda b, c: (0, 0)
    const3 = lambda b, c: (0, 0, 0)
    const4 = lambda b, c: (0, 0, 0, 0)
    sample_in_specs = [
        pl.BlockSpec((ml_rows, nb), lambda b, c: (OFF_MQ // ml_rows, 0),
                     pipeline_mode=pl.Buffered(1)),
        pl.BlockSpec((None, ML_V_PAD, nb), lambda b, c: (head(b, c), 0, 0)),
        c_blk,
        pl.BlockSpec((None, None, ML_QK, nb), lambda b, c: (layer, head(b, c), 0, 0)),
        pl.BlockSpec((None, ML_HEADS, nb), lambda b, c: (layer, 0, 0)),
    ]
    sample_out_shape = [
        jax.ShapeDtypeStruct((ML_HEADS, nb, ML_V_PAD), BF16),
        jax.ShapeDtypeStruct(c_t.shape, F32),
        jax.ShapeDtypeStruct((ML_HEADS, ML_QK, nb), F32),
        jax.ShapeDtypeStruct((ML_HEADS, nb), F32),
    ]
    sample_out_specs = [
        pl.BlockSpec((None, nb, ML_V_PAD), lambda b, c: (head(b, c), 0, 0)),
        co_blk,
        pl.BlockSpec((None, ML_QK, nb), lambda b, c: (head(b, c), 0, 0)),
        pl.BlockSpec((ML_HEADS, nb), const2),
    ]
    sample_scratch = [
        pltpu.VMEM((SUBLANES, nb), F32),
        pltpu.VMEM((ML_QK, nb), F32),
        pltpu.VMEM((ML_V, nb), F32),
    ]
    in_specs = [
        pl.BlockSpec((nseq, CHUNK, D_MODEL), lambda b, c: (b, c, 0)),
        pl.BlockSpec((1, D_MODEL), const2),
        pl.BlockSpec((D_MODEL, PROJ_PAD), const2, pipeline_mode=pl.Buffered(1)),
        pl.BlockSpec((CHUNK, RET_DIM), lambda b, c: (c, 0)),
        pl.BlockSpec((CHUNK, RET_DIM), lambda b, c: (c, 0)),
        pl.BlockSpec((RET_HEADS, CHUNK, CHUNK), const3),
        pl.BlockSpec((CHUNK, LANES), const2),
        pl.BlockSpec((SUBLANES, CHUNK), const2),
        pl.BlockSpec((SUBLANES, LANES), const2),
        pl.BlockSpec((1, RET_WIDTH), const2),
        pl.BlockSpec((1, ML_HEADS * ML_V_PAD), const2),
        pl.BlockSpec((SUBLANES, LANES), const2),
        pl.BlockSpec((SUBLANES, LANES), const2),
        pl.BlockSpec((len(POOL_WINDOWS), POOL_GROUP, POOL_GROUP), const3),
        pl.BlockSpec((1, POOL_WIDTH), const2),
    ] + sample_in_specs
    assert len(in_specs) == FRONT_INPUTS
    in_specs += [pl.BlockSpec(memory_space=pl.ANY)] * len(stacked_prev)
    out_shape = [
        jax.ShapeDtypeStruct((batch, seq, MIX_PAD), BF16),
        jax.ShapeDtypeStruct((batch, RET_HEADS, RET_DIM, RET_DIM), F32),
        jax.ShapeDtypeStruct((batch, ML_HEADS, ML_QK_PAD, ML_V_PAD), F32),
        jax.ShapeDtypeStruct((batch, SUBLANES, LANES), F32),
        jax.ShapeDtypeStruct((batch, SUBLANES, LANES), F32),
        jax.ShapeDtypeStruct((batch, 2 * SUBLANES, POOL_WIDTH), F32),
    ] + sample_out_shape
    out_specs = [
        pl.BlockSpec((nseq, CHUNK, MIX_PAD), lambda b, c: (b, c, 0)),
        pl.BlockSpec((batch, RET_HEADS, RET_DIM, RET_DIM), const4),
        pl.BlockSpec((batch, ML_HEADS, ML_QK_PAD, ML_V_PAD), const4),
        pl.BlockSpec((batch, SUBLANES, LANES), const3),
        pl.BlockSpec((batch, SUBLANES, LANES), const3),
        pl.BlockSpec((batch, 2 * SUBLANES, POOL_WIDTH), const3),
    ] + sample_out_specs
    scratch = [
        pltpu.VMEM((nseq, RET_HEADS, RET_DIM, RET_DIM), F32),
        pltpu.VMEM((nseq, ML_HEADS, ML_QK_PAD, ML_V_PAD), F32),
        pltpu.VMEM((nseq, SUBLANES, LANES), F32),
        pltpu.VMEM((nseq, SUBLANES, LANES), F32),
        pltpu.VMEM((nseq, 2 * SUBLANES + CHUNK, POOL_WIDTH), F32),
    ] + sample_scratch
    n_prompt_out = len(out_shape) - len(sample_out_shape)
    outs = pl.pallas_call(
        functools.partial(_prompt_front_kernel, n_aliased=len(stacked_prev)),
        grid=(batch // nseq, nchunk),
        in_specs=in_specs,
        out_specs=out_specs,
        out_shape=out_shape,
        scratch_shapes=scratch,
        input_output_aliases={FRONT_INPUTS + i: n_prompt_out + 1 + i
                              for i in range(len(stacked_prev))},
        compiler_params=pltpu.CompilerParams(
            dimension_semantics=("arbitrary", "arbitrary"), vmem_limit_bytes=VMEM_LIMIT),
        name="prompt_front",
    )(x, g_pre, w, tables["cos"], tables["sin"], tables["dintra"], tables["dq"], tables["dk"],
      tables["dch"], gains["g_ret"], gains["g_ml"], gains["gb_r"], gains["gb_c"],
      gains["w_pool"], gains["pool_scale"], proj_t, gains["g_ml_t"], c_t, n_t, m_t,
      *stacked_prev)
    mixed, *states = outs[:n_prompt_out]
    return (mixed.reshape(batch * seq, MIX_PAD), *states), tuple(outs[n_prompt_out:])


DEC_BLOCK = 8
DEC_RET_INPUTS = 9


def _decode_ret_pool_kernel(*refs, n_aliased):
    (p_ref, cos_ref, sin_ref, dch_ref, gret_ref, wpool_ref, pscale_ref, s_ref,
     pool_ref) = refs[:DEC_RET_INPUTS]
    (mix_ref, mixp_ref, so_ref, poolo_ref, qr_scr, kt_scr, v_scr,
     qs_scr) = refs[DEC_RET_INPUTS + n_aliased:]
    if n_aliased == 0:
        for later in range(1, so_ref.shape[0]):
            so_ref[later] = jnp.zeros(so_ref.shape[1:], F32)
        so_ref = so_ref.at[0]
    step = pl.program_id(0)
    nb = p_ref.shape[0]
    cos2 = cos_ref[0:1, :]
    sin2 = sin_ref[0:1, :]

    def ret_qkv(h):
        lo, hi = h * RET_DIM, (h + 1) * RET_DIM
        qr = _rotate(p_ref[:, OFF_RQ + lo:OFF_RQ + hi], cos2, sin2)
        kr = _rotate(p_ref[:, OFF_RK + lo:OFF_RK + hi], cos2, sin2) * (RET_DIM ** -0.5)
        return qr, kr, p_ref[:, OFF_RV + lo:OFF_RV + hi]

    @pl.when(step == 0)
    def _():
        for h in range(RET_HEADS):
            qr, kr, v = ret_qkv(h)
            qr_scr[h] = qr
            kt_scr[h] = kr.T
            v_scr[h] = v.astype(BF16)

    lane = lax.broadcasted_iota(jnp.int32, (LANES, nb), 1)

    def body(bb, carry):
        b = step * DEC_BLOCK + bb
        sel = lane == b
        for h in range(RET_HEADS):
            s_old = s_ref[bb, h]
            q16 = jnp.broadcast_to(qr_scr[h, pl.ds(b, 1), :], (2 * SUBLANES, RET_DIM)).astype(BF16)
            qs_scr[h, pl.ds(b, 1), :] = _dot(q16, s_old.astype(BF16))[0:1, :]
            k_m = jnp.where(sel, kt_scr[h], 0.0).astype(BF16)
            so_ref[bb, h] = s_old * dch_ref[h:h + 1, :] + _dot(k_m, v_scr[h])
        return carry

    lax.fori_loop(0, DEC_BLOCK, body, 0)

    @pl.when(step == pl.num_programs(0) - 1)
    def _():
        for h in range(RET_HEADS):
            lo, hi = h * RET_DIM, (h + 1) * RET_DIM
            qr, kr, v = ret_qkv(h)
            g = p_ref[:, OFF_RG + lo:OFF_RG + hi]
            o = jnp.sum(qr * kr, axis=-1, keepdims=True) * v + qs_scr[h] * dch_ref[h:h + 1, :]
            out = _rms(o) * gret_ref[:, lo:hi] * (g * jax.nn.sigmoid(g))
            mix_ref[:, lo:hi] = out.astype(BF16)
        u = p_ref[:, OFF_PU:OFF_PU + POOL_WIDTH]
        for g, win in enumerate(POOL_WINDOWS):
            lo, hi = g * POOL_GROUP, (g + 1) * POOL_GROUP
            wsum = u[:, lo:hi]
            for s in range(1, win):
                wsum = wsum + pool_ref[POOL_BUF - s, :, lo:hi]
            z = wsum / float(win) - u[:, lo:hi]
            y = _dot(z.astype(BF16), wpool_ref[g]) * pscale_ref[:, lo:hi]
            mixp_ref[:, lo:hi] = y.astype(BF16)
        for r in range(POOL_BUF - 1):
            poolo_ref[r] = pool_ref[r + 1]
        poolo_ref[POOL_BUF - 1] = u


def _decode_ret_pool(proj, tables, gains, layer, s_ret, pool_t, stacked_prev):
    nb = proj.shape[0]
    const2 = lambda s: (0, 0)
    const3 = lambda s: (0, 0, 0)
    full2 = lambda shape: pl.BlockSpec(shape, const2)
    s_blk = pl.BlockSpec((None, DEC_BLOCK, RET_HEADS, RET_DIM, RET_DIM),
                         lambda s: (layer, s, 0, 0, 0))
    pool_blk = (POOL_BUF, nb, POOL_WIDTH)
    in_specs = [
        full2((nb, PROJ_PAD)),
        full2((SUBLANES, RET_DIM)),
        full2((SUBLANES, RET_DIM)),
        full2((SUBLANES, LANES)),
        full2((1, RET_WIDTH)),
        pl.BlockSpec((len(POOL_WINDOWS), POOL_GROUP, POOL_GROUP), const3),
        full2((1, POOL_WIDTH)),
        s_blk,
        pl.BlockSpec((None,) + pool_blk, lambda s: (layer, 0, 0, 0)),
    ]
    assert len(in_specs) == DEC_RET_INPUTS
    in_specs += [pl.BlockSpec(memory_space=pl.ANY)] * len(stacked_prev)
    if stacked_prev:
        so_blk = s_blk
    else:
        assert layer == 0
        so_blk = pl.BlockSpec((s_ret.shape[0],) + s_blk.block_shape[1:],
                              lambda s: (0, s, 0, 0, 0))
    out_shape = [
        jax.ShapeDtypeStruct((nb, RET_WIDTH), BF16),
        jax.ShapeDtypeStruct((nb, POOL_WIDTH), BF16),
        jax.ShapeDtypeStruct(s_ret.shape, F32),
        jax.ShapeDtypeStruct(pool_blk, F32),
    ]
    out_specs = [
        full2((nb, RET_WIDTH)),
        full2((nb, POOL_WIDTH)),
        so_blk,
        pl.BlockSpec(pool_blk, const3),
    ]
    scratch = [
        pltpu.VMEM((RET_HEADS, nb, RET_DIM), F32),
        pltpu.VMEM((RET_HEADS, RET_DIM, nb), F32),
        pltpu.VMEM((RET_HEADS, nb, RET_DIM), BF16),
        pltpu.VMEM((RET_HEADS, nb, RET_DIM), F32),
    ]
    return pl.pallas_call(
        functools.partial(_decode_ret_pool_kernel, n_aliased=len(stacked_prev)),
        grid=(nb // DEC_BLOCK,),
        in_specs=in_specs,
        out_specs=out_specs,
        out_shape=out_shape,
        scratch_shapes=scratch,
        input_output_aliases={DEC_RET_INPUTS + i: 2 + i for i in range(len(stacked_prev))},
        compiler_params=pltpu.CompilerParams(
            dimension_semantics=("arbitrary",), vmem_limit_bytes=VMEM_LIMIT),
        name="decode_ret_pool",
    )(proj, tables["cos_s"], tables["sin_s"], tables["dch"], gains["g_ret"], gains["w_pool"],
      gains["pool_scale"], s_ret, pool_t, *stacked_prev)


def _sample_mlstm_update(h, j, last_j, pt_ref, gb_ref, gml_ref, c_ref, n_ref, m_ref,
                         mix_ref, co_ref, no_ref, mo_ref, rows_scr, wk_scr, qc_scr):
    dblk = c_ref.shape[0]
    q_row0 = pl.multiple_of(h * ML_QK_PAD, ML_QK_PAD)
    k_row0 = pl.multiple_of(OFF_MK - OFF_MQ + h * ML_QK_PAD, ML_QK_PAD)
    v_row0 = pl.multiple_of(OFF_MV - OFF_MQ + h * ML_V_PAD, ML_V_PAD)
    o_row0 = pl.multiple_of(OFF_MO - OFF_MQ + h * ML_V_PAD, ML_V_PAD)

    @pl.when(j == 0)
    def _():
        i_pre = pt_ref[pl.ds(GATE_I_LANE + h, 1), :] + gb_ref[pl.ds(h, 1), :]
        lf = _log_sigmoid(pt_ref[pl.ds(GATE_F_LANE + h, 1), :] + gb_ref[pl.ds(ML_HEADS + h, 1), :])
        inter = lf + m_ref[pl.ds(h, 1), :]
        m_new = jnp.maximum(inter, i_pre)
        w = jnp.exp(i_pre - m_new)
        rows_scr[0:1, :] = jnp.exp(inter - m_new)
        rows_scr[1:2, :] = w
        rows_scr[2:3, :] = m_new
        wk_scr[...] = w * (pt_ref[pl.ds(k_row0, ML_QK), :] * (ML_QK ** -0.5))
        qc_scr[...] = jnp.zeros_like(qc_scr)

    a = rows_scr[0:1, :]
    v_t = pt_ref[pl.ds(v_row0, ML_V), :]
    qc = qc_scr[...]
    for dd in range(dblk):
        d = j * dblk + dd
        c_old = c_ref[dd]
        co_ref[dd] = a * c_old + wk_scr[pl.ds(d, 1), :] * v_t
        qc = qc + pt_ref[pl.ds(q_row0 + d, 1), :] * c_old
    qc_scr[...] = qc

    @pl.when(j == last_j)
    def _():
        w = rows_scr[1:2, :]
        m_new = rows_scr[2:3, :]
        q_t = pt_ref[pl.ds(q_row0, ML_QK), :]
        k_t = pt_ref[pl.ds(k_row0, ML_QK), :] * (ML_QK ** -0.5)
        n_old = n_ref[...]
        s = jnp.sum(q_t * k_t, axis=0, keepdims=True) * w
        num = s * v_t + a * qc_scr[...]
        den = s + a * jnp.sum(q_t * n_old, axis=0, keepdims=True)
        hh = num / jnp.maximum(jnp.abs(den), jnp.exp(-m_new))
        scale = lax.rsqrt(jnp.sum(hh * hh, axis=0, keepdims=True) / ML_V + EPS)
        out_t = (hh * scale * gml_ref[0:ML_V, :]
                 * jax.nn.sigmoid(pt_ref[pl.ds(o_row0, ML_V), :]))
        out_t = jnp.concatenate([out_t, jnp.zeros((ML_V_PAD - ML_V, out_t.shape[1]), F32)], axis=0)
        mix_ref[...] = out_t.T.astype(BF16)
        no_ref[...] = a * n_old + wk_scr[...]
        mo_ref[pl.ds(h, 1), :] = m_new


def _tables(seq):
    half = RET_DIM // 2
    inv_freq = ROPE_THETA ** (-jnp.arange(half, dtype=F32) / half)

    def cos_sin(pos):
        ang = pos[:, None] * inv_freq[None, :]
        cos, sin = jnp.cos(ang), jnp.sin(ang)
        return jnp.concatenate([cos, cos], axis=-1), jnp.concatenate([-sin, sin], axis=-1)

    cos_p, sin_p = cos_sin(jnp.arange(seq, dtype=F32))
    cos_s, sin_s = cos_sin((PAST_LEN + jnp.arange(1)).astype(F32))
    log_gamma = jnp.log1p(-jnp.exp2(-5.0 - jnp.arange(RET_HEADS, dtype=F32)))
    idx = jnp.arange(CHUNK, dtype=F32)
    diff = idx[:, None] - idx[None, :]
    causal = diff >= 0
    lg = log_gamma[:, None, None]
    dintra = jnp.where(causal, jnp.exp(lg * jnp.where(causal, diff, 0.0)), 0.0)
    dq = jnp.exp(log_gamma[:, None] * (idx + 1.0))
    dk = jnp.exp(log_gamma[:, None] * (CHUNK - 1.0 - idx))

    def pad_rows(t, rows):
        return jnp.pad(t, ((0, rows - t.shape[0]), (0, 0)))

    def chunk_decay(c):
        return pad_rows(jnp.broadcast_to(jnp.exp(log_gamma * c)[:, None], (RET_HEADS, LANES)), SUBLANES)

    return {
        "cos": cos_p, "sin": sin_p,
        "cos_s": jnp.broadcast_to(cos_s, (SUBLANES, RET_DIM)),
        "sin_s": jnp.broadcast_to(sin_s, (SUBLANES, RET_DIM)),
        "dintra": dintra,
        "dq": jnp.pad(dq.T, ((0, 0), (0, LANES - RET_HEADS))),
        "dk": pad_rows(dk, SUBLANES),
        "dch": chunk_decay(float(CHUNK)),
        "dch_s": chunk_decay(1.0),
    }


def _pad_last(t, width):
    return jnp.pad(t, [(0, 0)] * (t.ndim - 1) + [(0, width - t.shape[-1])])


def _prep_gains(l, g_ret_norm, g_mlstm_norm, b_mlstm_i, b_mlstm_f, w_pool, pool_scale):
    gate_bias = jnp.concatenate([b_mlstm_i[l], b_mlstm_f[l]])
    g_ml = _pad_last(g_mlstm_norm[l].reshape(ML_HEADS, ML_V), ML_V_PAD)
    lanes = jnp.concatenate([jnp.zeros((GATE_I_LANE,), F32), _pad_last(gate_bias, LANES - GATE_I_LANE)])
    return {
        "g_ret": g_ret_norm[l][None, :],
        "g_ml": g_ml.reshape(1, -1),
        "g_ml_t": jnp.broadcast_to(g_ml[:, :, None], (ML_HEADS, ML_V_PAD, LANES)),
        "gb_r": jnp.broadcast_to(gate_bias[:, None], (SUBLANES, LANES)),
        "gb_c": jnp.broadcast_to(lanes[None, :], (SUBLANES, LANES)),
        "w_pool": w_pool[l].astype(BF16),
        "pool_scale": pool_scale[l][None, :],
    }


@jax.jit
def kernel(x_prompt, x_sample, state_ret, state_mlstm_c, state_mlstm_n, state_mlstm_m, state_pool,
           w_in, w_out, g_ret_norm, g_mlstm_norm, b_mlstm_i, b_mlstm_f, w_pool, pool_scale,
           g_pre_mix, g_post_mix, g_pre_ffn, g_post_ffn, w_ffn_gate, w_ffn_up, w_ffn_down):
    batch, seq, d = x_prompt.shape
    nb = x_sample.shape[0]
    depth = w_in.shape[0]
    tables = _tables(seq)
    tables_s = dict(tables, dch=tables["dch_s"])
    w_in_t = jnp.swapaxes(w_in, 1, 2)

    yp = x_prompt.reshape(batch * seq, d)
    ys = x_sample.reshape(nb, d)
    new_p, new_s = [], []
    stacked_ret, stacked_c = (), ()
    c_t = jnp.transpose(state_mlstm_c, (0, 2, 3, 4, 1))
    n_t = jnp.transpose(state_mlstm_n, (0, 2, 3, 1))
    m_t = jnp.transpose(state_mlstm_m, (0, 2, 1))
    pool_t = jnp.transpose(state_pool, (0, 2, 1, 3))

    t = _token_tiles(batch * seq)
    t_s = _token_tiles(nb)
    for l in range(depth):
        w_in_l = _prep_w_in(w_in_t, l)
        w_out_l = _prep_w_out(w_out, l)
        gains = _prep_gains(l, g_ret_norm, g_mlstm_norm, b_mlstm_i, b_mlstm_f, w_pool, pool_scale)
        row = lambda g: g[l][None, :]

        proj, proj_t = _inproj(ys, row(g_pre_mix), w_in_l, tm=t_s["inproj_tm"], tn=t_s["inproj_tn"])
        (mixed, s_ret, s_c, s_n, s_m, s_pool), (mix_ml, s_c_s, n_new, m_new) = _prompt_front(
            yp.reshape(batch, seq, d), row(g_pre_mix), w_in_l, tables, gains,
            l, proj_t, c_t, n_t, m_t, stacked_c)
        new_p.append((s_ret, s_c[:, :, :ML_QK, :ML_V], s_n[:, :ML_HEADS, :ML_QK],
                      s_m[:, :ML_HEADS, 0], s_pool[:, 1:, :]))

        mix_ret, mix_pool, s_ret, pool_new = _decode_ret_pool(
            proj, tables_s, gains, l, state_ret, pool_t, stacked_ret)
        s_c = s_c_s
        mixed_s = jnp.concatenate(
            [mix_ret, jnp.swapaxes(mix_ml, 0, 1).reshape(nb, ML_HEADS * ML_V_PAD), mix_pool], axis=1)
        stacked_ret, stacked_c = (s_ret,), (s_c,)
        new_s.append((n_new, m_new, pool_new))

        x1, x1_s = _outproj(mixed, mixed_s, w_out_l, yp, ys, row(g_post_mix),
                            tm=t["outproj_tm"])
        yp, ys = _ffn(x1, x1_s, row(g_pre_ffn), w_ffn_gate, w_ffn_up, w_ffn_down, row(g_post_ffn),
                      l, tm=t["ffn_tm"], tf=t["ffn_tf"])

    stack = lambda states, i: jnp.stack([s[i] for s in states], axis=0)
    return ((yp.reshape(batch, seq, d), ys.reshape(nb, 1, d))
            + tuple(stack(new_p, i) for i in range(5))
            + (stacked_ret[0],
               jnp.transpose(stacked_c[0], (0, 4, 1, 2, 3)),
               jnp.transpose(stack(new_s, 0), (0, 3, 1, 2)),
               jnp.transpose(stack(new_s, 1), (0, 2, 1)),
               jnp.transpose(stack(new_s, 2), (0, 2, 1, 3))))
```

```python
import functools

import jax
import jax.numpy as jnp
from jax import lax
from jax.experimental import pallas as pl
from jax.experimental.pallas import tpu as pltpu

F32 = jnp.float32
BF16 = jnp.bfloat16

D_MODEL = 2048
PAST_LEN = 16384
RET_HEADS = 6
RET_DIM = 128
RET_WIDTH = RET_HEADS * RET_DIM
ML_HEADS = 4
ML_QK = 96
ML_V = 192
ML_WIDTH = ML_HEADS * ML_V
POOL_WIDTH = 512
POOL_WINDOWS = (2, 4, 8, 16)
POOL_GROUP = POOL_WIDTH // len(POOL_WINDOWS)
POOL_BUF = max(POOL_WINDOWS) - 1
D_FF = 5632
CHUNK = 128
ROPE_THETA = 10000.0
EPS = 1e-6

LANES = 128
SUBLANES = 8
VMEM_LIMIT = 56 * 1024 * 1024

ML_QK_PAD = LANES
ML_V_PAD = 2 * LANES
OFF_RQ = 0
OFF_RK = OFF_RQ + RET_WIDTH
OFF_RV = OFF_RK + RET_WIDTH
OFF_RG = OFF_RV + RET_WIDTH
OFF_MQ = OFF_RG + RET_WIDTH
OFF_MK = OFF_MQ + ML_HEADS * ML_QK_PAD
OFF_MV = OFF_MK + ML_HEADS * ML_QK_PAD
OFF_MO = OFF_MV + ML_HEADS * ML_V_PAD
OFF_PU = OFF_MO + ML_HEADS * ML_V_PAD
PROJ_PAD = OFF_PU + POOL_WIDTH
GATE_I_LANE = ML_QK
GATE_F_LANE = ML_QK + ML_HEADS
MIX_OFF_ML = RET_WIDTH
MIX_OFF_POOL = MIX_OFF_ML + ML_HEADS * ML_V_PAD
MIX_PAD = MIX_OFF_POOL + POOL_WIDTH

_HI = lax.Precision.HIGHEST


def _token_tiles(m):
    return {
        "outproj_tm": min(m, 512),
        "ffn_tm": min(m, 1024), "ffn_tf": 256,
    }


def _dot(a, b):
    return jnp.dot(a, b, preferred_element_type=F32)


def _rms(x, width=None):
    n = x.shape[-1] if width is None else width
    return x * lax.rsqrt(jnp.sum(x * x, axis=-1, keepdims=True) / n + EPS)


def _log_sigmoid(x):
    return jnp.minimum(x, 0.0) - jnp.log1p(jnp.exp(-jnp.abs(x)))


def _rotate(x, cos2, sin2):
    return x * cos2 + pltpu.roll(x, RET_DIM // 2, 1) * sin2


def _outproj_kernel(m_ref, ms_ref, w_ref, x_ref, xs_ref, gpost_ref, x1_ref, x1s_ref):
    @pl.when(pl.program_id(0) == 0)
    def _():
        x1s_ref[...] = xs_ref[...] + _rms(_dot(ms_ref[...], w_ref[...])) * gpost_ref[...]

    x1_ref[...] = x_ref[...] + _rms(_dot(m_ref[...], w_ref[...])) * gpost_ref[...]


def _outproj(mixed, mixed_s, w, x, x_s, g_post, tm):
    m, ns = x.shape[0], x_s.shape[0]
    return pl.pallas_call(
        _outproj_kernel,
        grid=(m // tm,),
        in_specs=[
            pl.BlockSpec((tm, MIX_PAD), lambda i: (i, 0)),
            pl.BlockSpec((ns, MIX_PAD), lambda i: (0, 0)),
            pl.BlockSpec((MIX_PAD, D_MODEL), lambda i: (0, 0), pipeline_mode=pl.Buffered(1)),
            pl.BlockSpec((tm, D_MODEL), lambda i: (i, 0)),
            pl.BlockSpec((ns, D_MODEL), lambda i: (0, 0)),
            pl.BlockSpec((1, D_MODEL), lambda i: (0, 0)),
        ],
        out_specs=[
            pl.BlockSpec((tm, D_MODEL), lambda i: (i, 0)),
            pl.BlockSpec((ns, D_MODEL), lambda i: (0, 0)),
        ],
        out_shape=[
            jax.ShapeDtypeStruct((m, D_MODEL), F32),
            jax.ShapeDtypeStruct((ns, D_MODEL), F32),
        ],
        compiler_params=pltpu.CompilerParams(
            dimension_semantics=("arbitrary",), vmem_limit_bytes=VMEM_LIMIT),
        name="outproj",
    )(mixed, mixed_s, w, x, x_s, g_post)


FFN_ROW_CHUNK = 256


def _ffn_kernel(x1_ref, x1s_ref, gpre_ref, wg_ref, wu_ref, wd_ref, gpost_ref, o_ref, os_ref,
                hn_ref, hns_ref):
    first_tile = pl.program_id(0) == 0
    f = pl.program_id(1)
    last = pl.num_programs(1) - 1

    def swiglu_down(hn, wg, wu, wd):
        gate = _dot(hn, wg)
        act = (gate * jax.nn.sigmoid(gate) * _dot(hn, wu)).astype(BF16)
        return _dot(act, wd)

    def middle_step(h_ref, acc_ref):
        acc_ref[...] += swiglu_down(h_ref[...], wg_ref[...].astype(BF16),
                                    wu_ref[...].astype(BF16), wd_ref[...].astype(BF16))

    def row_chunks(ref):
        rows = ref.shape[0]
        chunk = min(rows, FFN_ROW_CHUNK)
        return [slice(r, r + chunk) for r in range(0, rows, chunk)]

    def first_step(x_ref, h_ref, acc_ref):
        wg, wu, wd = (w[...].astype(BF16) for w in (wg_ref, wu_ref, wd_ref))
        for rows in row_chunks(x_ref):
            hn = (_rms(x_ref[rows, :]) * gpre_ref[...]).astype(BF16)
            h_ref[rows, :] = hn
            acc_ref[rows, :] = swiglu_down(hn, wg, wu, wd)

    def last_step(x_ref, h_ref, acc_ref):
        wg, wu, wd = (w[...].astype(BF16) for w in (wg_ref, wu_ref, wd_ref))
        for rows in row_chunks(x_ref):
            acc = acc_ref[rows, :] + swiglu_down(h_ref[rows, :], wg, wu, wd)
            acc_ref[rows, :] = x_ref[rows, :] + _rms(acc) * gpost_ref[...]

    middle = (f > 0) & (f < last)
    for on_tile, refs in ((first_tile, (x1s_ref, hns_ref, os_ref)), (True, (x1_ref, hn_ref, o_ref))):
        pl.when(on_tile & (f == 0))(functools.partial(first_step, *refs))
        pl.when(on_tile & middle)(functools.partial(middle_step, *refs[1:]))
        pl.when(on_tile & (f == last))(functools.partial(last_step, *refs))


def _ffn(x1, x1_s, g_pre, wg, wu, wd, g_post, layer, tm, tf):
    m, ns = x1.shape[0], x1_s.shape[0]
    return pl.pallas_call(
        _ffn_kernel,
        grid=(m // tm, D_FF // tf),
        in_specs=[
            pl.BlockSpec((tm, D_MODEL), lambda i, f: (i, 0)),
            pl.BlockSpec((ns, D_MODEL), lambda i, f: (0, 0)),
            pl.BlockSpec((1, D_MODEL), lambda i, f: (0, 0)),
            pl.BlockSpec((None, D_MODEL, tf), lambda i, f: (layer, 0, f)),
            pl.BlockSpec((None, D_MODEL, tf), lambda i, f: (layer, 0, f)),
            pl.BlockSpec((None, tf, D_MODEL), lambda i, f: (layer, f, 0)),
            pl.BlockSpec((1, D_MODEL), lambda i, f: (0, 0)),
        ],
        out_specs=[
            pl.BlockSpec((tm, D_MODEL), lambda i, f: (i, 0)),
            pl.BlockSpec((ns, D_MODEL), lambda i, f: (0, 0)),
        ],
        out_shape=[
            jax.ShapeDtypeStruct((m, D_MODEL), F32),
            jax.ShapeDtypeStruct((ns, D_MODEL), F32),
        ],
        scratch_shapes=[pltpu.VMEM((tm, D_MODEL), BF16), pltpu.VMEM((ns, D_MODEL), BF16)],
        compiler_params=pltpu.CompilerParams(
            dimension_semantics=("arbitrary", "arbitrary"), vmem_limit_bytes=VMEM_LIMIT),
        name="ffn",
    )(x1, x1_s, g_pre, wg, wu, wd, g_post)


SRC_MQ = 4 * RET_WIDTH
SRC_MK = SRC_MQ + ML_HEADS * ML_QK
SRC_MV = SRC_MK + ML_HEADS * ML_QK
SRC_MO = SRC_MV + ML_WIDTH
SRC_MI = SRC_MO + ML_WIDTH
SRC_MF = SRC_MI + ML_HEADS
SRC_PU = SRC_MF + ML_HEADS
IN_WIDTH = SRC_PU + POOL_WIDTH


def _prep_w_in_kernel(w_ref, xs_ref, g_ref, o_ref, ps_ref, pst_ref, xn_scr):
    tk = w_ref.shape[1]
    kb = pl.program_id(0)

    @pl.when(kb == 0)
    def _():
        xn = (_rms(xs_ref[...]) * g_ref[...]).astype(BF16)
        for n in range(xn_scr.shape[0]):
            xn_scr[n] = xn[:, n * tk:(n + 1) * tk]
        ps_ref[...] = jnp.zeros_like(ps_ref)

    def put(dst, pieces, width):
        rows = [w_ref[src:src + n, :] for src, n in pieces]
        have = sum(n for _, n in pieces)
        if have < width:
            rows.append(jnp.zeros((width - have, tk), F32))
        block = rows[0] if len(rows) == 1 else jnp.concatenate(rows, axis=0)
        o_ref[:, dst:dst + width] = block.T.astype(BF16)

    for r in range(0, 4 * RET_WIDTH, 2 * LANES):
        put(r, [(r, 2 * LANES)], 2 * LANES)
    for h in range(ML_HEADS):
        q_rows = [(SRC_MQ + h * ML_QK, ML_QK)]
        if h == 0:
            q_rows.append((SRC_MI, 2 * ML_HEADS))
        put(OFF_MQ + h * ML_QK_PAD, q_rows, ML_QK_PAD)
        put(OFF_MK + h * ML_QK_PAD, [(SRC_MK + h * ML_QK, ML_QK)], ML_QK_PAD)
        put(OFF_MV + h * ML_V_PAD, [(SRC_MV + h * ML_V, ML_V)], ML_V_PAD)
        put(OFF_MO + h * ML_V_PAD, [(SRC_MO + h * ML_V, ML_V)], ML_V_PAD)
    for r in range(0, POOL_WIDTH, 2 * LANES):
        put(OFF_PU + r, [(SRC_PU + r, 2 * LANES)], 2 * LANES)

    ps_ref[...] += _dot(xn_scr[kb], o_ref[...])

    @pl.when(kb == pl.num_programs(0) - 1)
    def _():
        pst_ref[...] = ps_ref[...].T


def _prep_w_in(w_t, layer, x_s, g_pre, tk=256):
    d = w_t.shape[2]
    ns = x_s.shape[0]
    return pl.pallas_call(
        _prep_w_in_kernel,
        grid=(d // tk,),
        in_specs=[
            pl.BlockSpec((None, IN_WIDTH, tk), lambda i: (layer, 0, i)),
            pl.BlockSpec((ns, d), lambda i: (0, 0)),
            pl.BlockSpec((1, d), lambda i: (0, 0)),
        ],
        out_specs=[
            pl.BlockSpec((tk, PROJ_PAD), lambda i: (i, 0)),
            pl.BlockSpec((ns, PROJ_PAD), lambda i: (0, 0)),
            pl.BlockSpec((PROJ_PAD, ns), lambda i: (0, 0)),
        ],
        out_shape=[
            jax.ShapeDtypeStruct((d, PROJ_PAD), BF16),
            jax.ShapeDtypeStruct((ns, PROJ_PAD), F32),
            jax.ShapeDtypeStruct((PROJ_PAD, ns), F32),
        ],
        scratch_shapes=[pltpu.VMEM((d // tk, ns, tk), BF16)],
        compiler_params=pltpu.CompilerParams(
            dimension_semantics=("arbitrary",), vmem_limit_bytes=VMEM_LIMIT),
        name="prep_w_in",
    )(w_t, x_s, g_pre)


def _prep_w_out_kernel(w_ref, o_ref):
    def put(dst, src, rows):
        o_ref[dst:dst + rows, :] = w_ref[src:src + rows, :].astype(BF16)

    put(0, 0, RET_WIDTH)
    for h in range(ML_HEADS):
        dst = MIX_OFF_ML + h * ML_V_PAD
        put(dst, RET_WIDTH + h * ML_V, ML_V)
        o_ref[dst + ML_V:dst + ML_V_PAD, :] = jnp.zeros((ML_V_PAD - ML_V, o_ref.shape[1]), BF16)
    put(MIX_OFF_POOL, RET_WIDTH + ML_WIDTH, POOL_WIDTH)


def _prep_w_out(w, layer, tn=512):
    _, k, n = w.shape
    return pl.pallas_call(
        _prep_w_out_kernel,
        grid=(n // tn,),
        in_specs=[pl.BlockSpec((None, k, tn), lambda j: (layer, 0, j))],
        out_specs=pl.BlockSpec((MIX_PAD, tn), lambda j: (0, j)),
        out_shape=jax.ShapeDtypeStruct((MIX_PAD, n), BF16),
        compiler_params=pltpu.CompilerParams(
            dimension_semantics=("parallel",), vmem_limit_bytes=VMEM_LIMIT),
        name="prep_w_out",
    )(w)


PROMPT_BLOCK = 2


def _ret_head(q, k, get_v, get_g, cos2, sin2, dintra, dq_col, dk_row, dch_row, g_norm, s_old,
              store):
    qb = _rotate(q, cos2, sin2).astype(BF16)
    kr_t = (_rotate(k, cos2, sin2) * (RET_DIM ** -0.5)).T
    kb = kr_t.astype(BF16)
    kdb = (kr_t * dk_row).astype(BF16)
    yield
    scores = (_dot(qb, kb) * dintra).astype(BF16)
    vb = get_v().astype(BF16)
    g = get_g()
    gate = g * jax.nn.sigmoid(g)
    yield
    o = _dot(scores, vb) + _dot(qb, s_old.astype(BF16)) * dq_col
    s_new = s_old * dch_row + _dot(kdb, vb)
    store(_rms(o) * g_norm * gate, s_new)


def _ml_head(q, k, vb, og, causal, b_c, i_c, b_r, i_r, g_norm, c_old, n_old, m_prev, store):
    k_t = k.T
    qb = q.astype(BF16)
    logw = jnp.where(causal, b_c - b_r + i_r, -jnp.inf)
    inter = b_c + m_prev
    m_t = jnp.maximum(inter, jnp.max(logw, axis=-1, keepdims=True))
    w = jnp.exp(logw - m_t)
    a = jnp.exp(inter - m_t)
    m_new = m_t[CHUNK - 1:CHUNK, :]
    b_last = b_c[CHUNK - 1:CHUNK, :]
    w_r = jnp.exp(b_last - b_r + i_r - m_new)
    w_c = jnp.exp(b_last - b_c + i_c - m_new)
    a_end = jnp.exp(b_last + m_prev - m_new)
    kwb = (k_t * w_r).astype(BF16)
    n_new = a_end * n_old + jnp.sum(k * w_c, axis=0, keepdims=True)
    qn = jnp.sum(q * n_old, axis=-1, keepdims=True)
    floor = jnp.exp(-m_t)
    gate = jax.nn.sigmoid(og)
    yield
    s = _dot(qb, k_t.astype(BF16)) * w
    den = jnp.sum(s, axis=-1, keepdims=True) + a * qn
    sb = s.astype(BF16)
    yield
    num = _dot(sb, vb) + a * _dot(qb, c_old.astype(BF16))
    c_new = a_end * c_old + _dot(kwb, vb)
    hh = num / jnp.maximum(jnp.abs(den), floor)
    store(_rms(hh, ML_V) * g_norm * gate, c_new, n_new, m_new)


FRONT_INPUTS = 20


def _prompt_front_kernel(*refs, n_aliased):
    (x_ref, gpre_ref, w_ref, cos_ref, sin_ref, dintra_ref, dq_ref, dk_ref, dch_ref, gret_ref,
     gml_ref, gb_r_ref, gb_c_ref, wpool_ref, pscale_ref,
     pt_ref, gmlt_ref, cs_ref, ns_ref, ms_ref) = refs[:FRONT_INPUTS]
    (mix_ref, sret_ref, sc_ref, sn_ref, sm_ref, spool_ref, mixs_ref, cso_ref, nso_ref, mso_ref,
     s_scr, c_scr, n_scr, m_scr, u_scr, rows_scr, wk_scr, qc_scr) = refs[FRONT_INPUTS + n_aliased:]
    bp = pl.program_id(0)
    c = pl.program_id(1)
    nseq = x_ref.shape[0]

    if n_aliased == 0:
        for later in range(1, cso_ref.shape[0]):
            cso_ref[later] = jnp.zeros(cso_ref.shape[1:], F32)
        cso_ref = cso_ref.at[0]
    blocks_per_head = ML_QK // cs_ref.shape[0]
    step = bp * pl.num_programs(1) + c
    _sample_mlstm_update(step // blocks_per_head, step % blocks_per_head, blocks_per_head - 1,
                         pt_ref, gb_r_ref, gmlt_ref, cs_ref, ns_ref, ms_ref,
                         mixs_ref, cso_ref, nso_ref, mso_ref, rows_scr, wk_scr, qc_scr)

    @pl.when(c == 0)
    def _():
        s_scr[...] = jnp.zeros_like(s_scr)
        c_scr[...] = jnp.zeros_like(c_scr)
        n_scr[...] = jnp.zeros_like(n_scr)
        m_scr[...] = jnp.zeros_like(m_scr)
        u_scr[:, 0:2 * SUBLANES, :] = jnp.zeros((nseq, 2 * SUBLANES, POOL_WIDTH), F32)

    xn = (_rms(x_ref[...].reshape(nseq * CHUNK, D_MODEL)) * gpre_ref[...]).astype(BF16)

    def proj(lo, width):
        return _dot(xn, w_ref[:, lo:lo + width])

    def seq_rows(t, i):
        return t[i * CHUNK:(i + 1) * CHUNK]

    cos2 = cos_ref[...]
    sin2 = sin_ref[...]
    pair = 2 * RET_DIM

    def ret_project(h0):
        return [functools.partial(proj, off + h0 * RET_DIM, pair)
                for off in (OFF_RQ, OFF_RK, OFF_RV, OFF_RG)]

    def ret_mix(h0, projected):
        pq, pk = projected[:2]
        instances = []
        for h in (h0, h0 + 1):
            lo, hi = (h - h0) * RET_DIM, (h - h0 + 1) * RET_DIM
            for i in range(nseq):
                def store(out, s_new, h=h, i=i):
                    s_scr[i, h] = s_new
                    mix_ref[i, :, h * RET_DIM:(h + 1) * RET_DIM] = out.astype(BF16)

                def late(n, i=i, lo=lo, hi=hi):
                    return seq_rows(projected[n], i)[:, lo:hi]

                instances.append(_ret_head(
                    seq_rows(pq, i)[:, lo:hi], seq_rows(pk, i)[:, lo:hi],
                    functools.partial(late, 2), functools.partial(late, 3),
                    cos2, sin2, dintra_ref[h], dq_ref[:, h:h + 1], dk_ref[h:h + 1, :],
                    dch_ref[h:h + 1, :], gret_ref[:, h * RET_DIM:(h + 1) * RET_DIM], s_scr[i, h],
                    store))
        return instances

    row = lax.broadcasted_iota(jnp.int32, (CHUNK, CHUNK), 0)
    col = lax.broadcasted_iota(jnp.int32, (CHUNK, CHUNK), 1)
    causal = col <= row
    upper = (row <= col).astype(F32)
    lower = causal.astype(F32)
    gates = [None] * nseq

    def ml_project(h0):
        return [functools.partial(proj, OFF_MQ + h0 * ML_QK_PAD, 2 * ML_QK_PAD),
                functools.partial(proj, OFF_MK + h0 * ML_QK_PAD, 2 * ML_QK_PAD),
                functools.partial(proj, OFF_MV + h0 * ML_V_PAD, 2 * ML_V_PAD),
                functools.partial(proj, OFF_MO + h0 * ML_V_PAD, 2 * ML_V_PAD)]

    def ml_mix(h0, projected):
        pq, pk, pv, po = projected
        if h0 == 0:
            for i in range(nseq):
                gq0 = seq_rows(pq, i)[:, 0:ML_QK_PAD]
                i_r8 = gq0.T[GATE_I_LANE:GATE_I_LANE + SUBLANES, :] + gb_r_ref[...]
                b_r8 = jnp.dot(_log_sigmoid(i_r8), upper, precision=_HI, preferred_element_type=F32)
                pre_c = gq0 + gb_c_ref[0:1, :]
                b_c_all = jnp.dot(lower, _log_sigmoid(pre_c), precision=_HI,
                                  preferred_element_type=F32)
                gates[i] = (i_r8, b_r8, pre_c, b_c_all)
        instances = []
        for h in (h0, h0 + 1):
            qlo = (h - h0) * ML_QK_PAD
            vlo = (h - h0) * ML_V_PAD
            for i in range(nseq):
                def store(out, c_new, n_new, m_new, h=h, i=i):
                    c_scr[i, h] = c_new
                    n_scr[i, h:h + 1, :] = n_new
                    m_scr[i, h:h + 1, :] = jnp.broadcast_to(m_new, (1, LANES))
                    mix_ref[i, :, MIX_OFF_ML + h * ML_V_PAD:MIX_OFF_ML + (h + 1) * ML_V_PAD] = (
                        out.astype(BF16))

                i_r8, b_r8, pre_c, b_c_all = gates[i]
                instances.append(_ml_head(
                    seq_rows(pq, i)[:, qlo:qlo + ML_QK_PAD],
                    seq_rows(pk, i)[:, qlo:qlo + ML_QK_PAD] * (ML_QK ** -0.5),
                    seq_rows(pv, i)[:, vlo:vlo + ML_V_PAD].astype(BF16),
                    seq_rows(po, i)[:, vlo:vlo + ML_V_PAD],
                    causal,
                    b_c_all[:, GATE_F_LANE + h:GATE_F_LANE + h + 1],
                    pre_c[:, GATE_I_LANE + h:GATE_I_LANE + h + 1],
                    b_r8[ML_HEADS + h:ML_HEADS + h + 1, :], i_r8[h:h + 1, :],
                    gml_ref[:, h * ML_V_PAD:(h + 1) * ML_V_PAD],
                    c_scr[i, h], n_scr[i, h:h + 1, :], m_scr[i, h:h + 1, 0:1], store))
        return instances

    base = 2 * SUBLANES
    pos = (c * CHUNK + lax.broadcasted_iota(jnp.int32, (CHUNK, 1), 0)).astype(F32)

    def pool_seq(u, i):
        u_scr[i, base:base + CHUNK, :] = u
        zs = []
        for g, win in enumerate(POOL_WINDOWS):
            lo, hi = g * POOL_GROUP, (g + 1) * POOL_GROUP
            wsum = u[:, lo:hi]
            for s in range(1, win):
                wsum = wsum + u_scr[i, base - s:base - s + CHUNK, lo:hi]
            zs.append((wsum / jnp.minimum(pos + 1.0, float(win)) - u[:, lo:hi]).astype(BF16))
        u_scr[i, 0:base, :] = u_scr[i, CHUNK:CHUNK + base, :]
        yield
        for g, z in enumerate(zs):
            lo, hi = g * POOL_GROUP, (g + 1) * POOL_GROUP
            y = _dot(z, wpool_ref[g]) * pscale_ref[:, lo:hi]
            mix_ref[i, :, MIX_OFF_POOL + lo:MIX_OFF_POOL + hi] = y.astype(BF16)

    def pool_mix(projected):
        pu, = projected
        return [pool_seq(seq_rows(pu, i), i) for i in range(nseq)]

    stages = [(ret_project(h0), functools.partial(ret_mix, h0)) for h0 in range(0, RET_HEADS, 2)]
    stages += [(ml_project(h0), functools.partial(ml_mix, h0)) for h0 in range(0, ML_HEADS, 2)]
    stages.append(([functools.partial(proj, OFF_PU, POOL_WIDTH)], pool_mix))
    early = 2
    projected = [dot() for dot in stages[0][0][:early]]
    late_dots = list(stages[0][0][early:])
    for n, (_, mix) in enumerate(stages):
        pending = list(stages[n + 1][0]) if n + 1 < len(stages) else []
        upcoming = []
        live = mix(projected)
        while live:
            for _ in range(max(1, (len(late_dots) + len(pending)) // 2)):
                if late_dots:
                    projected.append(late_dots.pop(0)())
                elif pending:
                    upcoming.append(pending.pop(0)())
            live = [inst for inst in live if next(inst, StopIteration) is not StopIteration]
        upcoming += [dot() for dot in pending]
        projected = upcoming

    @pl.when(c == pl.num_programs(1) - 1)
    def _():
        for i in range(nseq):
            b = bp * nseq + i
            sret_ref[b] = s_scr[i]
            sc_ref[b] = c_scr[i]
            sn_ref[b] = n_scr[i]
            sm_ref[b] = m_scr[i]
            spool_ref[b] = u_scr[i, 0:base, :]


def _prompt_front(x, g_pre, w, tables, gains, layer, proj_t, c_t, n_t, m_t, stacked_prev):
    batch, seq, _ = x.shape
    nb = proj_t.shape[1]
    nchunk = seq // CHUNK
    nseq = PROMPT_BLOCK
    steps = (batch // nseq) * nchunk
    dblk = ML_HEADS * ML_QK // steps
    assert dblk * steps == ML_HEADS * ML_QK and ML_QK % dblk == 0
    per_head = ML_QK // dblk
    head = lambda b, c: (b * nchunk + c) // per_head
    c_blk = pl.BlockSpec((None, None, dblk, ML_V, nb),
                         lambda b, c: (layer, head(b, c), (b * nchunk + c) % per_head, 0, 0))
    if stacked_prev:
        co_blk = c_blk
    else:
        assert layer == 0
        co_blk = pl.BlockSpec((c_t.shape[0], None, dblk, ML_V, nb),
                              lambda b, c: (0, head(b, c), (b * nchunk + c) % per_head, 0, 0))
    ml_rows = OFF_PU - OFF_MQ
    assert OFF_MQ % ml_rows == 0
    const2 = lambda b, c: (0, 0)
    const3 = lambda b, c: (0, 0, 0)
    const4 = lambda b, c: (0, 0, 0, 0)
    sample_in_specs = [
        pl.BlockSpec((ml_rows, nb), lambda b, c: (OFF_MQ // ml_rows, 0),
                     pipeline_mode=pl.Buffered(1)),
        pl.BlockSpec((None, ML_V_PAD, nb), lambda b, c: (head(b, c), 0, 0)),
        c_blk,
        pl.BlockSpec((None, None, ML_QK, nb), lambda b, c: (layer, head(b, c), 0, 0)),
        pl.BlockSpec((None, ML_HEADS, nb), lambda b, c: (layer, 0, 0)),
    ]
    sample_out_shape = [
        jax.ShapeDtypeStruct((ML_HEADS, nb, ML_V_PAD), BF16),
        jax.ShapeDtypeStruct(c_t.shape, F32),
        jax.ShapeDtypeStruct((ML_HEADS, ML_QK, nb), F32),
        jax.ShapeDtypeStruct((ML_HEADS, nb), F32),
    ]
    sample_out_specs = [
        pl.BlockSpec((None, nb, ML_V_PAD), lambda b, c: (head(b, c), 0, 0)),
        co_blk,
        pl.BlockSpec((None, ML_QK, nb), lambda b, c: (head(b, c), 0, 0)),
        pl.BlockSpec((ML_HEADS, nb), const2),
    ]
    sample_scratch = [
        pltpu.VMEM((SUBLANES, nb), F32),
        pltpu.VMEM((ML_QK, nb), F32),
        pltpu.VMEM((ML_V, nb), F32),
    ]
    in_specs = [
        pl.BlockSpec((nseq, CHUNK, D_MODEL), lambda b, c: (b, c, 0)),
        pl.BlockSpec((1, D_MODEL), const2),
        pl.BlockSpec((D_MODEL, PROJ_PAD), const2, pipeline_mode=pl.Buffered(1)),
        pl.BlockSpec((CHUNK, RET_DIM), lambda b, c: (c, 0)),
        pl.BlockSpec((CHUNK, RET_DIM), lambda b, c: (c, 0)),
        pl.BlockSpec((RET_HEADS, CHUNK, CHUNK), const3),
        pl.BlockSpec((CHUNK, LANES), const2),
        pl.BlockSpec((SUBLANES, CHUNK), const2),
        pl.BlockSpec((SUBLANES, LANES), const2),
        pl.BlockSpec((1, RET_WIDTH), const2),
        pl.BlockSpec((1, ML_HEADS * ML_V_PAD), const2),
        pl.BlockSpec((SUBLANES, LANES), const2),
        pl.BlockSpec((SUBLANES, LANES), const2),
        pl.BlockSpec((len(POOL_WINDOWS), POOL_GROUP, POOL_GROUP), const3),
        pl.BlockSpec((1, POOL_WIDTH), const2),
    ] + sample_in_specs
    assert len(in_specs) == FRONT_INPUTS
    in_specs += [pl.BlockSpec(memory_space=pl.ANY)] * len(stacked_prev)
    out_shape = [
        jax.ShapeDtypeStruct((batch, seq, MIX_PAD), BF16),
        jax.ShapeDtypeStruct((batch, RET_HEADS, RET_DIM, RET_DIM), F32),
        jax.ShapeDtypeStruct((batch, ML_HEADS, ML_QK_PAD, ML_V_PAD), F32),
        jax.ShapeDtypeStruct((batch, SUBLANES, LANES), F32),
        jax.ShapeDtypeStruct((batch, SUBLANES, LANES), F32),
        jax.ShapeDtypeStruct((batch, 2 * SUBLANES, POOL_WIDTH), F32),
    ] + sample_out_shape
    out_specs = [
        pl.BlockSpec((nseq, CHUNK, MIX_PAD), lambda b, c: (b, c, 0)),
        pl.BlockSpec((batch, RET_HEADS, RET_DIM, RET_DIM), const4),
        pl.BlockSpec((batch, ML_HEADS, ML_QK_PAD, ML_V_PAD), const4),
        pl.BlockSpec((batch, SUBLANES, LANES), const3),
        pl.BlockSpec((batch, SUBLANES, LANES), const3),
        pl.BlockSpec((batch, 2 * SUBLANES, POOL_WIDTH), const3),
    ] + sample_out_specs
    scratch = [
        pltpu.VMEM((nseq, RET_HEADS, RET_DIM, RET_DIM), F32),
        pltpu.VMEM((nseq, ML_HEADS, ML_QK_PAD, ML_V_PAD), F32),
        pltpu.VMEM((nseq, SUBLANES, LANES), F32),
        pltpu.VMEM((nseq, SUBLANES, LANES), F32),
        pltpu.VMEM((nseq, 2 * SUBLANES + CHUNK, POOL_WIDTH), F32),
    ] + sample_scratch
    n_prompt_out = len(out_shape) - len(sample_out_shape)
    outs = pl.pallas_call(
        functools.partial(_prompt_front_kernel, n_aliased=len(stacked_prev)),
        grid=(batch // nseq, nchunk),
        in_specs=in_specs,
        out_specs=out_specs,
        out_shape=out_shape,
        scratch_shapes=scratch,
        input_output_aliases={FRONT_INPUTS + i: n_prompt_out + 1 + i
                              for i in range(len(stacked_prev))},
        compiler_params=pltpu.CompilerParams(
            dimension_semantics=("arbitrary", "arbitrary"), vmem_limit_bytes=VMEM_LIMIT),
        name="prompt_front",
    )(x, g_pre, w, tables["cos"], tables["sin"], tables["dintra"], tables["dq"], tables["dk"],
      tables["dch"], gains["g_ret"], gains["g_ml"], gains["gb_r"], gains["gb_c"],
      gains["w_pool"], gains["pool_scale"], proj_t, gains["g_ml_t"], c_t, n_t, m_t,
      *stacked_prev)
    mixed, *states = outs[:n_prompt_out]
    return (mixed.reshape(batch * seq, MIX_PAD), *states), tuple(outs[n_prompt_out:])


DEC_BLOCK = 8
DEC_RET_INPUTS = 9


def _decode_ret_pool_kernel(*refs, n_aliased):
    (p_ref, cos_ref, sin_ref, dch_ref, gret_ref, wpool_ref, pscale_ref, s_ref,
     pool_ref) = refs[:DEC_RET_INPUTS]
    (mix_ref, mixp_ref, so_ref, poolo_ref, qr_scr, kt_scr, v_scr,
     qs_scr) = refs[DEC_RET_INPUTS + n_aliased:]
    if n_aliased == 0:
        for later in range(1, so_ref.shape[0]):
            so_ref[later] = jnp.zeros(so_ref.shape[1:], F32)
        so_ref = so_ref.at[0]
    step = pl.program_id(0)
    nb = p_ref.shape[0]
    cos2 = cos_ref[0:1, :]
    sin2 = sin_ref[0:1, :]

    def ret_qkv(h):
        lo, hi = h * RET_DIM, (h + 1) * RET_DIM
        qr = _rotate(p_ref[:, OFF_RQ + lo:OFF_RQ + hi], cos2, sin2)
        kr = _rotate(p_ref[:, OFF_RK + lo:OFF_RK + hi], cos2, sin2) * (RET_DIM ** -0.5)
        return qr, kr, p_ref[:, OFF_RV + lo:OFF_RV + hi]

    @pl.when(step == 0)
    def _():
        for h in range(RET_HEADS):
            qr, kr, v = ret_qkv(h)
            qr_scr[h] = qr
            kt_scr[h] = kr.T
            v_scr[h] = v.astype(BF16)

    lane = lax.broadcasted_iota(jnp.int32, (LANES, nb), 1)

    def body(bb, carry):
        b = step * DEC_BLOCK + bb
        sel = lane == b
        for h in range(RET_HEADS):
            s_old = s_ref[bb, h]
            q16 = jnp.broadcast_to(qr_scr[h, pl.ds(b, 1), :], (2 * SUBLANES, RET_DIM)).astype(BF16)
            qs_scr[h, pl.ds(b, 1), :] = _dot(q16, s_old.astype(BF16))[0:1, :]
            k_m = jnp.where(sel, kt_scr[h], 0.0).astype(BF16)
            so_ref[bb, h] = s_old * dch_ref[h:h + 1, :] + _dot(k_m, v_scr[h])
        return carry

    lax.fori_loop(0, DEC_BLOCK, body, 0)

    @pl.when(step == pl.num_programs(0) - 1)
    def _():
        for h in range(RET_HEADS):
            lo, hi = h * RET_DIM, (h + 1) * RET_DIM
            qr, kr, v = ret_qkv(h)
            g = p_ref[:, OFF_RG + lo:OFF_RG + hi]
            o = jnp.sum(qr * kr, axis=-1, keepdims=True) * v + qs_scr[h] * dch_ref[h:h + 1, :]
            out = _rms(o) * gret_ref[:, lo:hi] * (g * jax.nn.sigmoid(g))
            mix_ref[:, lo:hi] = out.astype(BF16)
        u = p_ref[:, OFF_PU:OFF_PU + POOL_WIDTH]
        for g, win in enumerate(POOL_WINDOWS):
            lo, hi = g * POOL_GROUP, (g + 1) * POOL_GROUP
            wsum = u[:, lo:hi]
            for s in range(1, win):
                wsum = wsum + pool_ref[POOL_BUF - s, :, lo:hi]
            z = wsum / float(win) - u[:, lo:hi]
            y = _dot(z.astype(BF16), wpool_ref[g]) * pscale_ref[:, lo:hi]
            mixp_ref[:, lo:hi] = y.astype(BF16)
        for r in range(POOL_BUF - 1):
            poolo_ref[r] = pool_ref[r + 1]
        poolo_ref[POOL_BUF - 1] = u


def _decode_ret_pool(proj, tables, gains, layer, s_ret, pool_t, stacked_prev):
    nb = proj.shape[0]
    const2 = lambda s: (0, 0)
    const3 = lambda s: (0, 0, 0)
    full2 = lambda shape: pl.BlockSpec(shape, const2)
    s_blk = pl.BlockSpec((None, DEC_BLOCK, RET_HEADS, RET_DIM, RET_DIM),
                         lambda s: (layer, s, 0, 0, 0))
    pool_blk = (POOL_BUF, nb, POOL_WIDTH)
    in_specs = [
        full2((nb, PROJ_PAD)),
        full2((SUBLANES, RET_DIM)),
        full2((SUBLANES, RET_DIM)),
        full2((SUBLANES, LANES)),
        full2((1, RET_WIDTH)),
        pl.BlockSpec((len(POOL_WINDOWS), POOL_GROUP, POOL_GROUP), const3),
        full2((1, POOL_WIDTH)),
        s_blk,
        pl.BlockSpec((None,) + pool_blk, lambda s: (layer, 0, 0, 0)),
    ]
    assert len(in_specs) == DEC_RET_INPUTS
    in_specs += [pl.BlockSpec(memory_space=pl.ANY)] * len(stacked_prev)
    if stacked_prev:
        so_blk = s_blk
    else:
        assert layer == 0
        so_blk = pl.BlockSpec((s_ret.shape[0],) + s_blk.block_shape[1:],
                              lambda s: (0, s, 0, 0, 0))
    out_shape = [
        jax.ShapeDtypeStruct((nb, RET_WIDTH), BF16),
        jax.ShapeDtypeStruct((nb, POOL_WIDTH), BF16),
        jax.ShapeDtypeStruct(s_ret.shape, F32),
        jax.ShapeDtypeStruct(pool_blk, F32),
    ]
    out_specs = [
        full2((nb, RET_WIDTH)),
        full2((nb, POOL_WIDTH)),
        so_blk,
        pl.BlockSpec(pool_blk, const3),
    ]
    scratch = [
        pltpu.VMEM((RET_HEADS, nb, RET_DIM), F32),
        pltpu.VMEM((RET_HEADS, RET_DIM, nb), F32),
        pltpu.VMEM((RET_HEADS, nb, RET_DIM), BF16),
        pltpu.VMEM((RET_HEADS, nb, RET_DIM), F32),
    ]
    return pl.pallas_call(
        functools.partial(_decode_ret_pool_kernel, n_aliased=len(stacked_prev)),
        grid=(nb // DEC_BLOCK,),
        in_specs=in_specs,
        out_specs=out_specs,
        out_shape=out_shape,
        scratch_shapes=scratch,
        input_output_aliases={DEC_RET_INPUTS + i: 2 + i for i in range(len(stacked_prev))},
        compiler_params=pltpu.CompilerParams(
            dimension_semantics=("arbitrary",), vmem_limit_bytes=VMEM_LIMIT),
        name="decode_ret_pool",
    )(proj, tables["cos_s"], tables["sin_s"], tables["dch"], gains["g_ret"], gains["w_pool"],
      gains["pool_scale"], s_ret, pool_t, *stacked_prev)


def _sample_mlstm_update(h, j, last_j, pt_ref, gb_ref, gml_ref, c_ref, n_ref, m_ref,
                         mix_ref, co_ref, no_ref, mo_ref, rows_scr, wk_scr, qc_scr):
    dblk = c_ref.shape[0]
    q_row0 = pl.multiple_of(h * ML_QK_PAD, ML_QK_PAD)
    k_row0 = pl.multiple_of(OFF_MK - OFF_MQ + h * ML_QK_PAD, ML_QK_PAD)
    v_row0 = pl.multiple_of(OFF_MV - OFF_MQ + h * ML_V_PAD, ML_V_PAD)
    o_row0 = pl.multiple_of(OFF_MO - OFF_MQ + h * ML_V_PAD, ML_V_PAD)

    @pl.when(j == 0)
    def _():
        i_pre = pt_ref[pl.ds(GATE_I_LANE + h, 1), :] + gb_ref[pl.ds(h, 1), :]
        lf = _log_sigmoid(pt_ref[pl.ds(GATE_F_LANE + h, 1), :] + gb_ref[pl.ds(ML_HEADS + h, 1), :])
        inter = lf + m_ref[pl.ds(h, 1), :]
        m_new = jnp.maximum(inter, i_pre)
        w = jnp.exp(i_pre - m_new)
        rows_scr[0:1, :] = jnp.exp(inter - m_new)
        rows_scr[1:2, :] = w
        rows_scr[2:3, :] = m_new
        wk_scr[...] = w * (pt_ref[pl.ds(k_row0, ML_QK), :] * (ML_QK ** -0.5))
        qc_scr[...] = jnp.zeros_like(qc_scr)

    a = rows_scr[0:1, :]
    v_t = pt_ref[pl.ds(v_row0, ML_V), :]
    qc = qc_scr[...]
    for dd in range(dblk):
        d = j * dblk + dd
        c_old = c_ref[dd]
        co_ref[dd] = a * c_old + wk_scr[pl.ds(d, 1), :] * v_t
        qc = qc + pt_ref[pl.ds(q_row0 + d, 1), :] * c_old
    qc_scr[...] = qc

    @pl.when(j == last_j)
    def _():
        w = rows_scr[1:2, :]
        m_new = rows_scr[2:3, :]
        q_t = pt_ref[pl.ds(q_row0, ML_QK), :]
        k_t = pt_ref[pl.ds(k_row0, ML_QK), :] * (ML_QK ** -0.5)
        n_old = n_ref[...]
        s = jnp.sum(q_t * k_t, axis=0, keepdims=True) * w
        num = s * v_t + a * qc_scr[...]
        den = s + a * jnp.sum(q_t * n_old, axis=0, keepdims=True)
        hh = num / jnp.maximum(jnp.abs(den), jnp.exp(-m_new))
        scale = lax.rsqrt(jnp.sum(hh * hh, axis=0, keepdims=True) / ML_V + EPS)
        out_t = (hh * scale * gml_ref[0:ML_V, :]
                 * jax.nn.sigmoid(pt_ref[pl.ds(o_row0, ML_V), :]))
        out_t = jnp.concatenate([out_t, jnp.zeros((ML_V_PAD - ML_V, out_t.shape[1]), F32)], axis=0)
        mix_ref[...] = out_t.T.astype(BF16)
        no_ref[...] = a * n_old + wk_scr[...]
        mo_ref[pl.ds(h, 1), :] = m_new


def _tables(seq):
    half = RET_DIM // 2
    inv_freq = ROPE_THETA ** (-jnp.arange(half, dtype=F32) / half)

    def cos_sin(pos):
        ang = pos[:, None] * inv_freq[None, :]
        cos, sin = jnp.cos(ang), jnp.sin(ang)
        return jnp.concatenate([cos, cos], axis=-1), jnp.concatenate([-sin, sin], axis=-1)

    cos_p, sin_p = cos_sin(jnp.arange(seq, dtype=F32))
    cos_s, sin_s = cos_sin((PAST_LEN + jnp.arange(1)).astype(F32))
    log_gamma = jnp.log1p(-jnp.exp2(-5.0 - jnp.arange(RET_HEADS, dtype=F32)))
    idx = jnp.arange(CHUNK, dtype=F32)
    diff = idx[:, None] - idx[None, :]
    causal = diff >= 0
    lg = log_gamma[:, None, None]
    dintra = jnp.where(causal, jnp.exp(lg * jnp.where(causal, diff, 0.0)), 0.0)
    dq = jnp.exp(log_gamma[:, None] * (idx + 1.0))
    dk = jnp.exp(log_gamma[:, None] * (CHUNK - 1.0 - idx))

    def pad_rows(t, rows):
        return jnp.pad(t, ((0, rows - t.shape[0]), (0, 0)))

    def chunk_decay(c):
        return pad_rows(jnp.broadcast_to(jnp.exp(log_gamma * c)[:, None], (RET_HEADS, LANES)), SUBLANES)

    return {
        "cos": cos_p, "sin": sin_p,
        "cos_s": jnp.broadcast_to(cos_s, (SUBLANES, RET_DIM)),
        "sin_s": jnp.broadcast_to(sin_s, (SUBLANES, RET_DIM)),
        "dintra": dintra,
        "dq": jnp.pad(dq.T, ((0, 0), (0, LANES - RET_HEADS))),
        "dk": pad_rows(dk, SUBLANES),
        "dch": chunk_decay(float(CHUNK)),
        "dch_s": chunk_decay(1.0),
    }


def _pad_last(t, width):
    return jnp.pad(t, [(0, 0)] * (t.ndim - 1) + [(0, width - t.shape[-1])])


def _prep_gains(l, g_ret_norm, g_mlstm_norm, b_mlstm_i, b_mlstm_f, w_pool, pool_scale):
    gate_bias = jnp.concatenate([b_mlstm_i[l], b_mlstm_f[l]])
    g_ml = _pad_last(g_mlstm_norm[l].reshape(ML_HEADS, ML_V), ML_V_PAD)
    lanes = jnp.concatenate([jnp.zeros((GATE_I_LANE,), F32), _pad_last(gate_bias, LANES - GATE_I_LANE)])
    return {
        "g_ret": g_ret_norm[l][None, :],
        "g_ml": g_ml.reshape(1, -1),
        "g_ml_t": jnp.broadcast_to(g_ml[:, :, None], (ML_HEADS, ML_V_PAD, LANES)),
        "gb_r": jnp.broadcast_to(gate_bias[:, None], (SUBLANES, LANES)),
        "gb_c": jnp.broadcast_to(lanes[None, :], (SUBLANES, LANES)),
        "w_pool": w_pool[l].astype(BF16),
        "pool_scale": pool_scale[l][None, :],
    }


@jax.jit
def kernel(x_prompt, x_sample, state_ret, state_mlstm_c, state_mlstm_n, state_mlstm_m, state_pool,
           w_in, w_out, g_ret_norm, g_mlstm_norm, b_mlstm_i, b_mlstm_f, w_pool, pool_scale,
           g_pre_mix, g_post_mix, g_pre_ffn, g_post_ffn, w_ffn_gate, w_ffn_up, w_ffn_down):
    batch, seq, d = x_prompt.shape
    nb = x_sample.shape[0]
    depth = w_in.shape[0]
    tables = _tables(seq)
    tables_s = dict(tables, dch=tables["dch_s"])
    w_in_t = jnp.swapaxes(w_in, 1, 2)

    yp = x_prompt.reshape(batch * seq, d)
    ys = x_sample.reshape(nb, d)
    new_p, new_s = [], []
    stacked_ret, stacked_c = (), ()
    c_t = jnp.transpose(state_mlstm_c, (0, 2, 3, 4, 1))
    n_t = jnp.transpose(state_mlstm_n, (0, 2, 3, 1))
    m_t = jnp.transpose(state_mlstm_m, (0, 2, 1))
    pool_t = jnp.transpose(state_pool, (0, 2, 1, 3))

    t = _token_tiles(batch * seq)
    for l in range(depth):
        row = lambda g: g[l][None, :]
        w_in_l, proj, proj_t = _prep_w_in(w_in_t, l, ys, row(g_pre_mix))
        w_out_l = _prep_w_out(w_out, l)
        gains = _prep_gains(l, g_ret_norm, g_mlstm_norm, b_mlstm_i, b_mlstm_f, w_pool, pool_scale)

        (mixed, s_ret, s_c, s_n, s_m, s_pool), (mix_ml, s_c_s, n_new, m_new) = _prompt_front(
            yp.reshape(batch, seq, d), row(g_pre_mix), w_in_l, tables, gains,
            l, proj_t, c_t, n_t, m_t, stacked_c)
        new_p.append((s_ret, s_c[:, :, :ML_QK, :ML_V], s_n[:, :ML_HEADS, :ML_QK],
                      s_m[:, :ML_HEADS, 0], s_pool[:, 1:, :]))

        mix_ret, mix_pool, s_ret, pool_new = _decode_ret_pool(
            proj, tables_s, gains, l, state_ret, pool_t, stacked_ret)
        s_c = s_c_s
        mixed_s = jnp.concatenate(
            [mix_ret, jnp.swapaxes(mix_ml, 0, 1).reshape(nb, ML_HEADS * ML_V_PAD), mix_pool], axis=1)
        stacked_ret, stacked_c = (s_ret,), (s_c,)
        new_s.append((n_new, m_new, pool_new))

        x1, x1_s = _outproj(mixed, mixed_s, w_out_l, yp, ys, row(g_post_mix),
                            tm=t["outproj_tm"])
        yp, ys = _ffn(x1, x1_s, row(g_pre_ffn), w_ffn_gate, w_ffn_up, w_ffn_down, row(g_post_ffn),
                      l, tm=t["ffn_tm"], tf=t["ffn_tf"])

    stack = lambda states, i: jnp.stack([s[i] for s in states], axis=0)
    return ((yp.reshape(batch, seq, d), ys.reshape(nb, 1, d))
            + tuple(stack(new_p, i) for i in range(5))
            + (stacked_ret[0],
               jnp.transpose(stacked_c[0], (0, 4, 1, 2, 3)),
               jnp.transpose(stack(new_s, 0), (0, 3, 1, 2)),
               jnp.transpose(stack(new_s, 1), (0, 2, 1)),
               jnp.transpose(stack(new_s, 2), (0, 2, 1, 3))))
```

```python
import functools

import jax
import jax.numpy as jnp
from jax import lax
from jax.experimental import pallas as pl
from jax.experimental.pallas import tpu as pltpu

F32 = jnp.float32
BF16 = jnp.bfloat16

D_MODEL = 2048
PAST_LEN = 16384
RET_HEADS = 6
RET_DIM = 128
RET_WIDTH = RET_HEADS * RET_DIM
ML_HEADS = 4
ML_QK = 96
ML_V = 192
ML_WIDTH = ML_HEADS * ML_V
POOL_WIDTH = 512
POOL_WINDOWS = (2, 4, 8, 16)
POOL_GROUP = POOL_WIDTH // len(POOL_WINDOWS)
POOL_BUF = max(POOL_WINDOWS) - 1
D_FF = 5632
CHUNK = 128
ROPE_THETA = 10000.0
EPS = 1e-6

LANES = 128
SUBLANES = 8
VMEM_LIMIT = 60 * 1024 * 1024

ML_QK_PAD = LANES
ML_V_PAD = 2 * LANES
OFF_RQ = 0
OFF_RK = OFF_RQ + RET_WIDTH
OFF_RV = OFF_RK + RET_WIDTH
OFF_RG = OFF_RV + RET_WIDTH
OFF_MQ = OFF_RG + RET_WIDTH
OFF_MK = OFF_MQ + ML_HEADS * ML_QK_PAD
OFF_MV = OFF_MK + ML_HEADS * ML_QK_PAD
OFF_MO = OFF_MV + ML_HEADS * ML_V_PAD
OFF_PU = OFF_MO + ML_HEADS * ML_V_PAD
PROJ_PAD = OFF_PU + POOL_WIDTH
GATE_I_LANE = ML_QK
GATE_F_LANE = ML_QK + ML_HEADS
MIX_OFF_ML = RET_WIDTH
MIX_OFF_POOL = MIX_OFF_ML + ML_HEADS * ML_V_PAD
MIX_PAD = MIX_OFF_POOL + POOL_WIDTH

_HI = lax.Precision.HIGHEST


def _token_tiles(m):
    return {
        "outproj_tm": min(m, 512),
        "ffn_tm": min(m, 1024), "ffn_tf": 256,
    }


def _dot(a, b):
    return jnp.dot(a, b, preferred_element_type=F32)


def _rms(x, width=None):
    n = x.shape[-1] if width is None else width
    return x * lax.rsqrt(jnp.sum(x * x, axis=-1, keepdims=True) / n + EPS)


def _log_sigmoid(x):
    return jnp.minimum(x, 0.0) - jnp.log1p(jnp.exp(-jnp.abs(x)))


def _rotate(x, cos2, sin2):
    return x * cos2 + pltpu.roll(x, RET_DIM // 2, 1) * sin2


def _outproj_kernel(m_ref, ms_ref, w_ref, x_ref, xs_ref, gpost_ref, x1_ref, x1s_ref):
    @pl.when(pl.program_id(0) == 0)
    def _():
        x1s_ref[...] = xs_ref[...] + _rms(_dot(ms_ref[...], w_ref[...])) * gpost_ref[...]

    x1_ref[...] = x_ref[...] + _rms(_dot(m_ref[...], w_ref[...])) * gpost_ref[...]


def _outproj(mixed, mixed_s, w, x, x_s, g_post, tm):
    m, ns = x.shape[0], x_s.shape[0]
    return pl.pallas_call(
        _outproj_kernel,
        grid=(m // tm,),
        in_specs=[
            pl.BlockSpec((tm, MIX_PAD), lambda i: (i, 0)),
            pl.BlockSpec((ns, MIX_PAD), lambda i: (0, 0)),
            pl.BlockSpec((MIX_PAD, D_MODEL), lambda i: (0, 0), pipeline_mode=pl.Buffered(1)),
            pl.BlockSpec((tm, D_MODEL), lambda i: (i, 0)),
            pl.BlockSpec((ns, D_MODEL), lambda i: (0, 0)),
            pl.BlockSpec((1, D_MODEL), lambda i: (0, 0)),
        ],
        out_specs=[
            pl.BlockSpec((tm, D_MODEL), lambda i: (i, 0)),
            pl.BlockSpec((ns, D_MODEL), lambda i: (0, 0)),
        ],
        out_shape=[
            jax.ShapeDtypeStruct((m, D_MODEL), F32),
            jax.ShapeDtypeStruct((ns, D_MODEL), F32),
        ],
        compiler_params=pltpu.CompilerParams(
            dimension_semantics=("arbitrary",), vmem_limit_bytes=VMEM_LIMIT),
        name="outproj",
    )(mixed, mixed_s, w, x, x_s, g_post)


FFN_ROW_CHUNK = 256


def _ffn_kernel(x1_ref, x1s_ref, gpre_ref, wg_ref, wu_ref, wd_ref, gpost_ref, o_ref, os_ref,
                hn_ref, hns_ref):
    first_tile = pl.program_id(0) == 0
    f = pl.program_id(1)
    last = pl.num_programs(1) - 1

    def swiglu_down(hn, wg, wu, wd):
        gate = _dot(hn, wg)
        act = (gate * jax.nn.sigmoid(gate) * _dot(hn, wu)).astype(BF16)
        return _dot(act, wd)

    def middle_step(h_ref, acc_ref):
        acc_ref[...] += swiglu_down(h_ref[...], wg_ref[...].astype(BF16),
                                    wu_ref[...].astype(BF16), wd_ref[...].astype(BF16))

    def row_chunks(ref):
        rows = ref.shape[0]
        chunk = min(rows, FFN_ROW_CHUNK)
        return [slice(r, r + chunk) for r in range(0, rows, chunk)]

    def first_step(x_ref, h_ref, acc_ref):
        wg, wu, wd = (w[...].astype(BF16) for w in (wg_ref, wu_ref, wd_ref))
        for rows in row_chunks(x_ref):
            hn = (_rms(x_ref[rows, :]) * gpre_ref[...]).astype(BF16)
            h_ref[rows, :] = hn
            acc_ref[rows, :] = swiglu_down(hn, wg, wu, wd)

    def last_step(x_ref, h_ref, acc_ref):
        wg, wu, wd = (w[...].astype(BF16) for w in (wg_ref, wu_ref, wd_ref))
        for rows in row_chunks(x_ref):
            acc = acc_ref[rows, :] + swiglu_down(h_ref[rows, :], wg, wu, wd)
            acc_ref[rows, :] = x_ref[rows, :] + _rms(acc) * gpost_ref[...]

    middle = (f > 0) & (f < last)
    for on_tile, refs in ((first_tile, (x1s_ref, hns_ref, os_ref)), (True, (x1_ref, hn_ref, o_ref))):
        pl.when(on_tile & (f == 0))(functools.partial(first_step, *refs))
        pl.when(on_tile & middle)(functools.partial(middle_step, *refs[1:]))
        pl.when(on_tile & (f == last))(functools.partial(last_step, *refs))


def _ffn(x1, x1_s, g_pre, wg, wu, wd, g_post, layer, tm, tf):
    m, ns = x1.shape[0], x1_s.shape[0]
    return pl.pallas_call(
        _ffn_kernel,
        grid=(m // tm, D_FF // tf),
        in_specs=[
            pl.BlockSpec((tm, D_MODEL), lambda i, f: (i, 0)),
            pl.BlockSpec((ns, D_MODEL), lambda i, f: (0, 0)),
            pl.BlockSpec((1, D_MODEL), lambda i, f: (0, 0)),
            pl.BlockSpec((None, D_MODEL, tf), lambda i, f: (layer, 0, f)),
            pl.BlockSpec((None, D_MODEL, tf), lambda i, f: (layer, 0, f)),
            pl.BlockSpec((None, tf, D_MODEL), lambda i, f: (layer, f, 0)),
            pl.BlockSpec((1, D_MODEL), lambda i, f: (0, 0)),
        ],
        out_specs=[
            pl.BlockSpec((tm, D_MODEL), lambda i, f: (i, 0)),
            pl.BlockSpec((ns, D_MODEL), lambda i, f: (0, 0)),
        ],
        out_shape=[
            jax.ShapeDtypeStruct((m, D_MODEL), F32),
            jax.ShapeDtypeStruct((ns, D_MODEL), F32),
        ],
        scratch_shapes=[pltpu.VMEM((tm, D_MODEL), BF16), pltpu.VMEM((ns, D_MODEL), BF16)],
        compiler_params=pltpu.CompilerParams(
            dimension_semantics=("arbitrary", "arbitrary"), vmem_limit_bytes=VMEM_LIMIT),
        name="ffn",
    )(x1, x1_s, g_pre, wg, wu, wd, g_post)


SRC_MQ = 4 * RET_WIDTH
SRC_MK = SRC_MQ + ML_HEADS * ML_QK
SRC_MV = SRC_MK + ML_HEADS * ML_QK
SRC_MO = SRC_MV + ML_WIDTH
SRC_MI = SRC_MO + ML_WIDTH
SRC_MF = SRC_MI + ML_HEADS
SRC_PU = SRC_MF + ML_HEADS
IN_WIDTH = SRC_PU + POOL_WIDTH


def _prep_w_in_kernel(w_ref, xs_ref, g_ref, o_ref, ps_ref, pst_ref, xn_scr):
    tk = w_ref.shape[1]
    kb = pl.program_id(0)

    @pl.when(kb == 0)
    def _():
        xn = (_rms(xs_ref[...]) * g_ref[...]).astype(BF16)
        for n in range(xn_scr.shape[0]):
            xn_scr[n] = xn[:, n * tk:(n + 1) * tk]
        ps_ref[...] = jnp.zeros_like(ps_ref)

    def put(dst, pieces, width):
        rows = [w_ref[src:src + n, :] for src, n in pieces]
        have = sum(n for _, n in pieces)
        if have < width:
            rows.append(jnp.zeros((width - have, tk), F32))
        block = rows[0] if len(rows) == 1 else jnp.concatenate(rows, axis=0)
        o_ref[:, dst:dst + width] = block.T.astype(BF16)

    for r in range(0, 4 * RET_WIDTH, 2 * LANES):
        put(r, [(r, 2 * LANES)], 2 * LANES)
    for h in range(ML_HEADS):
        q_rows = [(SRC_MQ + h * ML_QK, ML_QK)]
        if h == 0:
            q_rows.append((SRC_MI, 2 * ML_HEADS))
        put(OFF_MQ + h * ML_QK_PAD, q_rows, ML_QK_PAD)
        put(OFF_MK + h * ML_QK_PAD, [(SRC_MK + h * ML_QK, ML_QK)], ML_QK_PAD)
        put(OFF_MV + h * ML_V_PAD, [(SRC_MV + h * ML_V, ML_V)], ML_V_PAD)
        put(OFF_MO + h * ML_V_PAD, [(SRC_MO + h * ML_V, ML_V)], ML_V_PAD)
    for r in range(0, POOL_WIDTH, 2 * LANES):
        put(OFF_PU + r, [(SRC_PU + r, 2 * LANES)], 2 * LANES)

    ps_ref[...] += _dot(xn_scr[kb], o_ref[...])

    @pl.when(kb == pl.num_programs(0) - 1)
    def _():
        pst_ref[...] = ps_ref[...].T


def _prep_w_in(w_t, layer, x_s, g_pre, tk=256):
    d = w_t.shape[2]
    ns = x_s.shape[0]
    return pl.pallas_call(
        _prep_w_in_kernel,
        grid=(d // tk,),
        in_specs=[
            pl.BlockSpec((None, IN_WIDTH, tk), lambda i: (layer, 0, i)),
            pl.BlockSpec((ns, d), lambda i: (0, 0)),
            pl.BlockSpec((1, d), lambda i: (0, 0)),
        ],
        out_specs=[
            pl.BlockSpec((tk, PROJ_PAD), lambda i: (i, 0)),
            pl.BlockSpec((ns, PROJ_PAD), lambda i: (0, 0)),
            pl.BlockSpec((PROJ_PAD, ns), lambda i: (0, 0)),
        ],
        out_shape=[
            jax.ShapeDtypeStruct((d, PROJ_PAD), BF16),
            jax.ShapeDtypeStruct((ns, PROJ_PAD), F32),
            jax.ShapeDtypeStruct((PROJ_PAD, ns), F32),
        ],
        scratch_shapes=[pltpu.VMEM((d // tk, ns, tk), BF16)],
        compiler_params=pltpu.CompilerParams(
            dimension_semantics=("arbitrary",), vmem_limit_bytes=VMEM_LIMIT),
        name="prep_w_in",
    )(w_t, x_s, g_pre)


def _prep_w_out_kernel(w_ref, o_ref):
    def put(dst, src, rows):
        o_ref[dst:dst + rows, :] = w_ref[src:src + rows, :].astype(BF16)

    put(0, 0, RET_WIDTH)
    for h in range(ML_HEADS):
        dst = MIX_OFF_ML + h * ML_V_PAD
        put(dst, RET_WIDTH + h * ML_V, ML_V)
        o_ref[dst + ML_V:dst + ML_V_PAD, :] = jnp.zeros((ML_V_PAD - ML_V, o_ref.shape[1]), BF16)
    put(MIX_OFF_POOL, RET_WIDTH + ML_WIDTH, POOL_WIDTH)


def _prep_w_out(w, layer, tn=512):
    _, k, n = w.shape
    return pl.pallas_call(
        _prep_w_out_kernel,
        grid=(n // tn,),
        in_specs=[pl.BlockSpec((None, k, tn), lambda j: (layer, 0, j))],
        out_specs=pl.BlockSpec((MIX_PAD, tn), lambda j: (0, j)),
        out_shape=jax.ShapeDtypeStruct((MIX_PAD, n), BF16),
        compiler_params=pltpu.CompilerParams(
            dimension_semantics=("parallel",), vmem_limit_bytes=VMEM_LIMIT),
        name="prep_w_out",
    )(w)


PROMPT_BLOCK = 2


def _ret_head(q, k, get_v, get_g, cos2, sin2, dintra, dq_col, dk_row, dch_row, g_norm, s_old,
              store):
    qb = _rotate(q, cos2, sin2).astype(BF16)
    kr_t = (_rotate(k, cos2, sin2) * (RET_DIM ** -0.5)).T
    kb = kr_t.astype(BF16)
    kdb = (kr_t * dk_row).astype(BF16)
    yield
    scores = (_dot(qb, kb) * dintra).astype(BF16)
    vb = get_v().astype(BF16)
    g = get_g()
    gate = g * jax.nn.sigmoid(g)
    yield
    o = _dot(scores, vb) + _dot(qb, s_old.astype(BF16)) * dq_col
    s_new = s_old * dch_row + _dot(kdb, vb)
    store(_rms(o) * g_norm * gate, s_new)


def _ml_head(q, k, vb, og, causal, b_c, i_c, b_r, i_r, g_norm, c_old, n_old, m_prev, store):
    k_t = k.T
    qb = q.astype(BF16)
    logw = jnp.where(causal, b_c - b_r + i_r, -jnp.inf)
    inter = b_c + m_prev
    m_t = jnp.maximum(inter, jnp.max(logw, axis=-1, keepdims=True))
    w = jnp.exp(logw - m_t)
    a = jnp.exp(inter - m_t)
    m_new = m_t[CHUNK - 1:CHUNK, :]
    b_last = b_c[CHUNK - 1:CHUNK, :]
    w_r = jnp.exp(b_last - b_r + i_r - m_new)
    w_c = jnp.exp(b_last - b_c + i_c - m_new)
    a_end = jnp.exp(b_last + m_prev - m_new)
    kwb = (k_t * w_r).astype(BF16)
    n_new = a_end * n_old + jnp.sum(k * w_c, axis=0, keepdims=True)
    qn = jnp.sum(q * n_old, axis=-1, keepdims=True)
    floor = jnp.exp(-m_t)
    gate = jax.nn.sigmoid(og)
    yield
    s = _dot(qb, k_t.astype(BF16)) * w
    den = jnp.sum(s, axis=-1, keepdims=True) + a * qn
    sb = s.astype(BF16)
    yield
    num = _dot(sb, vb) + a * _dot(qb, c_old.astype(BF16))
    c_new = a_end * c_old + _dot(kwb, vb)
    hh = num / jnp.maximum(jnp.abs(den), floor)
    store(_rms(hh, ML_V) * g_norm * gate, c_new, n_new, m_new)


FRONT_INPUTS = 20


def _prompt_front_kernel(*refs, n_aliased):
    (x_ref, gpre_ref, w_ref, cos_ref, sin_ref, dintra_ref, dq_ref, dk_ref, dch_ref, gret_ref,
     gml_ref, gb_r_ref, gb_c_ref, wpool_ref, pscale_ref,
     pt_ref, gmlt_ref, cs_ref, ns_ref, ms_ref) = refs[:FRONT_INPUTS]
    (mix_ref, sret_ref, sc_ref, sn_ref, sm_ref, spool_ref, mixs_ref, cso_ref, nso_ref, mso_ref,
     s_scr, c_scr, n_scr, m_scr, u_scr, rows_scr, wk_scr, qc_scr) = refs[FRONT_INPUTS + n_aliased:]
    bp = pl.program_id(0)
    c = pl.program_id(1)
    nseq = x_ref.shape[0]

    if n_aliased == 0:
        for later in range(1, cso_ref.shape[0]):
            cso_ref[later] = jnp.zeros(cso_ref.shape[1:], F32)
        cso_ref = cso_ref.at[0]
    blocks_per_head = ML_QK // cs_ref.shape[0]
    step = bp * pl.num_programs(1) + c
    _sample_mlstm_update(step // blocks_per_head, step % blocks_per_head, blocks_per_head - 1,
                         pt_ref, gb_r_ref, gmlt_ref, cs_ref, ns_ref, ms_ref,
                         mixs_ref, cso_ref, nso_ref, mso_ref, rows_scr, wk_scr, qc_scr)

    @pl.when(c == 0)
    def _():
        s_scr[...] = jnp.zeros_like(s_scr)
        c_scr[...] = jnp.zeros_like(c_scr)
        n_scr[...] = jnp.zeros_like(n_scr)
        m_scr[...] = jnp.zeros_like(m_scr)
        u_scr[:, 0:2 * SUBLANES, :] = jnp.zeros((nseq, 2 * SUBLANES, POOL_WIDTH), F32)

    xn = (_rms(x_ref[...].reshape(nseq * CHUNK, D_MODEL)) * gpre_ref[...]).astype(BF16)

    def proj(lo, width):
        return _dot(xn, w_ref[:, lo:lo + width])

    def seq_rows(t, i):
        return t[i * CHUNK:(i + 1) * CHUNK]

    cos2 = cos_ref[...]
    sin2 = sin_ref[...]
    pair = 2 * RET_DIM

    def ret_project(h0):
        return [functools.partial(proj, off + h0 * RET_DIM, pair)
                for off in (OFF_RQ, OFF_RK, OFF_RV, OFF_RG)]

    def ret_mix(h0, projected):
        pq, pk = projected[:2]
        instances = []
        for h in (h0, h0 + 1):
            lo, hi = (h - h0) * RET_DIM, (h - h0 + 1) * RET_DIM
            for i in range(nseq):
                def store(out, s_new, h=h, i=i):
                    s_scr[i, h] = s_new
                    mix_ref[i, :, h * RET_DIM:(h + 1) * RET_DIM] = out.astype(BF16)

                def late(n, i=i, lo=lo, hi=hi):
                    return seq_rows(projected[n], i)[:, lo:hi]

                instances.append(_ret_head(
                    seq_rows(pq, i)[:, lo:hi], seq_rows(pk, i)[:, lo:hi],
                    functools.partial(late, 2), functools.partial(late, 3),
                    cos2, sin2, dintra_ref[h], dq_ref[:, h:h + 1], dk_ref[h:h + 1, :],
                    dch_ref[h:h + 1, :], gret_ref[:, h * RET_DIM:(h + 1) * RET_DIM], s_scr[i, h],
                    store))
        return instances

    row = lax.broadcasted_iota(jnp.int32, (CHUNK, CHUNK), 0)
    col = lax.broadcasted_iota(jnp.int32, (CHUNK, CHUNK), 1)
    causal = col <= row
    upper = (row <= col).astype(F32)
    lower = causal.astype(F32)
    gates = [None] * nseq

    def ml_project(h0):
        return [functools.partial(proj, OFF_MQ + h0 * ML_QK_PAD, 2 * ML_QK_PAD),
                functools.partial(proj, OFF_MK + h0 * ML_QK_PAD, 2 * ML_QK_PAD),
                functools.partial(proj, OFF_MV + h0 * ML_V_PAD, 2 * ML_V_PAD),
                functools.partial(proj, OFF_MO + h0 * ML_V_PAD, 2 * ML_V_PAD)]

    def ml_mix(h0, projected):
        pq, pk, pv, po = projected
        if h0 == 0:
            for i in range(nseq):
                gq0 = seq_rows(pq, i)[:, 0:ML_QK_PAD]
                i_r8 = gq0.T[GATE_I_LANE:GATE_I_LANE + SUBLANES, :] + gb_r_ref[...]
                b_r8 = jnp.dot(_log_sigmoid(i_r8), upper, precision=_HI, preferred_element_type=F32)
                pre_c = gq0 + gb_c_ref[0:1, :]
                b_c_all = jnp.dot(lower, _log_sigmoid(pre_c), precision=_HI,
                                  preferred_element_type=F32)
                gates[i] = (i_r8, b_r8, pre_c, b_c_all)
        instances = []
        for h in (h0, h0 + 1):
            qlo = (h - h0) * ML_QK_PAD
            vlo = (h - h0) * ML_V_PAD
            for i in range(nseq):
                def store(out, c_new, n_new, m_new, h=h, i=i):
                    c_scr[i, h] = c_new
                    n_scr[i, h:h + 1, :] = n_new
                    m_scr[i, h:h + 1, :] = jnp.broadcast_to(m_new, (1, LANES))
                    mix_ref[i, :, MIX_OFF_ML + h * ML_V_PAD:MIX_OFF_ML + (h + 1) * ML_V_PAD] = (
                        out.astype(BF16))

                i_r8, b_r8, pre_c, b_c_all = gates[i]
                instances.append(_ml_head(
                    seq_rows(pq, i)[:, qlo:qlo + ML_QK_PAD],
                    seq_rows(pk, i)[:, qlo:qlo + ML_QK_PAD] * (ML_QK ** -0.5),
                    seq_rows(pv, i)[:, vlo:vlo + ML_V_PAD].astype(BF16),
                    seq_rows(po, i)[:, vlo:vlo + ML_V_PAD],
                    causal,
                    b_c_all[:, GATE_F_LANE + h:GATE_F_LANE + h + 1],
                    pre_c[:, GATE_I_LANE + h:GATE_I_LANE + h + 1],
                    b_r8[ML_HEADS + h:ML_HEADS + h + 1, :], i_r8[h:h + 1, :],
                    gml_ref[:, h * ML_V_PAD:(h + 1) * ML_V_PAD],
                    c_scr[i, h], n_scr[i, h:h + 1, :], m_scr[i, h:h + 1, 0:1], store))
        return instances

    base = 2 * SUBLANES
    pos = (c * CHUNK + lax.broadcasted_iota(jnp.int32, (CHUNK, 1), 0)).astype(F32)

    def pool_seq(u, i):
        u_scr[i, base:base + CHUNK, :] = u
        zs = []
        for g, win in enumerate(POOL_WINDOWS):
            lo, hi = g * POOL_GROUP, (g + 1) * POOL_GROUP
            wsum = u[:, lo:hi]
            for s in range(1, win):
                wsum = wsum + u_scr[i, base - s:base - s + CHUNK, lo:hi]
            zs.append((wsum / jnp.minimum(pos + 1.0, float(win)) - u[:, lo:hi]).astype(BF16))
        u_scr[i, 0:base, :] = u_scr[i, CHUNK:CHUNK + base, :]
        yield
        for g, z in enumerate(zs):
            lo, hi = g * POOL_GROUP, (g + 1) * POOL_GROUP
            y = _dot(z, wpool_ref[g]) * pscale_ref[:, lo:hi]
            mix_ref[i, :, MIX_OFF_POOL + lo:MIX_OFF_POOL + hi] = y.astype(BF16)

    def pool_mix(projected):
        pu, = projected
        return [pool_seq(seq_rows(pu, i), i) for i in range(nseq)]

    stages = [(ret_project(h0), functools.partial(ret_mix, h0)) for h0 in range(0, RET_HEADS, 2)]
    stages += [(ml_project(h0), functools.partial(ml_mix, h0)) for h0 in range(0, ML_HEADS, 2)]
    stages.append(([functools.partial(proj, OFF_PU, POOL_WIDTH)], pool_mix))
    early = 2
    projected = [dot() for dot in stages[0][0][:early]]
    late_dots = list(stages[0][0][early:])
    for n, (_, mix) in enumerate(stages):
        pending = list(stages[n + 1][0]) if n + 1 < len(stages) else []
        upcoming = []
        live = mix(projected)
        while live:
            for _ in range(max(1, (len(late_dots) + len(pending)) // 2)):
                if late_dots:
                    projected.append(late_dots.pop(0)())
                elif pending:
                    upcoming.append(pending.pop(0)())
            live = [inst for inst in live if next(inst, StopIteration) is not StopIteration]
        upcoming += [dot() for dot in pending]
        projected = upcoming

    @pl.when(c == pl.num_programs(1) - 1)
    def _():
        for i in range(nseq):
            b = bp * nseq + i
            sret_ref[b] = s_scr[i]
            sc_ref[b] = c_scr[i]
            sn_ref[b] = n_scr[i]
            sm_ref[b] = m_scr[i]
            spool_ref[b] = u_scr[i, 0:base, :]


def _prompt_front(x, g_pre, w, tables, gains, layer, proj_t, c_t, n_t, m_t, stacked_prev):
    batch, seq, _ = x.shape
    nb = proj_t.shape[1]
    nchunk = seq // CHUNK
    nseq = PROMPT_BLOCK
    steps = (batch // nseq) * nchunk
    dblk = ML_HEADS * ML_QK // steps
    assert dblk * steps == ML_HEADS * ML_QK and ML_QK % dblk == 0
    per_head = ML_QK // dblk
    head = lambda b, c: (b * nchunk + c) // per_head
    c_blk = pl.BlockSpec((None, None, dblk, ML_V, nb),
                         lambda b, c: (layer, head(b, c), (b * nchunk + c) % per_head, 0, 0))
    if stacked_prev:
        co_blk = c_blk
    else:
        assert layer == 0
        co_blk = pl.BlockSpec((c_t.shape[0], None, dblk, ML_V, nb),
                              lambda b, c: (0, head(b, c), (b * nchunk + c) % per_head, 0, 0))
    ml_rows = OFF_PU - OFF_MQ
    assert OFF_MQ % ml_rows == 0
    const2 = lambda b, c: (0, 0)
    const3 = lambda b, c: (0, 0, 0)
    const4 = lambda b, c: (0, 0, 0, 0)
    sample_in_specs = [
        pl.BlockSpec((ml_rows, nb), lambda b, c: (OFF_MQ // ml_rows, 0),
                     pipeline_mode=pl.Buffered(1)),
        pl.BlockSpec((None, ML_V_PAD, nb), lambda b, c: (head(b, c), 0, 0)),
        c_blk,
        pl.BlockSpec((None, None, ML_QK, nb), lambda b, c: (layer, head(b, c), 0, 0)),
        pl.BlockSpec((None, ML_HEADS, nb), lambda b, c: (layer, 0, 0)),
    ]
    sample_out_shape = [
        jax.ShapeDtypeStruct((ML_HEADS, nb, ML_V_PAD), BF16),
        jax.ShapeDtypeStruct(c_t.shape, F32),
        jax.ShapeDtypeStruct((ML_HEADS, ML_QK, nb), F32),
        jax.ShapeDtypeStruct((ML_HEADS, nb), F32),
    ]
    sample_out_specs = [
        pl.BlockSpec((None, nb, ML_V_PAD), lambda b, c: (head(b, c), 0, 0)),
        co_blk,
        pl.BlockSpec((None, ML_QK, nb), lambda b, c: (head(b, c), 0, 0)),
        pl.BlockSpec((ML_HEADS, nb), const2),
    ]
    sample_scratch = [
        pltpu.VMEM((SUBLANES, nb), F32),
        pltpu.VMEM((ML_QK, nb), F32),
        pltpu.VMEM((ML_V, nb), F32),
    ]
    in_specs = [
        pl.BlockSpec((nseq, CHUNK, D_MODEL), lambda b, c: (b, c, 0)),
        pl.BlockSpec((1, D_MODEL), const2),
        pl.BlockSpec((D_MODEL, PROJ_PAD), const2, pipeline_mode=pl.Buffered(1)),
        pl.BlockSpec((CHUNK, RET_DIM), lambda b, c: (c, 0)),
        pl.BlockSpec((CHUNK, RET_DIM), lambda b, c: (c, 0)),
        pl.BlockSpec((RET_HEADS, CHUNK, CHUNK), const3),
        pl.BlockSpec((CHUNK, LANES), const2),
        pl.BlockSpec((SUBLANES, CHUNK), const2),
        pl.BlockSpec((SUBLANES, LANES), const2),
        pl.BlockSpec((1, RET_WIDTH), const2),
        pl.BlockSpec((1, ML_HEADS * ML_V_PAD), const2),
        pl.BlockSpec((SUBLANES, LANES), const2),
        pl.BlockSpec((SUBLANES, LANES), const2),
        pl.BlockSpec((len(POOL_WINDOWS), POOL_GROUP, POOL_GROUP), const3),
        pl.BlockSpec((1, POOL_WIDTH), const2),
    ] + sample_in_specs
    assert len(in_specs) == FRONT_INPUTS
    in_specs += [pl.BlockSpec(memory_space=pl.ANY)] * len(stacked_prev)
    out_shape = [
        jax.ShapeDtypeStruct((batch, seq, MIX_PAD), BF16),
        jax.ShapeDtypeStruct((batch, RET_HEADS, RET_DIM, RET_DIM), F32),
        jax.ShapeDtypeStruct((batch, ML_HEADS, ML_QK_PAD, ML_V_PAD), F32),
        jax.ShapeDtypeStruct((batch, SUBLANES, LANES), F32),
        jax.ShapeDtypeStruct((batch, SUBLANES, LANES), F32),
        jax.ShapeDtypeStruct((batch, 2 * SUBLANES, POOL_WIDTH), F32),
    ] + sample_out_shape
    out_specs = [
        pl.BlockSpec((nseq, CHUNK, MIX_PAD), lambda b, c: (b, c, 0)),
        pl.BlockSpec((batch, RET_HEADS, RET_DIM, RET_DIM), const4),
        pl.BlockSpec((batch, ML_HEADS, ML_QK_PAD, ML_V_PAD), const4),
        pl.BlockSpec((batch, SUBLANES, LANES), const3),
        pl.BlockSpec((batch, SUBLANES, LANES), const3),
        pl.BlockSpec((batch, 2 * SUBLANES, POOL_WIDTH), const3),
    ] + sample_out_specs
    scratch = [
        pltpu.VMEM((nseq, RET_HEADS, RET_DIM, RET_DIM), F32),
        pltpu.VMEM((nseq, ML_HEADS, ML_QK_PAD, ML_V_PAD), F32),
        pltpu.VMEM((nseq, SUBLANES, LANES), F32),
        pltpu.VMEM((nseq, SUBLANES, LANES), F32),
        pltpu.VMEM((nseq, 2 * SUBLANES + CHUNK, POOL_WIDTH), F32),
    ] + sample_scratch
    n_prompt_out = len(out_shape) - len(sample_out_shape)
    outs = pl.pallas_call(
        functools.partial(_prompt_front_kernel, n_aliased=len(stacked_prev)),
        grid=(batch // nseq, nchunk),
        in_specs=in_specs,
        out_specs=out_specs,
        out_shape=out_shape,
        scratch_shapes=scratch,
        input_output_aliases={FRONT_INPUTS + i: n_prompt_out + 1 + i
                              for i in range(len(stacked_prev))},
        compiler_params=pltpu.CompilerParams(
            dimension_semantics=("arbitrary", "arbitrary"), vmem_limit_bytes=VMEM_LIMIT),
        name="prompt_front",
    )(x, g_pre, w, tables["cos"], tables["sin"], tables["dintra"], tables["dq"], tables["dk"],
      tables["dch"], gains["g_ret"], gains["g_ml"], gains["gb_r"], gains["gb_c"],
      gains["w_pool"], gains["pool_scale"], proj_t, gains["g_ml_t"], c_t, n_t, m_t,
      *stacked_prev)
    mixed, *states = outs[:n_prompt_out]
    return (mixed.reshape(batch * seq, MIX_PAD), *states), tuple(outs[n_prompt_out:])


DEC_BLOCK = 8
DEC_RET_INPUTS = 9


def _decode_ret_pool_kernel(*refs, n_aliased):
    (p_ref, cos_ref, sin_ref, dch_ref, gret_ref, wpool_ref, pscale_ref, s_ref,
     pool_ref) = refs[:DEC_RET_INPUTS]
    (mix_ref, mixp_ref, so_ref, poolo_ref, qr_scr, kt_scr, v_scr,
     qs_scr) = refs[DEC_RET_INPUTS + n_aliased:]
    if n_aliased == 0:
        for later in range(1, so_ref.shape[0]):
            so_ref[later] = jnp.zeros(so_ref.shape[1:], F32)
        so_ref = so_ref.at[0]
    step = pl.program_id(0)
    nb = p_ref.shape[0]
    cos2 = cos_ref[0:1, :]
    sin2 = sin_ref[0:1, :]

    def ret_qkv(h):
        lo, hi = h * RET_DIM, (h + 1) * RET_DIM
        qr = _rotate(p_ref[:, OFF_RQ + lo:OFF_RQ + hi], cos2, sin2)
        kr = _rotate(p_ref[:, OFF_RK + lo:OFF_RK + hi], cos2, sin2) * (RET_DIM ** -0.5)
        return qr, kr, p_ref[:, OFF_RV + lo:OFF_RV + hi]

    @pl.when(step == 0)
    def _():
        for h in range(RET_HEADS):
            qr, kr, v = ret_qkv(h)
            qr_scr[h] = qr
            kt_scr[h] = kr.T
            v_scr[h] = v.astype(BF16)

    lane = lax.broadcasted_iota(jnp.int32, (LANES, nb), 1)

    def body(bb, carry):
        b = step * DEC_BLOCK + bb
        sel = lane == b
        for h in range(RET_HEADS):
            s_old = s_ref[bb, h]
            q16 = jnp.broadcast_to(qr_scr[h, pl.ds(b, 1), :], (2 * SUBLANES, RET_DIM)).astype(BF16)
            qs_scr[h, pl.ds(b, 1), :] = _dot(q16, s_old.astype(BF16))[0:1, :]
            k_m = jnp.where(sel, kt_scr[h], 0.0).astype(BF16)
            so_ref[bb, h] = s_old * dch_ref[h:h + 1, :] + _dot(k_m, v_scr[h])
        return carry

    lax.fori_loop(0, DEC_BLOCK, body, 0)

    @pl.when(step == pl.num_programs(0) - 1)
    def _():
        for h in range(RET_HEADS):
            lo, hi = h * RET_DIM, (h + 1) * RET_DIM
            qr, kr, v = ret_qkv(h)
            g = p_ref[:, OFF_RG + lo:OFF_RG + hi]
            o = jnp.sum(qr * kr, axis=-1, keepdims=True) * v + qs_scr[h] * dch_ref[h:h + 1, :]
            out = _rms(o) * gret_ref[:, lo:hi] * (g * jax.nn.sigmoid(g))
            mix_ref[:, lo:hi] = out.astype(BF16)
        u = p_ref[:, OFF_PU:OFF_PU + POOL_WIDTH]
        for g, win in enumerate(POOL_WINDOWS):
            lo, hi = g * POOL_GROUP, (g + 1) * POOL_GROUP
            wsum = u[:, lo:hi]
            for s in range(1, win):
                wsum = wsum + pool_ref[POOL_BUF - s, :, lo:hi]
            z = wsum / float(win) - u[:, lo:hi]
            y = _dot(z.astype(BF16), wpool_ref[g]) * pscale_ref[:, lo:hi]
            mixp_ref[:, lo:hi] = y.astype(BF16)
        for r in range(POOL_BUF - 1):
            poolo_ref[r] = pool_ref[r + 1]
        poolo_ref[POOL_BUF - 1] = u


def _decode_ret_pool(proj, tables, gains, layer, s_ret, pool_t, stacked_prev):
    nb = proj.shape[0]
    const2 = lambda s: (0, 0)
    const3 = lambda s: (0, 0, 0)
    full2 = lambda shape: pl.BlockSpec(shape, const2)
    s_blk = pl.BlockSpec((None, DEC_BLOCK, RET_HEADS, RET_DIM, RET_DIM),
                         lambda s: (layer, s, 0, 0, 0))
    pool_blk = (POOL_BUF, nb, POOL_WIDTH)
    in_specs = [
        full2((nb, PROJ_PAD)),
        full2((SUBLANES, RET_DIM)),
        full2((SUBLANES, RET_DIM)),
        full2((SUBLANES, LANES)),
        full2((1, RET_WIDTH)),
        pl.BlockSpec((len(POOL_WINDOWS), POOL_GROUP, POOL_GROUP), const3),
        full2((1, POOL_WIDTH)),
        s_blk,
        pl.BlockSpec((None,) + pool_blk, lambda s: (layer, 0, 0, 0)),
    ]
    assert len(in_specs) == DEC_RET_INPUTS
    in_specs += [pl.BlockSpec(memory_space=pl.ANY)] * len(stacked_prev)
    if stacked_prev:
        so_blk = s_blk
    else:
        assert layer == 0
        so_blk = pl.BlockSpec((s_ret.shape[0],) + s_blk.block_shape[1:],
                              lambda s: (0, s, 0, 0, 0))
    out_shape = [
        jax.ShapeDtypeStruct((nb, RET_WIDTH), BF16),
        jax.ShapeDtypeStruct((nb, POOL_WIDTH), BF16),
        jax.ShapeDtypeStruct(s_ret.shape, F32),
        jax.ShapeDtypeStruct(pool_blk, F32),
    ]
    out_specs = [
        full2((nb, RET_WIDTH)),
        full2((nb, POOL_WIDTH)),
        so_blk,
        pl.BlockSpec(pool_blk, const3),
    ]
    scratch = [
        pltpu.VMEM((RET_HEADS, nb, RET_DIM), F32),
        pltpu.VMEM((RET_HEADS, RET_DIM, nb), F32),
        pltpu.VMEM((RET_HEADS, nb, RET_DIM), BF16),
        pltpu.VMEM((RET_HEADS, nb, RET_DIM), F32),
    ]
    return pl.pallas_call(
        functools.partial(_decode_ret_pool_kernel, n_aliased=len(stacked_prev)),
        grid=(nb // DEC_BLOCK,),
        in_specs=in_specs,
        out_specs=out_specs,
        out_shape=out_shape,
        scratch_shapes=scratch,
        input_output_aliases={DEC_RET_INPUTS + i: 2 + i for i in range(len(stacked_prev))},
        compiler_params=pltpu.CompilerParams(
            dimension_semantics=("arbitrary",), vmem_limit_bytes=VMEM_LIMIT),
        name="decode_ret_pool",
    )(proj, tables["cos_s"], tables["sin_s"], tables["dch"], gains["g_ret"], gains["w_pool"],
      gains["pool_scale"], s_ret, pool_t, *stacked_prev)


def _sample_mlstm_update(h, j, last_j, pt_ref, gb_ref, gml_ref, c_ref, n_ref, m_ref,
                         mix_ref, co_ref, no_ref, mo_ref, rows_scr, wk_scr, qc_scr):
    dblk = c_ref.shape[0]
    q_row0 = pl.multiple_of(h * ML_QK_PAD, ML_QK_PAD)
    k_row0 = pl.multiple_of(OFF_MK - OFF_MQ + h * ML_QK_PAD, ML_QK_PAD)
    v_row0 = pl.multiple_of(OFF_MV - OFF_MQ + h * ML_V_PAD, ML_V_PAD)
    o_row0 = pl.multiple_of(OFF_MO - OFF_MQ + h * ML_V_PAD, ML_V_PAD)

    @pl.when(j == 0)
    def _():
        i_pre = pt_ref[pl.ds(GATE_I_LANE + h, 1), :] + gb_ref[pl.ds(h, 1), :]
        lf = _log_sigmoid(pt_ref[pl.ds(GATE_F_LANE + h, 1), :] + gb_ref[pl.ds(ML_HEADS + h, 1), :])
        inter = lf + m_ref[pl.ds(h, 1), :]
        m_new = jnp.maximum(inter, i_pre)
        w = jnp.exp(i_pre - m_new)
        rows_scr[0:1, :] = jnp.exp(inter - m_new)
        rows_scr[1:2, :] = w
        rows_scr[2:3, :] = m_new
        wk_scr[...] = w * (pt_ref[pl.ds(k_row0, ML_QK), :] * (ML_QK ** -0.5))
        qc_scr[...] = jnp.zeros_like(qc_scr)

    a = rows_scr[0:1, :]
    v_t = pt_ref[pl.ds(v_row0, ML_V), :]
    qc = qc_scr[...]
    for dd in range(dblk):
        d = j * dblk + dd
        c_old = c_ref[dd]
        co_ref[dd] = a * c_old + wk_scr[pl.ds(d, 1), :] * v_t
        qc = qc + pt_ref[pl.ds(q_row0 + d, 1), :] * c_old
    qc_scr[...] = qc

    @pl.when(j == last_j)
    def _():
        w = rows_scr[1:2, :]
        m_new = rows_scr[2:3, :]
        q_t = pt_ref[pl.ds(q_row0, ML_QK), :]
        k_t = pt_ref[pl.ds(k_row0, ML_QK), :] * (ML_QK ** -0.5)
        n_old = n_ref[...]
        s = jnp.sum(q_t * k_t, axis=0, keepdims=True) * w
        num = s * v_t + a * qc_scr[...]
        den = s + a * jnp.sum(q_t * n_old, axis=0, keepdims=True)
        hh = num / jnp.maximum(jnp.abs(den), jnp.exp(-m_new))
        scale = lax.rsqrt(jnp.sum(hh * hh, axis=0, keepdims=True) / ML_V + EPS)
        out_t = (hh * scale * gml_ref[0:ML_V, :]
                 * jax.nn.sigmoid(pt_ref[pl.ds(o_row0, ML_V), :]))
        out_t = jnp.concatenate([out_t, jnp.zeros((ML_V_PAD - ML_V, out_t.shape[1]), F32)], axis=0)
        mix_ref[...] = out_t.T.astype(BF16)
        no_ref[...] = a * n_old + wk_scr[...]
        mo_ref[pl.ds(h, 1), :] = m_new


def _tables(seq):
    half = RET_DIM // 2
    inv_freq = ROPE_THETA ** (-jnp.arange(half, dtype=F32) / half)

    def cos_sin(pos):
        ang = pos[:, None] * inv_freq[None, :]
        cos, sin = jnp.cos(ang), jnp.sin(ang)
        return jnp.concatenate([cos, cos], axis=-1), jnp.concatenate([-sin, sin], axis=-1)

    cos_p, sin_p = cos_sin(jnp.arange(seq, dtype=F32))
    cos_s, sin_s = cos_sin((PAST_LEN + jnp.arange(1)).astype(F32))
    log_gamma = jnp.log1p(-jnp.exp2(-5.0 - jnp.arange(RET_HEADS, dtype=F32)))
    idx = jnp.arange(CHUNK, dtype=F32)
    diff = idx[:, None] - idx[None, :]
    causal = diff >= 0
    lg = log_gamma[:, None, None]
    dintra = jnp.where(causal, jnp.exp(lg * jnp.where(causal, diff, 0.0)), 0.0)
    dq = jnp.exp(log_gamma[:, None] * (idx + 1.0))
    dk = jnp.exp(log_gamma[:, None] * (CHUNK - 1.0 - idx))

    def pad_rows(t, rows):
        return jnp.pad(t, ((0, rows - t.shape[0]), (0, 0)))

    def chunk_decay(c):
        return pad_rows(jnp.broadcast_to(jnp.exp(log_gamma * c)[:, None], (RET_HEADS, LANES)), SUBLANES)

    return {
        "cos": cos_p, "sin": sin_p,
        "cos_s": jnp.broadcast_to(cos_s, (SUBLANES, RET_DIM)),
        "sin_s": jnp.broadcast_to(sin_s, (SUBLANES, RET_DIM)),
        "dintra": dintra,
        "dq": jnp.pad(dq.T, ((0, 0), (0, LANES - RET_HEADS))),
        "dk": pad_rows(dk, SUBLANES),
        "dch": chunk_decay(float(CHUNK)),
        "dch_s": chunk_decay(1.0),
    }


def _pad_last(t, width):
    return jnp.pad(t, [(0, 0)] * (t.ndim - 1) + [(0, width - t.shape[-1])])


def _prep_gains(l, g_ret_norm, g_mlstm_norm, b_mlstm_i, b_mlstm_f, w_pool, pool_scale):
    gate_bias = jnp.concatenate([b_mlstm_i[l], b_mlstm_f[l]])
    g_ml = _pad_last(g_mlstm_norm[l].reshape(ML_HEADS, ML_V), ML_V_PAD)
    lanes = jnp.concatenate([jnp.zeros((GATE_I_LANE,), F32), _pad_last(gate_bias, LANES - GATE_I_LANE)])
    return {
        "g_ret": g_ret_norm[l][None, :],
        "g_ml": g_ml.reshape(1, -1),
        "g_ml_t": jnp.broadcast_to(g_ml[:, :, None], (ML_HEADS, ML_V_PAD, LANES)),
        "gb_r": jnp.broadcast_to(gate_bias[:, None], (SUBLANES, LANES)),
        "gb_c": jnp.broadcast_to(lanes[None, :], (SUBLANES, LANES)),
        "w_pool": w_pool[l].astype(BF16),
        "pool_scale": pool_scale[l][None, :],
    }


@jax.jit
def kernel(x_prompt, x_sample, state_ret, state_mlstm_c, state_mlstm_n, state_mlstm_m, state_pool,
           w_in, w_out, g_ret_norm, g_mlstm_norm, b_mlstm_i, b_mlstm_f, w_pool, pool_scale,
           g_pre_mix, g_post_mix, g_pre_ffn, g_post_ffn, w_ffn_gate, w_ffn_up, w_ffn_down):
    batch, seq, d = x_prompt.shape
    nb = x_sample.shape[0]
    depth = w_in.shape[0]
    tables = _tables(seq)
    tables_s = dict(tables, dch=tables["dch_s"])
    w_in_t = jnp.swapaxes(w_in, 1, 2)

    yp = x_prompt.reshape(batch * seq, d)
    ys = x_sample.reshape(nb, d)
    new_p, new_s = [], []
    stacked_ret, stacked_c = (), ()
    c_t = jnp.transpose(state_mlstm_c, (0, 2, 3, 4, 1))
    n_t = jnp.transpose(state_mlstm_n, (0, 2, 3, 1))
    m_t = jnp.transpose(state_mlstm_m, (0, 2, 1))
    pool_t = jnp.transpose(state_pool, (0, 2, 1, 3))

    t = _token_tiles(batch * seq)
    for l in range(depth):
        row = lambda g: g[l][None, :]
        w_in_l, proj, proj_t = _prep_w_in(w_in_t, l, ys, row(g_pre_mix))
        w_out_l = _prep_w_out(w_out, l)
        gains = _prep_gains(l, g_ret_norm, g_mlstm_norm, b_mlstm_i, b_mlstm_f, w_pool, pool_scale)

        (mixed, s_ret, s_c, s_n, s_m, s_pool), (mix_ml, s_c_s, n_new, m_new) = _prompt_front(
            yp.reshape(batch, seq, d), row(g_pre_mix), w_in_l, tables, gains,
            l, proj_t, c_t, n_t, m_t, stacked_c)
        new_p.append((s_ret, s_c[:, :, :ML_QK, :ML_V], s_n[:, :ML_HEADS, :ML_QK],
                      s_m[:, :ML_HEADS, 0], s_pool[:, 1:, :]))

        mix_ret, mix_pool, s_ret, pool_new = _decode_ret_pool(
            proj, tables_s, gains, l, state_ret, pool_t, stacked_ret)
        s_c = s_c_s
        mixed_s = jnp.concatenate(
            [mix_ret, jnp.swapaxes(mix_ml, 0, 1).reshape(nb, ML_HEADS * ML_V_PAD), mix_pool], axis=1)
        stacked_ret, stacked_c = (s_ret,), (s_c,)
        new_s.append((n_new, m_new, pool_new))

        x1, x1_s = _outproj(mixed, mixed_s, w_out_l, yp, ys, row(g_post_mix),
                            tm=t["outproj_tm"])
        yp, ys = _ffn(x1, x1_s, row(g_pre_ffn), w_ffn_gate, w_ffn_up, w_ffn_down, row(g_post_ffn),
                      l, tm=t["ffn_tm"], tf=t["ffn_tf"])

    stack = lambda states, i: jnp.stack([s[i] for s in states], axis=0)
    return ((yp.reshape(batch, seq, d), ys.reshape(nb, 1, d))
            + tuple(stack(new_p, i) for i in range(5))
            + (stacked_ret[0],
               jnp.transpose(stacked_c[0], (0, 4, 1, 2, 3)),
               jnp.transpose(stack(new_s, 0), (0, 3, 1, 2)),
               jnp.transpose(stack(new_s, 1), (0, 2, 1)),
               jnp.transpose(stack(new_s, 2), (0, 2, 1, 3))))
```

```python
import functools

import jax
import jax.numpy as jnp
from jax import lax
from jax.experimental import pallas as pl
from jax.experimental.pallas import tpu as pltpu

F32 = jnp.float32
BF16 = jnp.bfloat16

D_MODEL = 2048
PAST_LEN = 16384
RET_HEADS = 6
RET_DIM = 128
RET_WIDTH = RET_HEADS * RET_DIM
ML_HEADS = 4
ML_QK = 96
ML_V = 192
ML_WIDTH = ML_HEADS * ML_V
POOL_WIDTH = 512
POOL_WINDOWS = (2, 4, 8, 16)
POOL_GROUP = POOL_WIDTH // len(POOL_WINDOWS)
POOL_BUF = max(POOL_WINDOWS) - 1
D_FF = 5632
CHUNK = 128
ROPE_THETA = 10000.0
EPS = 1e-6

LANES = 128
SUBLANES = 8
VMEM_LIMIT = 60 * 1024 * 1024

ML_QK_PAD = LANES
ML_V_PAD = 2 * LANES
OFF_RQ = 0
OFF_RK = OFF_RQ + RET_WIDTH
OFF_RV = OFF_RK + RET_WIDTH
OFF_RG = OFF_RV + RET_WIDTH
OFF_MQ = OFF_RG + RET_WIDTH
OFF_MK = OFF_MQ + ML_HEADS * ML_QK_PAD
OFF_MV = OFF_MK + ML_HEADS * ML_QK_PAD
OFF_MO = OFF_MV + ML_HEADS * ML_V_PAD
OFF_PU = OFF_MO + ML_HEADS * ML_V_PAD
PROJ_PAD = OFF_PU + POOL_WIDTH
GATE_I_LANE = ML_QK
GATE_F_LANE = ML_QK + ML_HEADS
MIX_OFF_ML = RET_WIDTH
MIX_OFF_POOL = MIX_OFF_ML + ML_HEADS * ML_V_PAD
MIX_PAD = MIX_OFF_POOL + POOL_WIDTH

_HI = lax.Precision.HIGHEST


def _token_tiles(m):
    return {
        "outproj_tm": min(m, 512),
        "ffn_tm": min(m, 1024), "ffn_tf": 256,
    }


def _dot(a, b):
    return jnp.dot(a, b, preferred_element_type=F32)


def _rms(x, width=None):
    n = x.shape[-1] if width is None else width
    return x * lax.rsqrt(jnp.sum(x * x, axis=-1, keepdims=True) / n + EPS)


def _log_sigmoid(x):
    return jnp.minimum(x, 0.0) - jnp.log1p(jnp.exp(-jnp.abs(x)))


def _rotate(x, cos2, sin2):
    return x * cos2 + pltpu.roll(x, RET_DIM // 2, 1) * sin2


def _outproj_kernel(m_ref, ms_ref, w_ref, x_ref, xs_ref, gpost_ref, x1_ref, x1s_ref):
    @pl.when(pl.program_id(0) == 0)
    def _():
        x1s_ref[...] = xs_ref[...] + _rms(_dot(ms_ref[...], w_ref[...])) * gpost_ref[...]

    x1_ref[...] = x_ref[...] + _rms(_dot(m_ref[...], w_ref[...])) * gpost_ref[...]


def _outproj(mixed, mixed_s, w, x, x_s, g_post, tm):
    m, ns = x.shape[0], x_s.shape[0]
    return pl.pallas_call(
        _outproj_kernel,
        grid=(m // tm,),
        in_specs=[
            pl.BlockSpec((tm, MIX_PAD), lambda i: (i, 0)),
            pl.BlockSpec((ns, MIX_PAD), lambda i: (0, 0)),
            pl.BlockSpec((MIX_PAD, D_MODEL), lambda i: (0, 0), pipeline_mode=pl.Buffered(1)),
            pl.BlockSpec((tm, D_MODEL), lambda i: (i, 0)),
            pl.BlockSpec((ns, D_MODEL), lambda i: (0, 0)),
            pl.BlockSpec((1, D_MODEL), lambda i: (0, 0)),
        ],
        out_specs=[
            pl.BlockSpec((tm, D_MODEL), lambda i: (i, 0)),
            pl.BlockSpec((ns, D_MODEL), lambda i: (0, 0)),
        ],
        out_shape=[
            jax.ShapeDtypeStruct((m, D_MODEL), F32),
            jax.ShapeDtypeStruct((ns, D_MODEL), F32),
        ],
        compiler_params=pltpu.CompilerParams(
            dimension_semantics=("arbitrary",), vmem_limit_bytes=VMEM_LIMIT),
        name="outproj",
    )(mixed, mixed_s, w, x, x_s, g_post)


FFN_ROW_CHUNK = 256


def _ffn_kernel(x1_ref, x1s_ref, gpre_ref, wg_ref, wu_ref, wd_ref, gpost_ref, o_ref, os_ref,
                hn_ref, hns_ref):
    first_tile = pl.program_id(0) == 0
    f = pl.program_id(1)
    last = pl.num_programs(1) - 1

    def swiglu_down(hn, wg, wu, wd):
        gate = _dot(hn, wg)
        act = (gate * jax.nn.sigmoid(gate) * _dot(hn, wu)).astype(BF16)
        return _dot(act, wd)

    def middle_step(h_ref, acc_ref):
        acc_ref[...] += swiglu_down(h_ref[...], wg_ref[...].astype(BF16),
                                    wu_ref[...].astype(BF16), wd_ref[...].astype(BF16))

    def row_chunks(ref):
        rows = ref.shape[0]
        chunk = min(rows, FFN_ROW_CHUNK)
        return [slice(r, r + chunk) for r in range(0, rows, chunk)]

    def first_step(x_ref, h_ref, acc_ref):
        wg, wu, wd = (w[...].astype(BF16) for w in (wg_ref, wu_ref, wd_ref))
        for rows in row_chunks(x_ref):
            hn = (_rms(x_ref[rows, :]) * gpre_ref[...]).astype(BF16)
            h_ref[rows, :] = hn
            acc_ref[rows, :] = swiglu_down(hn, wg, wu, wd)

    def last_step(x_ref, h_ref, acc_ref):
        wg, wu, wd = (w[...].astype(BF16) for w in (wg_ref, wu_ref, wd_ref))
        for rows in row_chunks(x_ref):
            acc = acc_ref[rows, :] + swiglu_down(h_ref[rows, :], wg, wu, wd)
            acc_ref[rows, :] = x_ref[rows, :] + _rms(acc) * gpost_ref[...]

    middle = (f > 0) & (f < last)
    for on_tile, refs in ((first_tile, (x1s_ref, hns_ref, os_ref)), (True, (x1_ref, hn_ref, o_ref))):
        pl.when(on_tile & (f == 0))(functools.partial(first_step, *refs))
        pl.when(on_tile & middle)(functools.partial(middle_step, *refs[1:]))
        pl.when(on_tile & (f == last))(functools.partial(last_step, *refs))


def _ffn(x1, x1_s, g_pre, wg, wu, wd, g_post, layer, tm, tf):
    m, ns = x1.shape[0], x1_s.shape[0]
    return pl.pallas_call(
        _ffn_kernel,
        grid=(m // tm, D_FF // tf),
        in_specs=[
            pl.BlockSpec((tm, D_MODEL), lambda i, f: (i, 0)),
            pl.BlockSpec((ns, D_MODEL), lambda i, f: (0, 0)),
            pl.BlockSpec((1, D_MODEL), lambda i, f: (0, 0)),
            pl.BlockSpec((None, D_MODEL, tf), lambda i, f: (layer, 0, f)),
            pl.BlockSpec((None, D_MODEL, tf), lambda i, f: (layer, 0, f)),
            pl.BlockSpec((None, tf, D_MODEL), lambda i, f: (layer, f, 0)),
            pl.BlockSpec((1, D_MODEL), lambda i, f: (0, 0)),
        ],
        out_specs=[
            pl.BlockSpec((tm, D_MODEL), lambda i, f: (i, 0)),
            pl.BlockSpec((ns, D_MODEL), lambda i, f: (0, 0)),
        ],
        out_shape=[
            jax.ShapeDtypeStruct((m, D_MODEL), F32),
            jax.ShapeDtypeStruct((ns, D_MODEL), F32),
        ],
        scratch_shapes=[pltpu.VMEM((tm, D_MODEL), BF16), pltpu.VMEM((ns, D_MODEL), BF16)],
        compiler_params=pltpu.CompilerParams(
            dimension_semantics=("arbitrary", "arbitrary"), vmem_limit_bytes=VMEM_LIMIT),
        name="ffn",
    )(x1, x1_s, g_pre, wg, wu, wd, g_post)


SRC_MQ = 4 * RET_WIDTH
SRC_MK = SRC_MQ + ML_HEADS * ML_QK
SRC_MV = SRC_MK + ML_HEADS * ML_QK
SRC_MO = SRC_MV + ML_WIDTH
SRC_MI = SRC_MO + ML_WIDTH
SRC_MF = SRC_MI + ML_HEADS
SRC_PU = SRC_MF + ML_HEADS
IN_WIDTH = SRC_PU + POOL_WIDTH


def _prep_w_in_kernel(w_ref, xs_ref, g_ref, o_ref, ps_ref, pst_ref, xn_scr):
    tk = w_ref.shape[1]
    kb = pl.program_id(0)

    @pl.when(kb == 0)
    def _():
        xn = (_rms(xs_ref[...]) * g_ref[...]).astype(BF16)
        for n in range(xn_scr.shape[0]):
            xn_scr[n] = xn[:, n * tk:(n + 1) * tk]
        ps_ref[...] = jnp.zeros_like(ps_ref)

    def put(dst, pieces, width):
        rows = [w_ref[src:src + n, :] for src, n in pieces]
        have = sum(n for _, n in pieces)
        if have < width:
            rows.append(jnp.zeros((width - have, tk), F32))
        block = rows[0] if len(rows) == 1 else jnp.concatenate(rows, axis=0)
        o_ref[:, dst:dst + width] = block.T.astype(BF16)

    for r in range(0, 4 * RET_WIDTH, 2 * LANES):
        put(r, [(r, 2 * LANES)], 2 * LANES)
    for h in range(ML_HEADS):
        q_rows = [(SRC_MQ + h * ML_QK, ML_QK)]
        if h == 0:
            q_rows.append((SRC_MI, 2 * ML_HEADS))
        put(OFF_MQ + h * ML_QK_PAD, q_rows, ML_QK_PAD)
        put(OFF_MK + h * ML_QK_PAD, [(SRC_MK + h * ML_QK, ML_QK)], ML_QK_PAD)
        put(OFF_MV + h * ML_V_PAD, [(SRC_MV + h * ML_V, ML_V)], ML_V_PAD)
        put(OFF_MO + h * ML_V_PAD, [(SRC_MO + h * ML_V, ML_V)], ML_V_PAD)
    for r in range(0, POOL_WIDTH, 2 * LANES):
        put(OFF_PU + r, [(SRC_PU + r, 2 * LANES)], 2 * LANES)

    ps_ref[...] += _dot(xn_scr[kb], o_ref[...])

    @pl.when(kb == pl.num_programs(0) - 1)
    def _():
        pst_ref[...] = ps_ref[...].T


def _prep_w_in(w_t, layer, x_s, g_pre, tk=256):
    d = w_t.shape[2]
    ns = x_s.shape[0]
    return pl.pallas_call(
        _prep_w_in_kernel,
        grid=(d // tk,),
        in_specs=[
            pl.BlockSpec((None, IN_WIDTH, tk), lambda i: (layer, 0, i)),
            pl.BlockSpec((ns, d), lambda i: (0, 0)),
            pl.BlockSpec((1, d), lambda i: (0, 0)),
        ],
        out_specs=[
            pl.BlockSpec((tk, PROJ_PAD), lambda i: (i, 0)),
            pl.BlockSpec((ns, PROJ_PAD), lambda i: (0, 0)),
            pl.BlockSpec((PROJ_PAD, ns), lambda i: (0, 0)),
        ],
        out_shape=[
            jax.ShapeDtypeStruct((d, PROJ_PAD), BF16),
            jax.ShapeDtypeStruct((ns, PROJ_PAD), F32),
            jax.ShapeDtypeStruct((PROJ_PAD, ns), F32),
        ],
        scratch_shapes=[pltpu.VMEM((d // tk, ns, tk), BF16)],
        compiler_params=pltpu.CompilerParams(
            dimension_semantics=("arbitrary",), vmem_limit_bytes=VMEM_LIMIT),
        name="prep_w_in",
    )(w_t, x_s, g_pre)


def _prep_w_out_kernel(w_ref, o_ref):
    def put(dst, src, rows):
        o_ref[dst:dst + rows, :] = w_ref[src:src + rows, :].astype(BF16)

    put(0, 0, RET_WIDTH)
    for h in range(ML_HEADS):
        dst = MIX_OFF_ML + h * ML_V_PAD
        put(dst, RET_WIDTH + h * ML_V, ML_V)
        o_ref[dst + ML_V:dst + ML_V_PAD, :] = jnp.zeros((ML_V_PAD - ML_V, o_ref.shape[1]), BF16)
    put(MIX_OFF_POOL, RET_WIDTH + ML_WIDTH, POOL_WIDTH)


def _prep_w_out(w, layer, tn=512):
    _, k, n = w.shape
    return pl.pallas_call(
        _prep_w_out_kernel,
        grid=(n // tn,),
        in_specs=[pl.BlockSpec((None, k, tn), lambda j: (layer, 0, j))],
        out_specs=pl.BlockSpec((MIX_PAD, tn), lambda j: (0, j)),
        out_shape=jax.ShapeDtypeStruct((MIX_PAD, n), BF16),
        compiler_params=pltpu.CompilerParams(
            dimension_semantics=("parallel",), vmem_limit_bytes=VMEM_LIMIT),
        name="prep_w_out",
    )(w)


PROMPT_BLOCK = 2


def _ret_head(q, k, get_v, get_g, cos2, sin2, dintra, dq_col, dk_row, dch_row, g_norm, s_old,
              store):
    qb = _rotate(q, cos2, sin2).astype(BF16)
    kr_t = (_rotate(k, cos2, sin2) * (RET_DIM ** -0.5)).T
    kb = kr_t.astype(BF16)
    kdb = (kr_t * dk_row).astype(BF16)
    yield
    scores = (_dot(qb, kb) * dintra).astype(BF16)
    vb = get_v().astype(BF16)
    g = get_g()
    gate = g * jax.nn.sigmoid(g)
    yield
    o = _dot(scores, vb) + _dot(qb, s_old.astype(BF16)) * dq_col
    s_new = s_old * dch_row + _dot(kdb, vb)
    store(_rms(o) * g_norm * gate, s_new)


def _ml_head(q, k, vb, og, causal, b_c, i_c, b_r, i_r, g_norm, c_old, n_old, m_prev, store):
    k_t = k.T
    qb = q.astype(BF16)
    logw = jnp.where(causal, b_c - b_r + i_r, -jnp.inf)
    inter = b_c + m_prev
    m_t = jnp.maximum(inter, jnp.max(logw, axis=-1, keepdims=True))
    w = jnp.exp(logw - m_t)
    a = jnp.exp(inter - m_t)
    m_new = m_t[CHUNK - 1:CHUNK, :]
    b_last = b_c[CHUNK - 1:CHUNK, :]
    w_r = jnp.exp(b_last - b_r + i_r - m_new)
    w_c = jnp.exp(b_last - b_c + i_c - m_new)
    a_end = jnp.exp(b_last + m_prev - m_new)
    kwb = (k_t * w_r).astype(BF16)
    n_new = a_end * n_old + jnp.sum(k * w_c, axis=0, keepdims=True)
    qn = jnp.sum(q * n_old, axis=-1, keepdims=True)
    floor = jnp.exp(-m_t)
    gate = jax.nn.sigmoid(og)
    yield
    s = _dot(qb, k_t.astype(BF16)) * w
    den = jnp.sum(s, axis=-1, keepdims=True) + a * qn
    sb = s.astype(BF16)
    yield
    num = _dot(sb, vb) + a * _dot(qb, c_old.astype(BF16))
    c_new = a_end * c_old + _dot(kwb, vb)
    hh = num / jnp.maximum(jnp.abs(den), floor)
    store(_rms(hh, ML_V) * g_norm * gate, c_new, n_new, m_new)


FRONT_INPUTS = 21


def _prompt_front_kernel(*refs, n_aliased):
    (x_ref, xnext_ref, gpre_ref, w_ref, cos_ref, sin_ref, dintra_ref, dq_ref, dk_ref, dch_ref,
     gret_ref, gml_ref, gb_r_ref, gb_c_ref, wpool_ref, pscale_ref,
     pt_ref, gmlt_ref, cs_ref, ns_ref, ms_ref) = refs[:FRONT_INPUTS]
    (mix_ref, sret_ref, sc_ref, sn_ref, sm_ref, spool_ref, mixs_ref, cso_ref, nso_ref, mso_ref,
     s_scr, c_scr, n_scr, m_scr, u_scr, rows_scr, wk_scr, qc_scr,
     xn_scr, pq_scr, pk_scr) = refs[FRONT_INPUTS + n_aliased:]
    bp = pl.program_id(0)
    c = pl.program_id(1)
    nseq = x_ref.shape[0]

    if n_aliased == 0:
        for later in range(1, cso_ref.shape[0]):
            cso_ref[later] = jnp.zeros(cso_ref.shape[1:], F32)
        cso_ref = cso_ref.at[0]
    blocks_per_head = ML_QK // cs_ref.shape[0]
    step = bp * pl.num_programs(1) + c
    _sample_mlstm_update(step // blocks_per_head, step % blocks_per_head, blocks_per_head - 1,
                         pt_ref, gb_r_ref, gmlt_ref, cs_ref, ns_ref, ms_ref,
                         mixs_ref, cso_ref, nso_ref, mso_ref, rows_scr, wk_scr, qc_scr)

    @pl.when(c == 0)
    def _():
        s_scr[...] = jnp.zeros_like(s_scr)
        c_scr[...] = jnp.zeros_like(c_scr)
        n_scr[...] = jnp.zeros_like(n_scr)
        m_scr[...] = jnp.zeros_like(m_scr)
        u_scr[:, 0:2 * SUBLANES, :] = jnp.zeros((nseq, 2 * SUBLANES, POOL_WIDTH), F32)

    def normalised(ref):
        return (_rms(ref[...].reshape(nseq * CHUNK, D_MODEL)) * gpre_ref[...]).astype(BF16)

    first_q = (OFF_RQ, 2 * RET_DIM)
    first_k = (OFF_RK, 2 * RET_DIM)

    @pl.when(step == 0)
    def _():
        xn0 = normalised(x_ref)
        xn_scr[...] = xn0
        pq_scr[...] = _dot(xn0, w_ref[:, first_q[0]:first_q[0] + first_q[1]])
        pk_scr[...] = _dot(xn0, w_ref[:, first_k[0]:first_k[0] + first_k[1]])

    xn = xn_scr[...]

    def proj(lo, width):
        return _dot(xn, w_ref[:, lo:lo + width])

    def seq_rows(t, i):
        return t[i * CHUNK:(i + 1) * CHUNK]

    cos2 = cos_ref[...]
    sin2 = sin_ref[...]
    pair = 2 * RET_DIM

    def ret_project(h0):
        return [functools.partial(proj, off + h0 * RET_DIM, pair)
                for off in (OFF_RQ, OFF_RK, OFF_RV, OFF_RG)]

    def ret_mix(h0, projected):
        pq, pk = projected[:2]
        instances = []
        for h in (h0, h0 + 1):
            lo, hi = (h - h0) * RET_DIM, (h - h0 + 1) * RET_DIM
            for i in range(nseq):
                def store(out, s_new, h=h, i=i):
                    s_scr[i, h] = s_new
                    mix_ref[i, :, h * RET_DIM:(h + 1) * RET_DIM] = out.astype(BF16)

                def late(n, i=i, lo=lo, hi=hi):
                    return seq_rows(projected[n], i)[:, lo:hi]

                instances.append(_ret_head(
                    seq_rows(pq, i)[:, lo:hi], seq_rows(pk, i)[:, lo:hi],
                    functools.partial(late, 2), functools.partial(late, 3),
                    cos2, sin2, dintra_ref[h], dq_ref[:, h:h + 1], dk_ref[h:h + 1, :],
                    dch_ref[h:h + 1, :], gret_ref[:, h * RET_DIM:(h + 1) * RET_DIM], s_scr[i, h],
                    store))
        return instances

    row = lax.broadcasted_iota(jnp.int32, (CHUNK, CHUNK), 0)
    col = lax.broadcasted_iota(jnp.int32, (CHUNK, CHUNK), 1)
    causal = col <= row
    upper = (row <= col).astype(F32)
    lower = causal.astype(F32)
    gates = [None] * nseq

    def ml_project(h0):
        return [functools.partial(proj, OFF_MQ + h0 * ML_QK_PAD, 2 * ML_QK_PAD),
                functools.partial(proj, OFF_MK + h0 * ML_QK_PAD, 2 * ML_QK_PAD),
                functools.partial(proj, OFF_MV + h0 * ML_V_PAD, 2 * ML_V_PAD),
                functools.partial(proj, OFF_MO + h0 * ML_V_PAD, 2 * ML_V_PAD)]

    def ml_mix(h0, projected):
        pq, pk, pv, po = projected
        if h0 == 0:
            for i in range(nseq):
                gq0 = seq_rows(pq, i)[:, 0:ML_QK_PAD]
                i_r8 = gq0.T[GATE_I_LANE:GATE_I_LANE + SUBLANES, :] + gb_r_ref[...]
                b_r8 = jnp.dot(_log_sigmoid(i_r8), upper, precision=_HI, preferred_element_type=F32)
                pre_c = gq0 + gb_c_ref[0:1, :]
                b_c_all = jnp.dot(lower, _log_sigmoid(pre_c), precision=_HI,
                                  preferred_element_type=F32)
                gates[i] = (i_r8, b_r8, pre_c, b_c_all)
        instances = []
        for h in (h0, h0 + 1):
            qlo = (h - h0) * ML_QK_PAD
            vlo = (h - h0) * ML_V_PAD
            for i in range(nseq):
                def store(out, c_new, n_new, m_new, h=h, i=i):
                    c_scr[i, h] = c_new
                    n_scr[i, h:h + 1, :] = n_new
                    m_scr[i, h:h + 1, :] = jnp.broadcast_to(m_new, (1, LANES))
                    mix_ref[i, :, MIX_OFF_ML + h * ML_V_PAD:MIX_OFF_ML + (h + 1) * ML_V_PAD] = (
                        out.astype(BF16))

                i_r8, b_r8, pre_c, b_c_all = gates[i]
                instances.append(_ml_head(
                    seq_rows(pq, i)[:, qlo:qlo + ML_QK_PAD],
                    seq_rows(pk, i)[:, qlo:qlo + ML_QK_PAD] * (ML_QK ** -0.5),
                    seq_rows(pv, i)[:, vlo:vlo + ML_V_PAD].astype(BF16),
                    seq_rows(po, i)[:, vlo:vlo + ML_V_PAD],
                    causal,
                    b_c_all[:, GATE_F_LANE + h:GATE_F_LANE + h + 1],
                    pre_c[:, GATE_I_LANE + h:GATE_I_LANE + h + 1],
                    b_r8[ML_HEADS + h:ML_HEADS + h + 1, :], i_r8[h:h + 1, :],
                    gml_ref[:, h * ML_V_PAD:(h + 1) * ML_V_PAD],
                    c_scr[i, h], n_scr[i, h:h + 1, :], m_scr[i, h:h + 1, 0:1], store))
        return instances

    base = 2 * SUBLANES
    pos = (c * CHUNK + lax.broadcasted_iota(jnp.int32, (CHUNK, 1), 0)).astype(F32)

    def pool_seq(u, i):
        u_scr[i, base:base + CHUNK, :] = u
        zs = []
        for g, win in enumerate(POOL_WINDOWS):
            lo, hi = g * POOL_GROUP, (g + 1) * POOL_GROUP
            wsum = u[:, lo:hi]
            for s in range(1, win):
                wsum = wsum + u_scr[i, base - s:base - s + CHUNK, lo:hi]
            zs.append((wsum / jnp.minimum(pos + 1.0, float(win)) - u[:, lo:hi]).astype(BF16))
        u_scr[i, 0:base, :] = u_scr[i, CHUNK:CHUNK + base, :]
        yield
        for g, z in enumerate(zs):
            lo, hi = g * POOL_GROUP, (g + 1) * POOL_GROUP
            y = _dot(z, wpool_ref[g]) * pscale_ref[:, lo:hi]
            mix_ref[i, :, MIX_OFF_POOL + lo:MIX_OFF_POOL + hi] = y.astype(BF16)

    def pool_mix(projected):
        pu, = projected
        return [pool_seq(seq_rows(pu, i), i) for i in range(nseq)]

    stages = [(ret_project(h0), functools.partial(ret_mix, h0)) for h0 in range(0, RET_HEADS, 2)]
    stages += [(ml_project(h0), functools.partial(ml_mix, h0)) for h0 in range(0, ML_HEADS, 2)]
    stages.append(([functools.partial(proj, OFF_PU, POOL_WIDTH)], pool_mix))
    early = 2
    projected = [pq_scr[...], pk_scr[...]]
    late_dots = list(stages[0][0][early:])

    carried = []

    def carry_next(scr, cols):
        if not carried:
            carried.append(normalised(xnext_ref))
            xn_scr[...] = carried[0]
        scr[...] = _dot(carried[0], w_ref[:, cols[0]:cols[0] + cols[1]])

    next_step_dots = [functools.partial(carry_next, pq_scr, first_q),
                      functools.partial(carry_next, pk_scr, first_k)]
    for n, (_, mix) in enumerate(stages):
        pending = list(stages[n + 1][0]) if n + 1 < len(stages) else next_step_dots
        upcoming = []
        live = mix(projected)
        while live:
            for _ in range(max(1, (len(late_dots) + len(pending)) // 2)):
                if late_dots:
                    projected.append(late_dots.pop(0)())
                elif pending:
                    upcoming.append(pending.pop(0)())
            live = [inst for inst in live if next(inst, StopIteration) is not StopIteration]
        upcoming += [dot() for dot in pending]
        projected = upcoming

    @pl.when(c == pl.num_programs(1) - 1)
    def _():
        for i in range(nseq):
            b = bp * nseq + i
            sret_ref[b] = s_scr[i]
            sc_ref[b] = c_scr[i]
            sn_ref[b] = n_scr[i]
            sm_ref[b] = m_scr[i]
            spool_ref[b] = u_scr[i, 0:base, :]


def _prompt_front(x, g_pre, w, tables, gains, layer, proj_t, c_t, n_t, m_t, stacked_prev):
    batch, seq, _ = x.shape
    nb = proj_t.shape[1]
    nchunk = seq // CHUNK
    nseq = PROMPT_BLOCK
    steps = (batch // nseq) * nchunk
    dblk = ML_HEADS * ML_QK // steps
    assert dblk * steps == ML_HEADS * ML_QK and ML_QK % dblk == 0
    per_head = ML_QK // dblk
    head = lambda b, c: (b * nchunk + c) // per_head
    c_blk = pl.BlockSpec((None, None, dblk, ML_V, nb),
                         lambda b, c: (layer, head(b, c), (b * nchunk + c) % per_head, 0, 0))
    if stacked_prev:
        co_blk = c_blk
    else:
        assert layer == 0
        co_blk = pl.BlockSpec((c_t.shape[0], None, dblk, ML_V, nb),
                              lambda b, c: (0, head(b, c), (b * nchunk + c) % per_head, 0, 0))
    ml_rows = OFF_PU - OFF_MQ
    assert OFF_MQ % ml_rows == 0
    const2 = lambda b, c: (0, 0)
    const3 = lambda b, c: (0, 0, 0)
    const4 = lambda b, c: (0, 0, 0, 0)
    sample_in_specs = [
        pl.BlockSpec((ml_rows, nb), lambda b, c: (OFF_MQ // ml_rows, 0),
                     pipeline_mode=pl.Buffered(1)),
        pl.BlockSpec((None, ML_V_PAD, nb), lambda b, c: (head(b, c), 0, 0)),
        c_blk,
        pl.BlockSpec((None, None, ML_QK, nb), lambda b, c: (layer, head(b, c), 0, 0)),
        pl.BlockSpec((None, ML_HEADS, nb), lambda b, c: (layer, 0, 0)),
    ]
    sample_out_shape = [
        jax.ShapeDtypeStruct((ML_HEADS, nb, ML_V_PAD), BF16),
        jax.ShapeDtypeStruct(c_t.shape, F32),
        jax.ShapeDtypeStruct((ML_HEADS, ML_QK, nb), F32),
        jax.ShapeDtypeStruct((ML_HEADS, nb), F32),
    ]
    sample_out_specs = [
        pl.BlockSpec((None, nb, ML_V_PAD), lambda b, c: (head(b, c), 0, 0)),
        co_blk,
        pl.BlockSpec((None, ML_QK, nb), lambda b, c: (head(b, c), 0, 0)),
        pl.BlockSpec((ML_HEADS, nb), const2),
    ]
    sample_scratch = [
        pltpu.VMEM((SUBLANES, nb), F32),
        pltpu.VMEM((ML_QK, nb), F32),
        pltpu.VMEM((ML_V, nb), F32),
    ]
    following = lambda b, c: jnp.minimum(b * nchunk + c + 1, steps - 1)
    in_specs = [
        pl.BlockSpec((nseq, CHUNK, D_MODEL), const3, pipeline_mode=pl.Buffered(1)),
        pl.BlockSpec((nseq, CHUNK, D_MODEL),
                     lambda b, c: (following(b, c) // nchunk, following(b, c) % nchunk, 0)),
        pl.BlockSpec((1, D_MODEL), const2),
        pl.BlockSpec((D_MODEL, PROJ_PAD), const2, pipeline_mode=pl.Buffered(1)),
        pl.BlockSpec((CHUNK, RET_DIM), lambda b, c: (c, 0)),
        pl.BlockSpec((CHUNK, RET_DIM), lambda b, c: (c, 0)),
        pl.BlockSpec((RET_HEADS, CHUNK, CHUNK), const3),
        pl.BlockSpec((CHUNK, LANES), const2),
        pl.BlockSpec((SUBLANES, CHUNK), const2),
        pl.BlockSpec((SUBLANES, LANES), const2),
        pl.BlockSpec((1, RET_WIDTH), const2),
        pl.BlockSpec((1, ML_HEADS * ML_V_PAD), const2),
        pl.BlockSpec((SUBLANES, LANES), const2),
        pl.BlockSpec((SUBLANES, LANES), const2),
        pl.BlockSpec((len(POOL_WINDOWS), POOL_GROUP, POOL_GROUP), const3),
        pl.BlockSpec((1, POOL_WIDTH), const2),
    ] + sample_in_specs
    assert len(in_specs) == FRONT_INPUTS
    in_specs += [pl.BlockSpec(memory_space=pl.ANY)] * len(stacked_prev)
    out_shape = [
        jax.ShapeDtypeStruct((batch, seq, MIX_PAD), BF16),
        jax.ShapeDtypeStruct((batch, RET_HEADS, RET_DIM, RET_DIM), F32),
        jax.ShapeDtypeStruct((batch, ML_HEADS, ML_QK_PAD, ML_V_PAD), F32),
        jax.ShapeDtypeStruct((batch, SUBLANES, LANES), F32),
        jax.ShapeDtypeStruct((batch, SUBLANES, LANES), F32),
        jax.ShapeDtypeStruct((batch, 2 * SUBLANES, POOL_WIDTH), F32),
    ] + sample_out_shape
    out_specs = [
        pl.BlockSpec((nseq, CHUNK, MIX_PAD), lambda b, c: (b, c, 0)),
        pl.BlockSpec((batch, RET_HEADS, RET_DIM, RET_DIM), const4),
        pl.BlockSpec((batch, ML_HEADS, ML_QK_PAD, ML_V_PAD), const4),
        pl.BlockSpec((batch, SUBLANES, LANES), const3),
        pl.BlockSpec((batch, SUBLANES, LANES), const3),
        pl.BlockSpec((batch, 2 * SUBLANES, POOL_WIDTH), const3),
    ] + sample_out_specs
    scratch = [
        pltpu.VMEM((nseq, RET_HEADS, RET_DIM, RET_DIM), F32),
        pltpu.VMEM((nseq, ML_HEADS, ML_QK_PAD, ML_V_PAD), F32),
        pltpu.VMEM((nseq, SUBLANES, LANES), F32),
        pltpu.VMEM((nseq, SUBLANES, LANES), F32),
        pltpu.VMEM((nseq, 2 * SUBLANES + CHUNK, POOL_WIDTH), F32),
    ] + sample_scratch + [
        pltpu.VMEM((nseq * CHUNK, D_MODEL), BF16),
        pltpu.VMEM((nseq * CHUNK, 2 * RET_DIM), F32),
        pltpu.VMEM((nseq * CHUNK, 2 * RET_DIM), F32),
    ]
    n_prompt_out = len(out_shape) - len(sample_out_shape)
    outs = pl.pallas_call(
        functools.partial(_prompt_front_kernel, n_aliased=len(stacked_prev)),
        grid=(batch // nseq, nchunk),
        in_specs=in_specs,
        out_specs=out_specs,
        out_shape=out_shape,
        scratch_shapes=scratch,
        input_output_aliases={FRONT_INPUTS + i: n_prompt_out + 1 + i
                              for i in range(len(stacked_prev))},
        compiler_params=pltpu.CompilerParams(
            dimension_semantics=("arbitrary", "arbitrary"), vmem_limit_bytes=VMEM_LIMIT),
        name="prompt_front",
    )(x, x, g_pre, w, tables["cos"], tables["sin"], tables["dintra"], tables["dq"], tables["dk"],
      tables["dch"], gains["g_ret"], gains["g_ml"], gains["gb_r"], gains["gb_c"],
      gains["w_pool"], gains["pool_scale"], proj_t, gains["g_ml_t"], c_t, n_t, m_t,
      *stacked_prev)
    mixed, *states = outs[:n_prompt_out]
    return (mixed.reshape(batch * seq, MIX_PAD), *states), tuple(outs[n_prompt_out:])


DEC_BLOCK = 8
DEC_RET_INPUTS = 9


def _decode_ret_pool_kernel(*refs, n_aliased):
    (p_ref, cos_ref, sin_ref, dch_ref, gret_ref, wpool_ref, pscale_ref, s_ref,
     pool_ref) = refs[:DEC_RET_INPUTS]
    (mix_ref, mixp_ref, so_ref, poolo_ref, qr_scr, kt_scr, v_scr,
     qs_scr) = refs[DEC_RET_INPUTS + n_aliased:]
    if n_aliased == 0:
        for later in range(1, so_ref.shape[0]):
            so_ref[later] = jnp.zeros(so_ref.shape[1:], F32)
        so_ref = so_ref.at[0]
    step = pl.program_id(0)
    nb = p_ref.shape[0]
    cos2 = cos_ref[0:1, :]
    sin2 = sin_ref[0:1, :]

    def ret_qkv(h):
        lo, hi = h * RET_DIM, (h + 1) * RET_DIM
        qr = _rotate(p_ref[:, OFF_RQ + lo:OFF_RQ + hi], cos2, sin2)
        kr = _rotate(p_ref[:, OFF_RK + lo:OFF_RK + hi], cos2, sin2) * (RET_DIM ** -0.5)
        return qr, kr, p_ref[:, OFF_RV + lo:OFF_RV + hi]

    @pl.when(step == 0)
    def _():
        for h in range(RET_HEADS):
            qr, kr, v = ret_qkv(h)
            qr_scr[h] = qr
            kt_scr[h] = kr.T
            v_scr[h] = v.astype(BF16)

    lane = lax.broadcasted_iota(jnp.int32, (LANES, nb), 1)

    def body(bb, carry):
        b = step * DEC_BLOCK + bb
        sel = lane == b
        for h in range(RET_HEADS):
            s_old = s_ref[bb, h]
            q16 = jnp.broadcast_to(qr_scr[h, pl.ds(b, 1), :], (2 * SUBLANES, RET_DIM)).astype(BF16)
            qs_scr[h, pl.ds(b, 1), :] = _dot(q16, s_old.astype(BF16))[0:1, :]
            k_m = jnp.where(sel, kt_scr[h], 0.0).astype(BF16)
            so_ref[bb, h] = s_old * dch_ref[h:h + 1, :] + _dot(k_m, v_scr[h])
        return carry

    lax.fori_loop(0, DEC_BLOCK, body, 0)

    @pl.when(step == pl.num_programs(0) - 1)
    def _():
        for h in range(RET_HEADS):
            lo, hi = h * RET_DIM, (h + 1) * RET_DIM
            qr, kr, v = ret_qkv(h)
            g = p_ref[:, OFF_RG + lo:OFF_RG + hi]
            o = jnp.sum(qr * kr, axis=-1, keepdims=True) * v + qs_scr[h] * dch_ref[h:h + 1, :]
            out = _rms(o) * gret_ref[:, lo:hi] * (g * jax.nn.sigmoid(g))
            mix_ref[:, lo:hi] = out.astype(BF16)
        u = p_ref[:, OFF_PU:OFF_PU + POOL_WIDTH]
        for g, win in enumerate(POOL_WINDOWS):
            lo, hi = g * POOL_GROUP, (g + 1) * POOL_GROUP
            wsum = u[:, lo:hi]
            for s in range(1, win):
                wsum = wsum + pool_ref[POOL_BUF - s, :, lo:hi]
            z = wsum / float(win) - u[:, lo:hi]
            y = _dot(z.astype(BF16), wpool_ref[g]) * pscale_ref[:, lo:hi]
            mixp_ref[:, lo:hi] = y.astype(BF16)
        for r in range(POOL_BUF - 1):
            poolo_ref[r] = pool_ref[r + 1]
        poolo_ref[POOL_BUF - 1] = u


def _decode_ret_pool(proj, tables, gains, layer, s_ret, pool_t, stacked_prev):
    nb = proj.shape[0]
    const2 = lambda s: (0, 0)
    const3 = lambda s: (0, 0, 0)
    full2 = lambda shape: pl.BlockSpec(shape, const2)
    s_blk = pl.BlockSpec((None, DEC_BLOCK, RET_HEADS, RET_DIM, RET_DIM),
                         lambda s: (layer, s, 0, 0, 0))
    pool_blk = (POOL_BUF, nb, POOL_WIDTH)
    in_specs = [
        full2((nb, PROJ_PAD)),
        full2((SUBLANES, RET_DIM)),
        full2((SUBLANES, RET_DIM)),
        full2((SUBLANES, LANES)),
        full2((1, RET_WIDTH)),
        pl.BlockSpec((len(POOL_WINDOWS), POOL_GROUP, POOL_GROUP), const3),
        full2((1, POOL_WIDTH)),
        s_blk,
        pl.BlockSpec((None,) + pool_blk, lambda s: (layer, 0, 0, 0)),
    ]
    assert len(in_specs) == DEC_RET_INPUTS
    in_specs += [pl.BlockSpec(memory_space=pl.ANY)] * len(stacked_prev)
    if stacked_prev:
        so_blk = s_blk
    else:
        assert layer == 0
        so_blk = pl.BlockSpec((s_ret.shape[0],) + s_blk.block_shape[1:],
                              lambda s: (0, s, 0, 0, 0))
    out_shape = [
        jax.ShapeDtypeStruct((nb, RET_WIDTH), BF16),
        jax.ShapeDtypeStruct((nb, POOL_WIDTH), BF16),
        jax.ShapeDtypeStruct(s_ret.shape, F32),
        jax.ShapeDtypeStruct(pool_blk, F32),
    ]
    out_specs = [
        full2((nb, RET_WIDTH)),
        full2((nb, POOL_WIDTH)),
        so_blk,
        pl.BlockSpec(pool_blk, const3),
    ]
    scratch = [
        pltpu.VMEM((RET_HEADS, nb, RET_DIM), F32),
        pltpu.VMEM((RET_HEADS, RET_DIM, nb), F32),
        pltpu.VMEM((RET_HEADS, nb, RET_DIM), BF16),
        pltpu.VMEM((RET_HEADS, nb, RET_DIM), F32),
    ]
    return pl.pallas_call(
        functools.partial(_decode_ret_pool_kernel, n_aliased=len(stacked_prev)),
        grid=(nb // DEC_BLOCK,),
        in_specs=in_specs,
        out_specs=out_specs,
        out_shape=out_shape,
        scratch_shapes=scratch,
        input_output_aliases={DEC_RET_INPUTS + i: 2 + i for i in range(len(stacked_prev))},
        compiler_params=pltpu.CompilerParams(
            dimension_semantics=("arbitrary",), vmem_limit_bytes=VMEM_LIMIT),
        name="decode_ret_pool",
    )(proj, tables["cos_s"], tables["sin_s"], tables["dch"], gains["g_ret"], gains["w_pool"],
      gains["pool_scale"], s_ret, pool_t, *stacked_prev)


def _sample_mlstm_update(h, j, last_j, pt_ref, gb_ref, gml_ref, c_ref, n_ref, m_ref,
                         mix_ref, co_ref, no_ref, mo_ref, rows_scr, wk_scr, qc_scr):
    dblk = c_ref.shape[0]
    q_row0 = pl.multiple_of(h * ML_QK_PAD, ML_QK_PAD)
    k_row0 = pl.multiple_of(OFF_MK - OFF_MQ + h * ML_QK_PAD, ML_QK_PAD)
    v_row0 = pl.multiple_of(OFF_MV - OFF_MQ + h * ML_V_PAD, ML_V_PAD)
    o_row0 = pl.multiple_of(OFF_MO - OFF_MQ + h * ML_V_PAD, ML_V_PAD)

    @pl.when(j == 0)
    def _():
        i_pre = pt_ref[pl.ds(GATE_I_LANE + h, 1), :] + gb_ref[pl.ds(h, 1), :]
        lf = _log_sigmoid(pt_ref[pl.ds(GATE_F_LANE + h, 1), :] + gb_ref[pl.ds(ML_HEADS + h, 1), :])
        inter = lf + m_ref[pl.ds(h, 1), :]
        m_new = jnp.maximum(inter, i_pre)
        w = jnp.exp(i_pre - m_new)
        rows_scr[0:1, :] = jnp.exp(inter - m_new)
        rows_scr[1:2, :] = w
        rows_scr[2:3, :] = m_new
        wk_scr[...] = w * (pt_ref[pl.ds(k_row0, ML_QK), :] * (ML_QK ** -0.5))
        qc_scr[...] = jnp.zeros_like(qc_scr)

    a = rows_scr[0:1, :]
    v_t = pt_ref[pl.ds(v_row0, ML_V), :]
    qc = qc_scr[...]
    for dd in range(dblk):
        d = j * dblk + dd
        c_old = c_ref[dd]
        co_ref[dd] = a * c_old + wk_scr[pl.ds(d, 1), :] * v_t
        qc = qc + pt_ref[pl.ds(q_row0 + d, 1), :] * c_old
    qc_scr[...] = qc

    @pl.when(j == last_j)
    def _():
        w = rows_scr[1:2, :]
        m_new = rows_scr[2:3, :]
        q_t = pt_ref[pl.ds(q_row0, ML_QK), :]
        k_t = pt_ref[pl.ds(k_row0, ML_QK), :] * (ML_QK ** -0.5)
        n_old = n_ref[...]
        s = jnp.sum(q_t * k_t, axis=0, keepdims=True) * w
        num = s * v_t + a * qc_scr[...]
        den = s + a * jnp.sum(q_t * n_old, axis=0, keepdims=True)
        hh = num / jnp.maximum(jnp.abs(den), jnp.exp(-m_new))
        scale = lax.rsqrt(jnp.sum(hh * hh, axis=0, keepdims=True) / ML_V + EPS)
        out_t = (hh * scale * gml_ref[0:ML_V, :]
                 * jax.nn.sigmoid(pt_ref[pl.ds(o_row0, ML_V), :]))
        out_t = jnp.concatenate([out_t, jnp.zeros((ML_V_PAD - ML_V, out_t.shape[1]), F32)], axis=0)
        mix_ref[...] = out_t.T.astype(BF16)
        no_ref[...] = a * n_old + wk_scr[...]
        mo_ref[pl.ds(h, 1), :] = m_new


def _tables(seq):
    half = RET_DIM // 2
    inv_freq = ROPE_THETA ** (-jnp.arange(half, dtype=F32) / half)

    def cos_sin(pos):
        ang = pos[:, None] * inv_freq[None, :]
        cos, sin = jnp.cos(ang), jnp.sin(ang)
        return jnp.concatenate([cos, cos], axis=-1), jnp.concatenate([-sin, sin], axis=-1)

    cos_p, sin_p = cos_sin(jnp.arange(seq, dtype=F32))
    cos_s, sin_s = cos_sin((PAST_LEN + jnp.arange(1)).astype(F32))
    log_gamma = jnp.log1p(-jnp.exp2(-5.0 - jnp.arange(RET_HEADS, dtype=F32)))
    idx = jnp.arange(CHUNK, dtype=F32)
    diff = idx[:, None] - idx[None, :]
    causal = diff >= 0
    lg = log_gamma[:, None, None]
    dintra = jnp.where(causal, jnp.exp(lg * jnp.where(causal, diff, 0.0)), 0.0)
    dq = jnp.exp(log_gamma[:, None] * (idx + 1.0))
    dk = jnp.exp(log_gamma[:, None] * (CHUNK - 1.0 - idx))

    def pad_rows(t, rows):
        return jnp.pad(t, ((0, rows - t.shape[0]), (0, 0)))

    def chunk_decay(c):
        return pad_rows(jnp.broadcast_to(jnp.exp(log_gamma * c)[:, None], (RET_HEADS, LANES)), SUBLANES)

    return {
        "cos": cos_p, "sin": sin_p,
        "cos_s": jnp.broadcast_to(cos_s, (SUBLANES, RET_DIM)),
        "sin_s": jnp.broadcast_to(sin_s, (SUBLANES, RET_DIM)),
        "dintra": dintra,
        "dq": jnp.pad(dq.T, ((0, 0), (0, LANES - RET_HEADS))),
        "dk": pad_rows(dk, SUBLANES),
        "dch": chunk_decay(float(CHUNK)),
        "dch_s": chunk_decay(1.0),
    }


def _pad_last(t, width):
    return jnp.pad(t, [(0, 0)] * (t.ndim - 1) + [(0, width - t.shape[-1])])


def _prep_gains(l, g_ret_norm, g_mlstm_norm, b_mlstm_i, b_mlstm_f, w_pool, pool_scale):
    gate_bias = jnp.concatenate([b_mlstm_i[l], b_mlstm_f[l]])
    g_ml = _pad_last(g_mlstm_norm[l].reshape(ML_HEADS, ML_V), ML_V_PAD)
    lanes = jnp.concatenate([jnp.zeros((GATE_I_LANE,), F32), _pad_last(gate_bias, LANES - GATE_I_LANE)])
    return {
        "g_ret": g_ret_norm[l][None, :],
        "g_ml": g_ml.reshape(1, -1),
        "g_ml_t": jnp.broadcast_to(g_ml[:, :, None], (ML_HEADS, ML_V_PAD, LANES)),
        "gb_r": jnp.broadcast_to(gate_bias[:, None], (SUBLANES, LANES)),
        "gb_c": jnp.broadcast_to(lanes[None, :], (SUBLANES, LANES)),
        "w_pool": w_pool[l].astype(BF16),
        "pool_scale": pool_scale[l][None, :],
    }


@jax.jit
def kernel(x_prompt, x_sample, state_ret, state_mlstm_c, state_mlstm_n, state_mlstm_m, state_pool,
           w_in, w_out, g_ret_norm, g_mlstm_norm, b_mlstm_i, b_mlstm_f, w_pool, pool_scale,
           g_pre_mix, g_post_mix, g_pre_ffn, g_post_ffn, w_ffn_gate, w_ffn_up, w_ffn_down):
    batch, seq, d = x_prompt.shape
    nb = x_sample.shape[0]
    depth = w_in.shape[0]
    tables = _tables(seq)
    tables_s = dict(tables, dch=tables["dch_s"])
    w_in_t = jnp.swapaxes(w_in, 1, 2)

    yp = x_prompt.reshape(batch * seq, d)
    ys = x_sample.reshape(nb, d)
    new_p, new_s = [], []
    stacked_ret, stacked_c = (), ()
    c_t = jnp.transpose(state_mlstm_c, (0, 2, 3, 4, 1))
    n_t = jnp.transpose(state_mlstm_n, (0, 2, 3, 1))
    m_t = jnp.transpose(state_mlstm_m, (0, 2, 1))
    pool_t = jnp.transpose(state_pool, (0, 2, 1, 3))

    t = _token_tiles(batch * seq)
    for l in range(depth):
        row = lambda g: g[l][None, :]
        w_in_l, proj, proj_t = _prep_w_in(w_in_t, l, ys, row(g_pre_mix))
        w_out_l = _prep_w_out(w_out, l)
        gains = _prep_gains(l, g_ret_norm, g_mlstm_norm, b_mlstm_i, b_mlstm_f, w_pool, pool_scale)

        (mixed, s_ret, s_c, s_n, s_m, s_pool), (mix_ml, s_c_s, n_new, m_new) = _prompt_front(
            yp.reshape(batch, seq, d), row(g_pre_mix), w_in_l, tables, gains,
            l, proj_t, c_t, n_t, m_t, stacked_c)
        new_p.append((s_ret, s_c[:, :, :ML_QK, :ML_V], s_n[:, :ML_HEADS, :ML_QK],
                      s_m[:, :ML_HEADS, 0], s_pool[:, 1:, :]))

        mix_ret, mix_pool, s_ret, pool_new = _decode_ret_pool(
            proj, tables_s, gains, l, state_ret, pool_t, stacked_ret)
        s_c = s_c_s
        mixed_s = jnp.concatenate(
            [mix_ret, jnp.swapaxes(mix_ml, 0, 1).reshape(nb, ML_HEADS * ML_V_PAD), mix_pool], axis=1)
        stacked_ret, stacked_c = (s_ret,), (s_c,)
        new_s.append((n_new, m_new, pool_new))

        x1, x1_s = _outproj(mixed, mixed_s, w_out_l, yp, ys, row(g_post_mix),
                            tm=t["outproj_tm"])
        yp, ys = _ffn(x1, x1_s, row(g_pre_ffn), w_ffn_gate, w_ffn_up, w_ffn_down, row(g_post_ffn),
                      l, tm=t["ffn_tm"], tf=t["ffn_tf"])

    stack = lambda states, i: jnp.stack([s[i] for s in states], axis=0)
    return ((yp.reshape(batch, seq, d), ys.reshape(nb, 1, d))
            + tuple(stack(new_p, i) for i in range(5))
            + (stacked_ret[0],
               jnp.transpose(stacked_c[0], (0, 4, 1, 2, 3)),
               jnp.transpose(stack(new_s, 0), (0, 3, 1, 2)),
               jnp.transpose(stack(new_s, 1), (0, 2, 1)),
               jnp.transpose(stack(new_s, 2), (0, 2, 1, 3))))
```
